```python
import math
import jax, jax.numpy as jnp
from jax import lax
import numpy as np

D_MODEL = 1024
BATCH = 8
SEQ = 4096
DEPTH = 4

N_MIXERS = 2
D_FF = 4 * D_MODEL
RMS_EPS = 1e-6
NEG_INF = -1e30

GDN_HEADS = D_MODEL // 128
GDN_DK = 128
GDN_DV = 128
GDN_CONV = 5
GDN_CHUNK = 64
GDN_QKV = GDN_HEADS * (2 * GDN_DK + GDN_DV)
GDN_IN = GDN_QKV + GDN_HEADS * GDN_DV + 4 * GDN_HEADS

DSWA_CONFIGS = ((128, 1), (512, 4), (2048, 16))
DSWA_HEADS_PER_GROUP = 6
DSWA_HEAD_DIM = 64
DSWA_HEADS = len(DSWA_CONFIGS) * DSWA_HEADS_PER_GROUP
DSWA_WIDTH = DSWA_HEADS * DSWA_HEAD_DIM

REL_BUCKETS = 32
REL_MAX_DIST = 1024

N_LAYERS_A = (DEPTH + 1) // 2
N_LAYERS_B = DEPTH // 2

kernel_name = 'hybrid_gdn_dilated_swa_encoder'


def rmsnorm(x, g):
    xf = x.astype(jnp.float32)
    y = xf * lax.rsqrt(jnp.mean(xf * xf, axis=-1, keepdims=True) + RMS_EPS)
    return (y * g.astype(jnp.float32)).astype(x.dtype)


def l2norm(t):
    return t * lax.rsqrt(jnp.sum(t * t, axis=-1, keepdims=True) + 1e-6)


def rel_bucket(rel):
    nb = REL_BUCKETS // 2
    max_exact = nb // 2
    ret = jnp.where(rel > 0, nb, 0)
    n = jnp.abs(rel)
    nf = jnp.maximum(n, 1).astype(jnp.float32)
    large = max_exact + (jnp.log(nf / max_exact) / math.log(REL_MAX_DIST / max_exact)
                         * (nb - max_exact)).astype(jnp.int32)
    large = jnp.minimum(large, nb - 1)
    return ret + jnp.where(n < max_exact, n, large)


def gated_delta_chunked(q, k, v, g, beta):
    B, H, S, DK = q.shape
    DV = v.shape[-1]
    C = GDN_CHUNK
    nc = S // C
    q = q.reshape(B, H, nc, C, DK)
    k = k.reshape(B, H, nc, C, DK)
    v = v.reshape(B, H, nc, C, DV)
    g = g.reshape(B, H, nc, C)
    beta = beta.reshape(B, H, nc, C)
    G = jnp.cumsum(g, axis=-1)
    incl = jnp.tril(jnp.ones((C, C), bool))
    strict = jnp.tril(jnp.ones((C, C), bool), -1)
    diff = jnp.where(incl, G[..., :, None] - G[..., None, :], 0.0)
    decay = jnp.where(incl, jnp.exp(diff), 0.0)
    kb = k * beta[..., None]
    a = jnp.where(strict, jnp.einsum('bhnid,bhnjd->bhnij', kb, k) * decay, 0.0)
    rhs = jnp.concatenate([v * beta[..., None], kb * jnp.exp(G)[..., None]], axis=-1)
    sol = lax.linalg.triangular_solve(a, rhs, left_side=True, lower=True, unit_diagonal=True)
    u, w = sol[..., :DV], sol[..., DV:]
    intra = jnp.einsum('bhnid,bhnjd->bhnij', q, k) * decay
    g_last = G[..., -1]
    q_dec = q * jnp.exp(G)[..., None]
    k_dec = k * jnp.exp(g_last[..., None] - G)[..., None]
    xs = tuple(jnp.moveaxis(t, 2, 0) for t in (u, w, intra, q_dec, k_dec, g_last))

    def step(state, inp):
        u_c, w_c, intra_c, qd_c, kd_c, gl_c = inp
        v_new = u_c - jnp.einsum('bhck,bhkv->bhcv', w_c, state)
        o_c = (jnp.einsum('bhck,bhkv->bhcv', qd_c, state)
               + jnp.einsum('bhij,bhjv->bhiv', intra_c, v_new))
        state = (state * jnp.exp(gl_c)[..., None, None]
                 + jnp.einsum('bhck,bhcv->bhkv', kd_c, v_new))
        return state, o_c

    s0 = jnp.zeros((B, H, DK, DV), jnp.float32)
    _, o = lax.scan(step, s0, xs)
    return jnp.moveaxis(o, 0, 2).reshape(B, H, S, DV)


def gated_deltanet_bidir(h, w_in, conv_w, a_log, dt_bias, norm_w, w_out):
    B, S, _ = h.shape
    H, DK, DV = GDN_HEADS, GDN_DK, GDN_DV
    f32 = jnp.float32
    proj = h @ w_in
    qkv = proj[..., :GDN_QKV]
    z = proj[..., GDN_QKV:GDN_QKV + H * DV].reshape(B, S, H, DV)
    ab = proj[..., GDN_QKV + H * DV:]
    pad = GDN_CONV // 2
    qkv = jax.nn.silu(lax.conv_general_dilated(
        qkv, conv_w, window_strides=(1,), padding=[(pad, pad)],
        dimension_numbers=('NWC', 'WIO', 'NWC'), feature_group_count=GDN_QKV))
    qkv = qkv.astype(f32)
    q = qkv[..., :H * DK].reshape(B, S, H, DK)
    k = qkv[..., H * DK:2 * H * DK].reshape(B, S, H, DK)
    v = qkv[..., 2 * H * DK:].reshape(B, S, H, DV)
    q = l2norm(q) * (DK ** -0.5)
    k = l2norm(k)
    a = ab[..., :2 * H].reshape(B, S, 2, H).astype(f32)
    b = ab[..., 2 * H:].reshape(B, S, 2, H).astype(f32)
    g = -jnp.exp(a_log.astype(f32)) * jax.nn.softplus(a + dt_bias.astype(f32))
    beta = jax.nn.sigmoid(b)
    qh, kh, vh = (jnp.transpose(t, (0, 2, 1, 3)) for t in (q, k, v))
    gh = jnp.transpose(g, (2, 0, 3, 1))
    bh = jnp.transpose(beta, (2, 0, 3, 1))
    o_fwd = gated_delta_chunked(qh, kh, vh, gh[0], bh[0])
    flip = lambda t: jnp.flip(t, axis=2)
    o_bwd = flip(gated_delta_chunked(flip(qh), flip(kh), flip(vh), flip(gh[1]), flip(bh[1])))
    o = jnp.transpose(o_fwd + o_bwd, (0, 2, 1, 3))
    o = o * lax.rsqrt(jnp.mean(o * o, axis=-1, keepdims=True) + RMS_EPS)
    o = o * norm_w.astype(f32) * jax.nn.silu(z.astype(f32))
    return o.reshape(B, S, H * DV).astype(h.dtype) @ w_out


def banded_attention(q, k, v, bias, half):
    N, L, H, E = q.shape
    blk = half
    nb = -(-L // blk)
    lp = nb * blk
    qb = jnp.pad(q, ((0, 0), (0, lp - L), (0, 0), (0, 0))).reshape(N, nb, blk, H, E)

    def windows(t):
        tb = jnp.pad(t, ((0, 0), (blk, lp - L + blk), (0, 0), (0, 0))).reshape(N, nb + 2, blk, H, E)
        return jnp.concatenate([tb[:, :-2], tb[:, 1:-1], tb[:, 2:]], axis=2)

    kw, vw = windows(k), windows(v)
    s = jnp.einsum('nbqhe,nbkhe->nbhqk', qb, kw).astype(jnp.float32) * (E ** -0.5)
    s = s + bias[None, None]
    q_idx = jnp.arange(lp).reshape(nb, blk)
    k_idx = jnp.arange(nb)[:, None] * blk - blk + jnp.arange(3 * blk)[None, :]
    off = k_idx[:, None, :] - q_idx[:, :, None]
    valid = (jnp.abs(off) <= half) & (k_idx[:, None, :] >= 0) & (k_idx[:, None, :] < L)
    s = jnp.where(valid[None, :, None], s, NEG_INF)
    m = jnp.max(s, axis=-1, keepdims=True)
    lse = m + jnp.log(jnp.sum(jnp.exp(s - m), axis=-1, keepdims=True))
    p = jnp.exp(s - lse)
    o = jnp.einsum('nbhqk,nbkhe->nbqhe', p.astype(v.dtype), vw).reshape(N, lp, H, E)[:, :L]
    lse = jnp.transpose(lse[..., 0], (0, 1, 3, 2)).reshape(N, lp, H)[:, :L]
    return o, lse


def to_sub(t, d):
    B, S, Hh, E = t.shape
    return jnp.swapaxes(t.reshape(B, S // d, d, Hh, E), 1, 2).reshape(B * d, S // d, Hh, E)


def from_sub(t, d, B):
    L = t.shape[1]
    rest = t.shape[2:]
    return jnp.swapaxes(t.reshape(B, d, L, *rest), 1, 2).reshape(B, L * d, *rest)


def dilated_window_attention(h, w_in, w_out, rel_table):
    B, S, _ = h.shape
    Hg, E = DSWA_HEADS_PER_GROUP, DSWA_HEAD_DIM
    qkv = (h @ w_in).reshape(B, S, 3, DSWA_HEADS, E)
    q, k, v = qkv[:, :, 0], qkv[:, :, 1], qkv[:, :, 2]
    outs, lses = [], []
    for gi, (window, dil) in enumerate(DSWA_CONFIGS):
        half = window // (2 * dil)
        hs = slice(gi * Hg, (gi + 1) * Hg)
        rel = (jnp.arange(3 * half)[None, :] - half - jnp.arange(half)[:, None]) * dil
        bias = jnp.take(rel_table, rel_bucket(rel), axis=0)[..., hs]
        bias = jnp.transpose(bias, (2, 0, 1)).astype(jnp.float32)
        o, lse = banded_attention(to_sub(q[:, :, hs], dil), to_sub(k[:, :, hs], dil),
                                  to_sub(v[:, :, hs], dil), bias, half)
        outs.append(from_sub(o, dil, B))
        lses.append(from_sub(lse, dil, B))
    o = jnp.stack(outs, axis=2)
    alpha = jax.nn.softmax(jnp.stack(lses, axis=2), axis=2)
    o = (o * alpha[..., None].astype(o.dtype)).reshape(B, S, DSWA_WIDTH)
    return o @ w_out


def squared_relu_mlp(h, w1, w2):
    return jnp.square(jax.nn.relu(h @ w1)) @ w2


def setup_inputs(seed: int = 0) -> dict:
    key = jax.random.key(seed)
    ks = jax.random.split(key, 16)
    f32 = jnp.float32
    nrm = lambda kk, shape, scale: jax.random.normal(kk, shape, f32) * scale
    x = nrm(ks[0], (BATCH, SEQ, D_MODEL), 1.0)
    norm_mix = 1.0 + nrm(ks[1], (DEPTH, D_MODEL), 0.02)
    norm_mlp = 1.0 + nrm(ks[2], (DEPTH, D_MODEL), 0.02)
    norm_final = 1.0 + nrm(ks[3], (D_MODEL,), 0.02)
    rel_bias = nrm(ks[4], (REL_BUCKETS, DSWA_HEADS), 0.2)
    gdn_w_in = nrm(ks[5], (N_LAYERS_A, D_MODEL, GDN_IN), D_MODEL ** -0.5)
    gdn_conv_w = nrm(ks[6], (N_LAYERS_A, GDN_CONV, 1, GDN_QKV), GDN_CONV ** -0.5)
    gdn_a_log = jnp.log(jax.random.uniform(ks[7], (N_LAYERS_A, 2, GDN_HEADS), f32, 1.0, 16.0))
    dt = jnp.exp(jax.random.uniform(ks[8], (N_LAYERS_A, 2, GDN_HEADS), f32,
                                    math.log(1e-3), math.log(1e-1)))
    gdn_dt_bias = dt + jnp.log(-jnp.expm1(-dt))
    gdn_norm_w = 1.0 + nrm(ks[9], (N_LAYERS_A, GDN_DV), 0.02)
    gdn_w_out = nrm(ks[10], (N_LAYERS_A, GDN_HEADS * GDN_DV, D_MODEL), (GDN_HEADS * GDN_DV) ** -0.5)
    dswa_w_in = nrm(ks[11], (N_LAYERS_B, D_MODEL, 3 * DSWA_WIDTH), D_MODEL ** -0.5)
    dswa_w_out = nrm(ks[12], (N_LAYERS_B, DSWA_WIDTH, D_MODEL), DSWA_WIDTH ** -0.5)
    mlp_w1 = nrm(ks[13], (DEPTH, D_MODEL, D_FF), D_MODEL ** -0.5)
    mlp_w2 = nrm(ks[14], (DEPTH, D_FF, D_MODEL), D_FF ** -0.5)
    return {'x': x, 'norm_mix': norm_mix, 'norm_mlp': norm_mlp, 'norm_final': norm_final,
            'rel_bias': rel_bias, 'gdn_w_in': gdn_w_in, 'gdn_conv_w': gdn_conv_w,
            'gdn_a_log': gdn_a_log, 'gdn_dt_bias': gdn_dt_bias, 'gdn_norm_w': gdn_norm_w,
            'gdn_w_out': gdn_w_out, 'dswa_w_in': dswa_w_in, 'dswa_w_out': dswa_w_out,
            'mlp_w1': mlp_w1, 'mlp_w2': mlp_w2}


def reference(x, norm_mix, norm_mlp, norm_final, rel_bias, gdn_w_in, gdn_conv_w,
              gdn_a_log, gdn_dt_bias, gdn_norm_w, gdn_w_out, dswa_w_in, dswa_w_out,
              mlp_w1, mlp_w2):
    for i in range(DEPTH):
        h = rmsnorm(x, norm_mix[i])
        j = i // N_MIXERS
        if i % N_MIXERS == 0:
            y = gated_deltanet_bidir(h, gdn_w_in[j], gdn_conv_w[j], gdn_a_log[j],
                                     gdn_dt_bias[j], gdn_norm_w[j], gdn_w_out[j])
        else:
            y = dilated_window_attention(h, dswa_w_in[j], dswa_w_out[j], rel_bias)
        x = x + y
        h = rmsnorm(x, norm_mlp[i])
        x = x + squared_relu_mlp(h, mlp_w1[i], mlp_w2[i])
    return rmsnorm(x, norm_final)
```

```python
import functools
import math

import jax
import jax.numpy as jnp
from jax import lax
from jax.experimental import pallas as pl
from jax.experimental.pallas import tpu as pltpu

F32 = jnp.float32
BF16 = jnp.bfloat16

RMS_EPS = 1e-6
L2_EPS = 1e-6
NEG_INF = -1e30

LANES = 128
SUBLANES = 8
VMEM_LIMIT = 56 * 1024 * 1024

GDN_DK = 128
GDN_CONV = 5
GDN_CHUNK = LANES
DSWA_HEAD_DIM = 64
DSWA_CONFIGS = ((128, 1), (512, 4), (2048, 16))
DSWA_HALF = 64
REL_BUCKETS = 32
REL_MAX_DIST = 1024


def _params(*sem):
    return pltpu.CompilerParams(dimension_semantics=sem, vmem_limit_bytes=VMEM_LIMIT)


def _dot(a, b):
    return jnp.dot(a.astype(BF16), b.astype(BF16), preferred_element_type=F32)


def _dot_nt(a, b):
    return lax.dot_general(a.astype(BF16), b.astype(BF16), (((1,), (1,)), ((), ())),
                           preferred_element_type=F32)


def _dot_exact(a, b):
    return jnp.dot(a, b, precision=lax.Precision.HIGHEST, preferred_element_type=F32)


def _rms(x, g):
    return x * lax.rsqrt(jnp.mean(x * x, axis=-1, keepdims=True) + RMS_EPS) * g


def _sigmoid(x):
    return 1.0 / (1.0 + jnp.exp(-x))


def _softplus(x):
    return jnp.maximum(x, 0.0) + jnp.log1p(jnp.exp(-jnp.abs(x)))


def _norm_proj_kernel(x_ref, g_ref, *refs, n_out, chunk):
    w_refs, o_refs = refs[:n_out], refs[n_out:]
    h = _rms(x_ref[...], g_ref[...]).astype(BF16)
    for w_ref, o_ref in zip(w_refs, o_refs):
        n = w_ref.shape[1]
        for n0 in range(0, n, chunk):
            n1 = min(n0 + chunk, n)
            o_ref[:, n0:n1] = jnp.dot(h, w_ref[:, n0:n1], preferred_element_type=F32)


def _norm_proj(x2, g, ws, tm=256, chunk=512):
    T, D = x2.shape
    n_out = len(ws)
    return pl.pallas_call(
        functools.partial(_norm_proj_kernel, n_out=n_out, chunk=chunk),
        grid=(T // tm,),
        in_specs=[pl.BlockSpec((tm, D), lambda i: (i, 0)),
                  pl.BlockSpec((1, D), lambda i: (0, 0))]
                 + [pl.BlockSpec(w.shape, lambda i: (0, 0)) for w in ws],
        out_specs=[pl.BlockSpec((tm, w.shape[1]), lambda i: (i, 0)) for w in ws],
        out_shape=[jax.ShapeDtypeStruct((T, w.shape[1]), F32) for w in ws],
        compiler_params=_params("parallel"),
        name="norm_proj",
    )(x2, g.reshape(1, D), *ws)


def _gdn_gate_kernel(x_ref, g_ref, w_ref, wt_ref, prow_ref, pcol_ref, col_ref, row_ref, *, nh):
    C = GDN_CHUNK
    h = _rms(x_ref[...], g_ref[...]).astype(BF16)
    a_c = jnp.dot(h, w_ref[...], preferred_element_type=F32)
    a_r = lax.dot_general(wt_ref[...], h, (((1,), (1,)), ((), ())),
                          preferred_element_type=F32)
    ri = lax.broadcasted_iota(jnp.int32, (C, C), 0)
    ci = lax.broadcasted_iota(jnp.int32, (C, C), 1)
    lower = (ri >= ci).astype(F32)
    upper = (ri <= ci).astype(F32)
    g_c = -jnp.exp(prow_ref[0:1, :]) * _softplus(a_c + prow_ref[1:2, :])
    g_r = -jnp.exp(pcol_ref[:, 0:1]) * _softplus(a_r + pcol_ref[:, 1:2])
    beta_c = _sigmoid(a_c)
    beta_r = _sigmoid(a_r)
    for c in range(x_ref.shape[0] // C):
        sl = slice(c * C, (c + 1) * C)
        gc = g_c[sl, :]
        cum = jnp.where(ci < nh, _dot_exact(lower, gc), _dot_exact(upper, gc))
        col_ref[sl, :] = jnp.where(ci < 2 * nh, cum, beta_c[sl, :])
        gr = g_r[:, sl]
        cum_r = jnp.where(ri < nh, _dot_exact(gr, upper), _dot_exact(gr, lower))
        full = jnp.where(ri < 2 * nh, cum_r, beta_r[:, sl])
        row_ref[0, :, sl] = full[:4 * nh, :]


def _gdn_gates(x2, g, w_ab, a_log, dt_bias, B, S, tm=256):
    T, D = x2.shape
    nh = a_log.shape[1]
    w_pad = jnp.zeros((D, LANES), F32).at[:, :4 * nh].set(w_ab)
    alog = jnp.zeros((LANES,), F32).at[:2 * nh].set(a_log.reshape(-1))
    dtb = jnp.zeros((LANES,), F32).at[:2 * nh].set(dt_bias.reshape(-1))
    prow = jnp.zeros((SUBLANES, LANES), F32).at[0].set(alog).at[1].set(dtb)
    pcol = jnp.zeros((LANES, LANES), F32).at[:, 0].set(alog).at[:, 1].set(dtb)
    nt = S // tm
    return pl.pallas_call(
        functools.partial(_gdn_gate_kernel, nh=nh),
        grid=(T // tm,),
        in_specs=[pl.BlockSpec((tm, D), lambda i: (i, 0)),
                  pl.BlockSpec((1, D), lambda i: (0, 0)),
                  pl.BlockSpec((D, LANES), lambda i: (0, 0)),
                  pl.BlockSpec((LANES, D), lambda i: (0, 0)),
                  pl.BlockSpec((SUBLANES, LANES), lambda i: (0, 0)),
                  pl.BlockSpec((LANES, LANES), lambda i: (0, 0))],
        out_specs=[pl.BlockSpec((tm, LANES), lambda i: (i, 0)),
                   pl.BlockSpec((1, 4 * nh, tm), lambda i: (i // nt, 0, i % nt))],
        out_shape=[jax.ShapeDtypeStruct((T, LANES), F32),
                   jax.ShapeDtypeStruct((B, 4 * nh, S), F32)],
        compiler_params=_params("parallel"),
        name="gdn_gates",
    )(x2, g.reshape(1, D), w_pad.astype(BF16), w_pad.T.astype(BF16), prow, pcol)


def _gdn_conv_kernel(x_ref, xp_ref, xn_ref, w_ref, o_ref, xe_ref, *, nh, nt):
    t = pl.program_id(1)
    j = pl.program_id(2)
    tp = x_ref.shape[1]
    pad = GDN_CONV // 2
    xe_ref[0:SUBLANES, :] = jnp.where(t > 0, xp_ref[0], 0.0)
    xe_ref[SUBLANES:SUBLANES + tp, :] = x_ref[0]
    xe_ref[SUBLANES + tp:, :] = jnp.where(t < nt - 1, xn_ref[0], 0.0)
    acc = None
    for i in range(GDN_CONV):
        term = w_ref[i:i + 1, :] * xe_ref[pl.ds(SUBLANES - pad + i, tp), :]
        acc = term if acc is None else acc + term
    y = acc * _sigmoid(acc)
    inv = lax.rsqrt(jnp.sum(y * y, axis=-1, keepdims=True) + L2_EPS)
    fac = jnp.where(j < nh, inv * (GDN_DK ** -0.5), jnp.where(j < 2 * nh, inv, 1.0))
    o_ref[0] = y * fac


def _gdn_conv(qkv, conv_w, nh, tp=512):
    B, S, W = qkv.shape
    nt = S // tp
    r = tp // SUBLANES
    return pl.pallas_call(
        functools.partial(_gdn_conv_kernel, nh=nh, nt=nt),
        grid=(B, nt, W // LANES),
        in_specs=[pl.BlockSpec((1, tp, LANES), lambda b, t, j: (b, t, j)),
                  pl.BlockSpec((1, SUBLANES, LANES),
                               lambda b, t, j: (b, jnp.maximum(t * r - 1, 0), j)),
                  pl.BlockSpec((1, SUBLANES, LANES),
                               lambda b, t, j: (b, jnp.minimum((t + 1) * r, S // SUBLANES - 1), j)),
                  pl.BlockSpec((GDN_CONV, LANES), lambda b, t, j: (0, j))],
        out_specs=pl.BlockSpec((1, tp, LANES), lambda b, t, j: (b, t, j)),
        out_shape=jax.ShapeDtypeStruct((B, S, W), F32),
        scratch_shapes=[pltpu.VMEM((tp + 2 * SUBLANES, LANES), F32)],
        compiler_params=_params("parallel", "parallel", "parallel"),
        name="gdn_conv",
    )(qkv, qkv, qkv, conv_w)


def _gdn_scan_kernel(q_ref, k_ref, v_ref, col_ref, row_ref, o_ref, state_ref, *, rev, nh):
    C = GDN_CHUNK
    t = pl.program_id(1)

    @pl.when(t == 0)
    def _():
        state_ref[...] = jnp.zeros_like(state_ref)

    ri = lax.broadcasted_iota(jnp.int32, (C, C), 0)
    ci = lax.broadcasted_iota(jnp.int32, (C, C), 1)
    incl = (ri <= ci) if rev else (ri >= ci)
    strict = (ri < ci) if rev else (ri > ci)
    eye = (ri == ci).astype(F32)
    levels = [((ri >> (lg + 1)) == (ci >> (lg + 1))) & ((ri >> lg) != (ci >> lg))
              for lg in range(int(math.log2(C)))]
    col = col_ref[0]
    row = row_ref[0]
    last = 0 if rev else C - 1
    for h in range(nh):
        hs = slice(h * LANES, (h + 1) * LANES)
        gi = (nh if rev else 0) + h
        q = q_ref[0, :, hs]
        k = k_ref[0, :, hs]
        v = v_ref[0, :, hs]
        g_col = col[:, gi:gi + 1]
        b_col = col[:, 2 * nh + gi:2 * nh + gi + 1]
        g_row = row[gi:gi + 1, :]
        g_last = g_col[last:last + 1, :]
        decay = jnp.where(incl, jnp.exp(jnp.where(incl, g_col - g_row, 0.0)), 0.0)
        kb = k * b_col
        a = jnp.where(strict, _dot_nt(kb, k) * decay, 0.0)
        intra = _dot_nt(q, k) * decay
        tinv = eye - jnp.where(levels[0], a, 0.0)
        for m in levels[1:]:
            tinv = tinv - _dot(_dot(tinv, jnp.where(m, a, 0.0)), tinv)
        e_col = jnp.exp(g_col)
        rhs = jnp.concatenate([v * b_col, kb * e_col], axis=1)
        sol = _dot(tinv, rhs)
        u = sol[:, :LANES]
        w = sol[:, LANES:]
        state = state_ref[h]
        v_new = u - _dot(w, state)
        o_ref[0, :, hs] = _dot(q * e_col, state) + _dot(intra, v_new)
        k_dec = k * jnp.exp(g_last - g_col)
        state_ref[h] = state * jnp.exp(g_last) + _dot(k_dec.T, v_new)


def _gdn_scan(qkvn, col, row, nh, rev):
    B, S, _ = qkvn.shape
    C = GDN_CHUNK
    nt = S // C
    width = nh * LANES
    tmap = (lambda t: nt - 1 - t) if rev else (lambda t: t)
    col3 = col.reshape(B, S, LANES)
    return pl.pallas_call(
        functools.partial(_gdn_scan_kernel, rev=rev, nh=nh),
        grid=(B, nt),
        in_specs=[pl.BlockSpec((1, C, width), lambda b, t: (b, tmap(t), 0)),
                  pl.BlockSpec((1, C, width), lambda b, t: (b, tmap(t), 1)),
                  pl.BlockSpec((1, C, width), lambda b, t: (b, tmap(t), 2)),
                  pl.BlockSpec((1, C, LANES), lambda b, t: (b, tmap(t), 0)),
                  pl.BlockSpec((1, 4 * nh, C), lambda b, t: (b, 0, tmap(t)))],
        out_specs=pl.BlockSpec((1, C, width), lambda b, t: (b, tmap(t), 0)),
        out_shape=jax.ShapeDtypeStruct((B, S, width), F32),
        scratch_shapes=[pltpu.VMEM((nh, GDN_DK, LANES), F32)],
        compiler_params=_params("parallel", "arbitrary"),
        name="gdn_scan_bwd" if rev else "gdn_scan_fwd",
    )(qkvn, qkvn, qkvn, col3, row)


def _gdn_out_kernel(of_ref, ob_ref, z_ref, x_ref, nw_ref, w_ref, o_ref, y_ref, *, nh):
    for h in range(nh):
        hs = slice(h * LANES, (h + 1) * LANES)
        o = of_ref[:, hs] + ob_ref[:, hs]
        o = o * lax.rsqrt(jnp.mean(o * o, axis=-1, keepdims=True) + RMS_EPS)
        z = z_ref[:, hs]
        y_ref[:, hs] = (o * nw_ref[...] * (z * _sigmoid(z))).astype(BF16)
    o_ref[...] = x_ref[...] + jnp.dot(y_ref[...], w_ref[...], preferred_element_type=F32)


def _gdn_out(o_f, o_b, z, x2, norm_w, w_out, nh, tm=512):
    T, D = x2.shape
    W = o_f.shape[1]
    tok = lambda n: pl.BlockSpec((tm, n), lambda i: (i, 0))
    return pl.pallas_call(
        functools.partial(_gdn_out_kernel, nh=nh),
        grid=(T // tm,),
        in_specs=[tok(W), tok(W), tok(W), tok(D),
                  pl.BlockSpec((1, LANES), lambda i: (0, 0)),
                  pl.BlockSpec((W, D), lambda i: (0, 0))],
        out_specs=tok(D),
        out_shape=jax.ShapeDtypeStruct((T, D), F32),
        scratch_shapes=[pltpu.VMEM((tm, W), BF16)],
        compiler_params=_params("parallel"),
        name="gdn_out",
    )(o_f, o_b, z, x2, norm_w.reshape(1, LANES), w_out)


def _dswa_kernel(q_ref, k_ref, kp_ref, kn_ref, v_ref, vp_ref, vn_ref, bias_ref,
                 o_ref, lse_ref, *, sub_len):
    t = pl.program_id(2)
    lt = q_ref.shape[1]
    hb = DSWA_HALF
    q = q_ref[0] * (DSWA_HEAD_DIM ** -0.5)
    kext = jnp.concatenate([kp_ref[0], k_ref[0], kn_ref[0]], axis=0).astype(BF16)
    vext = jnp.concatenate([vp_ref[0], v_ref[0], vn_ref[0]], axis=0).astype(BF16)
    first_head = lax.broadcasted_iota(jnp.int32, (1, LANES), 1) < DSWA_HEAD_DIM
    n_pairs = q_ref.shape[2] // LANES
    for qb in range(lt // hb):
        kpos = t * lt + (qb - 1) * hb + lax.broadcasted_iota(jnp.int32, (1, 3 * hb), 1)
        kvalid = (kpos >= 0) & (kpos < sub_len)
        rows = slice(qb * hb, (qb + 1) * hb)
        for pr in range(n_pairs):
            ls = slice(pr * LANES, (pr + 1) * LANES)
            qp = q[rows, ls]
            kw = kext[qb * hb:(qb + 3) * hb, ls]
            vw = vext[qb * hb:(qb + 3) * hb, ls]
            outs, lses = [], []
            for hh in range(2):
                msk = first_head if hh == 0 else jnp.logical_not(first_head)
                s = _dot_nt(jnp.where(msk, qp, 0.0), kw) + bias_ref[2 * pr + hh]
                s = jnp.where(kvalid, s, NEG_INF)
                m = jnp.max(s, axis=-1, keepdims=True)
                p = jnp.exp(s - m)
                l = jnp.sum(p, axis=-1, keepdims=True)
                outs.append(_dot(p, vw) * (1.0 / l))
                lses.append(m + jnp.log(l))
            o_ref[0, rows, ls] = jnp.where(first_head, outs[0], outs[1])
            lse_ref[0, rows, ls] = jnp.where(first_head, lses[0], lses[1])


def _rel_bucket(rel):
    nb = REL_BUCKETS // 2
    max_exact = nb // 2
    ret = jnp.where(rel > 0, nb, 0)
    n = jnp.abs(rel)
    nf = jnp.maximum(n, 1).astype(F32)
    large = max_exact + (jnp.log(nf / max_exact) / math.log(REL_MAX_DIST / max_exact)
                         * (nb - max_exact)).astype(jnp.int32)
    large = jnp.minimum(large, nb - 1)
    return ret + jnp.where(n < max_exact, n, large)


def _band_bias(rel_table, heads, dil):
    hb = DSWA_HALF
    off = jnp.arange(3 * hb)[None, :] - hb - jnp.arange(hb)[:, None]
    bias = jnp.take(rel_table, _rel_bucket(off * dil), axis=0)[..., heads]
    bias = jnp.transpose(bias, (2, 0, 1)).astype(F32)
    return jnp.where((jnp.abs(off) <= hb)[None], bias, NEG_INF)


def _dswa_group(qkv, bias, gi, dil, n_groups, lt=256):
    B, S, W3 = qkv.shape
    hb = DSWA_HALF
    gw = W3 // (3 * n_groups)
    L = S // dil
    lt = min(lt, L)
    nl = L // lt
    nblk = W3 // gw
    r = lt // hb
    view = qkv.reshape(B, L, dil * W3)
    main = lambda which: pl.BlockSpec(
        (1, lt, gw), lambda b, d, t: (b, t, d * nblk + which * n_groups + gi))
    prev = lambda which: pl.BlockSpec(
        (1, hb, gw), lambda b, d, t: (b, jnp.maximum(t * r - 1, 0), d * nblk + which * n_groups + gi))
    nxt = lambda which: pl.BlockSpec(
        (1, hb, gw), lambda b, d, t: (b, jnp.minimum((t + 1) * r, L // hb - 1),
                                      d * nblk + which * n_groups + gi))
    o, lse = pl.pallas_call(
        functools.partial(_dswa_kernel, sub_len=L),
        grid=(B, dil, nl),
        in_specs=[main(0), main(1), prev(1), nxt(1), main(2), prev(2), nxt(2),
                  pl.BlockSpec(bias.shape, lambda b, d, t: (0, 0, 0))],
        out_specs=[pl.BlockSpec((1, lt, gw), lambda b, d, t: (b, t, d)),
                   pl.BlockSpec((1, lt, gw), lambda b, d, t: (b, t, d))],
        out_shape=[jax.ShapeDtypeStruct((B, L, dil * gw), F32),
                   jax.ShapeDtypeStruct((B, L, dil * gw), F32)],
        compiler_params=_params("parallel", "parallel", "parallel"),
        name=f"dswa_d{dil}",
    )(view, view, view, view, view, view, view, bias)
    return o.reshape(B * S, gw), lse.reshape(B * S, gw)


def _dswa_out_kernel(*refs, ng):
    o_refs, l_refs = refs[:ng], refs[ng:2 * ng]
    x_ref, w_ref, out_ref = refs[2 * ng:]
    lses = [r[...] for r in l_refs]
    m = functools.reduce(jnp.maximum, lses)
    es = [jnp.exp(l - m) for l in lses]
    inv = 1.0 / functools.reduce(lambda a, b: a + b, es)
    acc = x_ref[...]
    gw = o_refs[0].shape[1]
    for g in range(ng):
        acc = acc + _dot(o_refs[g][...] * (es[g] * inv), w_ref[g * gw:(g + 1) * gw, :])
    out_ref[...] = acc


def _dswa_out(os_, lses, x2, w_out, tm=512):
    T, D = x2.shape
    ng = len(os_)
    gw = os_[0].shape[1]
    tok = lambda n: pl.BlockSpec((tm, n), lambda i: (i, 0))
    return pl.pallas_call(
        functools.partial(_dswa_out_kernel, ng=ng),
        grid=(T // tm,),
        in_specs=[tok(gw)] * (2 * ng) + [tok(D), pl.BlockSpec(w_out.shape, lambda i: (0, 0))],
        out_specs=tok(D),
        out_shape=jax.ShapeDtypeStruct((T, D), F32),
        compiler_params=_params("parallel"),
        name="dswa_out",
    )(*os_, *lses, x2, w_out)


def _mlp_kernel(x_ref, g_ref, w1_ref, w2_ref, gf_ref, o_ref, acc_ref, *, chunk, final):
    x = x_ref[...]
    h = _rms(x, g_ref[...]).astype(BF16)
    dff = w1_ref.shape[1]
    for c in range(0, dff, chunk):
        a = jnp.maximum(jnp.dot(h, w1_ref[:, c:c + chunk], preferred_element_type=F32), 0.0)
        part = jnp.dot((a * a).astype(BF16), w2_ref[c:c + chunk, :], preferred_element_type=F32)
        if c == 0:
            acc_ref[...] = part
        else:
            acc_ref[...] += part
    y = x + acc_ref[...]
    if final:
        y = _rms(y, gf_ref[...])
    o_ref[...] = y


def _mlp(x2, g, w1, w2, g_final, final, tm=512, chunk=512):
    T, D = x2.shape
    dff = w1.shape[1]
    return pl.pallas_call(
        functools.partial(_mlp_kernel, chunk=chunk, final=final),
        grid=(T // tm,),
        in_specs=[pl.BlockSpec((tm, D), lambda i: (i, 0)),
                  pl.BlockSpec((1, D), lambda i: (0, 0)),
                  pl.BlockSpec((D, dff), lambda i: (0, 0), pipeline_mode=pl.Buffered(1)),
                  pl.BlockSpec((dff, D), lambda i: (0, 0), pipeline_mode=pl.Buffered(1)),
                  pl.BlockSpec((1, D), lambda i: (0, 0))],
        out_specs=pl.BlockSpec((tm, D), lambda i: (i, 0)),
        out_shape=jax.ShapeDtypeStruct((T, D), F32),
        scratch_shapes=[pltpu.VMEM((tm, D), F32)],
        compiler_params=_params("parallel"),
        name="mlp",
    )(x2, g.reshape(1, D), w1, w2, g_final.reshape(1, D))


def _gdn_layer(x2, g, w_in, conv_w, a_log, dt_bias, norm_w, w_out, B, S):
    nh = a_log.shape[1]
    qkv_w = 3 * nh * LANES
    z_w = nh * LANES
    qkv, z = _norm_proj(x2, g, [w_in[:, :qkv_w].astype(BF16),
                                w_in[:, qkv_w:qkv_w + z_w].astype(BF16)])
    col, row = _gdn_gates(x2, g, w_in[:, qkv_w + z_w:], a_log, dt_bias, B, S)
    qkvn = _gdn_conv(qkv.reshape(B, S, qkv_w), conv_w.reshape(GDN_CONV, qkv_w), nh)
    o_f = _gdn_scan(qkvn, col, row, nh, rev=False)
    o_b = _gdn_scan(qkvn, col, row, nh, rev=True)
    return _gdn_out(o_f.reshape(B * S, z_w), o_b.reshape(B * S, z_w), z, x2, norm_w,
                    w_out.astype(BF16), nh)


def _dswa_layer(x2, g, w_in, w_out, rel_bias, B, S):
    ng = len(DSWA_CONFIGS)
    (qkv,) = _norm_proj(x2, g, [w_in.astype(BF16)])
    qkv = qkv.reshape(B, S, w_in.shape[1])
    hg = rel_bias.shape[1] // ng
    os_, lses = [], []
    for gi, (window, dil) in enumerate(DSWA_CONFIGS):
        assert window // (2 * dil) == DSWA_HALF
        bias = _band_bias(rel_bias, slice(gi * hg, (gi + 1) * hg), dil)
        o, lse = _dswa_group(qkv, bias, gi, dil, ng)
        os_.append(o)
        lses.append(lse)
    return _dswa_out(os_, lses, x2, w_out.astype(BF16))


def kernel(x, norm_mix, norm_mlp, norm_final, rel_bias, gdn_w_in, gdn_conv_w, gdn_a_log,
           gdn_dt_bias, gdn_norm_w, gdn_w_out, dswa_w_in, dswa_w_out, mlp_w1, mlp_w2):
    B, S, D = x.shape
    depth = norm_mix.shape[0]
    x2 = x.reshape(B * S, D)
    for i in range(depth):
        j = i // 2
        if i % 2 == 0:
            x2 = _gdn_layer(x2, norm_mix[i], gdn_w_in[j], gdn_conv_w[j], gdn_a_log[j],
                            gdn_dt_bias[j], gdn_norm_w[j], gdn_w_out[j], B, S)
        else:
            x2 = _dswa_layer(x2, norm_mix[i], dswa_w_in[j], dswa_w_out[j], rel_bias, B, S)
        x2 = _mlp(x2, norm_mlp[i], mlp_w1[i].astype(BF16), mlp_w2[i].astype(BF16),
                  norm_final, final=(i == depth - 1))
    return x2.reshape(B, S, D)
```

```python
import functools
import math

import jax
import jax.numpy as jnp
from jax import lax
from jax.experimental import pallas as pl
from jax.experimental.pallas import tpu as pltpu

F32 = jnp.float32
BF16 = jnp.bfloat16

RMS_EPS = 1e-6
L2_EPS = 1e-6
NEG_INF = -1e30

LANES = 128
SUBLANES = 8
VMEM_LIMIT = 56 * 1024 * 1024

GDN_DK = 128
GDN_CONV = 5
GDN_CHUNK = LANES
DSWA_HEAD_DIM = 64
DSWA_CONFIGS = ((128, 1), (512, 4), (2048, 16))
DSWA_HALF = 64
REL_BUCKETS = 32
REL_MAX_DIST = 1024


def _params(*sem):
    return pltpu.CompilerParams(dimension_semantics=sem, vmem_limit_bytes=VMEM_LIMIT)


def _dot(a, b):
    return jnp.dot(a.astype(BF16), b.astype(BF16), preferred_element_type=F32)


def _dot_nt(a, b):
    return lax.dot_general(a.astype(BF16), b.astype(BF16), (((1,), (1,)), ((), ())),
                           preferred_element_type=F32)


def _dot_exact(a, b):
    return jnp.dot(a, b, precision=lax.Precision.HIGHEST, preferred_element_type=F32)


def _rms(x, g):
    return x * lax.rsqrt(jnp.mean(x * x, axis=-1, keepdims=True) + RMS_EPS) * g


def _sigmoid(x):
    return 1.0 / (1.0 + jnp.exp(-x))


def _softplus(x):
    return jnp.maximum(x, 0.0) + jnp.log1p(jnp.exp(-jnp.abs(x)))


def _norm_proj_kernel(x_ref, g_ref, *refs, n_out, chunk):
    w_refs, o_refs = refs[:n_out], refs[n_out:]
    h = _rms(x_ref[...], g_ref[...]).astype(BF16)
    for w_ref, o_ref in zip(w_refs, o_refs):
        n = w_ref.shape[1]
        for n0 in range(0, n, chunk):
            n1 = min(n0 + chunk, n)
            o_ref[:, n0:n1] = jnp.dot(h, w_ref[:, n0:n1], preferred_element_type=F32)


def _norm_proj(x2, g, ws, tm=256, chunk=512):
    T, D = x2.shape
    n_out = len(ws)
    return pl.pallas_call(
        functools.partial(_norm_proj_kernel, n_out=n_out, chunk=chunk),
        grid=(T // tm,),
        in_specs=[pl.BlockSpec((tm, D), lambda i: (i, 0)),
                  pl.BlockSpec((1, D), lambda i: (0, 0))]
                 + [pl.BlockSpec(w.shape, lambda i: (0, 0)) for w in ws],
        out_specs=[pl.BlockSpec((tm, w.shape[1]), lambda i: (i, 0)) for w in ws],
        out_shape=[jax.ShapeDtypeStruct((T, w.shape[1]), F32) for w in ws],
        compiler_params=_params("parallel"),
        name="norm_proj",
    )(x2, g.reshape(1, D), *ws)


def _gdn_gate_kernel(x_ref, g_ref, w_ref, wt_ref, prow_ref, pcol_ref, col_ref, row_ref, *, nh):
    C = GDN_CHUNK
    h = _rms(x_ref[...], g_ref[...]).astype(BF16)
    a_c = jnp.dot(h, w_ref[...], preferred_element_type=F32)
    a_r = lax.dot_general(wt_ref[...], h, (((1,), (1,)), ((), ())),
                          preferred_element_type=F32)
    ri = lax.broadcasted_iota(jnp.int32, (C, C), 0)
    ci = lax.broadcasted_iota(jnp.int32, (C, C), 1)
    lower = (ri >= ci).astype(F32)
    upper = (ri <= ci).astype(F32)
    g_c = -jnp.exp(prow_ref[0:1, :]) * _softplus(a_c + prow_ref[1:2, :])
    g_r = -jnp.exp(pcol_ref[:, 0:1]) * _softplus(a_r + pcol_ref[:, 1:2])
    beta_c = _sigmoid(a_c)
    beta_r = _sigmoid(a_r)
    for c in range(x_ref.shape[0] // C):
        sl = slice(c * C, (c + 1) * C)
        gc = g_c[sl, :]
        cum = jnp.where(ci < nh, _dot_exact(lower, gc), _dot_exact(upper, gc))
        col_ref[sl, :] = jnp.where(ci < 2 * nh, cum, beta_c[sl, :])
        gr = g_r[:, sl]
        cum_r = jnp.where(ri < nh, _dot_exact(gr, upper), _dot_exact(gr, lower))
        full = jnp.where(ri < 2 * nh, cum_r, beta_r[:, sl])
        row_ref[0, :, sl] = full[:4 * nh, :]


def _gdn_gates(x2, g, w_ab, a_log, dt_bias, B, S, tm=256):
    T, D = x2.shape
    nh = a_log.shape[1]
    w_pad = jnp.zeros((D, LANES), F32).at[:, :4 * nh].set(w_ab)
    alog = jnp.zeros((LANES,), F32).at[:2 * nh].set(a_log.reshape(-1))
    dtb = jnp.zeros((LANES,), F32).at[:2 * nh].set(dt_bias.reshape(-1))
    prow = jnp.zeros((SUBLANES, LANES), F32).at[0].set(alog).at[1].set(dtb)
    pcol = jnp.zeros((LANES, LANES), F32).at[:, 0].set(alog).at[:, 1].set(dtb)
    nt = S // tm
    return pl.pallas_call(
        functools.partial(_gdn_gate_kernel, nh=nh),
        grid=(T // tm,),
        in_specs=[pl.BlockSpec((tm, D), lambda i: (i, 0)),
                  pl.BlockSpec((1, D), lambda i: (0, 0)),
                  pl.BlockSpec((D, LANES), lambda i: (0, 0)),
                  pl.BlockSpec((LANES, D), lambda i: (0, 0)),
                  pl.BlockSpec((SUBLANES, LANES), lambda i: (0, 0)),
                  pl.BlockSpec((LANES, LANES), lambda i: (0, 0))],
        out_specs=[pl.BlockSpec((tm, LANES), lambda i: (i, 0)),
                   pl.BlockSpec((1, 4 * nh, tm), lambda i: (i // nt, 0, i % nt))],
        out_shape=[jax.ShapeDtypeStruct((T, LANES), F32),
                   jax.ShapeDtypeStruct((B, 4 * nh, S), F32)],
        compiler_params=_params("parallel"),
        name="gdn_gates",
    )(x2, g.reshape(1, D), w_pad.astype(BF16), w_pad.T.astype(BF16), prow, pcol)


def _gdn_conv_kernel(x_ref, xp_ref, xn_ref, w_ref, o_ref, xe_ref, *, nh, nt):
    t = pl.program_id(1)
    j = pl.program_id(2)
    tp = x_ref.shape[1]
    pad = GDN_CONV // 2
    xe_ref[0:SUBLANES, :] = jnp.where(t > 0, xp_ref[0], 0.0)
    xe_ref[SUBLANES:SUBLANES + tp, :] = x_ref[0]
    xe_ref[SUBLANES + tp:, :] = jnp.where(t < nt - 1, xn_ref[0], 0.0)
    acc = None
    for i in range(GDN_CONV):
        term = w_ref[i:i + 1, :] * xe_ref[pl.ds(SUBLANES - pad + i, tp), :]
        acc = term if acc is None else acc + term
    y = acc * _sigmoid(acc)
    inv = lax.rsqrt(jnp.sum(y * y, axis=-1, keepdims=True) + L2_EPS)
    fac = jnp.where(j < nh, inv * (GDN_DK ** -0.5), jnp.where(j < 2 * nh, inv, 1.0))
    o_ref[0] = y * fac


def _gdn_conv(qkv, conv_w, nh, tp=512):
    B, S, W = qkv.shape
    nt = S // tp
    r = tp // SUBLANES
    return pl.pallas_call(
        functools.partial(_gdn_conv_kernel, nh=nh, nt=nt),
        grid=(B, nt, W // LANES),
        in_specs=[pl.BlockSpec((1, tp, LANES), lambda b, t, j: (b, t, j)),
                  pl.BlockSpec((1, SUBLANES, LANES),
                               lambda b, t, j: (b, jnp.maximum(t * r - 1, 0), j)),
                  pl.BlockSpec((1, SUBLANES, LANES),
                               lambda b, t, j: (b, jnp.minimum((t + 1) * r, S // SUBLANES - 1), j)),
                  pl.BlockSpec((GDN_CONV, LANES), lambda b, t, j: (0, j))],
        out_specs=pl.BlockSpec((1, tp, LANES), lambda b, t, j: (b, t, j)),
        out_shape=jax.ShapeDtypeStruct((B, S, W), F32),
        scratch_shapes=[pltpu.VMEM((tp + 2 * SUBLANES, LANES), F32)],
        compiler_params=_params("parallel", "parallel", "parallel"),
        name="gdn_conv",
    )(qkv, qkv, qkv, conv_w)


def _gdn_scan_kernel(q_ref, k_ref, v_ref, col_ref, row_ref, o_ref, state_ref, *, rev, nh):
    C = GDN_CHUNK
    t = pl.program_id(1)

    @pl.when(t == 0)
    def _():
        state_ref[...] = jnp.zeros_like(state_ref)

    ri = lax.broadcasted_iota(jnp.int32, (C, C), 0)
    ci = lax.broadcasted_iota(jnp.int32, (C, C), 1)
    incl = (ri <= ci) if rev else (ri >= ci)
    strict = (ri < ci) if rev else (ri > ci)
    eye = (ri == ci).astype(F32)
    levels = [((ri >> (lg + 1)) == (ci >> (lg + 1))) & ((ri >> lg) != (ci >> lg))
              for lg in range(int(math.log2(C)))]
    col = col_ref[0]
    row = row_ref[0]
    last = 0 if rev else C - 1
    heads = range(nh)
    hs = [slice(h * LANES, (h + 1) * LANES) for h in heads]
    gi = [(nh if rev else 0) + h for h in heads]
    q = [q_ref[0, :, hs[h]] for h in heads]
    k = [k_ref[0, :, hs[h]] for h in heads]
    g_col = [col[:, gi[h]:gi[h] + 1] for h in heads]
    b_col = [col[:, 2 * nh + gi[h]:2 * nh + gi[h] + 1] for h in heads]
    g_last = [g_col[h][last:last + 1, :] for h in heads]
    decay = [jnp.where(incl, jnp.exp(jnp.where(incl, g_col[h] - row[gi[h]:gi[h] + 1, :], 0.0)), 0.0)
             for h in heads]
    kf = [k[h].astype(BF16) for h in heads]
    kb = [k[h] * b_col[h] for h in heads]
    kk = [_dot_nt(kb[h], kf[h]) for h in heads]
    qk = [_dot_nt(q[h], kf[h]) for h in heads]
    a = [jnp.where(strict, kk[h] * decay[h], 0.0) for h in heads]
    intra = [(qk[h] * decay[h]).astype(BF16) for h in heads]
    tinv = [eye - jnp.where(levels[0], a[h], 0.0) for h in heads]
    for m in levels[1:]:
        tb = [tinv[h].astype(BF16) for h in heads]
        p = [_dot(tb[h], jnp.where(m, a[h], 0.0)) for h in heads]
        tinv = [tinv[h] - _dot(p[h], tb[h]) for h in heads]
    e_col = [jnp.exp(g_col[h]) for h in heads]
    rhs = [jnp.concatenate([v_ref[0, :, hs[h]] * b_col[h], kb[h] * e_col[h]], axis=1) for h in heads]
    sol = [_dot(tinv[h], rhs[h]) for h in heads]
    sb = [state_ref[h].astype(BF16) for h in heads]
    v_new = [sol[h][:, :LANES] - _dot(sol[h][:, LANES:], sb[h]) for h in heads]
    o_inter = [_dot(q[h] * e_col[h], sb[h]) for h in heads]
    for h in heads:
        o_ref[0, :, hs[h]] = o_inter[h] + _dot(intra[h], v_new[h])
    for h in heads:
        k_dec = k[h] * jnp.exp(g_last[h] - g_col[h])
        state_ref[h] = state_ref[h] * jnp.exp(g_last[h]) + _dot(k_dec.T, v_new[h])


def _gdn_scan(qkvn, col, row, nh, rev):
    B, S, _ = qkvn.shape
    C = GDN_CHUNK
    nt = S // C
    width = nh * LANES
    tmap = (lambda t: nt - 1 - t) if rev else (lambda t: t)
    col3 = col.reshape(B, S, LANES)
    return pl.pallas_call(
        functools.partial(_gdn_scan_kernel, rev=rev, nh=nh),
        grid=(B, nt),
        in_specs=[pl.BlockSpec((1, C, width), lambda b, t: (b, tmap(t), 0)),
                  pl.BlockSpec((1, C, width), lambda b, t: (b, tmap(t), 1)),
                  pl.BlockSpec((1, C, width), lambda b, t: (b, tmap(t), 2)),
                  pl.BlockSpec((1, C, LANES), lambda b, t: (b, tmap(t), 0)),
                  pl.BlockSpec((1, 4 * nh, C), lambda b, t: (b, 0, tmap(t)))],
        out_specs=pl.BlockSpec((1, C, width), lambda b, t: (b, tmap(t), 0)),
        out_shape=jax.ShapeDtypeStruct((B, S, width), F32),
        scratch_shapes=[pltpu.VMEM((nh, GDN_DK, LANES), F32)],
        compiler_params=_params("parallel", "arbitrary"),
        name="gdn_scan_bwd" if rev else "gdn_scan_fwd",
    )(qkvn, qkvn, qkvn, col3, row)


def _gdn_out_kernel(of_ref, ob_ref, z_ref, x_ref, nw_ref, w_ref, o_ref, y_ref, *, nh):
    for h in range(nh):
        hs = slice(h * LANES, (h + 1) * LANES)
        o = of_ref[:, hs] + ob_ref[:, hs]
        o = o * lax.rsqrt(jnp.mean(o * o, axis=-1, keepdims=True) + RMS_EPS)
        z = z_ref[:, hs]
        y_ref[:, hs] = (o * nw_ref[...] * (z * _sigmoid(z))).astype(BF16)
    o_ref[...] = x_ref[...] + jnp.dot(y_ref[...], w_ref[...], preferred_element_type=F32)


def _gdn_out(o_f, o_b, z, x2, norm_w, w_out, nh, tm=512):
    T, D = x2.shape
    W = o_f.shape[1]
    tok = lambda n: pl.BlockSpec((tm, n), lambda i: (i, 0))
    return pl.pallas_call(
        functools.partial(_gdn_out_kernel, nh=nh),
        grid=(T // tm,),
        in_specs=[tok(W), tok(W), tok(W), tok(D),
                  pl.BlockSpec((1, LANES), lambda i: (0, 0)),
                  pl.BlockSpec((W, D), lambda i: (0, 0))],
        out_specs=tok(D),
        out_shape=jax.ShapeDtypeStruct((T, D), F32),
        scratch_shapes=[pltpu.VMEM((tm, W), BF16)],
        compiler_params=_params("parallel"),
        name="gdn_out",
    )(o_f, o_b, z, x2, norm_w.reshape(1, LANES), w_out)


def _dswa_kernel(q_ref, k_ref, kp_ref, kn_ref, v_ref, vp_ref, vn_ref, bias_ref,
                 o_ref, lse_ref, *, sub_len):
    t = pl.program_id(2)
    lt = q_ref.shape[1]
    hb = DSWA_HALF
    q = q_ref[0] * (DSWA_HEAD_DIM ** -0.5)
    kext = jnp.concatenate([kp_ref[0], k_ref[0], kn_ref[0]], axis=0).astype(BF16)
    vext = jnp.concatenate([vp_ref[0], v_ref[0], vn_ref[0]], axis=0).astype(BF16)
    first_head = lax.broadcasted_iota(jnp.int32, (1, LANES), 1) < DSWA_HEAD_DIM
    n_pairs = q_ref.shape[2] // LANES
    for qb in range(lt // hb):
        kpos = t * lt + (qb - 1) * hb + lax.broadcasted_iota(jnp.int32, (1, 3 * hb), 1)
        kvalid = (kpos >= 0) & (kpos < sub_len)
        rows = slice(qb * hb, (qb + 1) * hb)
        win = slice(qb * hb, (qb + 3) * hb)
        heads = range(2 * n_pairs)
        ls = [slice((h // 2) * LANES, (h // 2 + 1) * LANES) for h in heads]
        msk = [first_head if h % 2 == 0 else jnp.logical_not(first_head) for h in heads]
        s = [_dot_nt(jnp.where(msk[h], q[rows, ls[h]], 0.0), kext[win, ls[h]]) for h in heads]
        s = [jnp.where(kvalid, s[h] + bias_ref[h], NEG_INF) for h in heads]
        m = [jnp.max(s[h], axis=-1, keepdims=True) for h in heads]
        p = [jnp.exp(s[h] - m[h]) for h in heads]
        l = [jnp.sum(p[h], axis=-1, keepdims=True) for h in heads]
        pv = [_dot(p[h], vext[win, ls[h]]) for h in heads]
        outs = [pv[h] * (1.0 / l[h]) for h in heads]
        lses = [m[h] + jnp.log(l[h]) for h in heads]
        for pr in range(n_pairs):
            o_ref[0, rows, ls[2 * pr]] = jnp.where(first_head, outs[2 * pr], outs[2 * pr + 1])
            lse_ref[0, rows, ls[2 * pr]] = jnp.where(first_head, lses[2 * pr], lses[2 * pr + 1])


def _rel_bucket(rel):
    nb = REL_BUCKETS // 2
    max_exact = nb // 2
    ret = jnp.where(rel > 0, nb, 0)
    n = jnp.abs(rel)
    nf = jnp.maximum(n, 1).astype(F32)
    large = max_exact + (jnp.log(nf / max_exact) / math.log(REL_MAX_DIST / max_exact)
                         * (nb - max_exact)).astype(jnp.int32)
    large = jnp.minimum(large, nb - 1)
    return ret + jnp.where(n < max_exact, n, large)


def _band_bias(rel_table, heads, dil):
    hb = DSWA_HALF
    off = jnp.arange(3 * hb)[None, :] - hb - jnp.arange(hb)[:, None]
    bias = jnp.take(rel_table, _rel_bucket(off * dil), axis=0)[..., heads]
    bias = jnp.transpose(bias, (2, 0, 1)).astype(F32)
    return jnp.where((jnp.abs(off) <= hb)[None], bias, NEG_INF)


def _dswa_group(qkv, bias, gi, dil, n_groups, lt=256):
    B, S, W3 = qkv.shape
    hb = DSWA_HALF
    gw = W3 // (3 * n_groups)
    L = S // dil
    lt = min(lt, L)
    nl = L // lt
    nblk = W3 // gw
    r = lt // hb
    view = qkv.reshape(B, L, dil * W3)
    main = lambda which: pl.BlockSpec(
        (1, lt, gw), lambda b, d, t: (b, t, d * nblk + which * n_groups + gi))
    prev = lambda which: pl.BlockSpec(
        (1, hb, gw), lambda b, d, t: (b, jnp.maximum(t * r - 1, 0), d * nblk + which * n_groups + gi))
    nxt = lambda which: pl.BlockSpec(
        (1, hb, gw), lambda b, d, t: (b, jnp.minimum((t + 1) * r, L // hb - 1),
                                      d * nblk + which * n_groups + gi))
    o, lse = pl.pallas_call(
        functools.partial(_dswa_kernel, sub_len=L),
        grid=(B, dil, nl),
        in_specs=[main(0), main(1), prev(1), nxt(1), main(2), prev(2), nxt(2),
                  pl.BlockSpec(bias.shape, lambda b, d, t: (0, 0, 0))],
        out_specs=[pl.BlockSpec((1, lt, gw), lambda b, d, t: (b, t, d)),
                   pl.BlockSpec((1, lt, gw), lambda b, d, t: (b, t, d))],
        out_shape=[jax.ShapeDtypeStruct((B, L, dil * gw), F32),
                   jax.ShapeDtypeStruct((B, L, dil * gw), F32)],
        compiler_params=_params("parallel", "parallel", "parallel"),
        name=f"dswa_d{dil}",
    )(view, view, view, view, view, view, view, bias)
    return o.reshape(B * S, gw), lse.reshape(B * S, gw)


def _dswa_out_kernel(*refs, ng):
    o_refs, l_refs = refs[:ng], refs[ng:2 * ng]
    x_ref, w_ref, out_ref = refs[2 * ng:]
    lses = [r[...] for r in l_refs]
    m = functools.reduce(jnp.maximum, lses)
    es = [jnp.exp(l - m) for l in lses]
    inv = 1.0 / functools.reduce(lambda a, b: a + b, es)
    acc = x_ref[...]
    gw = o_refs[0].shape[1]
    for g in range(ng):
        acc = acc + _dot(o_refs[g][...] * (es[g] * inv), w_ref[g * gw:(g + 1) * gw, :])
    out_ref[...] = acc


def _dswa_out(os_, lses, x2, w_out, tm=512):
    T, D = x2.shape
    ng = len(os_)
    gw = os_[0].shape[1]
    tok = lambda n: pl.BlockSpec((tm, n), lambda i: (i, 0))
    return pl.pallas_call(
        functools.partial(_dswa_out_kernel, ng=ng),
        grid=(T // tm,),
        in_specs=[tok(gw)] * (2 * ng) + [tok(D), pl.BlockSpec(w_out.shape, lambda i: (0, 0))],
        out_specs=tok(D),
        out_shape=jax.ShapeDtypeStruct((T, D), F32),
        compiler_params=_params("parallel"),
        name="dswa_out",
    )(*os_, *lses, x2, w_out)


def _mlp_kernel(x_ref, g_ref, w1_ref, w2_ref, gf_ref, o_ref, acc_ref, *, chunk, final):
    x = x_ref[...]
    h = _rms(x, g_ref[...]).astype(BF16)
    dff = w1_ref.shape[1]
    for c in range(0, dff, chunk):
        a = jnp.maximum(jnp.dot(h, w1_ref[:, c:c + chunk], preferred_element_type=F32), 0.0)
        part = jnp.dot((a * a).astype(BF16), w2_ref[c:c + chunk, :], preferred_element_type=F32)
        if c == 0:
            acc_ref[...] = part
        else:
            acc_ref[...] += part
    y = x + acc_ref[...]
    if final:
        y = _rms(y, gf_ref[...])
    o_ref[...] = y


def _mlp(x2, g, w1, w2, g_final, final, tm=512, chunk=512):
    T, D = x2.shape
    dff = w1.shape[1]
    return pl.pallas_call(
        functools.partial(_mlp_kernel, chunk=chunk, final=final),
        grid=(T // tm,),
        in_specs=[pl.BlockSpec((tm, D), lambda i: (i, 0)),
                  pl.BlockSpec((1, D), lambda i: (0, 0)),
                  pl.BlockSpec((D, dff), lambda i: (0, 0), pipeline_mode=pl.Buffered(1)),
                  pl.BlockSpec((dff, D), lambda i: (0, 0), pipeline_mode=pl.Buffered(1)),
                  pl.BlockSpec((1, D), lambda i: (0, 0))],
        out_specs=pl.BlockSpec((tm, D), lambda i: (i, 0)),
        out_shape=jax.ShapeDtypeStruct((T, D), F32),
        scratch_shapes=[pltpu.VMEM((tm, D), F32)],
        compiler_params=_params("parallel"),
        name="mlp",
    )(x2, g.reshape(1, D), w1, w2, g_final.reshape(1, D))


def _gdn_layer(x2, g, w_in, conv_w, a_log, dt_bias, norm_w, w_out, B, S):
    nh = a_log.shape[1]
    qkv_w = 3 * nh * LANES
    z_w = nh * LANES
    qkv, z = _norm_proj(x2, g, [w_in[:, :qkv_w].astype(BF16),
                                w_in[:, qkv_w:qkv_w + z_w].astype(BF16)])
    col, row = _gdn_gates(x2, g, w_in[:, qkv_w + z_w:], a_log, dt_bias, B, S)
    qkvn = _gdn_conv(qkv.reshape(B, S, qkv_w), conv_w.reshape(GDN_CONV, qkv_w), nh)
    o_f = _gdn_scan(qkvn, col, row, nh, rev=False)
    o_b = _gdn_scan(qkvn, col, row, nh, rev=True)
    return _gdn_out(o_f.reshape(B * S, z_w), o_b.reshape(B * S, z_w), z, x2, norm_w,
                    w_out.astype(BF16), nh)


def _dswa_layer(x2, g, w_in, w_out, rel_bias, B, S):
    ng = len(DSWA_CONFIGS)
    (qkv,) = _norm_proj(x2, g, [w_in.astype(BF16)])
    qkv = qkv.reshape(B, S, w_in.shape[1])
    hg = rel_bias.shape[1] // ng
    os_, lses = [], []
    for gi, (window, dil) in enumerate(DSWA_CONFIGS):
        assert window // (2 * dil) == DSWA_HALF
        bias = _band_bias(rel_bias, slice(gi * hg, (gi + 1) * hg), dil)
        o, lse = _dswa_group(qkv, bias, gi, dil, ng)
        os_.append(o)
        lses.append(lse)
    return _dswa_out(os_, lses, x2, w_out.astype(BF16))


def kernel(x, norm_mix, norm_mlp, norm_final, rel_bias, gdn_w_in, gdn_conv_w, gdn_a_log,
           gdn_dt_bias, gdn_norm_w, gdn_w_out, dswa_w_in, dswa_w_out, mlp_w1, mlp_w2):
    B, S, D = x.shape
    depth = norm_mix.shape[0]
    x2 = x.reshape(B * S, D)
    for i in range(depth):
        j = i // 2
        if i % 2 == 0:
            x2 = _gdn_layer(x2, norm_mix[i], gdn_w_in[j], gdn_conv_w[j], gdn_a_log[j],
                            gdn_dt_bias[j], gdn_norm_w[j], gdn_w_out[j], B, S)
        else:
            x2 = _dswa_layer(x2, norm_mix[i], dswa_w_in[j], dswa_w_out[j], rel_bias, B, S)
        x2 = _mlp(x2, norm_mlp[i], mlp_w1[i].astype(BF16), mlp_w2[i].astype(BF16),
                  norm_final, final=(i == depth - 1))
    return x2.reshape(B, S, D)
```

```python
import functools
import math

import jax
import jax.numpy as jnp
from jax import lax
from jax.experimental import pallas as pl
from jax.experimental.pallas import tpu as pltpu

F32 = jnp.float32
BF16 = jnp.bfloat16

RMS_EPS = 1e-6
L2_EPS = 1e-6
NEG_INF = -1e30

LANES = 128
SUBLANES = 8
VMEM_LIMIT = 56 * 1024 * 1024

GDN_DK = 128
GDN_CONV = 5
GDN_CHUNK = LANES
DSWA_HEAD_DIM = 64
DSWA_CONFIGS = ((128, 1), (512, 4), (2048, 16))
DSWA_HALF = 64
REL_BUCKETS = 32
REL_MAX_DIST = 1024


def _params(*sem):
    return pltpu.CompilerParams(dimension_semantics=sem, vmem_limit_bytes=VMEM_LIMIT)


def _dot(a, b):
    return jnp.dot(a.astype(BF16), b.astype(BF16), preferred_element_type=F32)


def _dot_nt(a, b):
    return lax.dot_general(a.astype(BF16), b.astype(BF16), (((1,), (1,)), ((), ())),
                           preferred_element_type=F32)


def _dot_exact(a, b):
    return jnp.dot(a, b, precision=lax.Precision.HIGHEST, preferred_element_type=F32)


def _rms(x, g):
    return x * lax.rsqrt(jnp.mean(x * x, axis=-1, keepdims=True) + RMS_EPS) * g


def _sigmoid(x):
    return 1.0 / (1.0 + jnp.exp(-x))


def _softplus(x):
    return jnp.maximum(x, 0.0) + jnp.log1p(jnp.exp(-jnp.abs(x)))


def _norm_proj_kernel(x_ref, g_ref, *refs, n_out, chunk):
    w_refs, o_refs = refs[:n_out], refs[n_out:]
    h = _rms(x_ref[...], g_ref[...]).astype(BF16)
    for w_ref, o_ref in zip(w_refs, o_refs):
        n = w_ref.shape[1]
        for n0 in range(0, n, chunk):
            n1 = min(n0 + chunk, n)
            o_ref[:, n0:n1] = jnp.dot(h, w_ref[:, n0:n1], preferred_element_type=F32)


def _norm_proj(x2, g, ws, tm=256, chunk=512):
    T, D = x2.shape
    n_out = len(ws)
    return pl.pallas_call(
        functools.partial(_norm_proj_kernel, n_out=n_out, chunk=chunk),
        grid=(T // tm,),
        in_specs=[pl.BlockSpec((tm, D), lambda i: (i, 0)),
                  pl.BlockSpec((1, D), lambda i: (0, 0))]
                 + [pl.BlockSpec(w.shape, lambda i: (0, 0)) for w in ws],
        out_specs=[pl.BlockSpec((tm, w.shape[1]), lambda i: (i, 0)) for w in ws],
        out_shape=[jax.ShapeDtypeStruct((T, w.shape[1]), F32) for w in ws],
        compiler_params=_params("parallel"),
        name="norm_proj",
    )(x2, g.reshape(1, D), *ws)


def _gdn_gate_kernel(x_ref, g_ref, w_ref, wt_ref, prow_ref, pcol_ref, col_ref, row_ref, *, nh):
    C = GDN_CHUNK
    h = _rms(x_ref[...], g_ref[...]).astype(BF16)
    a_c = jnp.dot(h, w_ref[...], preferred_element_type=F32)
    a_r = lax.dot_general(wt_ref[...], h, (((1,), (1,)), ((), ())),
                          preferred_element_type=F32)
    ri = lax.broadcasted_iota(jnp.int32, (C, C), 0)
    ci = lax.broadcasted_iota(jnp.int32, (C, C), 1)
    lower = (ri >= ci).astype(F32)
    upper = (ri <= ci).astype(F32)
    g_c = -jnp.exp(prow_ref[0:1, :]) * _softplus(a_c + prow_ref[1:2, :])
    g_r = -jnp.exp(pcol_ref[:, 0:1]) * _softplus(a_r + pcol_ref[:, 1:2])
    beta_c = _sigmoid(a_c)
    beta_r = _sigmoid(a_r)
    for c in range(x_ref.shape[0] // C):
        sl = slice(c * C, (c + 1) * C)
        gc = g_c[sl, :]
        cum = jnp.where(ci < nh, _dot_exact(lower, gc), _dot_exact(upper, gc))
        col_ref[sl, :] = jnp.where(ci < 2 * nh, cum, beta_c[sl, :])
        gr = g_r[:, sl]
        cum_r = jnp.where(ri < nh, _dot_exact(gr, upper), _dot_exact(gr, lower))
        full = jnp.where(ri < 2 * nh, cum_r, beta_r[:, sl])
        row_ref[0, :, sl] = full[:4 * nh, :]


def _gdn_gates(x2, g, w_ab, a_log, dt_bias, B, S, tm=256):
    T, D = x2.shape
    nh = a_log.shape[1]
    w_pad = jnp.zeros((D, LANES), F32).at[:, :4 * nh].set(w_ab)
    alog = jnp.zeros((LANES,), F32).at[:2 * nh].set(a_log.reshape(-1))
    dtb = jnp.zeros((LANES,), F32).at[:2 * nh].set(dt_bias.reshape(-1))
    prow = jnp.zeros((SUBLANES, LANES), F32).at[0].set(alog).at[1].set(dtb)
    pcol = jnp.zeros((LANES, LANES), F32).at[:, 0].set(alog).at[:, 1].set(dtb)
    nt = S // tm
    return pl.pallas_call(
        functools.partial(_gdn_gate_kernel, nh=nh),
        grid=(T // tm,),
        in_specs=[pl.BlockSpec((tm, D), lambda i: (i, 0)),
                  pl.BlockSpec((1, D), lambda i: (0, 0)),
                  pl.BlockSpec((D, LANES), lambda i: (0, 0)),
                  pl.BlockSpec((LANES, D), lambda i: (0, 0)),
                  pl.BlockSpec((SUBLANES, LANES), lambda i: (0, 0)),
                  pl.BlockSpec((LANES, LANES), lambda i: (0, 0))],
        out_specs=[pl.BlockSpec((tm, LANES), lambda i: (i, 0)),
                   pl.BlockSpec((1, 4 * nh, tm), lambda i: (i // nt, 0, i % nt))],
        out_shape=[jax.ShapeDtypeStruct((T, LANES), F32),
                   jax.ShapeDtypeStruct((B, 4 * nh, S), F32)],
        compiler_params=_params("parallel"),
        name="gdn_gates",
    )(x2, g.reshape(1, D), w_pad.astype(BF16), w_pad.T.astype(BF16), prow, pcol)


def _gdn_conv_kernel(x_ref, xp_ref, xn_ref, w_ref, o_ref, xe_ref, *, nh, nt):
    t = pl.program_id(1)
    tp = x_ref.shape[1]
    pad = GDN_CONV // 2
    xe_ref[0:SUBLANES, :] = jnp.where(t > 0, xp_ref[0], 0.0)
    xe_ref[SUBLANES:SUBLANES + tp, :] = x_ref[0]
    xe_ref[SUBLANES + tp:, :] = jnp.where(t < nt - 1, xn_ref[0], 0.0)
    for j in range(x_ref.shape[2] // LANES):
        ls = slice(j * LANES, (j + 1) * LANES)
        acc = None
        for i in range(GDN_CONV):
            term = w_ref[i:i + 1, ls] * xe_ref[pl.ds(SUBLANES - pad + i, tp), ls]
            acc = term if acc is None else acc + term
        y = acc * _sigmoid(acc)
        if j < 2 * nh:
            inv = lax.rsqrt(jnp.sum(y * y, axis=-1, keepdims=True) + L2_EPS)
            y = y * (inv * (GDN_DK ** -0.5) if j < nh else inv)
        o_ref[0, :, ls] = y.astype(o_ref.dtype)


def _gdn_conv(qkv, conv_w, nh, tp=256):
    B, S, W = qkv.shape
    nt = S // tp
    r = tp // SUBLANES
    return pl.pallas_call(
        functools.partial(_gdn_conv_kernel, nh=nh, nt=nt),
        grid=(B, nt),
        in_specs=[pl.BlockSpec((1, tp, W), lambda b, t: (b, t, 0)),
                  pl.BlockSpec((1, SUBLANES, W), lambda b, t: (b, jnp.maximum(t * r - 1, 0), 0)),
                  pl.BlockSpec((1, SUBLANES, W),
                               lambda b, t: (b, jnp.minimum((t + 1) * r, S // SUBLANES - 1), 0)),
                  pl.BlockSpec((GDN_CONV, W), lambda b, t: (0, 0))],
        out_specs=pl.BlockSpec((1, tp, W), lambda b, t: (b, t, 0)),
        out_shape=jax.ShapeDtypeStruct((B, S, W), BF16),
        scratch_shapes=[pltpu.VMEM((tp + 2 * SUBLANES, W), F32)],
        compiler_params=_params("parallel", "parallel"),
        name="gdn_conv",
    )(qkv, qkv, qkv, conv_w)


def _gdn_scan_kernel(q_ref, k_ref, v_ref, col_ref, row_ref, o_ref, state_ref, *, rev, nh):
    C = GDN_CHUNK
    t = pl.program_id(1)

    @pl.when(t == 0)
    def _():
        state_ref[...] = jnp.zeros_like(state_ref)

    ri = lax.broadcasted_iota(jnp.int32, (C, C), 0)
    ci = lax.broadcasted_iota(jnp.int32, (C, C), 1)
    incl = (ri <= ci) if rev else (ri >= ci)
    strict = (ri < ci) if rev else (ri > ci)
    eye = (ri == ci).astype(F32)
    levels = [((ri >> (lg + 1)) == (ci >> (lg + 1))) & ((ri >> lg) != (ci >> lg))
              for lg in range(int(math.log2(C)))]
    last = 0 if rev else C - 1
    nc = q_ref.shape[1] // C
    chunks = list(reversed(range(nc))) if rev else list(range(nc))
    heads = range(nh)
    hs = [slice(h * LANES, (h + 1) * LANES) for h in heads]
    gi = [(nh if rev else 0) + h for h in heads]
    units = [(c, h) for c in chunks for h in heads]
    rows = {c: slice(c * C, (c + 1) * C) for c in chunks}
    col = {c: col_ref[0, rows[c], :] for c in chunks}
    row = {c: row_ref[0, :, rows[c]] for c in chunks}
    kf = {u: k_ref[0, rows[u[0]], hs[u[1]]] for u in units}
    k = {u: kf[u].astype(F32) for u in units}
    g_col = {(c, h): col[c][:, gi[h]:gi[h] + 1] for c, h in units}
    b_col = {(c, h): col[c][:, 2 * nh + gi[h]:2 * nh + gi[h] + 1] for c, h in units}
    g_last = {u: g_col[u][last:last + 1, :] for u in units}
    decay = {(c, h): jnp.where(incl, jnp.exp(jnp.where(incl, g_col[c, h] - row[c][gi[h]:gi[h] + 1, :], 0.0)), 0.0)
             for c, h in units}
    kb = {u: k[u] * b_col[u] for u in units}
    res = {(c, h): _dot_nt(jnp.concatenate([kb[c, h].astype(BF16), q_ref[0, rows[c], hs[h]]], axis=0), kf[c, h])
           for c, h in units}
    a = {u: jnp.where(strict, res[u][:C] * decay[u], 0.0) for u in units}
    intra = {u: (res[u][C:] * decay[u]).astype(BF16) for u in units}
    tinv = {u: eye - jnp.where(levels[0], a[u], 0.0) for u in units}
    for m in levels[1:]:
        tb = {u: tinv[u].astype(BF16) for u in units}
        pm = {u: _dot(tb[u], jnp.where(m, a[u], 0.0)) for u in units}
        tinv = {u: tinv[u] - _dot(pm[u], tb[u]) for u in units}
    e_col = {u: jnp.exp(g_col[u]) for u in units}
    sol = {(c, h): _dot(tinv[c, h], jnp.concatenate(
        [v_ref[0, rows[c], hs[h]].astype(F32) * b_col[c, h], kb[c, h] * e_col[c, h]], axis=1)) for c, h in units}
    qd = {(c, h): (q_ref[0, rows[c], hs[h]].astype(F32) * e_col[c, h]).astype(BF16) for c, h in units}
    k_dec = {u: (k[u] * jnp.exp(g_last[u] - g_col[u])).T.astype(BF16) for u in units}
    state = [state_ref[h] for h in heads]
    for c in chunks:
        sb = [state[h].astype(BF16) for h in heads]
        r2 = [_dot(jnp.concatenate([sol[c, h][:, LANES:].astype(BF16), qd[c, h]], axis=0), sb[h]) for h in heads]
        v_new = [(sol[c, h][:, :LANES] - r2[h][:C]).astype(BF16) for h in heads]
        for h in heads:
            o_ref[0, rows[c], hs[h]] = r2[h][C:] + _dot(intra[c, h], v_new[h])
        state = [state[h] * jnp.exp(g_last[c, h]) + _dot(k_dec[c, h], v_new[h]) for h in heads]
    for h in heads:
        state_ref[h] = state[h]


def _gdn_scan(qkvn, col, row, nh, rev, nc=2):
    B, S, _ = qkvn.shape
    ts = nc * GDN_CHUNK
    nt = S // ts
    width = nh * LANES
    tmap = (lambda t: nt - 1 - t) if rev else (lambda t: t)
    col3 = col.reshape(B, S, LANES)
    return pl.pallas_call(
        functools.partial(_gdn_scan_kernel, rev=rev, nh=nh),
        grid=(B, nt),
        in_specs=[pl.BlockSpec((1, ts, width), lambda b, t: (b, tmap(t), 0)),
                  pl.BlockSpec((1, ts, width), lambda b, t: (b, tmap(t), 1)),
                  pl.BlockSpec((1, ts, width), lambda b, t: (b, tmap(t), 2)),
                  pl.BlockSpec((1, ts, LANES), lambda b, t: (b, tmap(t), 0)),
                  pl.BlockSpec((1, 4 * nh, ts), lambda b, t: (b, 0, tmap(t)))],
        out_specs=pl.BlockSpec((1, ts, width), lambda b, t: (b, tmap(t), 0)),
        out_shape=jax.ShapeDtypeStruct((B, S, width), F32),
        scratch_shapes=[pltpu.VMEM((nh, GDN_DK, LANES), F32)],
        compiler_params=_params("parallel", "arbitrary"),
        name="gdn_scan_bwd" if rev else "gdn_scan_fwd",
    )(qkvn, qkvn, qkvn, col3, row)


def _gdn_out_kernel(of_ref, ob_ref, z_ref, x_ref, nw_ref, w_ref, o_ref, y_ref, *, nh):
    for h in range(nh):
        hs = slice(h * LANES, (h + 1) * LANES)
        o = of_ref[:, hs] + ob_ref[:, hs]
        o = o * lax.rsqrt(jnp.mean(o * o, axis=-1, keepdims=True) + RMS_EPS)
        z = z_ref[:, hs]
        y_ref[:, hs] = (o * nw_ref[...] * (z * _sigmoid(z))).astype(BF16)
    o_ref[...] = x_ref[...] + jnp.dot(y_ref[...], w_ref[...], preferred_element_type=F32)


def _gdn_out(o_f, o_b, z, x2, norm_w, w_out, nh, tm=512):
    T, D = x2.shape
    W = o_f.shape[1]
    tok = lambda n: pl.BlockSpec((tm, n), lambda i: (i, 0))
    return pl.pallas_call(
        functools.partial(_gdn_out_kernel, nh=nh),
        grid=(T // tm,),
        in_specs=[tok(W), tok(W), tok(W), tok(D),
                  pl.BlockSpec((1, LANES), lambda i: (0, 0)),
                  pl.BlockSpec((W, D), lambda i: (0, 0))],
        out_specs=tok(D),
        out_shape=jax.ShapeDtypeStruct((T, D), F32),
        scratch_shapes=[pltpu.VMEM((tm, W), BF16)],
        compiler_params=_params("parallel"),
        name="gdn_out",
    )(o_f, o_b, z, x2, norm_w.reshape(1, LANES), w_out)


def _dswa_kernel(q_ref, k_ref, kp_ref, kn_ref, v_ref, vp_ref, vn_ref, bias_ref,
                 o_ref, lse_ref, *, sub_len):
    t = pl.program_id(2)
    lt = q_ref.shape[1]
    hb = DSWA_HALF
    q = q_ref[0] * (DSWA_HEAD_DIM ** -0.5)
    kext = jnp.concatenate([kp_ref[0], k_ref[0], kn_ref[0]], axis=0).astype(BF16)
    vext = jnp.concatenate([vp_ref[0], v_ref[0], vn_ref[0]], axis=0).astype(BF16)
    first_head = lax.broadcasted_iota(jnp.int32, (1, LANES), 1) < DSWA_HEAD_DIM
    n_pairs = q_ref.shape[2] // LANES
    blocks = range(lt // hb)
    units = [(qb, pr) for qb in blocks for pr in range(n_pairs)]
    rows = [slice(qb * hb, (qb + 1) * hb) for qb in blocks]
    win = [slice(qb * hb, (qb + 3) * hb) for qb in blocks]
    ls = [slice(pr * LANES, (pr + 1) * LANES) for pr in range(n_pairs)]
    kvalid = []
    for qb in blocks:
        kpos = t * lt + (qb - 1) * hb + lax.broadcasted_iota(jnp.int32, (1, 3 * hb), 1)
        kvalid.append((kpos >= 0) & (kpos < sub_len))
    qs = {(qb, pr): jnp.concatenate([jnp.where(first_head, q[rows[qb], ls[pr]], 0.0),
                                     jnp.where(first_head, 0.0, q[rows[qb], ls[pr]])], axis=0)
          for qb, pr in units}
    s = {(qb, pr): _dot_nt(qs[qb, pr], kext[win[qb], ls[pr]]) for qb, pr in units}
    s = {(qb, pr): jnp.where(kvalid[qb], s[qb, pr] + bias_ref[pr], NEG_INF) for qb, pr in units}
    m = {u: jnp.max(s[u], axis=-1, keepdims=True) for u in units}
    p = {u: jnp.exp(s[u] - m[u]) for u in units}
    l = {u: jnp.sum(p[u], axis=-1, keepdims=True) for u in units}
    pv = {(qb, pr): _dot(p[qb, pr], vext[win[qb], ls[pr]]) * (1.0 / l[qb, pr]) for qb, pr in units}
    lse = {u: m[u] + jnp.log(l[u]) for u in units}
    for qb, pr in units:
        o_ref[0, rows[qb], ls[pr]] = jnp.where(first_head, pv[qb, pr][:hb], pv[qb, pr][hb:])
        lse_ref[0, rows[qb], ls[pr]] = jnp.where(first_head, lse[qb, pr][:hb], lse[qb, pr][hb:])


def _rel_bucket(rel):
    nb = REL_BUCKETS // 2
    max_exact = nb // 2
    ret = jnp.where(rel > 0, nb, 0)
    n = jnp.abs(rel)
    nf = jnp.maximum(n, 1).astype(F32)
    large = max_exact + (jnp.log(nf / max_exact) / math.log(REL_MAX_DIST / max_exact)
                         * (nb - max_exact)).astype(jnp.int32)
    large = jnp.minimum(large, nb - 1)
    return ret + jnp.where(n < max_exact, n, large)


def _band_bias(rel_table, heads, dil):
    hb = DSWA_HALF
    off = jnp.arange(3 * hb)[None, :] - hb - jnp.arange(hb)[:, None]
    bias = jnp.take(rel_table, _rel_bucket(off * dil), axis=0)[..., heads]
    bias = jnp.transpose(bias, (2, 0, 1)).astype(F32)
    bias = jnp.where((jnp.abs(off) <= hb)[None], bias, NEG_INF)
    return bias.reshape(bias.shape[0] // 2, 2 * hb, 3 * hb)


def _dswa_group(qkv, bias, gi, dil, n_groups, lt=256):
    B, S, W3 = qkv.shape
    hb = DSWA_HALF
    gw = W3 // (3 * n_groups)
    L = S // dil
    lt = min(lt, L)
    nl = L // lt
    nblk = W3 // gw
    r = lt // hb
    view = qkv.reshape(B, L, dil * W3)
    main = lambda which: pl.BlockSpec(
        (1, lt, gw), lambda b, d, t: (b, t, d * nblk + which * n_groups + gi))
    prev = lambda which: pl.BlockSpec(
        (1, hb, gw), lambda b, d, t: (b, jnp.maximum(t * r - 1, 0), d * nblk + which * n_groups + gi))
    nxt = lambda which: pl.BlockSpec(
        (1, hb, gw), lambda b, d, t: (b, jnp.minimum((t + 1) * r, L // hb - 1),
                                      d * nblk + which * n_groups + gi))
    o, lse = pl.pallas_call(
        functools.partial(_dswa_kernel, sub_len=L),
        grid=(B, dil, nl),
        in_specs=[main(0), main(1), prev(1), nxt(1), main(2), prev(2), nxt(2),
                  pl.BlockSpec(bias.shape, lambda b, d, t: (0, 0, 0))],
        out_specs=[pl.BlockSpec((1, lt, gw), lambda b, d, t: (b, t, d)),
                   pl.BlockSpec((1, lt, gw), lambda b, d, t: (b, t, d))],
        out_shape=[jax.ShapeDtypeStruct((B, L, dil * gw), F32),
                   jax.ShapeDtypeStruct((B, L, dil * gw), F32)],
        compiler_params=_params("parallel", "parallel", "parallel"),
        name=f"dswa_d{dil}",
    )(view, view, view, view, view, view, view, bias)
    return o.reshape(B * S, gw), lse.reshape(B * S, gw)


def _dswa_out_kernel(*refs, ng):
    o_refs, l_refs = refs[:ng], refs[ng:2 * ng]
    x_ref, w_ref, out_ref = refs[2 * ng:]
    lses = [r[...] for r in l_refs]
    m = functools.reduce(jnp.maximum, lses)
    es = [jnp.exp(l - m) for l in lses]
    inv = 1.0 / functools.reduce(lambda a, b: a + b, es)
    acc = x_ref[...]
    gw = o_refs[0].shape[1]
    for g in range(ng):
        acc = acc + _dot(o_refs[g][...] * (es[g] * inv), w_ref[g * gw:(g + 1) * gw, :])
    out_ref[...] = acc


def _dswa_out(os_, lses, x2, w_out, tm=512):
    T, D = x2.shape
    ng = len(os_)
    gw = os_[0].shape[1]
    tok = lambda n: pl.BlockSpec((tm, n), lambda i: (i, 0))
    return pl.pallas_call(
        functools.partial(_dswa_out_kernel, ng=ng),
        grid=(T // tm,),
        in_specs=[tok(gw)] * (2 * ng) + [tok(D), pl.BlockSpec(w_out.shape, lambda i: (0, 0))],
        out_specs=tok(D),
        out_shape=jax.ShapeDtypeStruct((T, D), F32),
        compiler_params=_params("parallel"),
        name="dswa_out",
    )(*os_, *lses, x2, w_out)


def _mlp_kernel(x_ref, g_ref, w1_ref, w2_ref, gf_ref, o_ref, acc_ref, *, chunk, final):
    x = x_ref[...]
    h = _rms(x, g_ref[...]).astype(BF16)
    dff = w1_ref.shape[1]
    for c in range(0, dff, chunk):
        a = jnp.maximum(jnp.dot(h, w1_ref[:, c:c + chunk], preferred_element_type=F32), 0.0)
        part = jnp.dot((a * a).astype(BF16), w2_ref[c:c + chunk, :], preferred_element_type=F32)
        if c == 0:
            acc_ref[...] = part
        else:
            acc_ref[...] += part
    y = x + acc_ref[...]
    if final:
        y = _rms(y, gf_ref[...])
    o_ref[...] = y


def _mlp(x2, g, w1, w2, g_final, final, tm=512, chunk=512):
    T, D = x2.shape
    dff = w1.shape[1]
    return pl.pallas_call(
        functools.partial(_mlp_kernel, chunk=chunk, final=final),
        grid=(T // tm,),
        in_specs=[pl.BlockSpec((tm, D), lambda i: (i, 0)),
                  pl.BlockSpec((1, D), lambda i: (0, 0)),
                  pl.BlockSpec((D, dff), lambda i: (0, 0), pipeline_mode=pl.Buffered(1)),
                  pl.BlockSpec((dff, D), lambda i: (0, 0), pipeline_mode=pl.Buffered(1)),
                  pl.BlockSpec((1, D), lambda i: (0, 0))],
        out_specs=pl.BlockSpec((tm, D), lambda i: (i, 0)),
        out_shape=jax.ShapeDtypeStruct((T, D), F32),
        scratch_shapes=[pltpu.VMEM((tm, D), F32)],
        compiler_params=_params("parallel"),
        name="mlp",
    )(x2, g.reshape(1, D), w1, w2, g_final.reshape(1, D))


def _gdn_layer(x2, g, w_in, conv_w, a_log, dt_bias, norm_w, w_out, B, S):
    nh = a_log.shape[1]
    qkv_w = 3 * nh * LANES
    z_w = nh * LANES
    qkv, z = _norm_proj(x2, g, [w_in[:, :qkv_w].astype(BF16),
                                w_in[:, qkv_w:qkv_w + z_w].astype(BF16)])
    col, row = _gdn_gates(x2, g, w_in[:, qkv_w + z_w:], a_log, dt_bias, B, S)
    qkvn = _gdn_conv(qkv.reshape(B, S, qkv_w), conv_w.reshape(GDN_CONV, qkv_w), nh)
    o_f = _gdn_scan(qkvn, col, row, nh, rev=False)
    o_b = _gdn_scan(qkvn, col, row, nh, rev=True)
    return _gdn_out(o_f.reshape(B * S, z_w), o_b.reshape(B * S, z_w), z, x2, norm_w,
                    w_out.astype(BF16), nh)


def _dswa_layer(x2, g, w_in, w_out, rel_bias, B, S):
    ng = len(DSWA_CONFIGS)
    (qkv,) = _norm_proj(x2, g, [w_in.astype(BF16)])
    qkv = qkv.reshape(B, S, w_in.shape[1])
    hg = rel_bias.shape[1] // ng
    os_, lses = [], []
    for gi, (window, dil) in enumerate(DSWA_CONFIGS):
        assert window // (2 * dil) == DSWA_HALF
        bias = _band_bias(rel_bias, slice(gi * hg, (gi + 1) * hg), dil)
        o, lse = _dswa_group(qkv, bias, gi, dil, ng)
        os_.append(o)
        lses.append(lse)
    return _dswa_out(os_, lses, x2, w_out.astype(BF16))


def kernel(x, norm_mix, norm_mlp, norm_final, rel_bias, gdn_w_in, gdn_conv_w, gdn_a_log,
           gdn_dt_bias, gdn_norm_w, gdn_w_out, dswa_w_in, dswa_w_out, mlp_w1, mlp_w2):
    B, S, D = x.shape
    depth = norm_mix.shape[0]
    x2 = x.reshape(B * S, D)
    for i in range(depth):
        j = i // 2
        if i % 2 == 0:
            x2 = _gdn_layer(x2, norm_mix[i], gdn_w_in[j], gdn_conv_w[j], gdn_a_log[j],
                            gdn_dt_bias[j], gdn_norm_w[j], gdn_w_out[j], B, S)
        else:
            x2 = _dswa_layer(x2, norm_mix[i], dswa_w_in[j], dswa_w_out[j], rel_bias, B, S)
        x2 = _mlp(x2, norm_mlp[i], mlp_w1[i].astype(BF16), mlp_w2[i].astype(BF16),
                  norm_final, final=(i == depth - 1))
    return x2.reshape(B, S, D)
```

```python
import functools
import math

import jax
import jax.numpy as jnp
from jax import lax
from jax.experimental import pallas as pl
from jax.experimental.pallas import tpu as pltpu

F32 = jnp.float32
BF16 = jnp.bfloat16

RMS_EPS = 1e-6
L2_EPS = 1e-6
NEG_INF = -1e30

LANES = 128
SUBLANES = 8
VMEM_LIMIT = 56 * 1024 * 1024

GDN_DK = 128
GDN_CONV = 5
GDN_CHUNK = LANES
DSWA_HEAD_DIM = 64
DSWA_CONFIGS = ((128, 1), (512, 4), (2048, 16))
DSWA_HALF = 64
REL_BUCKETS = 32
REL_MAX_DIST = 1024


def _params(*sem):
    return pltpu.CompilerParams(dimension_semantics=sem, vmem_limit_bytes=VMEM_LIMIT)


def _dot(a, b):
    return jnp.dot(a.astype(BF16), b.astype(BF16), preferred_element_type=F32)


def _dot_nt(a, b):
    return lax.dot_general(a.astype(BF16), b.astype(BF16), (((1,), (1,)), ((), ())),
                           preferred_element_type=F32)


def _dot_exact(a, b):
    return jnp.dot(a, b, precision=lax.Precision.HIGHEST, preferred_element_type=F32)


def _rms(x, g):
    return x * lax.rsqrt(jnp.mean(x * x, axis=-1, keepdims=True) + RMS_EPS) * g


def _sigmoid(x):
    return 1.0 / (1.0 + jnp.exp(-x))


def _softplus(x):
    return jnp.maximum(x, 0.0) + jnp.log1p(jnp.exp(-jnp.abs(x)))


def _norm_proj_kernel(x_ref, g_ref, *refs, n_out, chunk):
    w_refs, o_refs = refs[:n_out], refs[n_out:]
    h = _rms(x_ref[...], g_ref[...]).astype(BF16)
    for w_ref, o_ref in zip(w_refs, o_refs):
        n = w_ref.shape[1]
        for n0 in range(0, n, chunk):
            n1 = min(n0 + chunk, n)
            o_ref[:, n0:n1] = jnp.dot(h, w_ref[:, n0:n1], preferred_element_type=F32)


def _norm_proj(x2, g, ws, tm=256, chunk=512):
    T, D = x2.shape
    n_out = len(ws)
    return pl.pallas_call(
        functools.partial(_norm_proj_kernel, n_out=n_out, chunk=chunk),
        grid=(T // tm,),
        in_specs=[pl.BlockSpec((tm, D), lambda i: (i, 0)),
                  pl.BlockSpec((1, D), lambda i: (0, 0))]
                 + [pl.BlockSpec(w.shape, lambda i: (0, 0)) for w in ws],
        out_specs=[pl.BlockSpec((tm, w.shape[1]), lambda i: (i, 0)) for w in ws],
        out_shape=[jax.ShapeDtypeStruct((T, w.shape[1]), F32) for w in ws],
        compiler_params=_params("parallel"),
        name="norm_proj",
    )(x2, g.reshape(1, D), *ws)


def _gdn_gate_kernel(x_ref, g_ref, w_ref, wt_ref, prow_ref, pcol_ref, col_ref, row_ref, *, nh):
    C = GDN_CHUNK
    h = _rms(x_ref[...], g_ref[...]).astype(BF16)
    a_c = jnp.dot(h, w_ref[...], preferred_element_type=F32)
    a_r = lax.dot_general(wt_ref[...], h, (((1,), (1,)), ((), ())),
                          preferred_element_type=F32)
    ri = lax.broadcasted_iota(jnp.int32, (C, C), 0)
    ci = lax.broadcasted_iota(jnp.int32, (C, C), 1)
    lower = (ri >= ci).astype(F32)
    upper = (ri <= ci).astype(F32)
    g_c = -jnp.exp(prow_ref[0:1, :]) * _softplus(a_c + prow_ref[1:2, :])
    g_r = -jnp.exp(pcol_ref[:, 0:1]) * _softplus(a_r + pcol_ref[:, 1:2])
    beta_c = _sigmoid(a_c)
    beta_r = _sigmoid(a_r)
    for c in range(x_ref.shape[0] // C):
        sl = slice(c * C, (c + 1) * C)
        gc = g_c[sl, :]
        cum = jnp.where(ci < nh, _dot_exact(lower, gc), _dot_exact(upper, gc))
        col_ref[sl, :] = jnp.where(ci < 2 * nh, cum, beta_c[sl, :])
        gr = g_r[:, sl]
        cum_r = jnp.where(ri < nh, _dot_exact(gr, upper), _dot_exact(gr, lower))
        full = jnp.where(ri < 2 * nh, cum_r, beta_r[:, sl])
        row_ref[0, :, sl] = full[:4 * nh, :]


def _gdn_gates(x2, g, w_ab, a_log, dt_bias, B, S, tm=256):
    T, D = x2.shape
    nh = a_log.shape[1]
    w_pad = jnp.zeros((D, LANES), F32).at[:, :4 * nh].set(w_ab)
    alog = jnp.zeros((LANES,), F32).at[:2 * nh].set(a_log.reshape(-1))
    dtb = jnp.zeros((LANES,), F32).at[:2 * nh].set(dt_bias.reshape(-1))
    prow = jnp.zeros((SUBLANES, LANES), F32).at[0].set(alog).at[1].set(dtb)
    pcol = jnp.zeros((LANES, LANES), F32).at[:, 0].set(alog).at[:, 1].set(dtb)
    nt = S // tm
    return pl.pallas_call(
        functools.partial(_gdn_gate_kernel, nh=nh),
        grid=(T // tm,),
        in_specs=[pl.BlockSpec((tm, D), lambda i: (i, 0)),
                  pl.BlockSpec((1, D), lambda i: (0, 0)),
                  pl.BlockSpec((D, LANES), lambda i: (0, 0)),
                  pl.BlockSpec((LANES, D), lambda i: (0, 0)),
                  pl.BlockSpec((SUBLANES, LANES), lambda i: (0, 0)),
                  pl.BlockSpec((LANES, LANES), lambda i: (0, 0))],
        out_specs=[pl.BlockSpec((tm, LANES), lambda i: (i, 0)),
                   pl.BlockSpec((1, 4 * nh, tm), lambda i: (i // nt, 0, i % nt))],
        out_shape=[jax.ShapeDtypeStruct((T, LANES), F32),
                   jax.ShapeDtypeStruct((B, 4 * nh, S), F32)],
        compiler_params=_params("parallel"),
        name="gdn_gates",
    )(x2, g.reshape(1, D), w_pad.astype(BF16), w_pad.T.astype(BF16), prow, pcol)


def _gdn_conv_kernel(x_ref, xp_ref, xn_ref, w_ref, o_ref, xe_ref, *, nh, nt):
    t = pl.program_id(1)
    tp = x_ref.shape[1]
    pad = GDN_CONV // 2
    xe_ref[0:SUBLANES, :] = jnp.where(t > 0, xp_ref[0], 0.0)
    xe_ref[SUBLANES:SUBLANES + tp, :] = x_ref[0]
    xe_ref[SUBLANES + tp:, :] = jnp.where(t < nt - 1, xn_ref[0], 0.0)
    for j in range(x_ref.shape[2] // LANES):
        ls = slice(j * LANES, (j + 1) * LANES)
        acc = None
        for i in range(GDN_CONV):
            term = w_ref[i:i + 1, ls] * xe_ref[pl.ds(SUBLANES - pad + i, tp), ls]
            acc = term if acc is None else acc + term
        y = acc * _sigmoid(acc)
        if j < 2 * nh:
            inv = lax.rsqrt(jnp.sum(y * y, axis=-1, keepdims=True) + L2_EPS)
            y = y * (inv * (GDN_DK ** -0.5) if j < nh else inv)
        o_ref[0, :, ls] = y.astype(o_ref.dtype)


def _gdn_conv(qkv, conv_w, nh, tp=256):
    B, S, W = qkv.shape
    nt = S // tp
    r = tp // SUBLANES
    return pl.pallas_call(
        functools.partial(_gdn_conv_kernel, nh=nh, nt=nt),
        grid=(B, nt),
        in_specs=[pl.BlockSpec((1, tp, W), lambda b, t: (b, t, 0)),
                  pl.BlockSpec((1, SUBLANES, W), lambda b, t: (b, jnp.maximum(t * r - 1, 0), 0)),
                  pl.BlockSpec((1, SUBLANES, W),
                               lambda b, t: (b, jnp.minimum((t + 1) * r, S // SUBLANES - 1), 0)),
                  pl.BlockSpec((GDN_CONV, W), lambda b, t: (0, 0))],
        out_specs=pl.BlockSpec((1, tp, W), lambda b, t: (b, t, 0)),
        out_shape=jax.ShapeDtypeStruct((B, S, W), BF16),
        scratch_shapes=[pltpu.VMEM((tp + 2 * SUBLANES, W), F32)],
        compiler_params=_params("parallel", "parallel"),
        name="gdn_conv",
    )(qkv, qkv, qkv, conv_w)


def _gdn_scan_kernel(q_ref, k_ref, v_ref, col_ref, row_ref, o_ref, state_ref, *, rev, nh):
    C = GDN_CHUNK
    t = pl.program_id(1)

    @pl.when(t == 0)
    def _():
        state_ref[...] = jnp.zeros_like(state_ref)

    ri = lax.broadcasted_iota(jnp.int32, (C, C), 0)
    ci = lax.broadcasted_iota(jnp.int32, (C, C), 1)
    incl = (ri <= ci) if rev else (ri >= ci)
    strict = (ri < ci) if rev else (ri > ci)
    eye = (ri == ci).astype(F32)
    levels = [((ri >> (lg + 1)) == (ci >> (lg + 1))) & ((ri >> lg) != (ci >> lg))
              for lg in range(int(math.log2(C)))]
    last = 0 if rev else C - 1
    nc = q_ref.shape[1] // C
    chunks = list(reversed(range(nc))) if rev else list(range(nc))
    heads = range(nh)
    hs = [slice(h * LANES, (h + 1) * LANES) for h in heads]
    gi = [(nh if rev else 0) + h for h in heads]
    units = [(c, h) for c in chunks for h in heads]
    rows = {c: slice(c * C, (c + 1) * C) for c in chunks}
    col = {c: col_ref[0, rows[c], :] for c in chunks}
    row = {c: row_ref[0, :, rows[c]] for c in chunks}
    kf = {u: k_ref[0, rows[u[0]], hs[u[1]]] for u in units}
    k = {u: kf[u].astype(F32) for u in units}
    g_col = {(c, h): col[c][:, gi[h]:gi[h] + 1] for c, h in units}
    b_col = {(c, h): col[c][:, 2 * nh + gi[h]:2 * nh + gi[h] + 1] for c, h in units}
    g_last = {u: g_col[u][last:last + 1, :] for u in units}
    decay = {(c, h): jnp.where(incl, jnp.exp(jnp.where(incl, g_col[c, h] - row[c][gi[h]:gi[h] + 1, :], 0.0)), 0.0)
             for c, h in units}
    kb = {u: k[u] * b_col[u] for u in units}
    res = {(c, h): _dot_nt(jnp.concatenate([kb[c, h].astype(BF16), q_ref[0, rows[c], hs[h]]], axis=0), kf[c, h])
           for c, h in units}
    a = {u: jnp.where(strict, res[u][:C] * decay[u], 0.0) for u in units}
    intra = {u: (res[u][C:] * decay[u]).astype(BF16) for u in units}
    tinv = {u: eye - jnp.where(levels[0], a[u], 0.0) for u in units}
    for m in levels[1:]:
        tb = {u: tinv[u].astype(BF16) for u in units}
        pm = {u: _dot(tb[u], jnp.where(m, a[u], 0.0)) for u in units}
        tinv = {u: tinv[u] - _dot(pm[u], tb[u]) for u in units}
    e_col = {u: jnp.exp(g_col[u]) for u in units}
    sol = {(c, h): _dot(tinv[c, h], jnp.concatenate(
        [v_ref[0, rows[c], hs[h]].astype(F32) * b_col[c, h], kb[c, h] * e_col[c, h]], axis=1)) for c, h in units}
    qd = {(c, h): (q_ref[0, rows[c], hs[h]].astype(F32) * e_col[c, h]).astype(BF16) for c, h in units}
    k_dec = {u: (k[u] * jnp.exp(g_last[u] - g_col[u])).T.astype(BF16) for u in units}
    state = [state_ref[h] for h in heads]
    for c in chunks:
        sb = [state[h].astype(BF16) for h in heads]
        r2 = [_dot(jnp.concatenate([sol[c, h][:, LANES:].astype(BF16), qd[c, h]], axis=0), sb[h]) for h in heads]
        v_new = [(sol[c, h][:, :LANES] - r2[h][:C]).astype(BF16) for h in heads]
        for h in heads:
            o_ref[0, rows[c], hs[h]] = r2[h][C:] + _dot(intra[c, h], v_new[h])
        state = [state[h] * jnp.exp(g_last[c, h]) + _dot(k_dec[c, h], v_new[h]) for h in heads]
    for h in heads:
        state_ref[h] = state[h]


def _gdn_scan(qkvn, col, row, nh, rev, nc=2):
    B, S, _ = qkvn.shape
    ts = nc * GDN_CHUNK
    nt = S // ts
    width = nh * LANES
    tmap = (lambda t: nt - 1 - t) if rev else (lambda t: t)
    col3 = col.reshape(B, S, LANES)
    return pl.pallas_call(
        functools.partial(_gdn_scan_kernel, rev=rev, nh=nh),
        grid=(B, nt),
        in_specs=[pl.BlockSpec((1, ts, width), lambda b, t: (b, tmap(t), 0)),
                  pl.BlockSpec((1, ts, width), lambda b, t: (b, tmap(t), 1)),
                  pl.BlockSpec((1, ts, width), lambda b, t: (b, tmap(t), 2)),
                  pl.BlockSpec((1, ts, LANES), lambda b, t: (b, tmap(t), 0)),
                  pl.BlockSpec((1, 4 * nh, ts), lambda b, t: (b, 0, tmap(t)))],
        out_specs=pl.BlockSpec((1, ts, width), lambda b, t: (b, tmap(t), 0)),
        out_shape=jax.ShapeDtypeStruct((B, S, width), F32),
        scratch_shapes=[pltpu.VMEM((nh, GDN_DK, LANES), F32)],
        compiler_params=_params("parallel", "arbitrary"),
        name="gdn_scan_bwd" if rev else "gdn_scan_fwd",
    )(qkvn, qkvn, qkvn, col3, row)


def _gdn_out_kernel(of_ref, ob_ref, z_ref, x_ref, nw_ref, w_ref, o_ref, y_ref, *, nh):
    for h in range(nh):
        hs = slice(h * LANES, (h + 1) * LANES)
        o = of_ref[:, hs] + ob_ref[:, hs]
        o = o * lax.rsqrt(jnp.mean(o * o, axis=-1, keepdims=True) + RMS_EPS)
        z = z_ref[:, hs]
        y_ref[:, hs] = (o * nw_ref[...] * (z * _sigmoid(z))).astype(BF16)
    o_ref[...] = x_ref[...] + jnp.dot(y_ref[...], w_ref[...], preferred_element_type=F32)


def _gdn_out(o_f, o_b, z, x2, norm_w, w_out, nh, tm=512):
    T, D = x2.shape
    W = o_f.shape[1]
    tok = lambda n: pl.BlockSpec((tm, n), lambda i: (i, 0))
    return pl.pallas_call(
        functools.partial(_gdn_out_kernel, nh=nh),
        grid=(T // tm,),
        in_specs=[tok(W), tok(W), tok(W), tok(D),
                  pl.BlockSpec((1, LANES), lambda i: (0, 0)),
                  pl.BlockSpec((W, D), lambda i: (0, 0))],
        out_specs=tok(D),
        out_shape=jax.ShapeDtypeStruct((T, D), F32),
        scratch_shapes=[pltpu.VMEM((tm, W), BF16)],
        compiler_params=_params("parallel"),
        name="gdn_out",
    )(o_f, o_b, z, x2, norm_w.reshape(1, LANES), w_out)


def _dswa_kernel(q_ref, k_ref, kp_ref, kn_ref, v_ref, vp_ref, vn_ref, bias_ref,
                 o_ref, lse_ref, *, sub_len):
    t = pl.program_id(2)
    lt = q_ref.shape[2]
    hb = DSWA_HALF
    q = q_ref[0, 0] * (DSWA_HEAD_DIM ** -0.5)
    kext = jnp.concatenate([kp_ref[0, 0], k_ref[0, 0], kn_ref[0, 0]], axis=0)
    vext = jnp.concatenate([vp_ref[0, 0], v_ref[0, 0], vn_ref[0, 0]], axis=0)
    first_head = lax.broadcasted_iota(jnp.int32, (1, LANES), 1) < DSWA_HEAD_DIM
    n_pairs = q_ref.shape[3] // LANES
    blocks = range(lt // hb)
    units = [(qb, pr) for qb in blocks for pr in range(n_pairs)]
    rows = [slice(qb * hb, (qb + 1) * hb) for qb in blocks]
    win = [slice(qb * hb, (qb + 3) * hb) for qb in blocks]
    ls = [slice(pr * LANES, (pr + 1) * LANES) for pr in range(n_pairs)]
    kvalid = []
    for qb in blocks:
        kpos = t * lt + (qb - 1) * hb + lax.broadcasted_iota(jnp.int32, (1, 3 * hb), 1)
        kvalid.append((kpos >= 0) & (kpos < sub_len))
    qs = {(qb, pr): jnp.concatenate([jnp.where(first_head, q[rows[qb], ls[pr]], 0.0),
                                     jnp.where(first_head, 0.0, q[rows[qb], ls[pr]])], axis=0)
          for qb, pr in units}
    s = {(qb, pr): _dot_nt(qs[qb, pr], kext[win[qb], ls[pr]]) for qb, pr in units}
    s = {(qb, pr): jnp.where(kvalid[qb], s[qb, pr] + bias_ref[pr], NEG_INF) for qb, pr in units}
    m = {u: jnp.max(s[u], axis=-1, keepdims=True) for u in units}
    p = {u: jnp.exp(s[u] - m[u]) for u in units}
    l = {u: jnp.sum(p[u], axis=-1, keepdims=True) for u in units}
    pv = {(qb, pr): _dot(p[qb, pr], vext[win[qb], ls[pr]]) * (1.0 / l[qb, pr]) for qb, pr in units}
    lse = {u: m[u] + jnp.log(l[u]) for u in units}
    for qb, pr in units:
        o_ref[0, 0, rows[qb], ls[pr]] = jnp.where(first_head, pv[qb, pr][:hb], pv[qb, pr][hb:])
        lse_ref[0, 0, rows[qb], ls[pr]] = jnp.where(first_head, lse[qb, pr][:hb], lse[qb, pr][hb:])


def _rel_bucket(rel):
    nb = REL_BUCKETS // 2
    max_exact = nb // 2
    ret = jnp.where(rel > 0, nb, 0)
    n = jnp.abs(rel)
    nf = jnp.maximum(n, 1).astype(F32)
    large = max_exact + (jnp.log(nf / max_exact) / math.log(REL_MAX_DIST / max_exact)
                         * (nb - max_exact)).astype(jnp.int32)
    large = jnp.minimum(large, nb - 1)
    return ret + jnp.where(n < max_exact, n, large)


def _band_bias(rel_table, heads, dil):
    hb = DSWA_HALF
    off = jnp.arange(3 * hb)[None, :] - hb - jnp.arange(hb)[:, None]
    bias = jnp.take(rel_table, _rel_bucket(off * dil), axis=0)[..., heads]
    bias = jnp.transpose(bias, (2, 0, 1)).astype(F32)
    bias = jnp.where((jnp.abs(off) <= hb)[None], bias, NEG_INF)
    return bias.reshape(bias.shape[0] // 2, 2 * hb, 3 * hb)


def _dswa_group(qkv, bias, lt=256):
    B, dil, L, W3 = qkv.shape
    hb = DSWA_HALF
    gw = W3 // 3
    lt = min(lt, L)
    r = lt // hb
    main = lambda which: pl.BlockSpec((1, 1, lt, gw), lambda b, d, t: (b, d, t, which))
    prev = lambda which: pl.BlockSpec(
        (1, 1, hb, gw), lambda b, d, t: (b, d, jnp.maximum(t * r - 1, 0), which))
    nxt = lambda which: pl.BlockSpec(
        (1, 1, hb, gw), lambda b, d, t: (b, d, jnp.minimum((t + 1) * r, L // hb - 1), which))
    out = pl.BlockSpec((1, 1, lt, gw), lambda b, d, t: (b, d, t, 0))
    return pl.pallas_call(
        functools.partial(_dswa_kernel, sub_len=L),
        grid=(B, dil, L // lt),
        in_specs=[main(0), main(1), prev(1), nxt(1), main(2), prev(2), nxt(2),
                  pl.BlockSpec(bias.shape, lambda b, d, t: (0, 0, 0))],
        out_specs=[out, out],
        out_shape=[jax.ShapeDtypeStruct((B, dil, L, gw), F32)] * 2,
        compiler_params=_params("parallel", "parallel", "parallel"),
        name=f"dswa_d{dil}",
    )(qkv, qkv, qkv, qkv, qkv, qkv, qkv, bias)


def _dswa_out_kernel(*refs, dils):
    ng = len(dils)
    o_refs, l_refs = refs[:ng], refs[ng:2 * ng]
    x_ref, w_ref, out_ref, o_s, l_s, y_s = refs[2 * ng:]
    tm = x_ref.shape[0]
    tiles = o_refs[0].shape[3] // LANES
    for g, d in enumerate(dils):
        for r in range(d):
            for j in range(tiles):
                ls = slice(j * LANES, (j + 1) * LANES)
                o_s[g * tiles + j, pl.ds(r, tm // d, stride=d), :] = o_refs[g][0, r, :, ls]
                l_s[g * tiles + j, pl.ds(r, tm // d, stride=d), :] = l_refs[g][0, r, :, ls]
    for j in range(tiles):
        lses = [l_s[g * tiles + j] for g in range(ng)]
        m = functools.reduce(jnp.maximum, lses)
        es = [jnp.exp(l - m) for l in lses]
        inv = 1.0 / functools.reduce(lambda a, b: a + b, es)
        for g in range(ng):
            c = g * tiles + j
            y_s[:, c * LANES:(c + 1) * LANES] = (o_s[c] * (es[g] * inv)).astype(BF16)
    out_ref[...] = x_ref[...] + jnp.dot(y_s[...], w_ref[...], preferred_element_type=F32)


def _dswa_out(os_, lses, x2, w_out, S, tm=512):
    T, D = x2.shape
    dils = tuple(o.shape[1] for o in os_)
    gw = os_[0].shape[3]
    nt = S // tm
    grp = lambda d: pl.BlockSpec((1, d, tm // d, gw), lambda i: (i // nt, 0, i % nt, 0))
    tok = lambda n: pl.BlockSpec((tm, n), lambda i: (i, 0))
    return pl.pallas_call(
        functools.partial(_dswa_out_kernel, dils=dils),
        grid=(T // tm,),
        in_specs=[grp(d) for d in dils] * 2 + [tok(D), pl.BlockSpec(w_out.shape, lambda i: (0, 0))],
        out_specs=tok(D),
        out_shape=jax.ShapeDtypeStruct((T, D), F32),
        scratch_shapes=[pltpu.VMEM((len(dils) * gw // LANES, tm, LANES), F32),
                        pltpu.VMEM((len(dils) * gw // LANES, tm, LANES), F32),
                        pltpu.VMEM((tm, len(dils) * gw), BF16)],
        compiler_params=_params("parallel"),
        name="dswa_out",
    )(*os_, *lses, x2, w_out)


def _mlp_kernel(x_ref, g_ref, w1_ref, w2_ref, gf_ref, o_ref, acc_ref, *, chunk, final):
    x = x_ref[...]
    h = _rms(x, g_ref[...]).astype(BF16)
    dff = w1_ref.shape[1]
    for c in range(0, dff, chunk):
        a = jnp.maximum(jnp.dot(h, w1_ref[:, c:c + chunk], preferred_element_type=F32), 0.0)
        part = jnp.dot((a * a).astype(BF16), w2_ref[c:c + chunk, :], preferred_element_type=F32)
        if c == 0:
            acc_ref[...] = part
        else:
            acc_ref[...] += part
    y = x + acc_ref[...]
    if final:
        y = _rms(y, gf_ref[...])
    o_ref[...] = y


def _mlp(x2, g, w1, w2, g_final, final, tm=512, chunk=512):
    T, D = x2.shape
    dff = w1.shape[1]
    return pl.pallas_call(
        functools.partial(_mlp_kernel, chunk=chunk, final=final),
        grid=(T // tm,),
        in_specs=[pl.BlockSpec((tm, D), lambda i: (i, 0)),
                  pl.BlockSpec((1, D), lambda i: (0, 0)),
                  pl.BlockSpec((D, dff), lambda i: (0, 0), pipeline_mode=pl.Buffered(1)),
                  pl.BlockSpec((dff, D), lambda i: (0, 0), pipeline_mode=pl.Buffered(1)),
                  pl.BlockSpec((1, D), lambda i: (0, 0))],
        out_specs=pl.BlockSpec((tm, D), lambda i: (i, 0)),
        out_shape=jax.ShapeDtypeStruct((T, D), F32),
        scratch_shapes=[pltpu.VMEM((tm, D), F32)],
        compiler_params=_params("parallel"),
        name="mlp",
    )(x2, g.reshape(1, D), w1, w2, g_final.reshape(1, D))


def _gdn_layer(x2, g, w_in, conv_w, a_log, dt_bias, norm_w, w_out, B, S):
    nh = a_log.shape[1]
    qkv_w = 3 * nh * LANES
    z_w = nh * LANES
    qkv, z = _norm_proj(x2, g, [w_in[:, :qkv_w].astype(BF16),
                                w_in[:, qkv_w:qkv_w + z_w].astype(BF16)])
    col, row = _gdn_gates(x2, g, w_in[:, qkv_w + z_w:], a_log, dt_bias, B, S)
    qkvn = _gdn_conv(qkv.reshape(B, S, qkv_w), conv_w.reshape(GDN_CONV, qkv_w), nh)
    o_f = _gdn_scan(qkvn, col, row, nh, rev=False)
    o_b = _gdn_scan(qkvn, col, row, nh, rev=True)
    return _gdn_out(o_f.reshape(B * S, z_w), o_b.reshape(B * S, z_w), z, x2, norm_w,
                    w_out.astype(BF16), nh)


def _dswa_proj_kernel(x_ref, g_ref, w_ref, *refs, dils, chunk):
    o_refs, acc_ref = refs[:-1], refs[-1]
    tm = x_ref.shape[0]
    n = w_ref.shape[1]
    ng = len(dils)
    gw = n // (3 * ng)
    tiles = gw // LANES
    h = _rms(x_ref[...], g_ref[...]).astype(BF16)
    for n0 in range(0, n, chunk):
        res = jnp.dot(h, w_ref[:, n0:n0 + chunk], preferred_element_type=F32)
        for j in range(chunk // LANES):
            acc_ref[n0 // LANES + j] = res[:, j * LANES:(j + 1) * LANES]
    for g, d in enumerate(dils):
        for r in range(d):
            for which in range(3):
                for j in range(tiles):
                    dst = slice((which * tiles + j) * LANES, (which * tiles + j + 1) * LANES)
                    o_refs[g][0, r, :, dst] = \
                        acc_ref[(which * ng + g) * tiles + j, pl.ds(r, tm // d, stride=d), :].astype(BF16)


def _dswa_proj(x2, g, w_in, dils, B, S, tm=256):
    T, D = x2.shape
    n = w_in.shape[1]
    gw3 = n // len(dils)
    nt = S // tm
    return pl.pallas_call(
        functools.partial(_dswa_proj_kernel, dils=dils, chunk=gw3),
        grid=(T // tm,),
        in_specs=[pl.BlockSpec((tm, D), lambda i: (i, 0)),
                  pl.BlockSpec((1, D), lambda i: (0, 0)),
                  pl.BlockSpec((D, n), lambda i: (0, 0))],
        out_specs=[pl.BlockSpec((1, d, tm // d, gw3), lambda i: (i // nt, 0, i % nt, 0)) for d in dils],
        out_shape=[jax.ShapeDtypeStruct((B, d, S // d, gw3), BF16) for d in dils],
        scratch_shapes=[pltpu.VMEM((n // LANES, tm, LANES), F32)],
        compiler_params=_params("parallel"),
        name="dswa_proj",
    )(x2, g.reshape(1, D), w_in)


def _dswa_layer(x2, g, w_in, w_out, rel_bias, B, S):
    ng = len(DSWA_CONFIGS)
    dils = tuple(d for _, d in DSWA_CONFIGS)
    qkvs = _dswa_proj(x2, g, w_in.astype(BF16), dils, B, S)
    hg = rel_bias.shape[1] // ng
    os_, lses = [], []
    for gi, (window, dil) in enumerate(DSWA_CONFIGS):
        assert window // (2 * dil) == DSWA_HALF
        bias = _band_bias(rel_bias, slice(gi * hg, (gi + 1) * hg), dil)
        o, lse = _dswa_group(qkvs[gi], bias)
        os_.append(o)
        lses.append(lse)
    return _dswa_out(os_, lses, x2, w_out.astype(BF16), S)


def kernel(x, norm_mix, norm_mlp, norm_final, rel_bias, gdn_w_in, gdn_conv_w, gdn_a_log,
           gdn_dt_bias, gdn_norm_w, gdn_w_out, dswa_w_in, dswa_w_out, mlp_w1, mlp_w2):
    B, S, D = x.shape
    depth = norm_mix.shape[0]
    x2 = x.reshape(B * S, D)
    for i in range(depth):
        j = i // 2
        if i % 2 == 0:
            x2 = _gdn_layer(x2, norm_mix[i], gdn_w_in[j], gdn_conv_w[j], gdn_a_log[j],
                            gdn_dt_bias[j], gdn_norm_w[j], gdn_w_out[j], B, S)
        else:
            x2 = _dswa_layer(x2, norm_mix[i], dswa_w_in[j], dswa_w_out[j], rel_bias, B, S)
        x2 = _mlp(x2, norm_mlp[i], mlp_w1[i].astype(BF16), mlp_w2[i].astype(BF16),
                  norm_final, final=(i == depth - 1))
    return x2.reshape(B, S, D)
```

```python
import functools
import math

import jax
import jax.numpy as jnp
from jax import lax
from jax.experimental import pallas as pl
from jax.experimental.pallas import tpu as pltpu

F32 = jnp.float32
BF16 = jnp.bfloat16

RMS_EPS = 1e-6
L2_EPS = 1e-6
NEG_INF = -1e30

LANES = 128
SUBLANES = 8
VMEM_LIMIT = 56 * 1024 * 1024

GDN_DK = 128
GDN_CONV = 5
GDN_CHUNK = LANES
DSWA_HEAD_DIM = 64
DSWA_CONFIGS = ((128, 1), (512, 4), (2048, 16))
DSWA_HALF = 64
REL_BUCKETS = 32
REL_MAX_DIST = 1024


def _params(*sem):
    return pltpu.CompilerParams(dimension_semantics=sem, vmem_limit_bytes=VMEM_LIMIT)


def _dot(a, b):
    return jnp.dot(a.astype(BF16), b.astype(BF16), preferred_element_type=F32)


def _dot_nt(a, b):
    return lax.dot_general(a.astype(BF16), b.astype(BF16), (((1,), (1,)), ((), ())),
                           preferred_element_type=F32)


def _dot_exact(a, b):
    return jnp.dot(a, b, precision=lax.Precision.HIGHEST, preferred_element_type=F32)


def _rms(x, g):
    return x * lax.rsqrt(jnp.mean(x * x, axis=-1, keepdims=True) + RMS_EPS) * g


def _sigmoid(x):
    return 1.0 / (1.0 + jnp.exp(-x))


def _softplus(x):
    return jnp.maximum(x, 0.0) + jnp.log1p(jnp.exp(-jnp.abs(x)))


CONV_HALO = 16


def _gdn_proj_kernel(x_ref, xp_ref, xn_ref, g_ref, wqkv_ref, wz_ref, w_ref, wt_ref, cw_ref,
                     prow_ref, pcol_ref, qkv_ref, z_ref, col_ref, row_ref, re_ref,
                     *, nh, nt, chunk):
    C = GDN_CHUNK
    tm = x_ref.shape[0]
    t = pl.program_id(0) % nt
    pad = GDN_CONV // 2
    g = g_ref[...]
    h = _rms(x_ref[...], g).astype(BF16)
    h_halo = jnp.concatenate([_rms(jnp.where(t > 0, xp_ref[...], 0.0), g).astype(BF16),
                              _rms(jnp.where(t < nt - 1, xn_ref[...], 0.0), g).astype(BF16)], axis=0)
    for n, c0 in enumerate(range(0, wqkv_ref.shape[1], chunk)):
        cs = slice(c0, c0 + chunk)
        buf = re_ref.at[n % 2]
        halo = jnp.dot(h_halo, wqkv_ref[:, cs], preferred_element_type=F32)
        buf[0:CONV_HALO, :] = halo[:CONV_HALO]
        buf[CONV_HALO:CONV_HALO + tm, :] = jnp.dot(h, wqkv_ref[:, cs], preferred_element_type=F32)
        buf[CONV_HALO + tm:, :] = halo[CONV_HALO:]
        ext = buf[...]
        acc = None
        for i in range(GDN_CONV):
            shifted = ext if i == pad else pltpu.roll(ext, (pad - i) % ext.shape[0], axis=0)
            term = cw_ref[i:i + 1, cs] * shifted[CONV_HALO:CONV_HALO + tm]
            acc = term if acc is None else acc + term
        y = acc * _sigmoid(acc)
        for j in range(chunk // LANES):
            yj = y[:, j * LANES:(j + 1) * LANES]
            head = c0 // LANES + j
            if head < 2 * nh:
                inv = lax.rsqrt(jnp.sum(yj * yj, axis=-1, keepdims=True) + L2_EPS)
                yj = yj * (inv * (GDN_DK ** -0.5) if head < nh else inv)
            qkv_ref[:, c0 + j * LANES:c0 + (j + 1) * LANES] = yj.astype(BF16)
    for c0 in range(0, wz_ref.shape[1], chunk):
        cs = slice(c0, c0 + chunk)
        z_ref[:, cs] = jnp.dot(h, wz_ref[:, cs], preferred_element_type=F32).astype(BF16)
    a_c = jnp.dot(h, w_ref[...], preferred_element_type=F32)
    a_r = lax.dot_general(wt_ref[...], h, (((1,), (1,)), ((), ())),
                          preferred_element_type=F32)
    ri = lax.broadcasted_iota(jnp.int32, (C, C), 0)
    ci = lax.broadcasted_iota(jnp.int32, (C, C), 1)
    lower = (ri >= ci).astype(F32)
    upper = (ri <= ci).astype(F32)
    g_c = -jnp.exp(prow_ref[0:1, :]) * _softplus(a_c + prow_ref[1:2, :])
    g_r = -jnp.exp(pcol_ref[:, 0:1]) * _softplus(a_r + pcol_ref[:, 1:2])
    beta_c = _sigmoid(a_c)
    beta_r = _sigmoid(a_r)
    for c in range(tm // C):
        sl = slice(c * C, (c + 1) * C)
        gc = g_c[sl, :]
        cum = jnp.where(ci < nh, _dot_exact(lower, gc), _dot_exact(upper, gc))
        col_ref[sl, :] = jnp.where(ci < 2 * nh, cum, beta_c[sl, :])
        gr = g_r[:, sl]
        cum_r = jnp.where(ri < nh, _dot_exact(gr, upper), _dot_exact(gr, lower))
        full = jnp.where(ri < 2 * nh, cum_r, beta_r[:, sl])
        row_ref[0, :, sl] = full[:4 * nh, :]


def _gdn_proj(x2, g, w_in, conv_w, a_log, dt_bias, B, S, tm=256, chunk=256):
    T, D = x2.shape
    nh = a_log.shape[1]
    qkv_w = 3 * nh * LANES
    z_w = nh * LANES
    w_pad = jnp.zeros((D, LANES), F32).at[:, :4 * nh].set(w_in[:, qkv_w + z_w:])
    alog = jnp.zeros((LANES,), F32).at[:2 * nh].set(a_log.reshape(-1))
    dtb = jnp.zeros((LANES,), F32).at[:2 * nh].set(dt_bias.reshape(-1))
    prow = jnp.zeros((SUBLANES, LANES), F32).at[0].set(alog).at[1].set(dtb)
    pcol = jnp.zeros((LANES, LANES), F32).at[:, 0].set(alog).at[:, 1].set(dtb)
    nt = S // tm
    r = tm // CONV_HALO
    const = lambda shape: pl.BlockSpec(shape, lambda i: (0, 0))
    return pl.pallas_call(
        functools.partial(_gdn_proj_kernel, nh=nh, nt=nt, chunk=chunk),
        grid=(T // tm,),
        in_specs=[pl.BlockSpec((tm, D), lambda i: (i, 0)),
                  pl.BlockSpec((CONV_HALO, D), lambda i: (jnp.maximum(i * r - 1, 0), 0)),
                  pl.BlockSpec((CONV_HALO, D), lambda i: (jnp.minimum((i + 1) * r, T // CONV_HALO - 1), 0)),
                  const((1, D)), const((D, qkv_w)), const((D, z_w)), const((D, LANES)), const((LANES, D)),
                  const((GDN_CONV, qkv_w)), const((SUBLANES, LANES)), const((LANES, LANES))],
        out_specs=[pl.BlockSpec((tm, qkv_w), lambda i: (i, 0)),
                   pl.BlockSpec((tm, z_w), lambda i: (i, 0)),
                   pl.BlockSpec((tm, LANES), lambda i: (i, 0)),
                   pl.BlockSpec((1, 4 * nh, tm), lambda i: (i // nt, 0, i % nt))],
        out_shape=[jax.ShapeDtypeStruct((T, qkv_w), BF16),
                   jax.ShapeDtypeStruct((T, z_w), BF16),
                   jax.ShapeDtypeStruct((T, LANES), F32),
                   jax.ShapeDtypeStruct((B, 4 * nh, S), F32)],
        scratch_shapes=[pltpu.VMEM((2, tm + 2 * CONV_HALO, chunk), F32)],
        compiler_params=_params("parallel"),
        name="gdn_proj",
    )(x2, x2, x2, g.reshape(1, D), w_in[:, :qkv_w].astype(BF16), w_in[:, qkv_w:qkv_w + z_w].astype(BF16),
      w_pad.astype(BF16), w_pad.T.astype(BF16), conv_w.reshape(GDN_CONV, qkv_w), prow, pcol)


def _gdn_scan_kernel(q_ref, k_ref, v_ref, col_ref, row_ref, o_ref, state_ref, *, rev, nh):
    C = GDN_CHUNK
    t = pl.program_id(1)

    @pl.when(t == 0)
    def _():
        state_ref[...] = jnp.zeros_like(state_ref)

    ri = lax.broadcasted_iota(jnp.int32, (C, C), 0)
    ci = lax.broadcasted_iota(jnp.int32, (C, C), 1)
    incl = (ri <= ci) if rev else (ri >= ci)
    strict = (ri < ci) if rev else (ri > ci)
    eye = (ri == ci).astype(F32)
    levels = [((ri >> (lg + 1)) == (ci >> (lg + 1))) & ((ri >> lg) != (ci >> lg))
              for lg in range(int(math.log2(C)))]
    last = 0 if rev else C - 1
    nc = q_ref.shape[1] // C
    chunks = list(reversed(range(nc))) if rev else list(range(nc))
    heads = range(nh)
    hs = [slice(h * LANES, (h + 1) * LANES) for h in heads]
    gi = [(nh if rev else 0) + h for h in heads]
    units = [(c, h) for c in chunks for h in heads]
    rows = {c: slice(c * C, (c + 1) * C) for c in chunks}
    col = {c: col_ref[0, rows[c], :] for c in chunks}
    row = {c: row_ref[0, :, rows[c]] for c in chunks}
    kf = {u: k_ref[0, rows[u[0]], hs[u[1]]] for u in units}
    k = {u: kf[u].astype(F32) for u in units}
    g_col = {(c, h): col[c][:, gi[h]:gi[h] + 1] for c, h in units}
    b_col = {(c, h): col[c][:, 2 * nh + gi[h]:2 * nh + gi[h] + 1] for c, h in units}
    g_last = {u: g_col[u][last:last + 1, :] for u in units}
    decay = {(c, h): jnp.where(incl, jnp.exp(jnp.where(incl, g_col[c, h] - row[c][gi[h]:gi[h] + 1, :], 0.0)), 0.0)
             for c, h in units}
    kb = {u: k[u] * b_col[u] for u in units}
    res = {(c, h): _dot_nt(jnp.concatenate([kb[c, h].astype(BF16), q_ref[0, rows[c], hs[h]]], axis=0), kf[c, h])
           for c, h in units}
    a = {u: jnp.where(strict, res[u][:C] * decay[u], 0.0) for u in units}
    intra = {u: (res[u][C:] * decay[u]).astype(BF16) for u in units}
    tinv = {u: eye - jnp.where(levels[0], a[u], 0.0) for u in units}
    for m in levels[1:]:
        tb = {u: tinv[u].astype(BF16) for u in units}
        pm = {u: _dot(tb[u], jnp.where(m, a[u], 0.0)) for u in units}
        tinv = {u: tinv[u] - _dot(pm[u], tb[u]) for u in units}
    e_col = {u: jnp.exp(g_col[u]) for u in units}
    sol = {(c, h): _dot(tinv[c, h], jnp.concatenate(
        [v_ref[0, rows[c], hs[h]].astype(F32) * b_col[c, h], kb[c, h] * e_col[c, h]], axis=1)) for c, h in units}
    qd = {(c, h): (q_ref[0, rows[c], hs[h]].astype(F32) * e_col[c, h]).astype(BF16) for c, h in units}
    k_dec = {u: (k[u] * jnp.exp(g_last[u] - g_col[u])).T.astype(BF16) for u in units}
    state = [state_ref[h] for h in heads]
    for c in chunks:
        sb = [state[h].astype(BF16) for h in heads]
        r2 = [_dot(jnp.concatenate([sol[c, h][:, LANES:].astype(BF16), qd[c, h]], axis=0), sb[h]) for h in heads]
        v_new = [(sol[c, h][:, :LANES] - r2[h][:C]).astype(BF16) for h in heads]
        for h in heads:
            o_ref[0, rows[c], hs[h]] = (r2[h][C:] + _dot(intra[c, h], v_new[h])).astype(o_ref.dtype)
        state = [state[h] * jnp.exp(g_last[c, h]) + _dot(k_dec[c, h], v_new[h]) for h in heads]
    for h in heads:
        state_ref[h] = state[h]


def _gdn_scan(qkvn, col, row, nh, rev, nc=2):
    B, S, _ = qkvn.shape
    ts = nc * GDN_CHUNK
    nt = S // ts
    width = nh * LANES
    tmap = (lambda t: nt - 1 - t) if rev else (lambda t: t)
    col3 = col.reshape(B, S, LANES)
    return pl.pallas_call(
        functools.partial(_gdn_scan_kernel, rev=rev, nh=nh),
        grid=(B, nt),
        in_specs=[pl.BlockSpec((1, ts, width), lambda b, t: (b, tmap(t), 0)),
                  pl.BlockSpec((1, ts, width), lambda b, t: (b, tmap(t), 1)),
                  pl.BlockSpec((1, ts, width), lambda b, t: (b, tmap(t), 2)),
                  pl.BlockSpec((1, ts, LANES), lambda b, t: (b, tmap(t), 0)),
                  pl.BlockSpec((1, 4 * nh, ts), lambda b, t: (b, 0, tmap(t)))],
        out_specs=pl.BlockSpec((1, ts, width), lambda b, t: (b, tmap(t), 0)),
        out_shape=jax.ShapeDtypeStruct((B, S, width), BF16),
        scratch_shapes=[pltpu.VMEM((nh, GDN_DK, LANES), F32)],
        compiler_params=_params("parallel", "arbitrary"),
        name="gdn_scan_bwd" if rev else "gdn_scan_fwd",
    )(qkvn, qkvn, qkvn, col3, row)


def _gdn_out_kernel(of_ref, ob_ref, z_ref, x_ref, nw_ref, w_ref, o_ref, y_ref, *, nh):
    for h in range(nh):
        hs = slice(h * LANES, (h + 1) * LANES)
        o = of_ref[:, hs].astype(F32) + ob_ref[:, hs].astype(F32)
        o = o * lax.rsqrt(jnp.mean(o * o, axis=-1, keepdims=True) + RMS_EPS)
        z = z_ref[:, hs].astype(F32)
        y_ref[:, hs] = (o * nw_ref[...] * (z * _sigmoid(z))).astype(BF16)
    o_ref[...] = x_ref[...] + jnp.dot(y_ref[...], w_ref[...], preferred_element_type=F32)


def _gdn_out(o_f, o_b, z, x2, norm_w, w_out, nh, tm=512):
    T, D = x2.shape
    W = o_f.shape[1]
    tok = lambda n: pl.BlockSpec((tm, n), lambda i: (i, 0))
    return pl.pallas_call(
        functools.partial(_gdn_out_kernel, nh=nh),
        grid=(T // tm,),
        in_specs=[tok(W), tok(W), tok(W), tok(D),
                  pl.BlockSpec((1, LANES), lambda i: (0, 0)),
                  pl.BlockSpec((W, D), lambda i: (0, 0))],
        out_specs=tok(D),
        out_shape=jax.ShapeDtypeStruct((T, D), F32),
        scratch_shapes=[pltpu.VMEM((tm, W), BF16)],
        compiler_params=_params("parallel"),
        name="gdn_out",
    )(o_f, o_b, z, x2, norm_w.reshape(1, LANES), w_out)


def _dswa_kernel(q_ref, k_ref, kp_ref, kn_ref, v_ref, vp_ref, vn_ref, bias_ref,
                 o_ref, lse_ref, *, sub_len):
    t = pl.program_id(2)
    lt = q_ref.shape[2]
    hb = DSWA_HALF
    q = q_ref[0, 0] * (DSWA_HEAD_DIM ** -0.5)
    kext = jnp.concatenate([kp_ref[0, 0], k_ref[0, 0], kn_ref[0, 0]], axis=0)
    vext = jnp.concatenate([vp_ref[0, 0], v_ref[0, 0], vn_ref[0, 0]], axis=0)
    first_head = lax.broadcasted_iota(jnp.int32, (1, LANES), 1) < DSWA_HEAD_DIM
    n_pairs = q_ref.shape[3] // LANES
    blocks = range(lt // hb)
    units = [(qb, pr) for qb in blocks for pr in range(n_pairs)]
    rows = [slice(qb * hb, (qb + 1) * hb) for qb in blocks]
    win = [slice(qb * hb, (qb + 3) * hb) for qb in blocks]
    ls = [slice(pr * LANES, (pr + 1) * LANES) for pr in range(n_pairs)]
    kvalid = []
    for qb in blocks:
        kpos = t * lt + (qb - 1) * hb + lax.broadcasted_iota(jnp.int32, (1, 3 * hb), 1)
        kvalid.append((kpos >= 0) & (kpos < sub_len))
    qs = {(qb, pr): jnp.concatenate([jnp.where(first_head, q[rows[qb], ls[pr]], 0.0),
                                     jnp.where(first_head, 0.0, q[rows[qb], ls[pr]])], axis=0)
          for qb, pr in units}
    s = {(qb, pr): _dot_nt(qs[qb, pr], kext[win[qb], ls[pr]]) for qb, pr in units}
    s = {(qb, pr): jnp.where(kvalid[qb], s[qb, pr] + bias_ref[pr], NEG_INF) for qb, pr in units}
    m = {u: jnp.max(s[u], axis=-1, keepdims=True) for u in units}
    p = {u: jnp.exp(s[u] - m[u]) for u in units}
    l = {u: jnp.sum(p[u], axis=-1, keepdims=True) for u in units}
    pv = {(qb, pr): _dot(p[qb, pr], vext[win[qb], ls[pr]]) * (1.0 / l[qb, pr]) for qb, pr in units}
    lse = {u: m[u] + jnp.log(l[u]) for u in units}
    for qb, pr in units:
        o_ref[0, 0, rows[qb], ls[pr]] = jnp.where(first_head, pv[qb, pr][:hb], pv[qb, pr][hb:])
        lse_ref[0, 0, rows[qb], ls[pr]] = jnp.where(first_head, lse[qb, pr][:hb], lse[qb, pr][hb:])


def _rel_bucket(rel):
    nb = REL_BUCKETS // 2
    max_exact = nb // 2
    ret = jnp.where(rel > 0, nb, 0)
    n = jnp.abs(rel)
    nf = jnp.maximum(n, 1).astype(F32)
    large = max_exact + (jnp.log(nf * (1.0 / max_exact)) / math.log(REL_MAX_DIST / max_exact)
                         * (nb - max_exact)).astype(jnp.int32)
    large = jnp.minimum(large, nb - 1)
    return ret + jnp.where(n < max_exact, n, large)


def _band_bias_kernel(tab_ref, o_ref, *, dils, hg):
    hb = DSWA_HALF
    off = (lax.broadcasted_iota(jnp.int32, (hb, 3 * hb), 1) - hb
           - lax.broadcasted_iota(jnp.int32, (hb, 3 * hb), 0))
    inband = jnp.abs(off) <= hb
    for g, d in enumerate(dils):
        bucket = _rel_bucket(off * d)
        for hh in range(hg):
            acc = jnp.zeros((hb, 3 * hb), F32)
            for b in range(REL_BUCKETS):
                acc = jnp.where(bucket == b, tab_ref[b, g * hg + hh], acc)
            o_ref[g, hh // 2, (hh % 2) * hb:(hh % 2 + 1) * hb, :] = jnp.where(inband, acc, NEG_INF)


def _band_bias(rel_table, dils):
    hb = DSWA_HALF
    hg = rel_table.shape[1] // len(dils)
    return pl.pallas_call(
        functools.partial(_band_bias_kernel, dils=dils, hg=hg),
        in_specs=[pl.BlockSpec(memory_space=pltpu.SMEM)],
        out_shape=jax.ShapeDtypeStruct((len(dils), hg // 2, 2 * hb, 3 * hb), F32),
        name="band_bias",
    )(rel_table)


def _dswa_group(qkv, bias, lt=256):
    B, dil, L, W3 = qkv.shape
    hb = DSWA_HALF
    gw = W3 // 3
    lt = min(lt, L)
    r = lt // hb
    main = lambda which: pl.BlockSpec((1, 1, lt, gw), lambda b, d, t: (b, d, t, which))
    prev = lambda which: pl.BlockSpec(
        (1, 1, hb, gw), lambda b, d, t: (b, d, jnp.maximum(t * r - 1, 0), which))
    nxt = lambda which: pl.BlockSpec(
        (1, 1, hb, gw), lambda b, d, t: (b, d, jnp.minimum((t + 1) * r, L // hb - 1), which))
    out = pl.BlockSpec((1, 1, lt, gw), lambda b, d, t: (b, d, t, 0))
    return pl.pallas_call(
        functools.partial(_dswa_kernel, sub_len=L),
        grid=(B, dil, L // lt),
        in_specs=[main(0), main(1), prev(1), nxt(1), main(2), prev(2), nxt(2),
                  pl.BlockSpec(bias.shape, lambda b, d, t: (0, 0, 0))],
        out_specs=[out, out],
        out_shape=[jax.ShapeDtypeStruct((B, dil, L, gw), F32)] * 2,
        compiler_params=_params("parallel", "parallel", "parallel"),
        name=f"dswa_d{dil}",
    )(qkv, qkv, qkv, qkv, qkv, qkv, qkv, bias)


def _dswa_out_kernel(*refs, dils):
    ng = len(dils)
    o_refs, l_refs = refs[:ng], refs[ng:2 * ng]
    x_ref, w_ref, out_ref, o_s, l_s, y_s = refs[2 * ng:]
    tm = x_ref.shape[0]
    tiles = o_refs[0].shape[3] // LANES
    for g, d in enumerate(dils):
        for r in range(d):
            for j in range(tiles):
                ls = slice(j * LANES, (j + 1) * LANES)
                o_s[g * tiles + j, pl.ds(r, tm // d, stride=d), :] = o_refs[g][0, r, :, ls]
                l_s[g * tiles + j, pl.ds(r, tm // d, stride=d), :] = l_refs[g][0, r, :, ls]
    for j in range(tiles):
        lses = [l_s[g * tiles + j] for g in range(ng)]
        m = functools.reduce(jnp.maximum, lses)
        es = [jnp.exp(l - m) for l in lses]
        inv = 1.0 / functools.reduce(lambda a, b: a + b, es)
        for g in range(ng):
            c = g * tiles + j
            y_s[:, c * LANES:(c + 1) * LANES] = (o_s[c] * (es[g] * inv)).astype(BF16)
    out_ref[...] = x_ref[...] + jnp.dot(y_s[...], w_ref[...], preferred_element_type=F32)


def _dswa_out(os_, lses, x2, w_out, S, tm=512):
    T, D = x2.shape
    dils = tuple(o.shape[1] for o in os_)
    gw = os_[0].shape[3]
    nt = S // tm
    grp = lambda d: pl.BlockSpec((1, d, tm // d, gw), lambda i: (i // nt, 0, i % nt, 0))
    tok = lambda n: pl.BlockSpec((tm, n), lambda i: (i, 0))
    return pl.pallas_call(
        functools.partial(_dswa_out_kernel, dils=dils),
        grid=(T // tm,),
        in_specs=[grp(d) for d in dils] * 2 + [tok(D), pl.BlockSpec(w_out.shape, lambda i: (0, 0))],
        out_specs=tok(D),
        out_shape=jax.ShapeDtypeStruct((T, D), F32),
        scratch_shapes=[pltpu.VMEM((len(dils) * gw // LANES, tm, LANES), F32),
                        pltpu.VMEM((len(dils) * gw // LANES, tm, LANES), F32),
                        pltpu.VMEM((tm, len(dils) * gw), BF16)],
        compiler_params=_params("parallel"),
        name="dswa_out",
    )(*os_, *lses, x2, w_out)


def _mlp_kernel(x_ref, g_ref, w1_ref, w2_ref, gf_ref, o_ref, acc_ref, *, chunk, final):
    x = x_ref[...]
    h = _rms(x, g_ref[...]).astype(BF16)
    dff = w1_ref.shape[1]
    for c in range(0, dff, chunk):
        a = jnp.maximum(jnp.dot(h, w1_ref[:, c:c + chunk], preferred_element_type=F32), 0.0)
        part = jnp.dot((a * a).astype(BF16), w2_ref[c:c + chunk, :], preferred_element_type=F32)
        if c == 0:
            acc_ref[...] = part
        else:
            acc_ref[...] += part
    y = x + acc_ref[...]
    if final:
        y = _rms(y, gf_ref[...])
    o_ref[...] = y


def _mlp(x2, g, w1, w2, g_final, final, tm=512, chunk=512):
    T, D = x2.shape
    dff = w1.shape[1]
    return pl.pallas_call(
        functools.partial(_mlp_kernel, chunk=chunk, final=final),
        grid=(T // tm,),
        in_specs=[pl.BlockSpec((tm, D), lambda i: (i, 0)),
                  pl.BlockSpec((1, D), lambda i: (0, 0)),
                  pl.BlockSpec((D, dff), lambda i: (0, 0), pipeline_mode=pl.Buffered(1)),
                  pl.BlockSpec((dff, D), lambda i: (0, 0), pipeline_mode=pl.Buffered(1)),
                  pl.BlockSpec((1, D), lambda i: (0, 0))],
        out_specs=pl.BlockSpec((tm, D), lambda i: (i, 0)),
        out_shape=jax.ShapeDtypeStruct((T, D), F32),
        scratch_shapes=[pltpu.VMEM((tm, D), F32)],
        compiler_params=_params("parallel"),
        name="mlp",
    )(x2, g.reshape(1, D), w1, w2, g_final.reshape(1, D))


def _gdn_layer(x2, g, w_in, conv_w, a_log, dt_bias, norm_w, w_out, B, S):
    nh = a_log.shape[1]
    z_w = nh * LANES
    qkvn, z, col, row = _gdn_proj(x2, g, w_in, conv_w, a_log, dt_bias, B, S)
    qkvn = qkvn.reshape(B, S, 3 * z_w)
    o_f = _gdn_scan(qkvn, col, row, nh, rev=False)
    o_b = _gdn_scan(qkvn, col, row, nh, rev=True)
    return _gdn_out(o_f.reshape(B * S, z_w), o_b.reshape(B * S, z_w), z, x2, norm_w,
                    w_out.astype(BF16), nh)


def _dswa_proj_kernel(x_ref, g_ref, w_ref, *refs, dils, chunk):
    o_refs, acc_ref = refs[:-1], refs[-1]
    tm = x_ref.shape[0]
    n = w_ref.shape[1]
    ng = len(dils)
    gw = n // (3 * ng)
    tiles = gw // LANES
    h = _rms(x_ref[...], g_ref[...]).astype(BF16)
    for n0 in range(0, n, chunk):
        res = jnp.dot(h, w_ref[:, n0:n0 + chunk], preferred_element_type=F32)
        for j in range(chunk // LANES):
            acc_ref[n0 // LANES + j] = res[:, j * LANES:(j + 1) * LANES]
    for g, d in enumerate(dils):
        for r in range(d):
            for which in range(3):
                for j in range(tiles):
                    dst = slice((which * tiles + j) * LANES, (which * tiles + j + 1) * LANES)
                    o_refs[g][0, r, :, dst] = \
                        acc_ref[(which * ng + g) * tiles + j, pl.ds(r, tm // d, stride=d), :].astype(BF16)


def _dswa_proj(x2, g, w_in, dils, B, S, tm=256):
    T, D = x2.shape
    n = w_in.shape[1]
    gw3 = n // len(dils)
    nt = S // tm
    return pl.pallas_call(
        functools.partial(_dswa_proj_kernel, dils=dils, chunk=gw3),
        grid=(T // tm,),
        in_specs=[pl.BlockSpec((tm, D), lambda i: (i, 0)),
                  pl.BlockSpec((1, D), lambda i: (0, 0)),
                  pl.BlockSpec((D, n), lambda i: (0, 0))],
        out_specs=[pl.BlockSpec((1, d, tm // d, gw3), lambda i: (i // nt, 0, i % nt, 0)) for d in dils],
        out_shape=[jax.ShapeDtypeStruct((B, d, S // d, gw3), BF16) for d in dils],
        scratch_shapes=[pltpu.VMEM((n // LANES, tm, LANES), F32)],
        compiler_params=_params("parallel"),
        name="dswa_proj",
    )(x2, g.reshape(1, D), w_in)


def _dswa_layer(x2, g, w_in, w_out, bias, B, S):
    dils = tuple(d for _, d in DSWA_CONFIGS)
    qkvs = _dswa_proj(x2, g, w_in.astype(BF16), dils, B, S)
    os_, lses = [], []
    for gi in range(len(dils)):
        o, lse = _dswa_group(qkvs[gi], bias[gi])
        os_.append(o)
        lses.append(lse)
    return _dswa_out(os_, lses, x2, w_out.astype(BF16), S)


def kernel(x, norm_mix, norm_mlp, norm_final, rel_bias, gdn_w_in, gdn_conv_w, gdn_a_log,
           gdn_dt_bias, gdn_norm_w, gdn_w_out, dswa_w_in, dswa_w_out, mlp_w1, mlp_w2):
    B, S, D = x.shape
    depth = norm_mix.shape[0]
    x2 = x.reshape(B * S, D)
    assert all(window // (2 * dil) == DSWA_HALF for window, dil in DSWA_CONFIGS)
    bias = _band_bias(rel_bias, tuple(d for _, d in DSWA_CONFIGS))
    for i in range(depth):
        j = i // 2
        if i % 2 == 0:
            x2 = _gdn_layer(x2, norm_mix[i], gdn_w_in[j], gdn_conv_w[j], gdn_a_log[j],
                            gdn_dt_bias[j], gdn_norm_w[j], gdn_w_out[j], B, S)
        else:
            x2 = _dswa_layer(x2, norm_mix[i], dswa_w_in[j], dswa_w_out[j], bias, B, S)
        x2 = _mlp(x2, norm_mlp[i], mlp_w1[i].astype(BF16), mlp_w2[i].astype(BF16),
                  norm_final, final=(i == depth - 1))
    return x2.reshape(B, S, D)
```

```python
import functools
import math

import jax
import jax.numpy as jnp
from jax import lax
from jax.experimental import pallas as pl
from jax.experimental.pallas import tpu as pltpu

F32 = jnp.float32
BF16 = jnp.bfloat16

RMS_EPS = 1e-6
L2_EPS = 1e-6
NEG_INF = -1e30

LANES = 128
SUBLANES = 8
VMEM_LIMIT = 56 * 1024 * 1024

GDN_DK = 128
GDN_CONV = 5
GDN_CHUNK = LANES
DSWA_HEAD_DIM = 64
DSWA_CONFIGS = ((128, 1), (512, 4), (2048, 16))
DSWA_HALF = 64
REL_BUCKETS = 32
REL_MAX_DIST = 1024


def _params(*sem):
    return pltpu.CompilerParams(dimension_semantics=sem, vmem_limit_bytes=VMEM_LIMIT)


def _dot(a, b):
    return jnp.dot(a.astype(BF16), b.astype(BF16), preferred_element_type=F32)


def _dot_nt(a, b):
    return lax.dot_general(a.astype(BF16), b.astype(BF16), (((1,), (1,)), ((), ())),
                           preferred_element_type=F32)


def _split3(x):
    hi = x.astype(BF16)
    r = x - hi.astype(F32)
    mid = r.astype(BF16)
    return hi, mid, (r - mid.astype(F32)).astype(BF16)


def _rms(x, g):
    return x * lax.rsqrt(jnp.mean(x * x, axis=-1, keepdims=True) + RMS_EPS) * g


def _sigmoid(x):
    return 1.0 / (1.0 + jnp.exp(-x))


def _softplus(x):
    return jnp.maximum(x, 0.0) + jnp.log1p(jnp.exp(-jnp.abs(x)))


CONV_HALO = 16
CONV_BLOCK = 64


def _gdn_proj_kernel(x_ref, xp_ref, xn_ref, g_ref, wqkv_ref, wz_ref, w_ref, wt_ref, cw_ref,
                     prow_ref, pcol_ref, qkv_ref, z_ref, col_ref, row_ref, *, nh, nt, chunk):
    C = GDN_CHUNK
    tm = x_ref.shape[0]
    t = pl.program_id(0) % nt
    pad = GDN_CONV // 2
    g = g_ref[...]
    h = _rms(x_ref[...], g).astype(BF16)
    h_halo = jnp.concatenate([_rms(jnp.where(t > 0, xp_ref[...], 0.0), g).astype(BF16),
                              _rms(jnp.where(t < nt - 1, xn_ref[...], 0.0), g).astype(BF16)], axis=0)
    CB = CONV_BLOCK
    taps = [i for i in range(GDN_CONV) if i != pad]
    sr = lax.broadcasted_iota(jnp.int32, (len(taps) * CB, CB + 2 * CONV_HALO), 0)
    sc = lax.broadcasted_iota(jnp.int32, (len(taps) * CB, CB + 2 * CONV_HALO), 1)
    src = (sr % CB) + (CONV_HALO - pad)
    for k, i in enumerate(taps):
        src = src + jnp.where(sr // CB == k, i, 0)
    shift_mat = (sc == src).astype(BF16)

    def project(c0):
        w = wqkv_ref[:, c0:c0 + chunk]
        main = jnp.dot(h, w, preferred_element_type=F32)
        halo = jnp.dot(h_halo, w, preferred_element_type=F32)
        ext = jnp.concatenate([halo[:CONV_HALO], main, halo[CONV_HALO:]], axis=0).astype(BF16)
        return main, ext

    starts = list(range(0, wqkv_ref.shape[1], chunk))
    nxt = project(starts[0])
    for n, c0 in enumerate(starts):
        cs = slice(c0, c0 + chunk)
        main, ext = nxt
        if n + 1 < len(starts):
            nxt = project(starts[n + 1])
        for b in range(tm // CB):
            rows = slice(b * CB, (b + 1) * CB)
            shifted = jnp.dot(shift_mat, ext[b * CB:(b + 1) * CB + 2 * CONV_HALO],
                              preferred_element_type=F32)
            acc = cw_ref[pad:pad + 1, cs] * main[rows]
            for k, i in enumerate(taps):
                acc = acc + cw_ref[i:i + 1, cs] * shifted[k * CB:(k + 1) * CB]
            y = acc * _sigmoid(acc)
            for j in range(chunk // LANES):
                yj = y[:, j * LANES:(j + 1) * LANES]
                head = c0 // LANES + j
                if head < 2 * nh:
                    inv = lax.rsqrt(jnp.sum(yj * yj, axis=-1, keepdims=True) + L2_EPS)
                    yj = yj * (inv * (GDN_DK ** -0.5) if head < nh else inv)
                qkv_ref[rows, c0 + j * LANES:c0 + (j + 1) * LANES] = yj.astype(BF16)
    for c0 in range(0, wz_ref.shape[1], chunk):
        cs = slice(c0, c0 + chunk)
        z_ref[:, cs] = jnp.dot(h, wz_ref[:, cs], preferred_element_type=F32).astype(BF16)
    a_c = jnp.dot(h, w_ref[...], preferred_element_type=F32)
    a_r = lax.dot_general(wt_ref[...], h, (((1,), (1,)), ((), ())),
                          preferred_element_type=F32)
    ri = lax.broadcasted_iota(jnp.int32, (C, C), 0)
    ci = lax.broadcasted_iota(jnp.int32, (C, C), 1)
    lower = (ri >= ci).astype(BF16)
    upper = (ri <= ci).astype(BF16)
    g_c = -jnp.exp(prow_ref[0:1, :]) * _softplus(a_c + prow_ref[1:2, :])
    g_r = -jnp.exp(pcol_ref[:, 0:1]) * _softplus(a_r + pcol_ref[:, 1:2])
    beta_c = _sigmoid(a_c)
    beta_r = _sigmoid(a_r)
    for c in range(tm // C):
        sl = slice(c * C, (c + 1) * C)
        gc = g_c[sl, :]
        pre = sum(jnp.dot(lower, part, preferred_element_type=F32) for part in _split3(gc))
        cum = jnp.where(ci < nh, pre, pre[C - 1:C, :] - pre + gc)
        col_ref[sl, :] = jnp.where(ci < 2 * nh, cum, beta_c[sl, :])
        gr = g_r[:, sl]
        pre_r = sum(jnp.dot(part, upper, preferred_element_type=F32) for part in _split3(gr))
        cum_r = jnp.where(ri < nh, pre_r, pre_r[:, C - 1:C] - pre_r + gr)
        full = jnp.where(ri < 2 * nh, cum_r, beta_r[:, sl])
        row_ref[0, :, sl] = full[:4 * nh, :]


def _gdn_proj(x2, g, w_in, conv_w, a_log, dt_bias, B, S, tm=512, chunk=256):
    T, D = x2.shape
    nh = a_log.shape[1]
    qkv_w = 3 * nh * LANES
    z_w = nh * LANES
    w_pad = jnp.zeros((D, LANES), F32).at[:, :4 * nh].set(w_in[:, qkv_w + z_w:])
    alog = jnp.zeros((LANES,), F32).at[:2 * nh].set(a_log.reshape(-1))
    dtb = jnp.zeros((LANES,), F32).at[:2 * nh].set(dt_bias.reshape(-1))
    prow = jnp.zeros((SUBLANES, LANES), F32).at[0].set(alog).at[1].set(dtb)
    pcol = jnp.zeros((LANES, LANES), F32).at[:, 0].set(alog).at[:, 1].set(dtb)
    nt = S // tm
    r = tm // CONV_HALO
    const = lambda shape: pl.BlockSpec(shape, lambda i: (0, 0))
    return pl.pallas_call(
        functools.partial(_gdn_proj_kernel, nh=nh, nt=nt, chunk=chunk),
        grid=(T // tm,),
        in_specs=[pl.BlockSpec((tm, D), lambda i: (i, 0)),
                  pl.BlockSpec((CONV_HALO, D), lambda i: (jnp.maximum(i * r - 1, 0), 0)),
                  pl.BlockSpec((CONV_HALO, D), lambda i: (jnp.minimum((i + 1) * r, T // CONV_HALO - 1), 0)),
                  const((1, D)), const((D, qkv_w)), const((D, z_w)), const((D, LANES)), const((LANES, D)),
                  const((GDN_CONV, qkv_w)), const((SUBLANES, LANES)), const((LANES, LANES))],
        out_specs=[pl.BlockSpec((tm, qkv_w), lambda i: (i, 0)),
                   pl.BlockSpec((tm, z_w), lambda i: (i, 0)),
                   pl.BlockSpec((tm, LANES), lambda i: (i, 0)),
                   pl.BlockSpec((1, 4 * nh, tm), lambda i: (i // nt, 0, i % nt))],
        out_shape=[jax.ShapeDtypeStruct((T, qkv_w), BF16),
                   jax.ShapeDtypeStruct((T, z_w), BF16),
                   jax.ShapeDtypeStruct((T, LANES), F32),
                   jax.ShapeDtypeStruct((B, 4 * nh, S), F32)],
        compiler_params=_params("parallel"),
        name="gdn_proj",
    )(x2, x2, x2, g.reshape(1, D), w_in[:, :qkv_w].astype(BF16), w_in[:, qkv_w:qkv_w + z_w].astype(BF16),
      w_pad.astype(BF16), w_pad.T.astype(BF16), conv_w.reshape(GDN_CONV, qkv_w), prow, pcol)


def _gdn_scan_kernel(q_ref, k_ref, v_ref, col_ref, row_ref, o_ref, state_ref, *, rev, nh):
    C = GDN_CHUNK
    t = pl.program_id(1)

    @pl.when(t == 0)
    def _():
        state_ref[...] = jnp.zeros_like(state_ref)

    ri = lax.broadcasted_iota(jnp.int32, (C, C), 0)
    ci = lax.broadcasted_iota(jnp.int32, (C, C), 1)
    incl = (ri <= ci) if rev else (ri >= ci)
    strict = (ri < ci) if rev else (ri > ci)
    eye = (ri == ci).astype(F32)
    levels = [((ri >> (lg + 1)) == (ci >> (lg + 1))) & ((ri >> lg) != (ci >> lg))
              for lg in range(int(math.log2(C)))]
    last = 0 if rev else C - 1
    nc = q_ref.shape[1] // C
    chunks = list(reversed(range(nc))) if rev else list(range(nc))
    heads = range(nh)
    hs = [slice(h * LANES, (h + 1) * LANES) for h in heads]
    gi = [(nh if rev else 0) + h for h in heads]
    units = [(c, h) for c in chunks for h in heads]
    rows = {c: slice(c * C, (c + 1) * C) for c in chunks}
    col = {c: col_ref[0, rows[c], :] for c in chunks}
    row = {c: row_ref[0, :, rows[c]] for c in chunks}
    kf = {u: k_ref[0, rows[u[0]], hs[u[1]]] for u in units}
    k = {u: kf[u].astype(F32) for u in units}
    g_col = {(c, h): col[c][:, gi[h]:gi[h] + 1] for c, h in units}
    b_col = {(c, h): col[c][:, 2 * nh + gi[h]:2 * nh + gi[h] + 1] for c, h in units}
    g_last = {u: g_col[u][last:last + 1, :] for u in units}
    decay = {(c, h): jnp.where(incl, jnp.exp(jnp.where(incl, g_col[c, h] - row[c][gi[h]:gi[h] + 1, :], 0.0)), 0.0)
             for c, h in units}
    kb = {u: k[u] * b_col[u] for u in units}
    res = {(c, h): _dot_nt(jnp.concatenate([kb[c, h].astype(BF16), q_ref[0, rows[c], hs[h]]], axis=0), kf[c, h])
           for c, h in units}
    a = {u: jnp.where(strict, res[u][:C] * decay[u], 0.0) for u in units}
    intra = {u: (res[u][C:] * decay[u]).astype(BF16) for u in units}
    tinv = {u: eye - jnp.where(levels[0], a[u], 0.0) for u in units}
    for m in levels[1:]:
        tb = {u: tinv[u].astype(BF16) for u in units}
        pm = {u: _dot(tb[u], jnp.where(m, a[u], 0.0)) for u in units}
        tinv = {u: tinv[u] - _dot(pm[u], tb[u]) for u in units}
    e_col = {u: jnp.exp(g_col[u]) for u in units}
    sol = {(c, h): _dot(tinv[c, h], jnp.concatenate(
        [v_ref[0, rows[c], hs[h]].astype(F32) * b_col[c, h], kb[c, h] * e_col[c, h]], axis=1)) for c, h in units}
    qd = {(c, h): (q_ref[0, rows[c], hs[h]].astype(F32) * e_col[c, h]).astype(BF16) for c, h in units}
    k_dec = {u: (k[u] * jnp.exp(g_last[u] - g_col[u])).T.astype(BF16) for u in units}
    state = [state_ref[h] for h in heads]
    for c in chunks:
        sb = [state[h].astype(BF16) for h in heads]
        r2 = [_dot(jnp.concatenate([sol[c, h][:, LANES:].astype(BF16), qd[c, h]], axis=0), sb[h]) for h in heads]
        v_new = [(sol[c, h][:, :LANES] - r2[h][:C]).astype(BF16) for h in heads]
        for h in heads:
            o_ref[0, rows[c], hs[h]] = (r2[h][C:] + _dot(intra[c, h], v_new[h])).astype(o_ref.dtype)
        state = [state[h] * jnp.exp(g_last[c, h]) + _dot(k_dec[c, h], v_new[h]) for h in heads]
    for h in heads:
        state_ref[h] = state[h]


def _gdn_scan(qkvn, col, row, nh, rev, nc=2):
    B, S, _ = qkvn.shape
    ts = nc * GDN_CHUNK
    nt = S // ts
    width = nh * LANES
    tmap = (lambda t: nt - 1 - t) if rev else (lambda t: t)
    col3 = col.reshape(B, S, LANES)
    return pl.pallas_call(
        functools.partial(_gdn_scan_kernel, rev=rev, nh=nh),
        grid=(B, nt),
        in_specs=[pl.BlockSpec((1, ts, width), lambda b, t: (b, tmap(t), 0)),
                  pl.BlockSpec((1, ts, width), lambda b, t: (b, tmap(t), 1)),
                  pl.BlockSpec((1, ts, width), lambda b, t: (b, tmap(t), 2)),
                  pl.BlockSpec((1, ts, LANES), lambda b, t: (b, tmap(t), 0)),
                  pl.BlockSpec((1, 4 * nh, ts), lambda b, t: (b, 0, tmap(t)))],
        out_specs=pl.BlockSpec((1, ts, width), lambda b, t: (b, tmap(t), 0)),
        out_shape=jax.ShapeDtypeStruct((B, S, width), BF16),
        scratch_shapes=[pltpu.VMEM((nh, GDN_DK, LANES), F32)],
        compiler_params=_params("parallel", "arbitrary"),
        name="gdn_scan_bwd" if rev else "gdn_scan_fwd",
    )(qkvn, qkvn, qkvn, col3, row)


def _gdn_out_kernel(of_ref, ob_ref, z_ref, x_ref, nw_ref, w_ref, o_ref, y_ref, *, nh):
    for h in range(nh):
        hs = slice(h * LANES, (h + 1) * LANES)
        o = of_ref[:, hs].astype(F32) + ob_ref[:, hs].astype(F32)
        o = o * lax.rsqrt(jnp.mean(o * o, axis=-1, keepdims=True) + RMS_EPS)
        z = z_ref[:, hs].astype(F32)
        y_ref[:, hs] = (o * nw_ref[...] * (z * _sigmoid(z))).astype(BF16)
    o_ref[...] = x_ref[...] + jnp.dot(y_ref[...], w_ref[...], preferred_element_type=F32)


def _gdn_out(o_f, o_b, z, x2, norm_w, w_out, nh, tm=512):
    T, D = x2.shape
    W = o_f.shape[1]
    tok = lambda n: pl.BlockSpec((tm, n), lambda i: (i, 0))
    return pl.pallas_call(
        functools.partial(_gdn_out_kernel, nh=nh),
        grid=(T // tm,),
        in_specs=[tok(W), tok(W), tok(W), tok(D),
                  pl.BlockSpec((1, LANES), lambda i: (0, 0)),
                  pl.BlockSpec((W, D), lambda i: (0, 0))],
        out_specs=tok(D),
        out_shape=jax.ShapeDtypeStruct((T, D), F32),
        scratch_shapes=[pltpu.VMEM((tm, W), BF16)],
        compiler_params=_params("parallel"),
        name="gdn_out",
    )(o_f, o_b, z, x2, norm_w.reshape(1, LANES), w_out)


def _dswa_kernel(q_ref, k_ref, kp_ref, kn_ref, v_ref, vp_ref, vn_ref, bias_ref,
                 o_ref, lse_ref, *, sub_len):
    t = pl.program_id(2)
    lt = q_ref.shape[2]
    hb = DSWA_HALF
    q = q_ref[0, 0] * (DSWA_HEAD_DIM ** -0.5)
    kext = jnp.concatenate([kp_ref[0, 0], k_ref[0, 0], kn_ref[0, 0]], axis=0)
    vext = jnp.concatenate([vp_ref[0, 0], v_ref[0, 0], vn_ref[0, 0]], axis=0)
    first_head = lax.broadcasted_iota(jnp.int32, (1, LANES), 1) < DSWA_HEAD_DIM
    n_pairs = q_ref.shape[3] // LANES
    blocks = range(lt // hb)
    units = [(qb, pr) for qb in blocks for pr in range(n_pairs)]
    rows = [slice(qb * hb, (qb + 1) * hb) for qb in blocks]
    win = [slice(qb * hb, (qb + 3) * hb) for qb in blocks]
    ls = [slice(pr * LANES, (pr + 1) * LANES) for pr in range(n_pairs)]
    kvalid = []
    for qb in blocks:
        kpos = t * lt + (qb - 1) * hb + lax.broadcasted_iota(jnp.int32, (1, 3 * hb), 1)
        kvalid.append((kpos >= 0) & (kpos < sub_len))
    qs = {(qb, pr): jnp.concatenate([jnp.where(first_head, q[rows[qb], ls[pr]], 0.0),
                                     jnp.where(first_head, 0.0, q[rows[qb], ls[pr]])], axis=0)
          for qb, pr in units}
    s = {(qb, pr): _dot_nt(qs[qb, pr], kext[win[qb], ls[pr]]) for qb, pr in units}
    s = {(qb, pr): jnp.where(kvalid[qb], s[qb, pr] + bias_ref[pr], NEG_INF) for qb, pr in units}
    m = {u: jnp.max(s[u], axis=-1, keepdims=True) for u in units}
    p = {u: jnp.exp(s[u] - m[u]) for u in units}
    l = {u: jnp.sum(p[u], axis=-1, keepdims=True) for u in units}
    pv = {(qb, pr): _dot(p[qb, pr], vext[win[qb], ls[pr]]) * (1.0 / l[qb, pr]) for qb, pr in units}
    lse = {u: m[u] + jnp.log(l[u]) for u in units}
    for qb, pr in units:
        o_ref[0, 0, rows[qb], ls[pr]] = jnp.where(first_head, pv[qb, pr][:hb], pv[qb, pr][hb:])
        lse_ref[0, 0, rows[qb], ls[pr]] = jnp.where(first_head, lse[qb, pr][:hb], lse[qb, pr][hb:])


def _rel_bucket(rel):
    nb = REL_BUCKETS // 2
    max_exact = nb // 2
    ret = jnp.where(rel > 0, nb, 0)
    n = jnp.abs(rel)
    nf = jnp.maximum(n, 1).astype(F32)
    large = max_exact + (jnp.log(nf * (1.0 / max_exact)) / math.log(REL_MAX_DIST / max_exact)
                         * (nb - max_exact)).astype(jnp.int32)
    large = jnp.minimum(large, nb - 1)
    return ret + jnp.where(n < max_exact, n, large)


def _band_bias_kernel(tab_ref, o_ref, *, dils, hg):
    hb = DSWA_HALF
    off = (lax.broadcasted_iota(jnp.int32, (hb, 3 * hb), 1) - hb
           - lax.broadcasted_iota(jnp.int32, (hb, 3 * hb), 0))
    inband = jnp.abs(off) <= hb
    for g, d in enumerate(dils):
        bucket = _rel_bucket(off * d)
        for hh in range(hg):
            acc = jnp.zeros((hb, 3 * hb), F32)
            for b in range(REL_BUCKETS):
                acc = jnp.where(bucket == b, tab_ref[b, g * hg + hh], acc)
            o_ref[g, hh // 2, (hh % 2) * hb:(hh % 2 + 1) * hb, :] = jnp.where(inband, acc, NEG_INF)


def _band_bias(rel_table, dils):
    hb = DSWA_HALF
    hg = rel_table.shape[1] // len(dils)
    return pl.pallas_call(
        functools.partial(_band_bias_kernel, dils=dils, hg=hg),
        in_specs=[pl.BlockSpec(memory_space=pltpu.SMEM)],
        out_shape=jax.ShapeDtypeStruct((len(dils), hg // 2, 2 * hb, 3 * hb), F32),
        name="band_bias",
    )(rel_table)


def _dswa_group(qkv, bias, lt=256):
    B, dil, L, W3 = qkv.shape
    hb = DSWA_HALF
    gw = W3 // 3
    lt = min(lt, L)
    r = lt // hb
    main = lambda which: pl.BlockSpec((1, 1, lt, gw), lambda b, d, t: (b, d, t, which))
    prev = lambda which: pl.BlockSpec(
        (1, 1, hb, gw), lambda b, d, t: (b, d, jnp.maximum(t * r - 1, 0), which))
    nxt = lambda which: pl.BlockSpec(
        (1, 1, hb, gw), lambda b, d, t: (b, d, jnp.minimum((t + 1) * r, L // hb - 1), which))
    out = pl.BlockSpec((1, 1, lt, gw), lambda b, d, t: (b, d, t, 0))
    return pl.pallas_call(
        functools.partial(_dswa_kernel, sub_len=L),
        grid=(B, dil, L // lt),
        in_specs=[main(0), main(1), prev(1), nxt(1), main(2), prev(2), nxt(2),
                  pl.BlockSpec(bias.shape, lambda b, d, t: (0, 0, 0))],
        out_specs=[out, out],
        out_shape=[jax.ShapeDtypeStruct((B, dil, L, gw), F32)] * 2,
        compiler_params=_params("parallel", "parallel", "parallel"),
        name=f"dswa_d{dil}",
    )(qkv, qkv, qkv, qkv, qkv, qkv, qkv, bias)


def _dswa_out_kernel(*refs, dils):
    ng = len(dils)
    o_refs, l_refs = refs[:ng], refs[ng:2 * ng]
    x_ref, w_ref, out_ref, o_s, l_s, y_s = refs[2 * ng:]
    tm = x_ref.shape[0]
    tiles = o_refs[0].shape[3] // LANES
    for g, d in enumerate(dils):
        for r in range(d):
            for j in range(tiles):
                ls = slice(j * LANES, (j + 1) * LANES)
                o_s[g * tiles + j, pl.ds(r, tm // d, stride=d), :] = o_refs[g][0, r, :, ls]
                l_s[g * tiles + j, pl.ds(r, tm // d, stride=d), :] = l_refs[g][0, r, :, ls]
    for j in range(tiles):
        lses = [l_s[g * tiles + j] for g in range(ng)]
        m = functools.reduce(jnp.maximum, lses)
        es = [jnp.exp(l - m) for l in lses]
        inv = 1.0 / functools.reduce(lambda a, b: a + b, es)
        for g in range(ng):
            c = g * tiles + j
            y_s[:, c * LANES:(c + 1) * LANES] = (o_s[c] * (es[g] * inv)).astype(BF16)
    out_ref[...] = x_ref[...] + jnp.dot(y_s[...], w_ref[...], preferred_element_type=F32)


def _dswa_out(os_, lses, x2, w_out, S, tm=512):
    T, D = x2.shape
    dils = tuple(o.shape[1] for o in os_)
    gw = os_[0].shape[3]
    nt = S // tm
    grp = lambda d: pl.BlockSpec((1, d, tm // d, gw), lambda i: (i // nt, 0, i % nt, 0))
    tok = lambda n: pl.BlockSpec((tm, n), lambda i: (i, 0))
    return pl.pallas_call(
        functools.partial(_dswa_out_kernel, dils=dils),
        grid=(T // tm,),
        in_specs=[grp(d) for d in dils] * 2 + [tok(D), pl.BlockSpec(w_out.shape, lambda i: (0, 0))],
        out_specs=tok(D),
        out_shape=jax.ShapeDtypeStruct((T, D), F32),
        scratch_shapes=[pltpu.VMEM((len(dils) * gw // LANES, tm, LANES), F32),
                        pltpu.VMEM((len(dils) * gw // LANES, tm, LANES), F32),
                        pltpu.VMEM((tm, len(dils) * gw), BF16)],
        compiler_params=_params("parallel"),
        name="dswa_out",
    )(*os_, *lses, x2, w_out)


def _mlp_kernel(x_ref, g_ref, w1_ref, w2_ref, gf_ref, o_ref, acc_ref, *, chunk, final):
    x = x_ref[...]
    h = _rms(x, g_ref[...]).astype(BF16)
    dff = w1_ref.shape[1]
    for c in range(0, dff, chunk):
        a = jnp.maximum(jnp.dot(h, w1_ref[:, c:c + chunk], preferred_element_type=F32), 0.0)
        part = jnp.dot((a * a).astype(BF16), w2_ref[c:c + chunk, :], preferred_element_type=F32)
        if c == 0:
            acc_ref[...] = part
        else:
            acc_ref[...] += part
    y = x + acc_ref[...]
    if final:
        y = _rms(y, gf_ref[...])
    o_ref[...] = y


def _mlp(x2, g, w1, w2, g_final, final, tm=512, chunk=512):
    T, D = x2.shape
    dff = w1.shape[1]
    return pl.pallas_call(
        functools.partial(_mlp_kernel, chunk=chunk, final=final),
        grid=(T // tm,),
        in_specs=[pl.BlockSpec((tm, D), lambda i: (i, 0)),
                  pl.BlockSpec((1, D), lambda i: (0, 0)),
                  pl.BlockSpec((D, dff), lambda i: (0, 0), pipeline_mode=pl.Buffered(1)),
                  pl.BlockSpec((dff, D), lambda i: (0, 0), pipeline_mode=pl.Buffered(1)),
                  pl.BlockSpec((1, D), lambda i: (0, 0))],
        out_specs=pl.BlockSpec((tm, D), lambda i: (i, 0)),
        out_shape=jax.ShapeDtypeStruct((T, D), F32),
        scratch_shapes=[pltpu.VMEM((tm, D), F32)],
        compiler_params=_params("parallel"),
        name="mlp",
    )(x2, g.reshape(1, D), w1, w2, g_final.reshape(1, D))


def _gdn_layer(x2, g, w_in, conv_w, a_log, dt_bias, norm_w, w_out, B, S):
    nh = a_log.shape[1]
    z_w = nh * LANES
    qkvn, z, col, row = _gdn_proj(x2, g, w_in, conv_w, a_log, dt_bias, B, S)
    qkvn = qkvn.reshape(B, S, 3 * z_w)
    o_f = _gdn_scan(qkvn, col, row, nh, rev=False)
    o_b = _gdn_scan(qkvn, col, row, nh, rev=True)
    return _gdn_out(o_f.reshape(B * S, z_w), o_b.reshape(B * S, z_w), z, x2, norm_w,
                    w_out.astype(BF16), nh)


def _dswa_proj_kernel(x_ref, g_ref, w_ref, *refs, dils, chunk):
    o_refs, acc_ref = refs[:-1], refs[-1]
    tm = x_ref.shape[0]
    n = w_ref.shape[1]
    ng = len(dils)
    gw = n // (3 * ng)
    tiles = gw // LANES
    h = _rms(x_ref[...], g_ref[...]).astype(BF16)
    for n0 in range(0, n, chunk):
        res = jnp.dot(h, w_ref[:, n0:n0 + chunk], preferred_element_type=F32)
        for j in range(chunk // LANES):
            acc_ref[n0 // LANES + j] = res[:, j * LANES:(j + 1) * LANES]
    for g, d in enumerate(dils):
        for r in range(d):
            for which in range(3):
                for j in range(tiles):
                    dst = slice((which * tiles + j) * LANES, (which * tiles + j + 1) * LANES)
                    o_refs[g][0, r, :, dst] = \
                        acc_ref[(which * ng + g) * tiles + j, pl.ds(r, tm // d, stride=d), :].astype(BF16)


def _dswa_proj(x2, g, w_in, dils, B, S, tm=512):
    T, D = x2.shape
    n = w_in.shape[1]
    gw3 = n // len(dils)
    nt = S // tm
    return pl.pallas_call(
        functools.partial(_dswa_proj_kernel, dils=dils, chunk=gw3),
        grid=(T // tm,),
        in_specs=[pl.BlockSpec((tm, D), lambda i: (i, 0)),
                  pl.BlockSpec((1, D), lambda i: (0, 0)),
                  pl.BlockSpec((D, n), lambda i: (0, 0))],
        out_specs=[pl.BlockSpec((1, d, tm // d, gw3), lambda i: (i // nt, 0, i % nt, 0)) for d in dils],
        out_shape=[jax.ShapeDtypeStruct((B, d, S // d, gw3), BF16) for d in dils],
        scratch_shapes=[pltpu.VMEM((n // LANES, tm, LANES), F32)],
        compiler_params=_params("parallel"),
        name="dswa_proj",
    )(x2, g.reshape(1, D), w_in)


def _dswa_layer(x2, g, w_in, w_out, bias, B, S):
    dils = tuple(d for _, d in DSWA_CONFIGS)
    qkvs = _dswa_proj(x2, g, w_in.astype(BF16), dils, B, S)
    os_, lses = [], []
    for gi in range(len(dils)):
        o, lse = _dswa_group(qkvs[gi], bias[gi])
        os_.append(o)
        lses.append(lse)
    return _dswa_out(os_, lses, x2, w_out.astype(BF16), S)


def kernel(x, norm_mix, norm_mlp, norm_final, rel_bias, gdn_w_in, gdn_conv_w, gdn_a_log,
           gdn_dt_bias, gdn_norm_w, gdn_w_out, dswa_w_in, dswa_w_out, mlp_w1, mlp_w2):
    B, S, D = x.shape
    depth = norm_mix.shape[0]
    x2 = x.reshape(B * S, D)
    assert all(window // (2 * dil) == DSWA_HALF for window, dil in DSWA_CONFIGS)
    bias = _band_bias(rel_bias, tuple(d for _, d in DSWA_CONFIGS))
    for i in range(depth):
        j = i // 2
        if i % 2 == 0:
            x2 = _gdn_layer(x2, norm_mix[i], gdn_w_in[j], gdn_conv_w[j], gdn_a_log[j],
                            gdn_dt_bias[j], gdn_norm_w[j], gdn_w_out[j], B, S)
        else:
            x2 = _dswa_layer(x2, norm_mix[i], dswa_w_in[j], dswa_w_out[j], bias, B, S)
        x2 = _mlp(x2, norm_mlp[i], mlp_w1[i].astype(BF16), mlp_w2[i].astype(BF16),
                  norm_final, final=(i == depth - 1))
    return x2.reshape(B, S, D)
```

```python
import functools
import math

import jax
import jax.numpy as jnp
from jax import lax
from jax.experimental import pallas as pl
from jax.experimental.pallas import tpu as pltpu

F32 = jnp.float32
BF16 = jnp.bfloat16

RMS_EPS = 1e-6
L2_EPS = 1e-6
NEG_INF = -1e30

LANES = 128
SUBLANES = 8
VMEM_LIMIT = 56 * 1024 * 1024

GDN_DK = 128
GDN_CONV = 5
GDN_CHUNK = LANES
DSWA_HEAD_DIM = 64
DSWA_CONFIGS = ((128, 1), (512, 4), (2048, 16))
DSWA_HALF = 64
REL_BUCKETS = 32
REL_MAX_DIST = 1024


def _params(*sem):
    return pltpu.CompilerParams(dimension_semantics=sem, vmem_limit_bytes=VMEM_LIMIT)


def _dot(a, b):
    return jnp.dot(a.astype(BF16), b.astype(BF16), preferred_element_type=F32)


def _dot_nt(a, b):
    return lax.dot_general(a.astype(BF16), b.astype(BF16), (((1,), (1,)), ((), ())),
                           preferred_element_type=F32)


def _split3(x):
    hi = x.astype(BF16)
    r = x - hi.astype(F32)
    mid = r.astype(BF16)
    return hi, mid, (r - mid.astype(F32)).astype(BF16)


def _rms(x, g):
    return x * lax.rsqrt(jnp.mean(x * x, axis=-1, keepdims=True) + RMS_EPS) * g


def _sigmoid(x):
    return 1.0 / (1.0 + jnp.exp(-x))


def _softplus(x):
    return jnp.maximum(x, 0.0) + jnp.log1p(jnp.exp(-jnp.abs(x)))


CONV_HALO = 16
CONV_BLOCK = 64


def _gdn_proj_kernel(x_ref, xp_ref, xn_ref, g_ref, wqkv_ref, wz_ref, w_ref, wt_ref, cw_ref,
                     prow_ref, pcol_ref, qkv_ref, z_ref, col_ref, row_ref, *, nh, nt, chunk):
    C = GDN_CHUNK
    tm = x_ref.shape[0]
    t = pl.program_id(0) % nt
    pad = GDN_CONV // 2
    g = g_ref[...]
    h = _rms(x_ref[...], g).astype(BF16)
    h_halo = jnp.concatenate([_rms(jnp.where(t > 0, xp_ref[...], 0.0), g).astype(BF16),
                              _rms(jnp.where(t < nt - 1, xn_ref[...], 0.0), g).astype(BF16)], axis=0)
    CB = CONV_BLOCK
    taps = [i for i in range(GDN_CONV) if i != pad]
    sr = lax.broadcasted_iota(jnp.int32, (len(taps) * CB, CB + 2 * CONV_HALO), 0)
    sc = lax.broadcasted_iota(jnp.int32, (len(taps) * CB, CB + 2 * CONV_HALO), 1)
    src = (sr % CB) + (CONV_HALO - pad)
    for k, i in enumerate(taps):
        src = src + jnp.where(sr // CB == k, i, 0)
    shift_mat = (sc == src).astype(BF16)

    def project(c0):
        w = wqkv_ref[:, c0:c0 + chunk]
        main = jnp.dot(h, w, preferred_element_type=F32)
        halo = jnp.dot(h_halo, w, preferred_element_type=F32)
        ext = jnp.concatenate([halo[:CONV_HALO], main, halo[CONV_HALO:]], axis=0).astype(BF16)
        return main, ext

    starts = list(range(0, wqkv_ref.shape[1], chunk))
    nxt = project(starts[0])
    for n, c0 in enumerate(starts):
        cs = slice(c0, c0 + chunk)
        main, ext = nxt
        if n + 1 < len(starts):
            nxt = project(starts[n + 1])
        for b in range(tm // CB):
            rows = slice(b * CB, (b + 1) * CB)
            shifted = jnp.dot(shift_mat, ext[b * CB:(b + 1) * CB + 2 * CONV_HALO],
                              preferred_element_type=F32)
            acc = cw_ref[pad:pad + 1, cs] * main[rows]
            for k, i in enumerate(taps):
                acc = acc + cw_ref[i:i + 1, cs] * shifted[k * CB:(k + 1) * CB]
            y = acc * _sigmoid(acc)
            for j in range(chunk // LANES):
                yj = y[:, j * LANES:(j + 1) * LANES]
                head = c0 // LANES + j
                if head < 2 * nh:
                    inv = lax.rsqrt(jnp.sum(yj * yj, axis=-1, keepdims=True) + L2_EPS)
                    yj = yj * (inv * (GDN_DK ** -0.5) if head < nh else inv)
                qkv_ref[rows, c0 + j * LANES:c0 + (j + 1) * LANES] = yj.astype(BF16)
    for c0 in range(0, wz_ref.shape[1], chunk):
        cs = slice(c0, c0 + chunk)
        z_ref[:, cs] = jnp.dot(h, wz_ref[:, cs], preferred_element_type=F32).astype(BF16)
    a_c = jnp.dot(h, w_ref[...], preferred_element_type=F32)
    a_r = lax.dot_general(wt_ref[...], h, (((1,), (1,)), ((), ())),
                          preferred_element_type=F32)
    ri = lax.broadcasted_iota(jnp.int32, (C, C), 0)
    ci = lax.broadcasted_iota(jnp.int32, (C, C), 1)
    lower = (ri >= ci).astype(BF16)
    upper = (ri <= ci).astype(BF16)
    g_c = -jnp.exp(prow_ref[0:1, :]) * _softplus(a_c + prow_ref[1:2, :])
    g_r = -jnp.exp(pcol_ref[:, 0:1]) * _softplus(a_r + pcol_ref[:, 1:2])
    beta_c = _sigmoid(a_c)
    beta_r = _sigmoid(a_r)
    for c in range(tm // C):
        sl = slice(c * C, (c + 1) * C)
        gc = g_c[sl, :]
        pre = sum(jnp.dot(lower, part, preferred_element_type=F32) for part in _split3(gc))
        cum = jnp.where(ci < nh, pre, pre[C - 1:C, :] - pre + gc)
        col_ref[sl, :] = jnp.where(ci < 2 * nh, cum, beta_c[sl, :])
        gr = g_r[:, sl]
        pre_r = sum(jnp.dot(part, upper, preferred_element_type=F32) for part in _split3(gr))
        cum_r = jnp.where(ri < nh, pre_r, pre_r[:, C - 1:C] - pre_r + gr)
        full = jnp.where(ri < 2 * nh, cum_r, beta_r[:, sl])
        row_ref[0, :, sl] = full[:4 * nh, :]


def _gdn_proj(x2, g, w_in, conv_w, a_log, dt_bias, B, S, tm=512, chunk=256):
    T, D = x2.shape
    nh = a_log.shape[1]
    qkv_w = 3 * nh * LANES
    z_w = nh * LANES
    w_pad = jnp.zeros((D, LANES), F32).at[:, :4 * nh].set(w_in[:, qkv_w + z_w:])
    alog = jnp.zeros((LANES,), F32).at[:2 * nh].set(a_log.reshape(-1))
    dtb = jnp.zeros((LANES,), F32).at[:2 * nh].set(dt_bias.reshape(-1))
    prow = jnp.zeros((SUBLANES, LANES), F32).at[0].set(alog).at[1].set(dtb)
    pcol = jnp.zeros((LANES, LANES), F32).at[:, 0].set(alog).at[:, 1].set(dtb)
    nt = S // tm
    r = tm // CONV_HALO
    const = lambda shape: pl.BlockSpec(shape, lambda i: (0, 0))
    return pl.pallas_call(
        functools.partial(_gdn_proj_kernel, nh=nh, nt=nt, chunk=chunk),
        grid=(T // tm,),
        in_specs=[pl.BlockSpec((tm, D), lambda i: (i, 0)),
                  pl.BlockSpec((CONV_HALO, D), lambda i: (jnp.maximum(i * r - 1, 0), 0)),
                  pl.BlockSpec((CONV_HALO, D), lambda i: (jnp.minimum((i + 1) * r, T // CONV_HALO - 1), 0)),
                  const((1, D)), const((D, qkv_w)), const((D, z_w)), const((D, LANES)), const((LANES, D)),
                  const((GDN_CONV, qkv_w)), const((SUBLANES, LANES)), const((LANES, LANES))],
        out_specs=[pl.BlockSpec((tm, qkv_w), lambda i: (i, 0)),
                   pl.BlockSpec((tm, z_w), lambda i: (i, 0)),
                   pl.BlockSpec((tm, LANES), lambda i: (i, 0)),
                   pl.BlockSpec((1, 4 * nh, tm), lambda i: (i // nt, 0, i % nt))],
        out_shape=[jax.ShapeDtypeStruct((T, qkv_w), BF16),
                   jax.ShapeDtypeStruct((T, z_w), BF16),
                   jax.ShapeDtypeStruct((T, LANES), F32),
                   jax.ShapeDtypeStruct((B, 4 * nh, S), F32)],
        compiler_params=_params("parallel"),
        name="gdn_proj",
    )(x2, x2, x2, g.reshape(1, D), w_in[:, :qkv_w].astype(BF16), w_in[:, qkv_w:qkv_w + z_w].astype(BF16),
      w_pad.astype(BF16), w_pad.T.astype(BF16), conv_w.reshape(GDN_CONV, qkv_w), prow, pcol)


def _gdn_scan_kernel(q_ref, k_ref, v_ref, col_ref, row_ref, o_ref, state_ref, *, rev, nh):
    C = GDN_CHUNK
    t = pl.program_id(1)

    @pl.when(t == 0)
    def _():
        state_ref[...] = jnp.zeros_like(state_ref)

    ri = lax.broadcasted_iota(jnp.int32, (C, C), 0)
    ci = lax.broadcasted_iota(jnp.int32, (C, C), 1)
    incl = (ri <= ci) if rev else (ri >= ci)
    strict = (ri < ci) if rev else (ri > ci)
    eye = (ri == ci).astype(F32)
    levels = [((ri >> (lg + 1)) == (ci >> (lg + 1))) & ((ri >> lg) != (ci >> lg))
              for lg in range(int(math.log2(C)))]
    last = 0 if rev else C - 1
    nc = q_ref.shape[1] // C
    chunks = list(reversed(range(nc))) if rev else list(range(nc))
    heads = range(nh)
    hs = [slice(h * LANES, (h + 1) * LANES) for h in heads]
    gi = [(nh if rev else 0) + h for h in heads]
    units = [(c, h) for c in chunks for h in heads]
    rows = {c: slice(c * C, (c + 1) * C) for c in chunks}
    col = {c: col_ref[0, rows[c], :] for c in chunks}
    row = {c: row_ref[0, :, rows[c]] for c in chunks}
    kf = {u: k_ref[0, rows[u[0]], hs[u[1]]] for u in units}
    k = {u: kf[u].astype(F32) for u in units}
    g_col = {(c, h): col[c][:, gi[h]:gi[h] + 1] for c, h in units}
    b_col = {(c, h): col[c][:, 2 * nh + gi[h]:2 * nh + gi[h] + 1] for c, h in units}
    g_last = {u: g_col[u][last:last + 1, :] for u in units}
    decay = {(c, h): jnp.where(incl, jnp.exp(jnp.where(incl, g_col[c, h] - row[c][gi[h]:gi[h] + 1, :], 0.0)), 0.0)
             for c, h in units}
    kb = {u: k[u] * b_col[u] for u in units}
    res = {(c, h): _dot_nt(jnp.concatenate([kb[c, h].astype(BF16), q_ref[0, rows[c], hs[h]]], axis=0), kf[c, h])
           for c, h in units}
    a = {u: jnp.where(strict, res[u][:C] * decay[u], 0.0) for u in units}
    intra = {u: (res[u][C:] * decay[u]).astype(BF16) for u in units}
    tinv = {u: eye - jnp.where(levels[0], a[u], 0.0) for u in units}
    for m in levels[1:]:
        tb = {u: tinv[u].astype(BF16) for u in units}
        pm = {u: _dot(tb[u], jnp.where(m, a[u], 0.0)) for u in units}
        tinv = {u: tinv[u] - _dot(pm[u], tb[u]) for u in units}
    e_col = {u: jnp.exp(g_col[u]) for u in units}
    sol = {(c, h): _dot(tinv[c, h], jnp.concatenate(
        [v_ref[0, rows[c], hs[h]].astype(F32) * b_col[c, h], kb[c, h] * e_col[c, h]], axis=1)) for c, h in units}
    qd = {(c, h): (q_ref[0, rows[c], hs[h]].astype(F32) * e_col[c, h]).astype(BF16) for c, h in units}
    k_dec = {u: (k[u] * jnp.exp(g_last[u] - g_col[u])).T.astype(BF16) for u in units}
    state = [state_ref[h] for h in heads]
    for c in chunks:
        sb = [state[h].astype(BF16) for h in heads]
        r2 = [_dot(jnp.concatenate([sol[c, h][:, LANES:].astype(BF16), qd[c, h]], axis=0), sb[h]) for h in heads]
        v_new = [(sol[c, h][:, :LANES] - r2[h][:C]).astype(BF16) for h in heads]
        for h in heads:
            o_ref[0, rows[c], hs[h]] = (r2[h][C:] + _dot(intra[c, h], v_new[h])).astype(o_ref.dtype)
        state = [state[h] * jnp.exp(g_last[c, h]) + _dot(k_dec[c, h], v_new[h]) for h in heads]
    for h in heads:
        state_ref[h] = state[h]


def _gdn_scan(qkvn, col, row, nh, rev, nc=2):
    B, S, _ = qkvn.shape
    ts = nc * GDN_CHUNK
    nt = S // ts
    width = nh * LANES
    tmap = (lambda t: nt - 1 - t) if rev else (lambda t: t)
    col3 = col.reshape(B, S, LANES)
    return pl.pallas_call(
        functools.partial(_gdn_scan_kernel, rev=rev, nh=nh),
        grid=(B, nt),
        in_specs=[pl.BlockSpec((1, ts, width), lambda b, t: (b, tmap(t), 0)),
                  pl.BlockSpec((1, ts, width), lambda b, t: (b, tmap(t), 1)),
                  pl.BlockSpec((1, ts, width), lambda b, t: (b, tmap(t), 2)),
                  pl.BlockSpec((1, ts, LANES), lambda b, t: (b, tmap(t), 0)),
                  pl.BlockSpec((1, 4 * nh, ts), lambda b, t: (b, 0, tmap(t)))],
        out_specs=pl.BlockSpec((1, ts, width), lambda b, t: (b, tmap(t), 0)),
        out_shape=jax.ShapeDtypeStruct((B, S, width), BF16),
        scratch_shapes=[pltpu.VMEM((nh, GDN_DK, LANES), F32)],
        compiler_params=_params("parallel", "arbitrary"),
        name="gdn_scan_bwd" if rev else "gdn_scan_fwd",
    )(qkvn, qkvn, qkvn, col3, row)


MLP_CHUNK = 512


def _out_mlp_tail(y_ref, x_ref, wo_ref, g_ref, w1_ref, w2_ref, gf_ref, o_ref, x1_ref, acc_ref, final):
    x1_ref[...] = x_ref[...] + jnp.dot(y_ref[...], wo_ref[...], preferred_element_type=F32)
    h = _rms(x1_ref[...], g_ref[...]).astype(BF16)
    for c in range(0, w1_ref.shape[1], MLP_CHUNK):
        a = jnp.maximum(jnp.dot(h, w1_ref[:, c:c + MLP_CHUNK], preferred_element_type=F32), 0.0)
        part = jnp.dot((a * a).astype(BF16), w2_ref[c:c + MLP_CHUNK, :], preferred_element_type=F32)
        if c == 0:
            acc_ref[...] = part
        else:
            acc_ref[...] += part
    out = x1_ref[...] + acc_ref[...]
    if final:
        out = _rms(out, gf_ref[...])
    o_ref[...] = out


def _out_mlp_call(kernel_fn, mixer_args, mixer_specs, mixer_scratch, y_width, x2, w_out, g, w1, w2,
                  g_final, tm, name):
    T, D = x2.shape
    once = lambda shape: pl.BlockSpec(shape, lambda i: (0, 0), pipeline_mode=pl.Buffered(1))
    return pl.pallas_call(
        kernel_fn,
        grid=(T // tm,),
        in_specs=mixer_specs + [pl.BlockSpec((tm, D), lambda i: (i, 0)), once(w_out.shape), once((1, D)),
                                once(w1.shape), once(w2.shape), once((1, D))],
        out_specs=pl.BlockSpec((tm, D), lambda i: (i, 0)),
        out_shape=jax.ShapeDtypeStruct((T, D), F32),
        scratch_shapes=mixer_scratch + [pltpu.VMEM((tm, y_width), BF16), pltpu.VMEM((tm, D), F32),
                                        pltpu.VMEM((tm, D), F32)],
        compiler_params=_params("parallel"),
        name=name,
    )(*mixer_args, x2, w_out, g.reshape(1, D), w1, w2, g_final.reshape(1, D))


def _gdn_out_mlp_kernel(of_ref, ob_ref, z_ref, nw_ref, x_ref, wo_ref, g_ref, w1_ref, w2_ref, gf_ref,
                        o_ref, y_ref, x1_ref, acc_ref, *, nh, final):
    for h in range(nh):
        hs = slice(h * LANES, (h + 1) * LANES)
        o = of_ref[:, hs].astype(F32) + ob_ref[:, hs].astype(F32)
        o = o * lax.rsqrt(jnp.mean(o * o, axis=-1, keepdims=True) + RMS_EPS)
        z = z_ref[:, hs].astype(F32)
        y_ref[:, hs] = (o * nw_ref[...] * (z * _sigmoid(z))).astype(BF16)
    _out_mlp_tail(y_ref, x_ref, wo_ref, g_ref, w1_ref, w2_ref, gf_ref, o_ref, x1_ref, acc_ref, final)


def _gdn_out_mlp(o_f, o_b, z, x2, norm_w, w_out, nh, g, w1, w2, g_final, final, tm=512):
    W = o_f.shape[1]
    tok = pl.BlockSpec((tm, W), lambda i: (i, 0))
    return _out_mlp_call(
        functools.partial(_gdn_out_mlp_kernel, nh=nh, final=final),
        (o_f, o_b, z, norm_w.reshape(1, LANES)),
        [tok, tok, tok, pl.BlockSpec((1, LANES), lambda i: (0, 0))],
        [], W, x2, w_out, g, w1, w2, g_final, tm, "gdn_out_mlp")


def _dswa_kernel(q_ref, k_ref, kp_ref, kn_ref, v_ref, vp_ref, vn_ref, bias_ref,
                 o_ref, lse_ref, *, sub_len):
    t = pl.program_id(2)
    lt = q_ref.shape[2]
    hb = DSWA_HALF
    q = q_ref[0, 0] * (DSWA_HEAD_DIM ** -0.5)
    kext = jnp.concatenate([kp_ref[0, 0], k_ref[0, 0], kn_ref[0, 0]], axis=0)
    vext = jnp.concatenate([vp_ref[0, 0], v_ref[0, 0], vn_ref[0, 0]], axis=0)
    first_head = lax.broadcasted_iota(jnp.int32, (1, LANES), 1) < DSWA_HEAD_DIM
    n_pairs = q_ref.shape[3] // LANES
    blocks = range(lt // hb)
    units = [(qb, pr) for qb in blocks for pr in range(n_pairs)]
    rows = [slice(qb * hb, (qb + 1) * hb) for qb in blocks]
    win = [slice(qb * hb, (qb + 3) * hb) for qb in blocks]
    ls = [slice(pr * LANES, (pr + 1) * LANES) for pr in range(n_pairs)]
    kvalid = []
    for qb in blocks:
        kpos = t * lt + (qb - 1) * hb + lax.broadcasted_iota(jnp.int32, (1, 3 * hb), 1)
        kvalid.append((kpos >= 0) & (kpos < sub_len))
    qs = {(qb, pr): jnp.concatenate([jnp.where(first_head, q[rows[qb], ls[pr]], 0.0),
                                     jnp.where(first_head, 0.0, q[rows[qb], ls[pr]])], axis=0)
          for qb, pr in units}
    s = {(qb, pr): _dot_nt(qs[qb, pr], kext[win[qb], ls[pr]]) for qb, pr in units}
    s = {(qb, pr): jnp.where(kvalid[qb], s[qb, pr] + bias_ref[pr], NEG_INF) for qb, pr in units}
    m = {u: jnp.max(s[u], axis=-1, keepdims=True) for u in units}
    p = {u: jnp.exp(s[u] - m[u]) for u in units}
    l = {u: jnp.sum(p[u], axis=-1, keepdims=True) for u in units}
    pv = {(qb, pr): _dot(p[qb, pr], vext[win[qb], ls[pr]]) * (1.0 / l[qb, pr]) for qb, pr in units}
    lse = {u: m[u] + jnp.log(l[u]) for u in units}
    for qb, pr in units:
        o_ref[0, 0, rows[qb], ls[pr]] = jnp.where(first_head, pv[qb, pr][:hb], pv[qb, pr][hb:])
        lse_ref[0, 0, rows[qb], ls[pr]] = jnp.where(first_head, lse[qb, pr][:hb], lse[qb, pr][hb:])


def _rel_bucket(rel):
    nb = REL_BUCKETS // 2
    max_exact = nb // 2
    ret = jnp.where(rel > 0, nb, 0)
    n = jnp.abs(rel)
    nf = jnp.maximum(n, 1).astype(F32)
    large = max_exact + (jnp.log(nf * (1.0 / max_exact)) / math.log(REL_MAX_DIST / max_exact)
                         * (nb - max_exact)).astype(jnp.int32)
    large = jnp.minimum(large, nb - 1)
    return ret + jnp.where(n < max_exact, n, large)


def _band_bias_kernel(tab_ref, o_ref, *, dils, hg):
    hb = DSWA_HALF
    off = (lax.broadcasted_iota(jnp.int32, (hb, 3 * hb), 1) - hb
           - lax.broadcasted_iota(jnp.int32, (hb, 3 * hb), 0))
    inband = jnp.abs(off) <= hb
    for g, d in enumerate(dils):
        bucket = _rel_bucket(off * d)
        for hh in range(hg):
            acc = jnp.zeros((hb, 3 * hb), F32)
            for b in range(REL_BUCKETS):
                acc = jnp.where(bucket == b, tab_ref[b, g * hg + hh], acc)
            o_ref[g, hh // 2, (hh % 2) * hb:(hh % 2 + 1) * hb, :] = jnp.where(inband, acc, NEG_INF)


def _band_bias(rel_table, dils):
    hb = DSWA_HALF
    hg = rel_table.shape[1] // len(dils)
    return pl.pallas_call(
        functools.partial(_band_bias_kernel, dils=dils, hg=hg),
        in_specs=[pl.BlockSpec(memory_space=pltpu.SMEM)],
        out_shape=jax.ShapeDtypeStruct((len(dils), hg // 2, 2 * hb, 3 * hb), F32),
        name="band_bias",
    )(rel_table)


def _dswa_group(qkv, bias, lt=256):
    B, dil, L, W3 = qkv.shape
    hb = DSWA_HALF
    gw = W3 // 3
    lt = min(lt, L)
    r = lt // hb
    main = lambda which: pl.BlockSpec((1, 1, lt, gw), lambda b, d, t: (b, d, t, which))
    prev = lambda which: pl.BlockSpec(
        (1, 1, hb, gw), lambda b, d, t: (b, d, jnp.maximum(t * r - 1, 0), which))
    nxt = lambda which: pl.BlockSpec(
        (1, 1, hb, gw), lambda b, d, t: (b, d, jnp.minimum((t + 1) * r, L // hb - 1), which))
    out = pl.BlockSpec((1, 1, lt, gw), lambda b, d, t: (b, d, t, 0))
    return pl.pallas_call(
        functools.partial(_dswa_kernel, sub_len=L),
        grid=(B, dil, L // lt),
        in_specs=[main(0), main(1), prev(1), nxt(1), main(2), prev(2), nxt(2),
                  pl.BlockSpec(bias.shape, lambda b, d, t: (0, 0, 0))],
        out_specs=[out, out],
        out_shape=[jax.ShapeDtypeStruct((B, dil, L, gw), F32)] * 2,
        compiler_params=_params("parallel", "parallel", "parallel"),
        name=f"dswa_d{dil}",
    )(qkv, qkv, qkv, qkv, qkv, qkv, qkv, bias)


def _dswa_out_mlp_kernel(*refs, dils, final):
    ng = len(dils)
    o_refs, l_refs = refs[:ng], refs[ng:2 * ng]
    x_ref, wo_ref, g_ref, w1_ref, w2_ref, gf_ref, out_ref, o_s, l_s, y_s, x1_ref, acc_ref = refs[2 * ng:]
    tm = x_ref.shape[0]
    tiles = o_refs[0].shape[3] // LANES
    for g, d in enumerate(dils):
        for r in range(d):
            for j in range(tiles):
                ls = slice(j * LANES, (j + 1) * LANES)
                o_s[g * tiles + j, pl.ds(r, tm // d, stride=d), :] = o_refs[g][0, r, :, ls]
                l_s[g * tiles + j, pl.ds(r, tm // d, stride=d), :] = l_refs[g][0, r, :, ls]
    for j in range(tiles):
        lses = [l_s[g * tiles + j] for g in range(ng)]
        m = functools.reduce(jnp.maximum, lses)
        es = [jnp.exp(l - m) for l in lses]
        inv = 1.0 / functools.reduce(lambda a, b: a + b, es)
        for g in range(ng):
            c = g * tiles + j
            y_s[:, c * LANES:(c + 1) * LANES] = (o_s[c] * (es[g] * inv)).astype(BF16)
    _out_mlp_tail(y_s, x_ref, wo_ref, g_ref, w1_ref, w2_ref, gf_ref, out_ref, x1_ref, acc_ref, final)


def _dswa_out_mlp(os_, lses, x2, w_out, S, g, w1, w2, g_final, final, tm=512):
    dils = tuple(o.shape[1] for o in os_)
    gw = os_[0].shape[3]
    nt = S // tm
    grp = lambda d: pl.BlockSpec((1, d, tm // d, gw), lambda i: (i // nt, 0, i % nt, 0))
    staging = pltpu.VMEM((len(dils) * gw // LANES, tm, LANES), F32)
    return _out_mlp_call(
        functools.partial(_dswa_out_mlp_kernel, dils=dils, final=final),
        (*os_, *lses), [grp(d) for d in dils] * 2, [staging, staging],
        len(dils) * gw, x2, w_out, g, w1, w2, g_final, tm, "dswa_out_mlp")


def _gdn_layer(x2, g, w_in, conv_w, a_log, dt_bias, norm_w, w_out, B, S, mlp):
    nh = a_log.shape[1]
    z_w = nh * LANES
    qkvn, z, col, row = _gdn_proj(x2, g, w_in, conv_w, a_log, dt_bias, B, S)
    qkvn = qkvn.reshape(B, S, 3 * z_w)
    o_f = _gdn_scan(qkvn, col, row, nh, rev=False)
    o_b = _gdn_scan(qkvn, col, row, nh, rev=True)
    return _gdn_out_mlp(o_f.reshape(B * S, z_w), o_b.reshape(B * S, z_w), z, x2, norm_w,
                        w_out.astype(BF16), nh, *mlp)


def _dswa_proj_kernel(x_ref, g_ref, w_ref, *refs, dils, chunk):
    o_refs, acc_ref = refs[:-1], refs[-1]
    tm = x_ref.shape[0]
    n = w_ref.shape[1]
    ng = len(dils)
    gw = n // (3 * ng)
    tiles = gw // LANES
    h = _rms(x_ref[...], g_ref[...]).astype(BF16)
    for n0 in range(0, n, chunk):
        res = jnp.dot(h, w_ref[:, n0:n0 + chunk], preferred_element_type=F32)
        for j in range(chunk // LANES):
            acc_ref[n0 // LANES + j] = res[:, j * LANES:(j + 1) * LANES]
    for g, d in enumerate(dils):
        for r in range(d):
            for which in range(3):
                for j in range(tiles):
                    dst = slice((which * tiles + j) * LANES, (which * tiles + j + 1) * LANES)
                    o_refs[g][0, r, :, dst] = \
                        acc_ref[(which * ng + g) * tiles + j, pl.ds(r, tm // d, stride=d), :].astype(BF16)


def _dswa_proj(x2, g, w_in, dils, B, S, tm=512):
    T, D = x2.shape
    n = w_in.shape[1]
    gw3 = n // len(dils)
    nt = S // tm
    return pl.pallas_call(
        functools.partial(_dswa_proj_kernel, dils=dils, chunk=gw3),
        grid=(T // tm,),
        in_specs=[pl.BlockSpec((tm, D), lambda i: (i, 0)),
                  pl.BlockSpec((1, D), lambda i: (0, 0)),
                  pl.BlockSpec((D, n), lambda i: (0, 0))],
        out_specs=[pl.BlockSpec((1, d, tm // d, gw3), lambda i: (i // nt, 0, i % nt, 0)) for d in dils],
        out_shape=[jax.ShapeDtypeStruct((B, d, S // d, gw3), BF16) for d in dils],
        scratch_shapes=[pltpu.VMEM((n // LANES, tm, LANES), F32)],
        compiler_params=_params("parallel"),
        name="dswa_proj",
    )(x2, g.reshape(1, D), w_in)


def _dswa_layer(x2, g, w_in, w_out, bias, B, S, mlp):
    dils = tuple(d for _, d in DSWA_CONFIGS)
    qkvs = _dswa_proj(x2, g, w_in.astype(BF16), dils, B, S)
    os_, lses = [], []
    for gi in range(len(dils)):
        o, lse = _dswa_group(qkvs[gi], bias[gi])
        os_.append(o)
        lses.append(lse)
    return _dswa_out_mlp(os_, lses, x2, w_out.astype(BF16), S, *mlp)


def kernel(x, norm_mix, norm_mlp, norm_final, rel_bias, gdn_w_in, gdn_conv_w, gdn_a_log,
           gdn_dt_bias, gdn_norm_w, gdn_w_out, dswa_w_in, dswa_w_out, mlp_w1, mlp_w2):
    B, S, D = x.shape
    depth = norm_mix.shape[0]
    x2 = x.reshape(B * S, D)
    assert all(window // (2 * dil) == DSWA_HALF for window, dil in DSWA_CONFIGS)
    bias = _band_bias(rel_bias, tuple(d for _, d in DSWA_CONFIGS))
    for i in range(depth):
        j = i // 2
        mlp = (norm_mlp[i], mlp_w1[i].astype(BF16), mlp_w2[i].astype(BF16), norm_final, i == depth - 1)
        if i % 2 == 0:
            x2 = _gdn_layer(x2, norm_mix[i], gdn_w_in[j], gdn_conv_w[j], gdn_a_log[j],
                            gdn_dt_bias[j], gdn_norm_w[j], gdn_w_out[j], B, S, mlp)
        else:
            x2 = _dswa_layer(x2, norm_mix[i], dswa_w_in[j], dswa_w_out[j], bias, B, S, mlp)
    return x2.reshape(B, S, D)
```

```python
import functools
import math

import jax
import jax.numpy as jnp
from jax import lax
from jax.experimental import pallas as pl
from jax.experimental.pallas import tpu as pltpu

F32 = jnp.float32
BF16 = jnp.bfloat16

RMS_EPS = 1e-6
L2_EPS = 1e-6
NEG_INF = -1e30

LANES = 128
SUBLANES = 8
VMEM_LIMIT = 56 * 1024 * 1024

GDN_DK = 128
GDN_CONV = 5
GDN_CHUNK = LANES
DSWA_HEAD_DIM = 64
DSWA_CONFIGS = ((128, 1), (512, 4), (2048, 16))
DSWA_HALF = 64
REL_BUCKETS = 32
REL_MAX_DIST = 1024


def _params(*sem):
    return pltpu.CompilerParams(dimension_semantics=sem, vmem_limit_bytes=VMEM_LIMIT)


def _dot(a, b):
    return jnp.dot(a.astype(BF16), b.astype(BF16), preferred_element_type=F32)


def _dot_nt(a, b):
    return lax.dot_general(a.astype(BF16), b.astype(BF16), (((1,), (1,)), ((), ())),
                           preferred_element_type=F32)


def _split3(x):
    hi = x.astype(BF16)
    r = x - hi.astype(F32)
    mid = r.astype(BF16)
    return hi, mid, (r - mid.astype(F32)).astype(BF16)


def _rms(x, g):
    return x * lax.rsqrt(jnp.mean(x * x, axis=-1, keepdims=True) + RMS_EPS) * g


def _sigmoid(x):
    return 1.0 / (1.0 + jnp.exp(-x))


def _softplus(x):
    return jnp.maximum(x, 0.0) + jnp.log1p(jnp.exp(-jnp.abs(x)))


CONV_HALO = 16
CONV_BLOCK = 64


def _gdn_proj_kernel(x_ref, xp_ref, xn_ref, g_ref, wqkv_ref, wz_ref, w_ref, wt_ref, cw_ref,
                     prow_ref, pcol_ref, qkv_ref, z_ref, col_ref, row_ref, *, nh, nt, chunk):
    C = GDN_CHUNK
    tm = x_ref.shape[0]
    t = pl.program_id(0) % nt
    pad = GDN_CONV // 2
    g = g_ref[...]
    h = _rms(x_ref[...], g).astype(BF16)
    h_halo = jnp.concatenate([_rms(jnp.where(t > 0, xp_ref[...], 0.0), g).astype(BF16),
                              _rms(jnp.where(t < nt - 1, xn_ref[...], 0.0), g).astype(BF16)], axis=0)
    CB = CONV_BLOCK
    taps = [i for i in range(GDN_CONV) if i != pad]
    sr = lax.broadcasted_iota(jnp.int32, (len(taps) * CB, CB + 2 * CONV_HALO), 0)
    sc = lax.broadcasted_iota(jnp.int32, (len(taps) * CB, CB + 2 * CONV_HALO), 1)
    src = (sr % CB) + (CONV_HALO - pad)
    for k, i in enumerate(taps):
        src = src + jnp.where(sr // CB == k, i, 0)
    shift_mat = (sc == src).astype(BF16)

    def project(c0):
        w = wqkv_ref[:, c0:c0 + chunk]
        main = jnp.dot(h, w, preferred_element_type=F32)
        halo = jnp.dot(h_halo, w, preferred_element_type=F32)
        ext = jnp.concatenate([halo[:CONV_HALO], main, halo[CONV_HALO:]], axis=0).astype(BF16)
        return main, ext

    starts = list(range(0, wqkv_ref.shape[1], chunk))
    nxt = project(starts[0])
    for n, c0 in enumerate(starts):
        cs = slice(c0, c0 + chunk)
        main, ext = nxt
        if n + 1 < len(starts):
            nxt = project(starts[n + 1])
        for b in range(tm // CB):
            rows = slice(b * CB, (b + 1) * CB)
            shifted = jnp.dot(shift_mat, ext[b * CB:(b + 1) * CB + 2 * CONV_HALO],
                              preferred_element_type=F32)
            acc = cw_ref[pad:pad + 1, cs] * main[rows]
            for k, i in enumerate(taps):
                acc = acc + cw_ref[i:i + 1, cs] * shifted[k * CB:(k + 1) * CB]
            y = acc * _sigmoid(acc)
            for j in range(chunk // LANES):
                yj = y[:, j * LANES:(j + 1) * LANES]
                head = c0 // LANES + j
                if head < 2 * nh:
                    inv = lax.rsqrt(jnp.sum(yj * yj, axis=-1, keepdims=True) + L2_EPS)
                    yj = yj * (inv * (GDN_DK ** -0.5) if head < nh else inv)
                qkv_ref[rows, c0 + j * LANES:c0 + (j + 1) * LANES] = yj.astype(BF16)
    for c0 in range(0, wz_ref.shape[1], chunk):
        cs = slice(c0, c0 + chunk)
        z_ref[:, cs] = jnp.dot(h, wz_ref[:, cs], preferred_element_type=F32).astype(BF16)
    a_c = jnp.dot(h, w_ref[...], preferred_element_type=F32)
    a_r = lax.dot_general(wt_ref[...], h, (((1,), (1,)), ((), ())),
                          preferred_element_type=F32)
    ri = lax.broadcasted_iota(jnp.int32, (C, C), 0)
    ci = lax.broadcasted_iota(jnp.int32, (C, C), 1)
    lower = (ri >= ci).astype(BF16)
    upper = (ri <= ci).astype(BF16)
    g_c = -jnp.exp(prow_ref[0:1, :]) * _softplus(a_c + prow_ref[1:2, :])
    g_r = -jnp.exp(pcol_ref[:, 0:1]) * _softplus(a_r + pcol_ref[:, 1:2])
    beta_c = _sigmoid(a_c)
    beta_r = _sigmoid(a_r)
    for c in range(tm // C):
        sl = slice(c * C, (c + 1) * C)
        gc = g_c[sl, :]
        pre = sum(jnp.dot(lower, part, preferred_element_type=F32) for part in _split3(gc))
        cum = jnp.where(ci < nh, pre, pre[C - 1:C, :] - pre + gc)
        col_ref[sl, :] = jnp.where(ci < 2 * nh, cum, beta_c[sl, :])
        gr = g_r[:, sl]
        pre_r = sum(jnp.dot(part, upper, preferred_element_type=F32) for part in _split3(gr))
        cum_r = jnp.where(ri < nh, pre_r, pre_r[:, C - 1:C] - pre_r + gr)
        full = jnp.where(ri < 2 * nh, cum_r, beta_r[:, sl])
        row_ref[0, :, sl] = full[:4 * nh, :]


def _gdn_proj(x2, g, w_in, conv_w, a_log, dt_bias, B, S, tm=1024, chunk=256):
    T, D = x2.shape
    nh = a_log.shape[1]
    qkv_w = 3 * nh * LANES
    z_w = nh * LANES
    w_pad = jnp.zeros((D, LANES), F32).at[:, :4 * nh].set(w_in[:, qkv_w + z_w:])
    alog = jnp.zeros((LANES,), F32).at[:2 * nh].set(a_log.reshape(-1))
    dtb = jnp.zeros((LANES,), F32).at[:2 * nh].set(dt_bias.reshape(-1))
    prow = jnp.zeros((SUBLANES, LANES), F32).at[0].set(alog).at[1].set(dtb)
    pcol = jnp.zeros((LANES, LANES), F32).at[:, 0].set(alog).at[:, 1].set(dtb)
    nt = S // tm
    r = tm // CONV_HALO
    const = lambda shape: pl.BlockSpec(shape, lambda i: (0, 0))
    return pl.pallas_call(
        functools.partial(_gdn_proj_kernel, nh=nh, nt=nt, chunk=chunk),
        grid=(T // tm,),
        in_specs=[pl.BlockSpec((tm, D), lambda i: (i, 0)),
                  pl.BlockSpec((CONV_HALO, D), lambda i: (jnp.maximum(i * r - 1, 0), 0)),
                  pl.BlockSpec((CONV_HALO, D), lambda i: (jnp.minimum((i + 1) * r, T // CONV_HALO - 1), 0)),
                  const((1, D)), const((D, qkv_w)), const((D, z_w)), const((D, LANES)), const((LANES, D)),
                  const((GDN_CONV, qkv_w)), const((SUBLANES, LANES)), const((LANES, LANES))],
        out_specs=[pl.BlockSpec((tm, qkv_w), lambda i: (i, 0)),
                   pl.BlockSpec((tm, z_w), lambda i: (i, 0)),
                   pl.BlockSpec((tm, LANES), lambda i: (i, 0)),
                   pl.BlockSpec((1, 4 * nh, tm), lambda i: (i // nt, 0, i % nt))],
        out_shape=[jax.ShapeDtypeStruct((T, qkv_w), BF16),
                   jax.ShapeDtypeStruct((T, z_w), BF16),
                   jax.ShapeDtypeStruct((T, LANES), F32),
                   jax.ShapeDtypeStruct((B, 4 * nh, S), F32)],
        compiler_params=_params("parallel"),
        name="gdn_proj",
    )(x2, x2, x2, g.reshape(1, D), w_in[:, :qkv_w].astype(BF16), w_in[:, qkv_w:qkv_w + z_w].astype(BF16),
      w_pad.astype(BF16), w_pad.T.astype(BF16), conv_w.reshape(GDN_CONV, qkv_w), prow, pcol)


def _gdn_scan_kernel(q_ref, k_ref, v_ref, col_ref, row_ref, o_ref, state_ref, *, rev, nh):
    C = GDN_CHUNK
    t = pl.program_id(1)

    @pl.when(t == 0)
    def _():
        state_ref[...] = jnp.zeros_like(state_ref)

    ri = lax.broadcasted_iota(jnp.int32, (C, C), 0)
    ci = lax.broadcasted_iota(jnp.int32, (C, C), 1)
    incl = (ri <= ci) if rev else (ri >= ci)
    strict = (ri < ci) if rev else (ri > ci)
    eye = (ri == ci).astype(F32)
    levels = [((ri >> (lg + 1)) == (ci >> (lg + 1))) & ((ri >> lg) != (ci >> lg))
              for lg in range(int(math.log2(C)))]
    last = 0 if rev else C - 1
    nc = q_ref.shape[1] // C
    chunks = list(reversed(range(nc))) if rev else list(range(nc))
    heads = range(nh)
    hs = [slice(h * LANES, (h + 1) * LANES) for h in heads]
    gi = [(nh if rev else 0) + h for h in heads]
    units = [(c, h) for c in chunks for h in heads]
    rows = {c: slice(c * C, (c + 1) * C) for c in chunks}
    col = {c: col_ref[0, rows[c], :] for c in chunks}
    row = {c: row_ref[0, :, rows[c]] for c in chunks}
    kf = {u: k_ref[0, rows[u[0]], hs[u[1]]] for u in units}
    k = {u: kf[u].astype(F32) for u in units}
    g_col = {(c, h): col[c][:, gi[h]:gi[h] + 1] for c, h in units}
    b_col = {(c, h): col[c][:, 2 * nh + gi[h]:2 * nh + gi[h] + 1] for c, h in units}
    g_last = {u: g_col[u][last:last + 1, :] for u in units}
    decay = {(c, h): jnp.where(incl, jnp.exp(jnp.where(incl, g_col[c, h] - row[c][gi[h]:gi[h] + 1, :], 0.0)), 0.0)
             for c, h in units}
    kb = {u: k[u] * b_col[u] for u in units}
    res = {(c, h): _dot_nt(jnp.concatenate([kb[c, h].astype(BF16), q_ref[0, rows[c], hs[h]]], axis=0), kf[c, h])
           for c, h in units}
    a = {u: jnp.where(strict, res[u][:C] * decay[u], 0.0) for u in units}
    intra = {u: (res[u][C:] * decay[u]).astype(BF16) for u in units}
    tinv = {u: eye - jnp.where(levels[0], a[u], 0.0) for u in units}
    for lg, m in enumerate(levels[1:], start=1):
        size = 1 << lg
        moving = [r0 for r0 in range(0, C, size) if ((r0 >> lg) & 1) == (0 if rev else 1)]
        tb = {u: tinv[u].astype(BF16) for u in units}
        if size < SUBLANES:
            pm = {u: _dot(tb[u], jnp.where(m, a[u], 0.0)) for u in units}
            tinv = {u: tinv[u] - _dot(pm[u], tb[u]) for u in units}
            continue
        lhs = {u: jnp.concatenate([tinv[u][r0:r0 + size] for r0 in moving], axis=0) for u in units}
        pm = {u: _dot(lhs[u], jnp.where(m, a[u], 0.0)) for u in units}
        new = {u: lhs[u] - _dot(pm[u], tb[u]) for u in units}
        tinv = {u: jnp.concatenate(
            [new[u][moving.index(r0) * size:(moving.index(r0) + 1) * size] if r0 in moving
             else tinv[u][r0:r0 + size] for r0 in range(0, C, size)], axis=0) for u in units}
    e_col = {u: jnp.exp(g_col[u]) for u in units}
    sol = {(c, h): _dot(tinv[c, h], jnp.concatenate(
        [v_ref[0, rows[c], hs[h]].astype(F32) * b_col[c, h], kb[c, h] * e_col[c, h]], axis=1)) for c, h in units}
    qd = {(c, h): (q_ref[0, rows[c], hs[h]].astype(F32) * e_col[c, h]).astype(BF16) for c, h in units}
    k_dec = {u: (k[u] * jnp.exp(g_last[u] - g_col[u])).T.astype(BF16) for u in units}
    state = [state_ref[h] for h in heads]
    for c in chunks:
        sb = [state[h].astype(BF16) for h in heads]
        r2 = [_dot(jnp.concatenate([sol[c, h][:, LANES:].astype(BF16), qd[c, h]], axis=0), sb[h]) for h in heads]
        v_new = [(sol[c, h][:, :LANES] - r2[h][:C]).astype(BF16) for h in heads]
        for h in heads:
            o_ref[0, rows[c], hs[h]] = (r2[h][C:] + _dot(intra[c, h], v_new[h])).astype(o_ref.dtype)
        state = [state[h] * jnp.exp(g_last[c, h]) + _dot(k_dec[c, h], v_new[h]) for h in heads]
    for h in heads:
        state_ref[h] = state[h]


def _gdn_scan(qkvn, col, row, nh, rev, nc=4):
    B, S, _ = qkvn.shape
    ts = nc * GDN_CHUNK
    nt = S // ts
    width = nh * LANES
    tmap = (lambda t: nt - 1 - t) if rev else (lambda t: t)
    col3 = col.reshape(B, S, LANES)
    return pl.pallas_call(
        functools.partial(_gdn_scan_kernel, rev=rev, nh=nh),
        grid=(B, nt),
        in_specs=[pl.BlockSpec((1, ts, width), lambda b, t: (b, tmap(t), 0)),
                  pl.BlockSpec((1, ts, width), lambda b, t: (b, tmap(t), 1)),
                  pl.BlockSpec((1, ts, width), lambda b, t: (b, tmap(t), 2)),
                  pl.BlockSpec((1, ts, LANES), lambda b, t: (b, tmap(t), 0)),
                  pl.BlockSpec((1, 4 * nh, ts), lambda b, t: (b, 0, tmap(t)))],
        out_specs=pl.BlockSpec((1, ts, width), lambda b, t: (b, tmap(t), 0)),
        out_shape=jax.ShapeDtypeStruct((B, S, width), BF16),
        scratch_shapes=[pltpu.VMEM((nh, GDN_DK, LANES), F32)],
        compiler_params=_params("parallel", "arbitrary"),
        name="gdn_scan_bwd" if rev else "gdn_scan_fwd",
    )(qkvn, qkvn, qkvn, col3, row)


MLP_CHUNK = 512


def _out_mlp_tail(y_ref, x_ref, wo_ref, g_ref, w1_ref, w2_ref, gf_ref, o_ref, x1_ref, acc_ref, final):
    x1_ref[...] = x_ref[...] + jnp.dot(y_ref[...], wo_ref[...], preferred_element_type=F32)
    h = _rms(x1_ref[...], g_ref[...]).astype(BF16)
    for c in range(0, w1_ref.shape[1], MLP_CHUNK):
        a = jnp.maximum(jnp.dot(h, w1_ref[:, c:c + MLP_CHUNK], preferred_element_type=F32), 0.0)
        part = jnp.dot((a * a).astype(BF16), w2_ref[c:c + MLP_CHUNK, :], preferred_element_type=F32)
        if c == 0:
            acc_ref[...] = part
        else:
            acc_ref[...] += part
    out = x1_ref[...] + acc_ref[...]
    if final:
        out = _rms(out, gf_ref[...])
    o_ref[...] = out


def _out_mlp_call(kernel_fn, mixer_args, mixer_specs, mixer_scratch, x2, w_out, g, w1, w2,
                  g_final, tm, name):
    T, D = x2.shape
    once = lambda shape: pl.BlockSpec(shape, lambda i: (0, 0), pipeline_mode=pl.Buffered(1))
    return pl.pallas_call(
        kernel_fn,
        grid=(T // tm,),
        in_specs=mixer_specs + [pl.BlockSpec((tm, D), lambda i: (i, 0)), once(w_out.shape), once((1, D)),
                                once(w1.shape), once(w2.shape), once((1, D))],
        out_specs=pl.BlockSpec((tm, D), lambda i: (i, 0)),
        out_shape=jax.ShapeDtypeStruct((T, D), F32),
        scratch_shapes=mixer_scratch + [pltpu.VMEM((tm, w_out.shape[0]), BF16), pltpu.VMEM((tm, D), F32),
                                        pltpu.VMEM((tm, D), F32)],
        compiler_params=_params("parallel"),
        name=name,
    )(*mixer_args, x2, w_out, g.reshape(1, D), w1, w2, g_final.reshape(1, D))


def _gdn_out_mlp_kernel(of_ref, ob_ref, z_ref, nw_ref, x_ref, wo_ref, g_ref, w1_ref, w2_ref, gf_ref,
                        o_ref, y_ref, x1_ref, acc_ref, *, nh, final):
    for h in range(nh):
        hs = slice(h * LANES, (h + 1) * LANES)
        o = of_ref[:, hs].astype(F32) + ob_ref[:, hs].astype(F32)
        o = o * lax.rsqrt(jnp.mean(o * o, axis=-1, keepdims=True) + RMS_EPS)
        z = z_ref[:, hs].astype(F32)
        y_ref[:, hs] = (o * nw_ref[...] * (z * _sigmoid(z))).astype(BF16)
    _out_mlp_tail(y_ref, x_ref, wo_ref, g_ref, w1_ref, w2_ref, gf_ref, o_ref, x1_ref, acc_ref, final)


def _gdn_out_mlp(o_f, o_b, z, x2, norm_w, w_out, nh, g, w1, w2, g_final, final, tm=512):
    W = o_f.shape[1]
    tok = pl.BlockSpec((tm, W), lambda i: (i, 0))
    return _out_mlp_call(
        functools.partial(_gdn_out_mlp_kernel, nh=nh, final=final),
        (o_f, o_b, z, norm_w.reshape(1, LANES)),
        [tok, tok, tok, pl.BlockSpec((1, LANES), lambda i: (0, 0))],
        [], x2, w_out, g, w1, w2, g_final, tm, "gdn_out_mlp")


def _dswa_kernel(q_ref, k_ref, kp_ref, kn_ref, v_ref, vp_ref, vn_ref, bias_ref,
                 o_ref, lse_ref, *, sub_len):
    t = pl.program_id(2)
    lt = q_ref.shape[2]
    hb = DSWA_HALF
    q = q_ref[0, 0] * (DSWA_HEAD_DIM ** -0.5)
    kext = jnp.concatenate([kp_ref[0, 0], k_ref[0, 0], kn_ref[0, 0]], axis=0)
    vext = jnp.concatenate([vp_ref[0, 0], v_ref[0, 0], vn_ref[0, 0]], axis=0)
    first_head = lax.broadcasted_iota(jnp.int32, (1, LANES), 1) < DSWA_HEAD_DIM
    n_pairs = q_ref.shape[3] // LANES
    blocks = range(lt // hb)
    units = [(qb, pr) for qb in blocks for pr in range(n_pairs)]
    rows = [slice(qb * hb, (qb + 1) * hb) for qb in blocks]
    win = [slice(qb * hb, (qb + 3) * hb) for qb in blocks]
    ls = [slice(pr * LANES, (pr + 1) * LANES) for pr in range(n_pairs)]
    kvalid = []
    for qb in blocks:
        kpos = t * lt + (qb - 1) * hb + lax.broadcasted_iota(jnp.int32, (1, 3 * hb), 1)
        kvalid.append((kpos >= 0) & (kpos < sub_len))
    qs = {(qb, pr): jnp.concatenate([jnp.where(first_head, q[rows[qb], ls[pr]], 0.0),
                                     jnp.where(first_head, 0.0, q[rows[qb], ls[pr]])], axis=0)
          for qb, pr in units}
    s = {(qb, pr): _dot_nt(qs[qb, pr], kext[win[qb], ls[pr]]) for qb, pr in units}
    s = {(qb, pr): jnp.where(kvalid[qb], s[qb, pr] + bias_ref[pr], NEG_INF) for qb, pr in units}
    m = {u: jnp.max(s[u], axis=-1, keepdims=True) for u in units}
    p = {u: jnp.exp(s[u] - m[u]) for u in units}
    l = {u: jnp.sum(p[u], axis=-1, keepdims=True) for u in units}
    pv = {(qb, pr): _dot(p[qb, pr], vext[win[qb], ls[pr]]) * (1.0 / l[qb, pr]) for qb, pr in units}
    lse = {u: m[u] + jnp.log(l[u]) for u in units}
    for qb, pr in units:
        o_ref[0, 0, rows[qb], ls[pr]] = jnp.where(first_head, pv[qb, pr][:hb], pv[qb, pr][hb:])
        lse_ref[0, 0, rows[qb], ls[pr]] = jnp.where(first_head, lse[qb, pr][:hb], lse[qb, pr][hb:])


def _rel_bucket(rel):
    nb = REL_BUCKETS // 2
    max_exact = nb // 2
    ret = jnp.where(rel > 0, nb, 0)
    n = jnp.abs(rel)
    nf = jnp.maximum(n, 1).astype(F32)
    large = max_exact + (jnp.log(nf * (1.0 / max_exact)) / math.log(REL_MAX_DIST / max_exact)
                         * (nb - max_exact)).astype(jnp.int32)
    large = jnp.minimum(large, nb - 1)
    return ret + jnp.where(n < max_exact, n, large)


def _band_bias_kernel(tab_ref, o_ref, *, dils, hg):
    hb = DSWA_HALF
    off = (lax.broadcasted_iota(jnp.int32, (hb, 3 * hb), 1) - hb
           - lax.broadcasted_iota(jnp.int32, (hb, 3 * hb), 0))
    inband = jnp.abs(off) <= hb
    for g, d in enumerate(dils):
        bucket = _rel_bucket(off * d)
        for hh in range(hg):
            acc = jnp.zeros((hb, 3 * hb), F32)
            for b in range(REL_BUCKETS):
                acc = jnp.where(bucket == b, tab_ref[b, g * hg + hh], acc)
            o_ref[g, hh // 2, (hh % 2) * hb:(hh % 2 + 1) * hb, :] = jnp.where(inband, acc, NEG_INF)


def _band_bias(rel_table, dils):
    hb = DSWA_HALF
    hg = rel_table.shape[1] // len(dils)
    return pl.pallas_call(
        functools.partial(_band_bias_kernel, dils=dils, hg=hg),
        in_specs=[pl.BlockSpec(memory_space=pltpu.SMEM)],
        out_shape=jax.ShapeDtypeStruct((len(dils), hg // 2, 2 * hb, 3 * hb), F32),
        name="band_bias",
    )(rel_table)


def _dswa_group(qkv, bias, lt=512):
    B, dil, L, W3 = qkv.shape
    hb = DSWA_HALF
    gw = W3 // 3
    lt = min(lt, L)
    r = lt // hb
    main = lambda which: pl.BlockSpec((1, 1, lt, gw), lambda b, d, t: (b, d, t, which))
    prev = lambda which: pl.BlockSpec(
        (1, 1, hb, gw), lambda b, d, t: (b, d, jnp.maximum(t * r - 1, 0), which))
    nxt = lambda which: pl.BlockSpec(
        (1, 1, hb, gw), lambda b, d, t: (b, d, jnp.minimum((t + 1) * r, L // hb - 1), which))
    out = pl.BlockSpec((1, 1, lt, gw), lambda b, d, t: (b, d, t, 0))
    return pl.pallas_call(
        functools.partial(_dswa_kernel, sub_len=L),
        grid=(B, dil, L // lt),
        in_specs=[main(0), main(1), prev(1), nxt(1), main(2), prev(2), nxt(2),
                  pl.BlockSpec(bias.shape, lambda b, d, t: (0, 0, 0))],
        out_specs=[out, out],
        out_shape=[jax.ShapeDtypeStruct((B, dil, L, gw), F32)] * 2,
        compiler_params=_params("parallel", "parallel", "parallel"),
        name=f"dswa_d{dil}",
    )(qkv, qkv, qkv, qkv, qkv, qkv, qkv, bias)


def _dswa_out_mlp_kernel(*refs, dils, final):
    ng = len(dils)
    o_refs, l_refs = refs[:ng], refs[ng:2 * ng]
    x_ref, wo_ref, g_ref, w1_ref, w2_ref, gf_ref, out_ref, o_s, l_s, y_s, x1_ref, acc_ref = refs[2 * ng:]
    tm = x_ref.shape[0]
    tiles = o_refs[0].shape[3] // LANES
    for g, d in enumerate(dils):
        for r in range(d):
            for j in range(tiles):
                ls = slice(j * LANES, (j + 1) * LANES)
                o_s[g * tiles + j, pl.ds(r, tm // d, stride=d), :] = o_refs[g][0, r, :, ls]
                l_s[g * tiles + j, pl.ds(r, tm // d, stride=d), :] = l_refs[g][0, r, :, ls]
    for j in range(tiles):
        lses = [l_s[g * tiles + j] for g in range(ng)]
        m = functools.reduce(jnp.maximum, lses)
        es = [jnp.exp(l - m) for l in lses]
        inv = 1.0 / functools.reduce(lambda a, b: a + b, es)
        for g in range(ng):
            c = g * tiles + j
            y_s[:, c * LANES:(c + 1) * LANES] = (o_s[c] * (es[g] * inv)).astype(BF16)
    _out_mlp_tail(y_s, x_ref, wo_ref, g_ref, w1_ref, w2_ref, gf_ref, out_ref, x1_ref, acc_ref, final)


def _dswa_out_mlp(os_, lses, x2, w_out, S, g, w1, w2, g_final, final, tm=512):
    dils = tuple(o.shape[1] for o in os_)
    gw = os_[0].shape[3]
    nt = S // tm
    grp = lambda d: pl.BlockSpec((1, d, tm // d, gw), lambda i: (i // nt, 0, i % nt, 0))
    staging = pltpu.VMEM((len(dils) * gw // LANES, tm, LANES), F32)
    return _out_mlp_call(
        functools.partial(_dswa_out_mlp_kernel, dils=dils, final=final),
        (*os_, *lses), [grp(d) for d in dils] * 2, [staging, staging],
        x2, w_out, g, w1, w2, g_final, tm, "dswa_out_mlp")


def _gdn_layer(x2, g, w_in, conv_w, a_log, dt_bias, norm_w, w_out, B, S, mlp):
    nh = a_log.shape[1]
    z_w = nh * LANES
    qkvn, z, col, row = _gdn_proj(x2, g, w_in, conv_w, a_log, dt_bias, B, S)
    qkvn = qkvn.reshape(B, S, 3 * z_w)
    o_f = _gdn_scan(qkvn, col, row, nh, rev=False)
    o_b = _gdn_scan(qkvn, col, row, nh, rev=True)
    return _gdn_out_mlp(o_f.reshape(B * S, z_w), o_b.reshape(B * S, z_w), z, x2, norm_w,
                        w_out.astype(BF16), nh, *mlp)


def _dswa_proj_kernel(x_ref, g_ref, w_ref, *refs, dils, chunk):
    o_refs, acc_ref = refs[:-1], refs[-1]
    tm = x_ref.shape[0]
    n = w_ref.shape[1]
    ng = len(dils)
    gw = n // (3 * ng)
    tiles = gw // LANES
    h = _rms(x_ref[...], g_ref[...]).astype(BF16)
    for n0 in range(0, n, chunk):
        res = jnp.dot(h, w_ref[:, n0:n0 + chunk], preferred_element_type=F32)
        for j in range(chunk // LANES):
            acc_ref[n0 // LANES + j] = res[:, j * LANES:(j + 1) * LANES]
    for g, d in enumerate(dils):
        for r in range(d):
            for which in range(3):
                for j in range(tiles):
                    dst = slice((which * tiles + j) * LANES, (which * tiles + j + 1) * LANES)
                    o_refs[g][0, r, :, dst] = \
                        acc_ref[(which * ng + g) * tiles + j, pl.ds(r, tm // d, stride=d), :].astype(BF16)


def _dswa_proj(x2, g, w_in, dils, B, S, tm=512):
    T, D = x2.shape
    n = w_in.shape[1]
    gw3 = n // len(dils)
    nt = S // tm
    return pl.pallas_call(
        functools.partial(_dswa_proj_kernel, dils=dils, chunk=gw3),
        grid=(T // tm,),
        in_specs=[pl.BlockSpec((tm, D), lambda i: (i, 0)),
                  pl.BlockSpec((1, D), lambda i: (0, 0)),
                  pl.BlockSpec((D, n), lambda i: (0, 0))],
        out_specs=[pl.BlockSpec((1, d, tm // d, gw3), lambda i: (i // nt, 0, i % nt, 0)) for d in dils],
        out_shape=[jax.ShapeDtypeStruct((B, d, S // d, gw3), BF16) for d in dils],
        scratch_shapes=[pltpu.VMEM((n // LANES, tm, LANES), F32)],
        compiler_params=_params("parallel"),
        name="dswa_proj",
    )(x2, g.reshape(1, D), w_in)


def _dswa_layer(x2, g, w_in, w_out, bias, B, S, mlp):
    dils = tuple(d for _, d in DSWA_CONFIGS)
    qkvs = _dswa_proj(x2, g, w_in.astype(BF16), dils, B, S)
    os_, lses = [], []
    for gi in range(len(dils)):
        o, lse = _dswa_group(qkvs[gi], bias[gi])
        os_.append(o)
        lses.append(lse)
    return _dswa_out_mlp(os_, lses, x2, w_out.astype(BF16), S, *mlp)


def kernel(x, norm_mix, norm_mlp, norm_final, rel_bias, gdn_w_in, gdn_conv_w, gdn_a_log,
           gdn_dt_bias, gdn_norm_w, gdn_w_out, dswa_w_in, dswa_w_out, mlp_w1, mlp_w2):
    B, S, D = x.shape
    depth = norm_mix.shape[0]
    x2 = x.reshape(B * S, D)
    assert all(window // (2 * dil) == DSWA_HALF for window, dil in DSWA_CONFIGS)
    bias = _band_bias(rel_bias, tuple(d for _, d in DSWA_CONFIGS))
    for i in range(depth):
        j = i // 2
        mlp = (norm_mlp[i], mlp_w1[i].astype(BF16), mlp_w2[i].astype(BF16), norm_final, i == depth - 1)
        if i % 2 == 0:
            x2 = _gdn_layer(x2, norm_mix[i], gdn_w_in[j], gdn_conv_w[j], gdn_a_log[j],
                            gdn_dt_bias[j], gdn_norm_w[j], gdn_w_out[j], B, S, mlp)
        else:
            x2 = _dswa_layer(x2, norm_mix[i], dswa_w_in[j], dswa_w_out[j], bias, B, S, mlp)
    return x2.reshape(B, S, D)
```

```python
import functools
import math

import jax
import jax.numpy as jnp
from jax import lax
from jax.experimental import pallas as pl
from jax.experimental.pallas import tpu as pltpu

F32 = jnp.float32
BF16 = jnp.bfloat16

RMS_EPS = 1e-6
L2_EPS = 1e-6
NEG_INF = -1e30
LOG2E = math.log2(math.e)

LANES = 128
SUBLANES = 8
VMEM_LIMIT = 56 * 1024 * 1024

GDN_DK = 128
GDN_CONV = 5
GDN_CHUNK = LANES
DSWA_HEAD_DIM = 64
DSWA_CONFIGS = ((128, 1), (512, 4), (2048, 16))
DSWA_HALF = 64
REL_BUCKETS = 32
REL_MAX_DIST = 1024


def _params(*sem):
    return pltpu.CompilerParams(dimension_semantics=sem, vmem_limit_bytes=VMEM_LIMIT)


def _dot(a, b):
    return jnp.dot(a.astype(BF16), b.astype(BF16), preferred_element_type=F32)


def _dot_nt(a, b):
    return lax.dot_general(a.astype(BF16), b.astype(BF16), (((1,), (1,)), ((), ())),
                           preferred_element_type=F32)


def _split3(x):
    hi = x.astype(BF16)
    r = x - hi.astype(F32)
    mid = r.astype(BF16)
    return hi, mid, (r - mid.astype(F32)).astype(BF16)


def _rms(x, g):
    return x * lax.rsqrt(jnp.mean(x * x, axis=-1, keepdims=True) + RMS_EPS) * g


def _sigmoid(x):
    return 1.0 / (1.0 + jnp.exp(-x))


def _softplus(x):
    return jnp.maximum(x, 0.0) + jnp.log1p(jnp.exp(-jnp.abs(x)))


CONV_HALO = 16
CONV_BLOCK = 64


def _gdn_proj_kernel(x_ref, xp_ref, xn_ref, g_ref, wqkv_ref, wz_ref, w_ref, wt_ref, cw_ref,
                     prow_ref, pcol_ref, qkv_ref, z_ref, col_ref, row_ref, *, nh, nt, chunk):
    C = GDN_CHUNK
    tm = x_ref.shape[0]
    t = pl.program_id(0) % nt
    pad = GDN_CONV // 2
    g = g_ref[...]
    h = _rms(x_ref[...], g).astype(BF16)
    h_halo = jnp.concatenate([_rms(jnp.where(t > 0, xp_ref[...], 0.0), g).astype(BF16),
                              _rms(jnp.where(t < nt - 1, xn_ref[...], 0.0), g).astype(BF16)], axis=0)
    CB = CONV_BLOCK
    taps = [i for i in range(GDN_CONV) if i != pad]
    sr = lax.broadcasted_iota(jnp.int32, (len(taps) * CB, CB + 2 * CONV_HALO), 0)
    sc = lax.broadcasted_iota(jnp.int32, (len(taps) * CB, CB + 2 * CONV_HALO), 1)
    src = (sr % CB) + (CONV_HALO - pad)
    for k, i in enumerate(taps):
        src = src + jnp.where(sr // CB == k, i, 0)
    shift_mat = (sc == src).astype(BF16)

    def project(c0):
        w = wqkv_ref[:, c0:c0 + chunk]
        main = jnp.dot(h, w, preferred_element_type=F32)
        halo = jnp.dot(h_halo, w, preferred_element_type=F32)
        ext = jnp.concatenate([halo[:CONV_HALO], main, halo[CONV_HALO:]], axis=0).astype(BF16)
        return main, ext

    starts = list(range(0, wqkv_ref.shape[1], chunk))
    nxt = project(starts[0])
    for n, c0 in enumerate(starts):
        cs = slice(c0, c0 + chunk)
        main, ext = nxt
        if n + 1 < len(starts):
            nxt = project(starts[n + 1])
        for b in range(tm // CB):
            rows = slice(b * CB, (b + 1) * CB)
            shifted = jnp.dot(shift_mat, ext[b * CB:(b + 1) * CB + 2 * CONV_HALO],
                              preferred_element_type=F32)
            acc = cw_ref[pad:pad + 1, cs] * main[rows]
            for k, i in enumerate(taps):
                acc = acc + cw_ref[i:i + 1, cs] * shifted[k * CB:(k + 1) * CB]
            y = acc * _sigmoid(acc)
            for j in range(chunk // LANES):
                yj = y[:, j * LANES:(j + 1) * LANES]
                head = c0 // LANES + j
                if head < 2 * nh:
                    inv = lax.rsqrt(jnp.sum(yj * yj, axis=-1, keepdims=True) + L2_EPS)
                    yj = yj * (inv * (GDN_DK ** -0.5) if head < nh else inv)
                qkv_ref[rows, c0 + j * LANES:c0 + (j + 1) * LANES] = yj.astype(BF16)
    for c0 in range(0, wz_ref.shape[1], chunk):
        cs = slice(c0, c0 + chunk)
        z_ref[:, cs] = jnp.dot(h, wz_ref[:, cs], preferred_element_type=F32).astype(BF16)
    a_c = jnp.dot(h, w_ref[...], preferred_element_type=F32)
    a_r = lax.dot_general(wt_ref[...], h, (((1,), (1,)), ((), ())),
                          preferred_element_type=F32)
    ri = lax.broadcasted_iota(jnp.int32, (C, C), 0)
    ci = lax.broadcasted_iota(jnp.int32, (C, C), 1)
    lower = (ri >= ci).astype(BF16)
    upper = (ri <= ci).astype(BF16)
    g_c = -jnp.exp(prow_ref[0:1, :]) * _softplus(a_c + prow_ref[1:2, :])
    g_r = -jnp.exp(pcol_ref[:, 0:1]) * _softplus(a_r + pcol_ref[:, 1:2])
    beta_c = _sigmoid(a_c)
    beta_r = _sigmoid(a_r)
    for c in range(tm // C):
        sl = slice(c * C, (c + 1) * C)
        gc = g_c[sl, :]
        pre = sum(jnp.dot(lower, part, preferred_element_type=F32) for part in _split3(gc))
        cum = jnp.where(ci < nh, pre, pre[C - 1:C, :] - pre + gc)
        col_ref[sl, :] = jnp.where(ci < 2 * nh, cum, beta_c[sl, :])
        gr = g_r[:, sl]
        pre_r = sum(jnp.dot(part, upper, preferred_element_type=F32) for part in _split3(gr))
        cum_r = jnp.where(ri < nh, pre_r, pre_r[:, C - 1:C] - pre_r + gr)
        full = jnp.where(ri < 2 * nh, cum_r, beta_r[:, sl])
        row_ref[0, :, sl] = full[:4 * nh, :]


def _gdn_proj(x2, g, w_in, conv_w, a_log, dt_bias, B, S, tm=1024, chunk=256):
    T, D = x2.shape
    nh = a_log.shape[1]
    qkv_w = 3 * nh * LANES
    z_w = nh * LANES
    w_pad = jnp.zeros((D, LANES), F32).at[:, :4 * nh].set(w_in[:, qkv_w + z_w:])
    alog = jnp.zeros((LANES,), F32).at[:2 * nh].set(a_log.reshape(-1))
    dtb = jnp.zeros((LANES,), F32).at[:2 * nh].set(dt_bias.reshape(-1))
    prow = jnp.zeros((SUBLANES, LANES), F32).at[0].set(alog).at[1].set(dtb)
    pcol = jnp.zeros((LANES, LANES), F32).at[:, 0].set(alog).at[:, 1].set(dtb)
    nt = S // tm
    r = tm // CONV_HALO
    const = lambda shape: pl.BlockSpec(shape, lambda i: (0, 0))
    return pl.pallas_call(
        functools.partial(_gdn_proj_kernel, nh=nh, nt=nt, chunk=chunk),
        grid=(T // tm,),
        in_specs=[pl.BlockSpec((tm, D), lambda i: (i, 0)),
                  pl.BlockSpec((CONV_HALO, D), lambda i: (jnp.maximum(i * r - 1, 0), 0)),
                  pl.BlockSpec((CONV_HALO, D), lambda i: (jnp.minimum((i + 1) * r, T // CONV_HALO - 1), 0)),
                  const((1, D)), const((D, qkv_w)), const((D, z_w)), const((D, LANES)), const((LANES, D)),
                  const((GDN_CONV, qkv_w)), const((SUBLANES, LANES)), const((LANES, LANES))],
        out_specs=[pl.BlockSpec((tm, qkv_w), lambda i: (i, 0)),
                   pl.BlockSpec((tm, z_w), lambda i: (i, 0)),
                   pl.BlockSpec((tm, LANES), lambda i: (i, 0)),
                   pl.BlockSpec((1, 4 * nh, tm), lambda i: (i // nt, 0, i % nt))],
        out_shape=[jax.ShapeDtypeStruct((T, qkv_w), BF16),
                   jax.ShapeDtypeStruct((T, z_w), BF16),
                   jax.ShapeDtypeStruct((T, LANES), F32),
                   jax.ShapeDtypeStruct((B, 4 * nh, S), F32)],
        compiler_params=_params("parallel"),
        name="gdn_proj",
    )(x2, x2, x2, g.reshape(1, D), w_in[:, :qkv_w].astype(BF16), w_in[:, qkv_w:qkv_w + z_w].astype(BF16),
      w_pad.astype(BF16), w_pad.T.astype(BF16), conv_w.reshape(GDN_CONV, qkv_w), prow, pcol)


def _gdn_scan_kernel(q_ref, k_ref, v_ref, col_ref, row_ref, o_ref, state_ref, *, rev, nh):
    C = GDN_CHUNK
    t = pl.program_id(1)

    @pl.when(t == 0)
    def _():
        state_ref[...] = jnp.zeros_like(state_ref)

    ri = lax.broadcasted_iota(jnp.int32, (C, C), 0)
    ci = lax.broadcasted_iota(jnp.int32, (C, C), 1)
    incl = (ri <= ci) if rev else (ri >= ci)
    strict = (ri < ci) if rev else (ri > ci)
    eye = (ri == ci).astype(F32)
    levels = [((ri >> (lg + 1)) == (ci >> (lg + 1))) & ((ri >> lg) != (ci >> lg))
              for lg in range(int(math.log2(C)))]
    last = 0 if rev else C - 1
    nc = q_ref.shape[1] // C
    chunks = list(reversed(range(nc))) if rev else list(range(nc))
    heads = range(nh)
    hs = [slice(h * LANES, (h + 1) * LANES) for h in heads]
    gi = [(nh if rev else 0) + h for h in heads]
    units = [(c, h) for c in chunks for h in heads]
    rows = {c: slice(c * C, (c + 1) * C) for c in chunks}
    col = {c: col_ref[0, rows[c], :] for c in chunks}
    row = {c: row_ref[0, :, rows[c]] for c in chunks}
    kf = {u: k_ref[0, rows[u[0]], hs[u[1]]] for u in units}
    k = {u: kf[u].astype(F32) for u in units}
    g_col = {(c, h): col[c][:, gi[h]:gi[h] + 1] for c, h in units}
    b_col = {(c, h): col[c][:, 2 * nh + gi[h]:2 * nh + gi[h] + 1] for c, h in units}
    g_last = {u: g_col[u][last:last + 1, :] for u in units}
    decay = {(c, h): jnp.where(incl, jnp.exp(jnp.where(incl, g_col[c, h] - row[c][gi[h]:gi[h] + 1, :], 0.0)), 0.0)
             for c, h in units}
    kb = {u: k[u] * b_col[u] for u in units}
    res = {(c, h): _dot_nt(jnp.concatenate([kb[c, h].astype(BF16), q_ref[0, rows[c], hs[h]]], axis=0), kf[c, h])
           for c, h in units}
    a = {u: jnp.where(strict, res[u][:C] * decay[u], 0.0) for u in units}
    intra = {u: (res[u][C:] * decay[u]).astype(BF16) for u in units}
    tinv = {u: eye - jnp.where(levels[0], a[u], 0.0) for u in units}
    for lg, m in enumerate(levels[1:], start=1):
        size = 1 << lg
        moving = [r0 for r0 in range(0, C, size) if ((r0 >> lg) & 1) == (0 if rev else 1)]
        tb = {u: tinv[u].astype(BF16) for u in units}
        if size < SUBLANES:
            pm = {u: _dot(tb[u], jnp.where(m, a[u], 0.0)) for u in units}
            tinv = {u: tinv[u] - _dot(pm[u], tb[u]) for u in units}
            continue
        lhs = {u: jnp.concatenate([tinv[u][r0:r0 + size] for r0 in moving], axis=0) for u in units}
        pm = {u: _dot(lhs[u], jnp.where(m, a[u], 0.0)) for u in units}
        new = {u: lhs[u] - _dot(pm[u], tb[u]) for u in units}
        tinv = {u: jnp.concatenate(
            [new[u][moving.index(r0) * size:(moving.index(r0) + 1) * size] if r0 in moving
             else tinv[u][r0:r0 + size] for r0 in range(0, C, size)], axis=0) for u in units}
    e_col = {u: jnp.exp(g_col[u]) for u in units}
    sol = {(c, h): _dot(tinv[c, h], jnp.concatenate(
        [v_ref[0, rows[c], hs[h]].astype(F32) * b_col[c, h], kb[c, h] * e_col[c, h]], axis=1)) for c, h in units}
    qd = {(c, h): (q_ref[0, rows[c], hs[h]].astype(F32) * e_col[c, h]).astype(BF16) for c, h in units}
    k_dec = {u: (k[u] * jnp.exp(g_last[u] - g_col[u])).T.astype(BF16) for u in units}
    state = [state_ref[h] for h in heads]
    for c in chunks:
        sb = [state[h].astype(BF16) for h in heads]
        r2 = [_dot(jnp.concatenate([sol[c, h][:, LANES:].astype(BF16), qd[c, h]], axis=0), sb[h]) for h in heads]
        v_new = [(sol[c, h][:, :LANES] - r2[h][:C]).astype(BF16) for h in heads]
        for h in heads:
            o_ref[0, rows[c], hs[h]] = (r2[h][C:] + _dot(intra[c, h], v_new[h])).astype(o_ref.dtype)
        state = [state[h] * jnp.exp(g_last[c, h]) + _dot(k_dec[c, h], v_new[h]) for h in heads]
    for h in heads:
        state_ref[h] = state[h]


def _gdn_scan(qkvn, col, row, nh, rev, nc=4):
    B, S, _ = qkvn.shape
    ts = nc * GDN_CHUNK
    nt = S // ts
    width = nh * LANES
    tmap = (lambda t: nt - 1 - t) if rev else (lambda t: t)
    col3 = col.reshape(B, S, LANES)
    return pl.pallas_call(
        functools.partial(_gdn_scan_kernel, rev=rev, nh=nh),
        grid=(B, nt),
        in_specs=[pl.BlockSpec((1, ts, width), lambda b, t: (b, tmap(t), 0)),
                  pl.BlockSpec((1, ts, width), lambda b, t: (b, tmap(t), 1)),
                  pl.BlockSpec((1, ts, width), lambda b, t: (b, tmap(t), 2)),
                  pl.BlockSpec((1, ts, LANES), lambda b, t: (b, tmap(t), 0)),
                  pl.BlockSpec((1, 4 * nh, ts), lambda b, t: (b, 0, tmap(t)))],
        out_specs=pl.BlockSpec((1, ts, width), lambda b, t: (b, tmap(t), 0)),
        out_shape=jax.ShapeDtypeStruct((B, S, width), BF16),
        scratch_shapes=[pltpu.VMEM((nh, GDN_DK, LANES), F32)],
        compiler_params=_params("parallel", "arbitrary"),
        name="gdn_scan_bwd" if rev else "gdn_scan_fwd",
    )(qkvn, qkvn, qkvn, col3, row)


MLP_CHUNK = 512


def _out_mlp_tail(y_ref, x_ref, wo_ref, g_ref, w1_ref, w2_ref, gf_ref, o_ref, x1_ref, acc_ref, final):
    x1_ref[...] = x_ref[...] + jnp.dot(y_ref[...], wo_ref[...], preferred_element_type=F32)
    h = _rms(x1_ref[...], g_ref[...]).astype(BF16)
    for c in range(0, w1_ref.shape[1], MLP_CHUNK):
        a = jnp.maximum(jnp.dot(h, w1_ref[:, c:c + MLP_CHUNK], preferred_element_type=F32), 0.0)
        part = jnp.dot((a * a).astype(BF16), w2_ref[c:c + MLP_CHUNK, :], preferred_element_type=F32)
        if c == 0:
            acc_ref[...] = part
        else:
            acc_ref[...] += part
    out = x1_ref[...] + acc_ref[...]
    if final:
        out = _rms(out, gf_ref[...])
    o_ref[...] = out


def _out_mlp_call(kernel_fn, mixer_args, mixer_specs, mixer_scratch, x2, w_out, g, w1, w2,
                  g_final, tm, name):
    T, D = x2.shape
    once = lambda shape: pl.BlockSpec(shape, lambda i: (0, 0), pipeline_mode=pl.Buffered(1))
    return pl.pallas_call(
        kernel_fn,
        grid=(T // tm,),
        in_specs=mixer_specs + [pl.BlockSpec((tm, D), lambda i: (i, 0)), once(w_out.shape), once((1, D)),
                                once(w1.shape), once(w2.shape), once((1, D))],
        out_specs=pl.BlockSpec((tm, D), lambda i: (i, 0)),
        out_shape=jax.ShapeDtypeStruct((T, D), F32),
        scratch_shapes=mixer_scratch + [pltpu.VMEM((tm, w_out.shape[0]), BF16), pltpu.VMEM((tm, D), F32),
                                        pltpu.VMEM((tm, D), F32)],
        compiler_params=_params("parallel"),
        name=name,
    )(*mixer_args, x2, w_out, g.reshape(1, D), w1, w2, g_final.reshape(1, D))


def _gdn_out_mlp_kernel(of_ref, ob_ref, z_ref, nw_ref, x_ref, wo_ref, g_ref, w1_ref, w2_ref, gf_ref,
                        o_ref, y_ref, x1_ref, acc_ref, *, nh, final):
    for h in range(nh):
        hs = slice(h * LANES, (h + 1) * LANES)
        o = of_ref[:, hs].astype(F32) + ob_ref[:, hs].astype(F32)
        o = o * lax.rsqrt(jnp.mean(o * o, axis=-1, keepdims=True) + RMS_EPS)
        z = z_ref[:, hs].astype(F32)
        y_ref[:, hs] = (o * nw_ref[...] * (z * _sigmoid(z))).astype(BF16)
    _out_mlp_tail(y_ref, x_ref, wo_ref, g_ref, w1_ref, w2_ref, gf_ref, o_ref, x1_ref, acc_ref, final)


def _gdn_out_mlp(o_f, o_b, z, x2, norm_w, w_out, nh, g, w1, w2, g_final, final, tm=512):
    W = o_f.shape[1]
    tok = pl.BlockSpec((tm, W), lambda i: (i, 0))
    return _out_mlp_call(
        functools.partial(_gdn_out_mlp_kernel, nh=nh, final=final),
        (o_f, o_b, z, norm_w.reshape(1, LANES)),
        [tok, tok, tok, pl.BlockSpec((1, LANES), lambda i: (0, 0))],
        [], x2, w_out, g, w1, w2, g_final, tm, "gdn_out_mlp")


def _dswa_kernel(q_ref, k_ref, kp_ref, kn_ref, v_ref, vp_ref, vn_ref, bias_ref,
                 o_ref, lse_ref, *, sub_len):
    t = pl.program_id(2)
    rb, lt = q_ref.shape[1], q_ref.shape[2]
    hb = DSWA_HALF
    first_head = lax.broadcasted_iota(jnp.int32, (1, LANES), 1) < DSWA_HEAD_DIM
    n_pairs = q_ref.shape[3] // LANES
    blocks = range(lt // hb)
    units = [(rc, qb, pr) for rc in range(rb) for qb in blocks for pr in range(n_pairs)]
    rows = [slice(qb * hb, (qb + 1) * hb) for qb in blocks]
    win = [slice(qb * hb, (qb + 3) * hb) for qb in blocks]
    ls = [slice(pr * LANES, (pr + 1) * LANES) for pr in range(n_pairs)]
    kext = [jnp.concatenate([kp_ref[0, rc], k_ref[0, rc], kn_ref[0, rc]], axis=0) for rc in range(rb)]
    vext = [jnp.concatenate([vp_ref[0, rc], v_ref[0, rc], vn_ref[0, rc]], axis=0) for rc in range(rb)]
    kpos = lambda qb: t * lt + (qb - 1) * hb + lax.broadcasted_iota(jnp.int32, (1, 3 * hb), 1)
    qs = {(rc, qb, pr): jnp.concatenate([jnp.where(first_head, q_ref[0, rc, rows[qb], ls[pr]], 0.0),
                                         jnp.where(first_head, 0.0, q_ref[0, rc, rows[qb], ls[pr]])], axis=0)
          for rc, qb, pr in units}
    s = {(rc, qb, pr): _dot_nt(qs[rc, qb, pr], kext[rc][win[qb], ls[pr]]) + bias_ref[pr] for rc, qb, pr in units}
    for rc, qb, pr in units:
        if qb == blocks[0]:
            s[rc, qb, pr] = jnp.where(kpos(qb) >= 0, s[rc, qb, pr], NEG_INF)
        if qb == blocks[-1]:
            s[rc, qb, pr] = jnp.where(kpos(qb) < sub_len, s[rc, qb, pr], NEG_INF)
    m = {u: jnp.max(s[u], axis=-1, keepdims=True) for u in units}
    p = {u: jnp.exp2(s[u] - m[u]) for u in units}
    l = {u: jnp.sum(p[u], axis=-1, keepdims=True) for u in units}
    pv = {(rc, qb, pr): _dot(p[rc, qb, pr], vext[rc][win[qb], ls[pr]]) * (1.0 / l[rc, qb, pr])
          for rc, qb, pr in units}
    lse = {u: (m[u] + jnp.log2(l[u])) * math.log(2.0) for u in units}
    for rc, qb, pr in units:
        o_ref[0, rc, rows[qb], ls[pr]] = jnp.where(first_head, pv[rc, qb, pr][:hb], pv[rc, qb, pr][hb:])
        lse_ref[0, rc, rows[qb], ls[pr]] = jnp.where(first_head, lse[rc, qb, pr][:hb], lse[rc, qb, pr][hb:])


def _rel_bucket(rel):
    nb = REL_BUCKETS // 2
    max_exact = nb // 2
    ret = jnp.where(rel > 0, nb, 0)
    n = jnp.abs(rel)
    nf = jnp.maximum(n, 1).astype(F32)
    large = max_exact + (jnp.log(nf * (1.0 / max_exact)) / math.log(REL_MAX_DIST / max_exact)
                         * (nb - max_exact)).astype(jnp.int32)
    large = jnp.minimum(large, nb - 1)
    return ret + jnp.where(n < max_exact, n, large)


def _band_bias_kernel(tab_ref, o_ref, *, dils, hg):
    hb = DSWA_HALF
    off = (lax.broadcasted_iota(jnp.int32, (hb, 3 * hb), 1) - hb
           - lax.broadcasted_iota(jnp.int32, (hb, 3 * hb), 0))
    inband = jnp.abs(off) <= hb
    for g, d in enumerate(dils):
        bucket = _rel_bucket(off * d)
        for hh in range(hg):
            acc = jnp.zeros((hb, 3 * hb), F32)
            for b in range(REL_BUCKETS):
                acc = jnp.where(bucket == b, tab_ref[b, g * hg + hh], acc)
            o_ref[g, hh // 2, (hh % 2) * hb:(hh % 2 + 1) * hb, :] = jnp.where(inband, acc * LOG2E, NEG_INF)


def _band_bias(rel_table, dils):
    hb = DSWA_HALF
    hg = rel_table.shape[1] // len(dils)
    return pl.pallas_call(
        functools.partial(_band_bias_kernel, dils=dils, hg=hg),
        in_specs=[pl.BlockSpec(memory_space=pltpu.SMEM)],
        out_shape=jax.ShapeDtypeStruct((len(dils), hg // 2, 2 * hb, 3 * hb), F32),
        name="band_bias",
    )(rel_table)


def _dswa_group(qkv, bias, lt=512):
    B, dil, L, W3 = qkv.shape
    hb = DSWA_HALF
    gw = W3 // 3
    rb = max(1, lt // L)
    lt = min(lt, L)
    r = lt // hb
    main = lambda which: pl.BlockSpec((1, rb, lt, gw), lambda b, d, t: (b, d, t, which))
    prev = lambda which: pl.BlockSpec(
        (1, rb, hb, gw), lambda b, d, t: (b, d, jnp.maximum(t * r - 1, 0), which))
    nxt = lambda which: pl.BlockSpec(
        (1, rb, hb, gw), lambda b, d, t: (b, d, jnp.minimum((t + 1) * r, L // hb - 1), which))
    out = pl.BlockSpec((1, rb, lt, gw), lambda b, d, t: (b, d, t, 0))
    return pl.pallas_call(
        functools.partial(_dswa_kernel, sub_len=L),
        grid=(B, dil // rb, L // lt),
        in_specs=[main(0), main(1), prev(1), nxt(1), main(2), prev(2), nxt(2),
                  pl.BlockSpec(bias.shape, lambda b, d, t: (0, 0, 0))],
        out_specs=[out, out],
        out_shape=[jax.ShapeDtypeStruct((B, dil, L, gw), F32)] * 2,
        compiler_params=_params("parallel", "parallel", "parallel"),
        name=f"dswa_d{dil}",
    )(qkv, qkv, qkv, qkv, qkv, qkv, qkv, bias)


def _dswa_out_mlp_kernel(*refs, dils, final):
    ng = len(dils)
    o_refs, l_refs = refs[:ng], refs[ng:2 * ng]
    x_ref, wo_ref, g_ref, w1_ref, w2_ref, gf_ref, out_ref, o_s, l_s, y_s, x1_ref, acc_ref = refs[2 * ng:]
    tm = x_ref.shape[0]
    tiles = o_refs[0].shape[3] // LANES
    for g, d in enumerate(dils):
        for r in range(d):
            for j in range(tiles):
                ls = slice(j * LANES, (j + 1) * LANES)
                o_s[g * tiles + j, pl.ds(r, tm // d, stride=d), :] = o_refs[g][0, r, :, ls]
                l_s[g * tiles + j, pl.ds(r, tm // d, stride=d), :] = l_refs[g][0, r, :, ls]
    for j in range(tiles):
        lses = [l_s[g * tiles + j] for g in range(ng)]
        m = functools.reduce(jnp.maximum, lses)
        es = [jnp.exp(l - m) for l in lses]
        inv = 1.0 / functools.reduce(lambda a, b: a + b, es)
        for g in range(ng):
            c = g * tiles + j
            y_s[:, c * LANES:(c + 1) * LANES] = (o_s[c] * (es[g] * inv)).astype(BF16)
    _out_mlp_tail(y_s, x_ref, wo_ref, g_ref, w1_ref, w2_ref, gf_ref, out_ref, x1_ref, acc_ref, final)


def _dswa_out_mlp(os_, lses, x2, w_out, S, g, w1, w2, g_final, final, tm=512):
    dils = tuple(o.shape[1] for o in os_)
    gw = os_[0].shape[3]
    nt = S // tm
    grp = lambda d: pl.BlockSpec((1, d, tm // d, gw), lambda i: (i // nt, 0, i % nt, 0))
    staging = pltpu.VMEM((len(dils) * gw // LANES, tm, LANES), F32)
    return _out_mlp_call(
        functools.partial(_dswa_out_mlp_kernel, dils=dils, final=final),
        (*os_, *lses), [grp(d) for d in dils] * 2, [staging, staging],
        x2, w_out, g, w1, w2, g_final, tm, "dswa_out_mlp")


def _gdn_layer(x2, g, w_in, conv_w, a_log, dt_bias, norm_w, w_out, B, S, mlp):
    nh = a_log.shape[1]
    z_w = nh * LANES
    qkvn, z, col, row = _gdn_proj(x2, g, w_in, conv_w, a_log, dt_bias, B, S)
    qkvn = qkvn.reshape(B, S, 3 * z_w)
    o_f = _gdn_scan(qkvn, col, row, nh, rev=False)
    o_b = _gdn_scan(qkvn, col, row, nh, rev=True)
    return _gdn_out_mlp(o_f.reshape(B * S, z_w), o_b.reshape(B * S, z_w), z, x2, norm_w,
                        w_out.astype(BF16), nh, *mlp)


def _dswa_proj_kernel(x_ref, g_ref, w_ref, *refs, dils, chunk):
    o_refs, acc_ref = refs[:-1], refs[-1]
    tm = x_ref.shape[0]
    n = w_ref.shape[1]
    ng = len(dils)
    gw = n // (3 * ng)
    tiles = gw // LANES
    h = _rms(x_ref[...], g_ref[...]).astype(BF16)
    def project(which):
        res = jnp.dot(h, w_ref[:, which * chunk:(which + 1) * chunk], preferred_element_type=F32)
        if which == 0:
            res = res * (DSWA_HEAD_DIM ** -0.5 * LOG2E)
        for j in range(chunk // LANES):
            acc_ref[which * ng * tiles + j] = res[:, j * LANES:(j + 1) * LANES]

    def regroup(which):
        for g, d in enumerate(dils):
            for r in range(d):
                for j in range(tiles):
                    dst = slice((which * tiles + j) * LANES, (which * tiles + j + 1) * LANES)
                    o_refs[g][0, r, :, dst] = \
                        acc_ref[(which * ng + g) * tiles + j, pl.ds(r, tm // d, stride=d), :].astype(BF16)

    project(0)
    for which in range(1, 3):
        project(which)
        regroup(which - 1)
    regroup(2)


def _dswa_proj(x2, g, w_in, dils, B, S, tm=512):
    T, D = x2.shape
    n = w_in.shape[1]
    gw3 = n // len(dils)
    nt = S // tm
    return pl.pallas_call(
        functools.partial(_dswa_proj_kernel, dils=dils, chunk=gw3),
        grid=(T // tm,),
        in_specs=[pl.BlockSpec((tm, D), lambda i: (i, 0)),
                  pl.BlockSpec((1, D), lambda i: (0, 0)),
                  pl.BlockSpec((D, n), lambda i: (0, 0))],
        out_specs=[pl.BlockSpec((1, d, tm // d, gw3), lambda i: (i // nt, 0, i % nt, 0)) for d in dils],
        out_shape=[jax.ShapeDtypeStruct((B, d, S // d, gw3), BF16) for d in dils],
        scratch_shapes=[pltpu.VMEM((n // LANES, tm, LANES), F32)],
        compiler_params=_params("parallel"),
        name="dswa_proj",
    )(x2, g.reshape(1, D), w_in)


def _dswa_layer(x2, g, w_in, w_out, bias, B, S, mlp):
    dils = tuple(d for _, d in DSWA_CONFIGS)
    qkvs = _dswa_proj(x2, g, w_in.astype(BF16), dils, B, S)
    os_, lses = [], []
    for gi in range(len(dils)):
        o, lse = _dswa_group(qkvs[gi], bias[gi])
        os_.append(o)
        lses.append(lse)
    return _dswa_out_mlp(os_, lses, x2, w_out.astype(BF16), S, *mlp)


def kernel(x, norm_mix, norm_mlp, norm_final, rel_bias, gdn_w_in, gdn_conv_w, gdn_a_log,
           gdn_dt_bias, gdn_norm_w, gdn_w_out, dswa_w_in, dswa_w_out, mlp_w1, mlp_w2):
    B, S, D = x.shape
    depth = norm_mix.shape[0]
    x2 = x.reshape(B * S, D)
    assert all(window // (2 * dil) == DSWA_HALF for window, dil in DSWA_CONFIGS)
    bias = _band_bias(rel_bias, tuple(d for _, d in DSWA_CONFIGS))
    for i in range(depth):
        j = i // 2
        mlp = (norm_mlp[i], mlp_w1[i].astype(BF16), mlp_w2[i].astype(BF16), norm_final, i == depth - 1)
        if i % 2 == 0:
            x2 = _gdn_layer(x2, norm_mix[i], gdn_w_in[j], gdn_conv_w[j], gdn_a_log[j],
                            gdn_dt_bias[j], gdn_norm_w[j], gdn_w_out[j], B, S, mlp)
        else:
            x2 = _dswa_layer(x2, norm_mix[i], dswa_w_in[j], dswa_w_out[j], bias, B, S, mlp)
    return x2.reshape(B, S, D)
```

```python
import functools
import math

import jax
import jax.numpy as jnp
from jax import lax
from jax.experimental import pallas as pl
from jax.experimental.pallas import tpu as pltpu

F32 = jnp.float32
BF16 = jnp.bfloat16

RMS_EPS = 1e-6
L2_EPS = 1e-6
NEG_INF = -1e30
LOG2E = math.log2(math.e)

LANES = 128
SUBLANES = 8
VMEM_LIMIT = 56 * 1024 * 1024

GDN_DK = 128
GDN_CONV = 5
GDN_CHUNK = LANES
DSWA_HEAD_DIM = 64
DSWA_CONFIGS = ((128, 1), (512, 4), (2048, 16))
DSWA_HALF = 64
REL_BUCKETS = 32
REL_MAX_DIST = 1024


def _params(*sem):
    return pltpu.CompilerParams(dimension_semantics=sem, vmem_limit_bytes=VMEM_LIMIT)


def _dot(a, b):
    return jnp.dot(a.astype(BF16), b.astype(BF16), preferred_element_type=F32)


def _dot_nt(a, b):
    return lax.dot_general(a.astype(BF16), b.astype(BF16), (((1,), (1,)), ((), ())),
                           preferred_element_type=F32)


def _split3(x):
    hi = x.astype(BF16)
    r = x - hi.astype(F32)
    mid = r.astype(BF16)
    return hi, mid, (r - mid.astype(F32)).astype(BF16)


def _rms(x, g):
    return x * lax.rsqrt(jnp.mean(x * x, axis=-1, keepdims=True) + RMS_EPS) * g


def _sigmoid(x):
    return 1.0 / (1.0 + jnp.exp(-x))


def _softplus(x):
    return jnp.maximum(x, 0.0) + jnp.log1p(jnp.exp(-jnp.abs(x)))


CONV_HALO = 16
CONV_BLOCK = 64


def _gdn_proj_kernel(x_ref, xp_ref, xn_ref, g_ref, wqkv_ref, wz_ref, w_ref, wt_ref, cw_ref,
                     prow_ref, pcol_ref, qkv_ref, z_ref, col_ref, row_ref, *, nh, nt, chunk):
    C = GDN_CHUNK
    tm = x_ref.shape[0]
    t = pl.program_id(0) % nt
    pad = GDN_CONV // 2
    g = g_ref[...]
    h = _rms(x_ref[...], g).astype(BF16)
    h_halo = jnp.concatenate([_rms(jnp.where(t > 0, xp_ref[...], 0.0), g).astype(BF16),
                              _rms(jnp.where(t < nt - 1, xn_ref[...], 0.0), g).astype(BF16)], axis=0)
    CB = CONV_BLOCK
    taps = [i for i in range(GDN_CONV) if i != pad]
    sr = lax.broadcasted_iota(jnp.int32, (len(taps) * CB, CB + 2 * CONV_HALO), 0)
    sc = lax.broadcasted_iota(jnp.int32, (len(taps) * CB, CB + 2 * CONV_HALO), 1)
    src = (sr % CB) + (CONV_HALO - pad)
    for k, i in enumerate(taps):
        src = src + jnp.where(sr // CB == k, i, 0)
    shift_mat = (sc == src).astype(BF16)

    def project(c0):
        w = wqkv_ref[:, c0:c0 + chunk]
        main = jnp.dot(h, w, preferred_element_type=F32)
        halo = jnp.dot(h_halo, w, preferred_element_type=F32)
        ext = jnp.concatenate([halo[:CONV_HALO], main, halo[CONV_HALO:]], axis=0).astype(BF16)
        return main, ext

    starts = list(range(0, wqkv_ref.shape[1], chunk))
    nxt = project(starts[0])
    for n, c0 in enumerate(starts):
        cs = slice(c0, c0 + chunk)
        main, ext = nxt
        if n + 1 < len(starts):
            nxt = project(starts[n + 1])
        for b in range(tm // CB):
            rows = slice(b * CB, (b + 1) * CB)
            shifted = jnp.dot(shift_mat, ext[b * CB:(b + 1) * CB + 2 * CONV_HALO],
                              preferred_element_type=F32)
            acc = cw_ref[pad:pad + 1, cs] * main[rows]
            for k, i in enumerate(taps):
                acc = acc + cw_ref[i:i + 1, cs] * shifted[k * CB:(k + 1) * CB]
            y = acc * _sigmoid(acc)
            for j in range(chunk // LANES):
                yj = y[:, j * LANES:(j + 1) * LANES]
                head = c0 // LANES + j
                if head < 2 * nh:
                    inv = lax.rsqrt(jnp.sum(yj * yj, axis=-1, keepdims=True) + L2_EPS)
                    yj = yj * (inv * (GDN_DK ** -0.5) if head < nh else inv)
                qkv_ref[rows, c0 + j * LANES:c0 + (j + 1) * LANES] = yj.astype(BF16)
    for c0 in range(0, wz_ref.shape[1], chunk):
        cs = slice(c0, c0 + chunk)
        z_ref[:, cs] = jnp.dot(h, wz_ref[:, cs], preferred_element_type=F32).astype(BF16)
    a_c = jnp.dot(h, w_ref[...], preferred_element_type=F32)
    a_r = lax.dot_general(wt_ref[...], h, (((1,), (1,)), ((), ())),
                          preferred_element_type=F32)
    ri = lax.broadcasted_iota(jnp.int32, (C, C), 0)
    ci = lax.broadcasted_iota(jnp.int32, (C, C), 1)
    lower = (ri >= ci).astype(BF16)
    upper = (ri <= ci).astype(BF16)
    g_c = -jnp.exp(prow_ref[0:1, :]) * _softplus(a_c + prow_ref[1:2, :])
    g_r = -jnp.exp(pcol_ref[:, 0:1]) * _softplus(a_r + pcol_ref[:, 1:2])
    beta_c = _sigmoid(a_c)
    beta_r = _sigmoid(a_r)
    for c in range(tm // C):
        sl = slice(c * C, (c + 1) * C)
        gc = g_c[sl, :]
        pre = sum(jnp.dot(lower, part, preferred_element_type=F32) for part in _split3(gc))
        cum = jnp.where(ci < nh, pre, pre[C - 1:C, :] - pre + gc)
        col_ref[sl, :] = jnp.where(ci < 2 * nh, cum, beta_c[sl, :])
        gr = g_r[:, sl]
        pre_r = sum(jnp.dot(part, upper, preferred_element_type=F32) for part in _split3(gr))
        cum_r = jnp.where(ri < nh, pre_r, pre_r[:, C - 1:C] - pre_r + gr)
        full = jnp.where(ri < 2 * nh, cum_r, beta_r[:, sl])
        row_ref[0, :, sl] = full[:4 * nh, :]


def _gdn_proj(x2, g, w_in, conv_w, a_log, dt_bias, B, S, tm=1024, chunk=256):
    T, D = x2.shape
    nh = a_log.shape[1]
    qkv_w = 3 * nh * LANES
    z_w = nh * LANES
    w_pad = jnp.zeros((D, LANES), F32).at[:, :4 * nh].set(w_in[:, qkv_w + z_w:])
    alog = jnp.zeros((LANES,), F32).at[:2 * nh].set(a_log.reshape(-1))
    dtb = jnp.zeros((LANES,), F32).at[:2 * nh].set(dt_bias.reshape(-1))
    prow = jnp.zeros((SUBLANES, LANES), F32).at[0].set(alog).at[1].set(dtb)
    pcol = jnp.zeros((LANES, LANES), F32).at[:, 0].set(alog).at[:, 1].set(dtb)
    nt = S // tm
    r = tm // CONV_HALO
    const = lambda shape: pl.BlockSpec(shape, lambda i: (0, 0))
    return pl.pallas_call(
        functools.partial(_gdn_proj_kernel, nh=nh, nt=nt, chunk=chunk),
        grid=(T // tm,),
        in_specs=[pl.BlockSpec((tm, D), lambda i: (i, 0)),
                  pl.BlockSpec((CONV_HALO, D), lambda i: (jnp.maximum(i * r - 1, 0), 0)),
                  pl.BlockSpec((CONV_HALO, D), lambda i: (jnp.minimum((i + 1) * r, T // CONV_HALO - 1), 0)),
                  const((1, D)), const((D, qkv_w)), const((D, z_w)), const((D, LANES)), const((LANES, D)),
                  const((GDN_CONV, qkv_w)), const((SUBLANES, LANES)), const((LANES, LANES))],
        out_specs=[pl.BlockSpec((tm, qkv_w), lambda i: (i, 0)),
                   pl.BlockSpec((tm, z_w), lambda i: (i, 0)),
                   pl.BlockSpec((tm, LANES), lambda i: (i, 0)),
                   pl.BlockSpec((1, 4 * nh, tm), lambda i: (i // nt, 0, i % nt))],
        out_shape=[jax.ShapeDtypeStruct((T, qkv_w), BF16),
                   jax.ShapeDtypeStruct((T, z_w), BF16),
                   jax.ShapeDtypeStruct((T, LANES), F32),
                   jax.ShapeDtypeStruct((B, 4 * nh, S), F32)],
        compiler_params=_params("parallel"),
        name="gdn_proj",
    )(x2, x2, x2, g.reshape(1, D), w_in[:, :qkv_w].astype(BF16), w_in[:, qkv_w:qkv_w + z_w].astype(BF16),
      w_pad.astype(BF16), w_pad.T.astype(BF16), conv_w.reshape(GDN_CONV, qkv_w), prow, pcol)


def _gdn_scan_kernel(q_ref, k_ref, v_ref, col_ref, row_ref, o_ref, state_ref, *, rev, nh):
    C = GDN_CHUNK
    t = pl.program_id(1)

    @pl.when(t == 0)
    def _():
        state_ref[...] = jnp.zeros_like(state_ref)

    ri = lax.broadcasted_iota(jnp.int32, (C, C), 0)
    ci = lax.broadcasted_iota(jnp.int32, (C, C), 1)
    incl = (ri <= ci) if rev else (ri >= ci)
    strict = (ri < ci) if rev else (ri > ci)
    eye = (ri == ci).astype(F32)
    levels = [((ri >> (lg + 1)) == (ci >> (lg + 1))) & ((ri >> lg) != (ci >> lg))
              for lg in range(int(math.log2(C)))]
    last = 0 if rev else C - 1
    nc = q_ref.shape[1] // C
    chunks = list(reversed(range(nc))) if rev else list(range(nc))
    heads = range(nh)
    hs = [slice(h * LANES, (h + 1) * LANES) for h in heads]
    gi = [(nh if rev else 0) + h for h in heads]
    units = [(c, h) for c in chunks for h in heads]
    rows = {c: slice(c * C, (c + 1) * C) for c in chunks}
    col = {c: col_ref[0, rows[c], :] for c in chunks}
    row = {c: row_ref[0, :, rows[c]] for c in chunks}
    kf = {u: k_ref[0, rows[u[0]], hs[u[1]]] for u in units}
    k = {u: kf[u].astype(F32) for u in units}
    g_col = {(c, h): col[c][:, gi[h]:gi[h] + 1] for c, h in units}
    b_col = {(c, h): col[c][:, 2 * nh + gi[h]:2 * nh + gi[h] + 1] for c, h in units}
    g_last = {u: g_col[u][last:last + 1, :] for u in units}
    decay = {(c, h): jnp.where(incl, jnp.exp(jnp.where(incl, g_col[c, h] - row[c][gi[h]:gi[h] + 1, :], 0.0)), 0.0)
             for c, h in units}
    kb = {u: k[u] * b_col[u] for u in units}
    res = {(c, h): _dot_nt(jnp.concatenate([kb[c, h].astype(BF16), q_ref[0, rows[c], hs[h]]], axis=0), kf[c, h])
           for c, h in units}
    a = {u: jnp.where(strict, res[u][:C] * decay[u], 0.0) for u in units}
    intra = {u: (res[u][C:] * decay[u]).astype(BF16) for u in units}
    tinv = {u: eye - jnp.where(levels[0], a[u], 0.0) for u in units}
    for lg, m in enumerate(levels[1:], start=1):
        size = 1 << lg
        moving = [r0 for r0 in range(0, C, size) if ((r0 >> lg) & 1) == (0 if rev else 1)]
        tb = {u: tinv[u].astype(BF16) for u in units}
        if size < SUBLANES:
            pm = {u: _dot(tb[u], jnp.where(m, a[u], 0.0)) for u in units}
            tinv = {u: tinv[u] - _dot(pm[u], tb[u]) for u in units}
            continue
        lhs = {u: jnp.concatenate([tinv[u][r0:r0 + size] for r0 in moving], axis=0) for u in units}
        pm = {u: _dot(lhs[u], jnp.where(m, a[u], 0.0)) for u in units}
        new = {u: lhs[u] - _dot(pm[u], tb[u]) for u in units}
        tinv = {u: jnp.concatenate(
            [new[u][moving.index(r0) * size:(moving.index(r0) + 1) * size] if r0 in moving
             else tinv[u][r0:r0 + size] for r0 in range(0, C, size)], axis=0) for u in units}
    e_col = {u: jnp.exp(g_col[u]) for u in units}
    sol = {(c, h): _dot(tinv[c, h], jnp.concatenate(
        [v_ref[0, rows[c], hs[h]].astype(F32) * b_col[c, h], kb[c, h] * e_col[c, h]], axis=1)) for c, h in units}
    qd = {(c, h): (q_ref[0, rows[c], hs[h]].astype(F32) * e_col[c, h]).astype(BF16) for c, h in units}
    k_dec = {u: (k[u] * jnp.exp(g_last[u] - g_col[u])).T.astype(BF16) for u in units}
    state = [state_ref[h] for h in heads]
    for c in chunks:
        sb = [state[h].astype(BF16) for h in heads]
        r2 = [_dot(jnp.concatenate([sol[c, h][:, LANES:].astype(BF16), qd[c, h]], axis=0), sb[h]) for h in heads]
        v_new = [(sol[c, h][:, :LANES] - r2[h][:C]).astype(BF16) for h in heads]
        for h in heads:
            o_ref[0, rows[c], hs[h]] = (r2[h][C:] + _dot(intra[c, h], v_new[h])).astype(o_ref.dtype)
        state = [state[h] * jnp.exp(g_last[c, h]) + _dot(k_dec[c, h], v_new[h]) for h in heads]
    for h in heads:
        state_ref[h] = state[h]


def _gdn_scan(qkvn, col, row, nh, rev, nc=4):
    B, S, _ = qkvn.shape
    ts = nc * GDN_CHUNK
    nt = S // ts
    width = nh * LANES
    tmap = (lambda t: nt - 1 - t) if rev else (lambda t: t)
    col3 = col.reshape(B, S, LANES)
    return pl.pallas_call(
        functools.partial(_gdn_scan_kernel, rev=rev, nh=nh),
        grid=(B, nt),
        in_specs=[pl.BlockSpec((1, ts, width), lambda b, t: (b, tmap(t), 0)),
                  pl.BlockSpec((1, ts, width), lambda b, t: (b, tmap(t), 1)),
                  pl.BlockSpec((1, ts, width), lambda b, t: (b, tmap(t), 2)),
                  pl.BlockSpec((1, ts, LANES), lambda b, t: (b, tmap(t), 0)),
                  pl.BlockSpec((1, 4 * nh, ts), lambda b, t: (b, 0, tmap(t)))],
        out_specs=pl.BlockSpec((1, ts, width), lambda b, t: (b, tmap(t), 0)),
        out_shape=jax.ShapeDtypeStruct((B, S, width), BF16),
        scratch_shapes=[pltpu.VMEM((nh, GDN_DK, LANES), F32)],
        compiler_params=_params("parallel", "arbitrary"),
        name="gdn_scan_bwd" if rev else "gdn_scan_fwd",
    )(qkvn, qkvn, qkvn, col3, row)


MLP_CHUNK = 512


def _anchor(v):
    folded = jnp.sum(v.reshape(v.shape[0] // SUBLANES, SUBLANES, LANES), axis=0)
    bits = lax.shift_right_logical(pltpu.bitcast(folded, jnp.uint32), jnp.uint32(32))
    return pltpu.bitcast(bits, F32)


def _out_mlp_tail(y_ref, x_ref, wo_ref, g_ref, w1_ref, w2_ref, gf_ref, o_ref, x1_ref, acc_ref, final,
                  anchors=()):
    x1_ref[...] = x_ref[...] + jnp.dot(y_ref[...], wo_ref[...], preferred_element_type=F32)
    h = _rms(x1_ref[...], g_ref[...]).astype(BF16)
    starts = list(range(0, w1_ref.shape[1], MLP_CHUNK))
    for n, c in enumerate(starts):
        a = jnp.maximum(jnp.dot(h, w1_ref[:, c:c + MLP_CHUNK], preferred_element_type=F32), 0.0)
        part = jnp.dot((a * a).astype(BF16), w2_ref[c:c + MLP_CHUNK, :], preferred_element_type=F32)
        if c == 0:
            acc_ref[...] = part
        else:
            acc_ref[...] += part
        for k in range(n, len(anchors), len(starts)):
            acc_ref[0:SUBLANES, 0:LANES] += anchors[k]
    out = x1_ref[...] + acc_ref[...]
    if final:
        out = _rms(out, gf_ref[...])
    o_ref[...] = out


def _mixer_out_mlp(prologue, y_ref, tail_refs, final, lookahead):
    if not lookahead:
        prologue(y_ref.at[0])
        _out_mlp_tail(y_ref.at[0], *tail_refs, final)
        return
    i = pl.program_id(0)

    @pl.when(i == 0)
    def _():
        y_ref[...] = jnp.zeros_like(y_ref)

    for parity in range(2):
        @pl.when(i % 2 == parity)
        def _():
            pieces = prologue(y_ref.at[parity])
            _out_mlp_tail(y_ref.at[1 - parity], *tail_refs, final, anchors=[_anchor(v) for v in pieces])


def _out_mlp_call(kernel_fn, mixer_args, mixer_specs, mixer_scratch, x2, w_out, g, w1, w2,
                  g_final, tm, name, lookahead):
    T, D = x2.shape
    n = T // tm
    if lookahead:
        cur = lambda i: jnp.minimum(i, n - 1)
        prev = lambda i: jnp.maximum(i - 1, 0)
    else:
        cur = prev = lambda i: i
    once = lambda shape: pl.BlockSpec(shape, lambda i: (0, 0), pipeline_mode=pl.Buffered(1))
    return pl.pallas_call(
        kernel_fn,
        grid=(n + 1 if lookahead else n,),
        in_specs=mixer_specs(cur) + [pl.BlockSpec((tm, D), lambda i: (prev(i), 0)), once(w_out.shape),
                                     once((1, D)), once(w1.shape), once(w2.shape), once((1, D))],
        out_specs=pl.BlockSpec((tm, D), lambda i: (prev(i), 0)),
        out_shape=jax.ShapeDtypeStruct((T, D), F32),
        scratch_shapes=mixer_scratch + [pltpu.VMEM((2 if lookahead else 1, tm, w_out.shape[0]), BF16),
                                        pltpu.VMEM((tm, D), F32), pltpu.VMEM((tm, D), F32)],
        compiler_params=_params("arbitrary" if lookahead else "parallel"),
        name=name,
    )(*mixer_args, x2, w_out, g.reshape(1, D), w1, w2, g_final.reshape(1, D))


def _gdn_out_mlp_kernel(of_ref, ob_ref, z_ref, nw_ref, x_ref, wo_ref, g_ref, w1_ref, w2_ref, gf_ref,
                        o_ref, y_ref, x1_ref, acc_ref, *, nh, final):
    def prologue(y):
        pieces = []
        for h in range(nh):
            hs = slice(h * LANES, (h + 1) * LANES)
            o = of_ref[:, hs].astype(F32) + ob_ref[:, hs].astype(F32)
            o = o * lax.rsqrt(jnp.mean(o * o, axis=-1, keepdims=True) + RMS_EPS)
            z = z_ref[:, hs].astype(F32)
            pieces.append(o * nw_ref[...] * (z * _sigmoid(z)))
            y[:, hs] = pieces[-1].astype(BF16)
        return pieces
    _mixer_out_mlp(prologue, y_ref, (x_ref, wo_ref, g_ref, w1_ref, w2_ref, gf_ref, o_ref, x1_ref, acc_ref),
                   final, lookahead=True)


def _gdn_out_mlp(o_f, o_b, z, x2, norm_w, w_out, nh, g, w1, w2, g_final, final, tm=512):
    W = o_f.shape[1]
    specs = lambda cur: [pl.BlockSpec((tm, W), lambda i: (cur(i), 0))] * 3 + \
                        [pl.BlockSpec((1, LANES), lambda i: (0, 0))]
    return _out_mlp_call(
        functools.partial(_gdn_out_mlp_kernel, nh=nh, final=final),
        (o_f, o_b, z, norm_w.reshape(1, LANES)), specs,
        [], x2, w_out, g, w1, w2, g_final, tm, "gdn_out_mlp", lookahead=True)


def _dswa_kernel(q_ref, k_ref, kp_ref, kn_ref, v_ref, vp_ref, vn_ref, bias_ref,
                 o_ref, lse_ref, *, sub_len):
    t = pl.program_id(2)
    rb, lt = q_ref.shape[1], q_ref.shape[2]
    hb = DSWA_HALF
    first_head = lax.broadcasted_iota(jnp.int32, (1, LANES), 1) < DSWA_HEAD_DIM
    n_pairs = q_ref.shape[3] // LANES
    blocks = range(lt // hb)
    units = [(rc, qb, pr) for rc in range(rb) for qb in blocks for pr in range(n_pairs)]
    rows = [slice(qb * hb, (qb + 1) * hb) for qb in blocks]
    win = [slice(qb * hb, (qb + 3) * hb) for qb in blocks]
    ls = [slice(pr * LANES, (pr + 1) * LANES) for pr in range(n_pairs)]
    kext = [jnp.concatenate([kp_ref[0, rc], k_ref[0, rc], kn_ref[0, rc]], axis=0) for rc in range(rb)]
    vext = [jnp.concatenate([vp_ref[0, rc], v_ref[0, rc], vn_ref[0, rc]], axis=0) for rc in range(rb)]
    kpos = lambda qb: t * lt + (qb - 1) * hb + lax.broadcasted_iota(jnp.int32, (1, 3 * hb), 1)
    qs = {(rc, qb, pr): jnp.concatenate([jnp.where(first_head, q_ref[0, rc, rows[qb], ls[pr]], 0.0),
                                         jnp.where(first_head, 0.0, q_ref[0, rc, rows[qb], ls[pr]])], axis=0)
          for rc, qb, pr in units}
    s = {(rc, qb, pr): _dot_nt(qs[rc, qb, pr], kext[rc][win[qb], ls[pr]]) + bias_ref[pr] for rc, qb, pr in units}
    for rc, qb, pr in units:
        if qb == blocks[0]:
            s[rc, qb, pr] = jnp.where(kpos(qb) >= 0, s[rc, qb, pr], NEG_INF)
        if qb == blocks[-1]:
            s[rc, qb, pr] = jnp.where(kpos(qb) < sub_len, s[rc, qb, pr], NEG_INF)
    m = {u: jnp.max(s[u], axis=-1, keepdims=True) for u in units}
    p = {u: jnp.exp2(s[u] - m[u]) for u in units}
    l = {u: jnp.sum(p[u], axis=-1, keepdims=True) for u in units}
    pv = {(rc, qb, pr): _dot(p[rc, qb, pr], vext[rc][win[qb], ls[pr]]) * (1.0 / l[rc, qb, pr])
          for rc, qb, pr in units}
    lse = {u: (m[u] + jnp.log2(l[u])) * math.log(2.0) for u in units}
    for rc, qb, pr in units:
        o_ref[0, rc, rows[qb], ls[pr]] = jnp.where(first_head, pv[rc, qb, pr][:hb], pv[rc, qb, pr][hb:])
        lse_ref[0, rc, rows[qb], ls[pr]] = jnp.where(first_head, lse[rc, qb, pr][:hb], lse[rc, qb, pr][hb:])


def _rel_bucket(rel):
    nb = REL_BUCKETS // 2
    max_exact = nb // 2
    ret = jnp.where(rel > 0, nb, 0)
    n = jnp.abs(rel)
    nf = jnp.maximum(n, 1).astype(F32)
    large = max_exact + (jnp.log(nf * (1.0 / max_exact)) / math.log(REL_MAX_DIST / max_exact)
                         * (nb - max_exact)).astype(jnp.int32)
    large = jnp.minimum(large, nb - 1)
    return ret + jnp.where(n < max_exact, n, large)


def _band_bias_kernel(tab_ref, o_ref, *, dils, hg):
    hb = DSWA_HALF
    off = (lax.broadcasted_iota(jnp.int32, (hb, 3 * hb), 1) - hb
           - lax.broadcasted_iota(jnp.int32, (hb, 3 * hb), 0))
    inband = jnp.abs(off) <= hb
    for g, d in enumerate(dils):
        bucket = _rel_bucket(off * d)
        for hh in range(hg):
            acc = jnp.zeros((hb, 3 * hb), F32)
            for b in range(REL_BUCKETS):
                acc = jnp.where(bucket == b, tab_ref[b, g * hg + hh], acc)
            o_ref[g, hh // 2, (hh % 2) * hb:(hh % 2 + 1) * hb, :] = jnp.where(inband, acc * LOG2E, NEG_INF)


def _band_bias(rel_table, dils):
    hb = DSWA_HALF
    hg = rel_table.shape[1] // len(dils)
    return pl.pallas_call(
        functools.partial(_band_bias_kernel, dils=dils, hg=hg),
        in_specs=[pl.BlockSpec(memory_space=pltpu.SMEM)],
        out_shape=jax.ShapeDtypeStruct((len(dils), hg // 2, 2 * hb, 3 * hb), F32),
        name="band_bias",
    )(rel_table)


def _dswa_group(qkv, bias, lt=512):
    B, dil, L, W3 = qkv.shape
    hb = DSWA_HALF
    gw = W3 // 3
    rb = max(1, lt // L)
    lt = min(lt, L)
    r = lt // hb
    main = lambda which: pl.BlockSpec((1, rb, lt, gw), lambda b, d, t: (b, d, t, which))
    prev = lambda which: pl.BlockSpec(
        (1, rb, hb, gw), lambda b, d, t: (b, d, jnp.maximum(t * r - 1, 0), which))
    nxt = lambda which: pl.BlockSpec(
        (1, rb, hb, gw), lambda b, d, t: (b, d, jnp.minimum((t + 1) * r, L // hb - 1), which))
    out = pl.BlockSpec((1, rb, lt, gw), lambda b, d, t: (b, d, t, 0))
    return pl.pallas_call(
        functools.partial(_dswa_kernel, sub_len=L),
        grid=(B, dil // rb, L // lt),
        in_specs=[main(0), main(1), prev(1), nxt(1), main(2), prev(2), nxt(2),
                  pl.BlockSpec(bias.shape, lambda b, d, t: (0, 0, 0))],
        out_specs=[out, out],
        out_shape=[jax.ShapeDtypeStruct((B, dil, L, gw), F32)] * 2,
        compiler_params=_params("parallel", "parallel", "parallel"),
        name=f"dswa_d{dil}",
    )(qkv, qkv, qkv, qkv, qkv, qkv, qkv, bias)


def _dswa_out_mlp_kernel(*refs, dils, final):
    ng = len(dils)
    o_refs, l_refs = refs[:ng], refs[ng:2 * ng]
    x_ref, wo_ref, g_ref, w1_ref, w2_ref, gf_ref, out_ref, o_s, l_s, y_s, x1_ref, acc_ref = refs[2 * ng:]
    tm = x_ref.shape[0]
    tiles = o_refs[0].shape[3] // LANES
    def prologue(y):
        for g, d in enumerate(dils):
            for r in range(d):
                for j in range(tiles):
                    ls = slice(j * LANES, (j + 1) * LANES)
                    o_s[g * tiles + j, pl.ds(r, tm // d, stride=d), :] = o_refs[g][0, r, :, ls]
                    l_s[g * tiles + j, pl.ds(r, tm // d, stride=d), :] = l_refs[g][0, r, :, ls]
        pieces = []
        for j in range(tiles):
            lses = [l_s[g * tiles + j] for g in range(ng)]
            m = functools.reduce(jnp.maximum, lses)
            es = [jnp.exp(l - m) for l in lses]
            inv = 1.0 / functools.reduce(lambda a, b: a + b, es)
            for g in range(ng):
                c = g * tiles + j
                pieces.append(o_s[c] * (es[g] * inv))
                y[:, c * LANES:(c + 1) * LANES] = pieces[-1].astype(BF16)
        return pieces
    _mixer_out_mlp(prologue, y_s, (x_ref, wo_ref, g_ref, w1_ref, w2_ref, gf_ref, out_ref, x1_ref, acc_ref),
                   final, lookahead=False)


def _dswa_out_mlp(os_, lses, x2, w_out, S, g, w1, w2, g_final, final, tm=512):
    dils = tuple(o.shape[1] for o in os_)
    gw = os_[0].shape[3]
    nt = S // tm
    specs = lambda cur: [pl.BlockSpec((1, d, tm // d, gw), lambda i: (cur(i) // nt, 0, cur(i) % nt, 0))
                         for d in dils] * 2
    staging = pltpu.VMEM((len(dils) * gw // LANES, tm, LANES), F32)
    return _out_mlp_call(
        functools.partial(_dswa_out_mlp_kernel, dils=dils, final=final),
        (*os_, *lses), specs, [staging, staging],
        x2, w_out, g, w1, w2, g_final, tm, "dswa_out_mlp", lookahead=False)


def _gdn_layer(x2, g, w_in, conv_w, a_log, dt_bias, norm_w, w_out, B, S, mlp):
    nh = a_log.shape[1]
    z_w = nh * LANES
    qkvn, z, col, row = _gdn_proj(x2, g, w_in, conv_w, a_log, dt_bias, B, S)
    qkvn = qkvn.reshape(B, S, 3 * z_w)
    o_f = _gdn_scan(qkvn, col, row, nh, rev=False)
    o_b = _gdn_scan(qkvn, col, row, nh, rev=True)
    return _gdn_out_mlp(o_f.reshape(B * S, z_w), o_b.reshape(B * S, z_w), z, x2, norm_w,
                        w_out.astype(BF16), nh, *mlp)


def _dswa_proj_kernel(x_ref, g_ref, w_ref, *refs, dils, chunk):
    o_refs, acc_ref = refs[:-1], refs[-1]
    tm = x_ref.shape[0]
    n = w_ref.shape[1]
    ng = len(dils)
    gw = n // (3 * ng)
    tiles = gw // LANES
    h = _rms(x_ref[...], g_ref[...]).astype(BF16)
    def project(which):
        res = jnp.dot(h, w_ref[:, which * chunk:(which + 1) * chunk], preferred_element_type=F32)
        if which == 0:
            res = res * (DSWA_HEAD_DIM ** -0.5 * LOG2E)
        for j in range(chunk // LANES):
            acc_ref[which * ng * tiles + j] = res[:, j * LANES:(j + 1) * LANES]

    def regroup(which):
        for g, d in enumerate(dils):
            for r in range(d):
                for j in range(tiles):
                    dst = slice((which * tiles + j) * LANES, (which * tiles + j + 1) * LANES)
                    o_refs[g][0, r, :, dst] = \
                        acc_ref[(which * ng + g) * tiles + j, pl.ds(r, tm // d, stride=d), :].astype(BF16)

    project(0)
    for which in range(1, 3):
        project(which)
        regroup(which - 1)
    regroup(2)


def _dswa_proj(x2, g, w_in, dils, B, S, tm=512):
    T, D = x2.shape
    n = w_in.shape[1]
    gw3 = n // len(dils)
    nt = S // tm
    return pl.pallas_call(
        functools.partial(_dswa_proj_kernel, dils=dils, chunk=gw3),
        grid=(T // tm,),
        in_specs=[pl.BlockSpec((tm, D), lambda i: (i, 0)),
                  pl.BlockSpec((1, D), lambda i: (0, 0)),
                  pl.BlockSpec((D, n), lambda i: (0, 0))],
        out_specs=[pl.BlockSpec((1, d, tm // d, gw3), lambda i: (i // nt, 0, i % nt, 0)) for d in dils],
        out_shape=[jax.ShapeDtypeStruct((B, d, S // d, gw3), BF16) for d in dils],
        scratch_shapes=[pltpu.VMEM((n // LANES, tm, LANES), F32)],
        compiler_params=_params("parallel"),
        name="dswa_proj",
    )(x2, g.reshape(1, D), w_in)


def _dswa_layer(x2, g, w_in, w_out, bias, B, S, mlp):
    dils = tuple(d for _, d in DSWA_CONFIGS)
    qkvs = _dswa_proj(x2, g, w_in.astype(BF16), dils, B, S)
    os_, lses = [], []
    for gi in range(len(dils)):
        o, lse = _dswa_group(qkvs[gi], bias[gi])
        os_.append(o)
        lses.append(lse)
    return _dswa_out_mlp(os_, lses, x2, w_out.astype(BF16), S, *mlp)


def kernel(x, norm_mix, norm_mlp, norm_final, rel_bias, gdn_w_in, gdn_conv_w, gdn_a_log,
           gdn_dt_bias, gdn_norm_w, gdn_w_out, dswa_w_in, dswa_w_out, mlp_w1, mlp_w2):
    B, S, D = x.shape
    depth = norm_mix.shape[0]
    x2 = x.reshape(B * S, D)
    assert all(window // (2 * dil) == DSWA_HALF for window, dil in DSWA_CONFIGS)
    bias = _band_bias(rel_bias, tuple(d for _, d in DSWA_CONFIGS))
    for i in range(depth):
        j = i // 2
        mlp = (norm_mlp[i], mlp_w1[i].astype(BF16), mlp_w2[i].astype(BF16), norm_final, i == depth - 1)
        if i % 2 == 0:
            x2 = _gdn_layer(x2, norm_mix[i], gdn_w_in[j], gdn_conv_w[j], gdn_a_log[j],
                            gdn_dt_bias[j], gdn_norm_w[j], gdn_w_out[j], B, S, mlp)
        else:
            x2 = _dswa_layer(x2, norm_mix[i], dswa_w_in[j], dswa_w_out[j], bias, B, S, mlp)
    return x2.reshape(B, S, D)
```

```python
import functools
import math

import jax
import jax.numpy as jnp
from jax import lax
from jax.experimental import pallas as pl
from jax.experimental.pallas import tpu as pltpu

F32 = jnp.float32
BF16 = jnp.bfloat16

RMS_EPS = 1e-6
L2_EPS = 1e-6
NEG_INF = -1e30
LOG2E = math.log2(math.e)

LANES = 128
SUBLANES = 8
VMEM_LIMIT = 56 * 1024 * 1024

GDN_DK = 128
GDN_CONV = 5
GDN_CHUNK = LANES
DSWA_HEAD_DIM = 64
DSWA_CONFIGS = ((128, 1), (512, 4), (2048, 16))
DSWA_HALF = 64
REL_BUCKETS = 32
REL_MAX_DIST = 1024


def _params(*sem):
    return pltpu.CompilerParams(dimension_semantics=sem, vmem_limit_bytes=VMEM_LIMIT)


def _dot(a, b):
    return jnp.dot(a.astype(BF16), b.astype(BF16), preferred_element_type=F32)


def _dot_nt(a, b):
    return lax.dot_general(a.astype(BF16), b.astype(BF16), (((1,), (1,)), ((), ())),
                           preferred_element_type=F32)


def _split3(x):
    hi = x.astype(BF16)
    r = x - hi.astype(F32)
    mid = r.astype(BF16)
    return hi, mid, (r - mid.astype(F32)).astype(BF16)


def _rms(x, g):
    return x * lax.rsqrt(jnp.mean(x * x, axis=-1, keepdims=True) + RMS_EPS) * g


def _sigmoid(x):
    return 1.0 / (1.0 + jnp.exp(-x))


def _softplus(x):
    return jnp.maximum(x, 0.0) + jnp.log1p(jnp.exp(-jnp.abs(x)))


CONV_HALO = 16
CONV_BLOCK = 64


def _gdn_proj_kernel(x_ref, xp_ref, xn_ref, g_ref, wqkv_ref, wz_ref, w_ref, wt_ref, cw_ref,
                     prow_ref, pcol_ref, qkv_ref, z_ref, col_ref, row_ref, *, nh, nt, chunk):
    C = GDN_CHUNK
    tm = x_ref.shape[0]
    t = pl.program_id(0) % nt
    pad = GDN_CONV // 2
    g = g_ref[...]
    h = _rms(x_ref[...], g).astype(BF16)
    h_halo = jnp.concatenate([_rms(jnp.where(t > 0, xp_ref[...], 0.0), g).astype(BF16),
                              _rms(jnp.where(t < nt - 1, xn_ref[...], 0.0), g).astype(BF16)], axis=0)
    CB = CONV_BLOCK
    taps = [i for i in range(GDN_CONV) if i != pad]
    sr = lax.broadcasted_iota(jnp.int32, (len(taps) * CB, CB + 2 * CONV_HALO), 0)
    sc = lax.broadcasted_iota(jnp.int32, (len(taps) * CB, CB + 2 * CONV_HALO), 1)
    src = (sr % CB) + (CONV_HALO - pad)
    for k, i in enumerate(taps):
        src = src + jnp.where(sr // CB == k, i, 0)
    shift_mat = (sc == src).astype(BF16)

    def project(c0):
        w = wqkv_ref[:, c0:c0 + chunk]
        main = jnp.dot(h, w, preferred_element_type=F32)
        halo = jnp.dot(h_halo, w, preferred_element_type=F32)
        ext = jnp.concatenate([halo[:CONV_HALO], main, halo[CONV_HALO:]], axis=0).astype(BF16)
        return main, ext

    starts = list(range(0, wqkv_ref.shape[1], chunk))
    nxt = project(starts[0])
    for n, c0 in enumerate(starts):
        cs = slice(c0, c0 + chunk)
        main, ext = nxt
        if n + 1 < len(starts):
            nxt = project(starts[n + 1])
        for b in range(tm // CB):
            rows = slice(b * CB, (b + 1) * CB)
            shifted = jnp.dot(shift_mat, ext[b * CB:(b + 1) * CB + 2 * CONV_HALO],
                              preferred_element_type=F32)
            acc = cw_ref[pad:pad + 1, cs] * main[rows]
            for k, i in enumerate(taps):
                acc = acc + cw_ref[i:i + 1, cs] * shifted[k * CB:(k + 1) * CB]
            y = acc * _sigmoid(acc)
            for j in range(chunk // LANES):
                yj = y[:, j * LANES:(j + 1) * LANES]
                head = c0 // LANES + j
                if head < 2 * nh:
                    inv = lax.rsqrt(jnp.sum(yj * yj, axis=-1, keepdims=True) + L2_EPS)
                    yj = yj * (inv * (GDN_DK ** -0.5) if head < nh else inv)
                qkv_ref[rows, c0 + j * LANES:c0 + (j + 1) * LANES] = yj.astype(BF16)
    for c0 in range(0, wz_ref.shape[1], chunk):
        cs = slice(c0, c0 + chunk)
        z_ref[:, cs] = jnp.dot(h, wz_ref[:, cs], preferred_element_type=F32).astype(BF16)
    a_c = jnp.dot(h, w_ref[...], preferred_element_type=F32)
    a_r = lax.dot_general(wt_ref[...], h, (((1,), (1,)), ((), ())),
                          preferred_element_type=F32)
    ri = lax.broadcasted_iota(jnp.int32, (C, C), 0)
    ci = lax.broadcasted_iota(jnp.int32, (C, C), 1)
    lower = (ri >= ci).astype(BF16)
    upper = (ri <= ci).astype(BF16)
    g_c = -jnp.exp(prow_ref[0:1, :]) * _softplus(a_c + prow_ref[1:2, :])
    g_r = -jnp.exp(pcol_ref[:, 0:1]) * _softplus(a_r + pcol_ref[:, 1:2])
    beta_c = _sigmoid(a_c)
    beta_r = _sigmoid(a_r)
    for c in range(tm // C):
        sl = slice(c * C, (c + 1) * C)
        gc = g_c[sl, :]
        pre = sum(jnp.dot(lower, part, preferred_element_type=F32) for part in _split3(gc))
        cum = jnp.where(ci < nh, pre, pre[C - 1:C, :] - pre + gc)
        col_ref[sl, :] = jnp.where(ci < 2 * nh, cum, beta_c[sl, :])
        gr = g_r[:, sl]
        pre_r = sum(jnp.dot(part, upper, preferred_element_type=F32) for part in _split3(gr))
        cum_r = jnp.where(ri < nh, pre_r, pre_r[:, C - 1:C] - pre_r + gr)
        full = jnp.where(ri < 2 * nh, cum_r, beta_r[:, sl])
        row_ref[0, :, sl] = full[:4 * nh, :]


def _gdn_proj(x2, g, w_in, conv_w, a_log, dt_bias, B, S, tm=1024, chunk=256):
    T, D = x2.shape
    nh = a_log.shape[1]
    qkv_w = 3 * nh * LANES
    z_w = nh * LANES
    w_pad = jnp.zeros((D, LANES), F32).at[:, :4 * nh].set(w_in[:, qkv_w + z_w:])
    alog = jnp.zeros((LANES,), F32).at[:2 * nh].set(a_log.reshape(-1))
    dtb = jnp.zeros((LANES,), F32).at[:2 * nh].set(dt_bias.reshape(-1))
    prow = jnp.zeros((SUBLANES, LANES), F32).at[0].set(alog).at[1].set(dtb)
    pcol = jnp.zeros((LANES, LANES), F32).at[:, 0].set(alog).at[:, 1].set(dtb)
    nt = S // tm
    r = tm // CONV_HALO
    const = lambda shape: pl.BlockSpec(shape, lambda i: (0, 0))
    return pl.pallas_call(
        functools.partial(_gdn_proj_kernel, nh=nh, nt=nt, chunk=chunk),
        grid=(T // tm,),
        in_specs=[pl.BlockSpec((tm, D), lambda i: (i, 0)),
                  pl.BlockSpec((CONV_HALO, D), lambda i: (jnp.maximum(i * r - 1, 0), 0)),
                  pl.BlockSpec((CONV_HALO, D), lambda i: (jnp.minimum((i + 1) * r, T // CONV_HALO - 1), 0)),
                  const((1, D)), const((D, qkv_w)), const((D, z_w)), const((D, LANES)), const((LANES, D)),
                  const((GDN_CONV, qkv_w)), const((SUBLANES, LANES)), const((LANES, LANES))],
        out_specs=[pl.BlockSpec((tm, qkv_w), lambda i: (i, 0)),
                   pl.BlockSpec((tm, z_w), lambda i: (i, 0)),
                   pl.BlockSpec((tm, LANES), lambda i: (i, 0)),
                   pl.BlockSpec((1, 4 * nh, tm), lambda i: (i // nt, 0, i % nt))],
        out_shape=[jax.ShapeDtypeStruct((T, qkv_w), BF16),
                   jax.ShapeDtypeStruct((T, z_w), BF16),
                   jax.ShapeDtypeStruct((T, LANES), F32),
                   jax.ShapeDtypeStruct((B, 4 * nh, S), F32)],
        compiler_params=_params("parallel"),
        name="gdn_proj",
    )(x2, x2, x2, g.reshape(1, D), w_in[:, :qkv_w].astype(BF16), w_in[:, qkv_w:qkv_w + z_w].astype(BF16),
      w_pad.astype(BF16), w_pad.T.astype(BF16), conv_w.reshape(GDN_CONV, qkv_w), prow, pcol)


def _gdn_scan_kernel(q_ref, k_ref, v_ref, col_ref, row_ref, o_ref, state_ref, *, rev, nh):
    C = GDN_CHUNK
    t = pl.program_id(1)

    @pl.when(t == 0)
    def _():
        state_ref[...] = jnp.zeros_like(state_ref)

    ri = lax.broadcasted_iota(jnp.int32, (C, C), 0)
    ci = lax.broadcasted_iota(jnp.int32, (C, C), 1)
    incl = (ri <= ci) if rev else (ri >= ci)
    strict = (ri < ci) if rev else (ri > ci)
    eye = (ri == ci).astype(F32)
    levels = [((ri >> (lg + 1)) == (ci >> (lg + 1))) & ((ri >> lg) != (ci >> lg))
              for lg in range(int(math.log2(C)))]
    last = 0 if rev else C - 1
    nc = q_ref.shape[1] // C
    chunks = list(reversed(range(nc))) if rev else list(range(nc))
    heads = range(nh)
    hs = [slice(h * LANES, (h + 1) * LANES) for h in heads]
    gi = [(nh if rev else 0) + h for h in heads]
    units = [(c, h) for c in chunks for h in heads]
    rows = {c: slice(c * C, (c + 1) * C) for c in chunks}
    col = {c: col_ref[0, rows[c], :] for c in chunks}
    row = {c: row_ref[0, :, rows[c]] for c in chunks}
    kf = {u: k_ref[0, rows[u[0]], hs[u[1]]] for u in units}
    k = {u: kf[u].astype(F32) for u in units}
    g_col = {(c, h): col[c][:, gi[h]:gi[h] + 1] for c, h in units}
    b_col = {(c, h): col[c][:, 2 * nh + gi[h]:2 * nh + gi[h] + 1] for c, h in units}
    g_last = {u: g_col[u][last:last + 1, :] for u in units}
    decay = {(c, h): jnp.where(incl, jnp.exp(jnp.where(incl, g_col[c, h] - row[c][gi[h]:gi[h] + 1, :], 0.0)), 0.0)
             for c, h in units}
    kb = {u: k[u] * b_col[u] for u in units}
    res = {(c, h): _dot_nt(jnp.concatenate([kb[c, h].astype(BF16), q_ref[0, rows[c], hs[h]]], axis=0), kf[c, h])
           for c, h in units}
    a = {u: jnp.where(strict, res[u][:C] * decay[u], 0.0) for u in units}
    intra = {u: (res[u][C:] * decay[u]).astype(BF16) for u in units}
    tinv = {u: eye - jnp.where(levels[0], a[u], 0.0) for u in units}
    for lg, m in enumerate(levels[1:], start=1):
        size = 1 << lg
        moving = [r0 for r0 in range(0, C, size) if ((r0 >> lg) & 1) == (0 if rev else 1)]
        tb = {u: tinv[u].astype(BF16) for u in units}
        if size < SUBLANES:
            pm = {u: _dot(tb[u], jnp.where(m, a[u], 0.0)) for u in units}
            tinv = {u: tinv[u] - _dot(pm[u], tb[u]) for u in units}
            continue
        lhs = {u: jnp.concatenate([tinv[u][r0:r0 + size] for r0 in moving], axis=0) for u in units}
        pm = {u: _dot(lhs[u], jnp.where(m, a[u], 0.0)) for u in units}
        new = {u: lhs[u] - _dot(pm[u], tb[u]) for u in units}
        tinv = {u: jnp.concatenate(
            [new[u][moving.index(r0) * size:(moving.index(r0) + 1) * size] if r0 in moving
             else tinv[u][r0:r0 + size] for r0 in range(0, C, size)], axis=0) for u in units}
    e_col = {u: jnp.exp(g_col[u]) for u in units}
    sol = {(c, h): _dot(tinv[c, h], jnp.concatenate(
        [v_ref[0, rows[c], hs[h]].astype(F32) * b_col[c, h], kb[c, h] * e_col[c, h]], axis=1)) for c, h in units}
    qd = {(c, h): (q_ref[0, rows[c], hs[h]].astype(F32) * e_col[c, h]).astype(BF16) for c, h in units}
    post = {u: jnp.concatenate([intra[u], (k[u] * jnp.exp(g_last[u] - g_col[u])).T.astype(BF16)], axis=0)
            for u in units}
    state = [state_ref[h] for h in heads]
    for c in chunks:
        sb = [state[h].astype(BF16) for h in heads]
        r2 = [_dot(jnp.concatenate([sol[c, h][:, LANES:].astype(BF16), qd[c, h]], axis=0), sb[h]) for h in heads]
        v_new = [(sol[c, h][:, :LANES] - r2[h][:C]).astype(BF16) for h in heads]
        r3 = [_dot(post[c, h], v_new[h]) for h in heads]
        for h in heads:
            o_ref[0, rows[c], hs[h]] = (r2[h][C:] + r3[h][:C]).astype(o_ref.dtype)
        state = [state[h] * jnp.exp(g_last[c, h]) + r3[h][C:] for h in heads]
    for h in heads:
        state_ref[h] = state[h]


def _gdn_scan(qkvn, col, row, nh, rev, nc=4):
    B, S, _ = qkvn.shape
    ts = nc * GDN_CHUNK
    nt = S // ts
    width = nh * LANES
    tmap = (lambda t: nt - 1 - t) if rev else (lambda t: t)
    col3 = col.reshape(B, S, LANES)
    return pl.pallas_call(
        functools.partial(_gdn_scan_kernel, rev=rev, nh=nh),
        grid=(B, nt),
        in_specs=[pl.BlockSpec((1, ts, width), lambda b, t: (b, tmap(t), 0)),
                  pl.BlockSpec((1, ts, width), lambda b, t: (b, tmap(t), 1)),
                  pl.BlockSpec((1, ts, width), lambda b, t: (b, tmap(t), 2)),
                  pl.BlockSpec((1, ts, LANES), lambda b, t: (b, tmap(t), 0)),
                  pl.BlockSpec((1, 4 * nh, ts), lambda b, t: (b, 0, tmap(t)))],
        out_specs=pl.BlockSpec((1, ts, width), lambda b, t: (b, tmap(t), 0)),
        out_shape=jax.ShapeDtypeStruct((B, S, width), BF16),
        scratch_shapes=[pltpu.VMEM((nh, GDN_DK, LANES), F32)],
        compiler_params=_params("parallel", "arbitrary"),
        name="gdn_scan_bwd" if rev else "gdn_scan_fwd",
    )(qkvn, qkvn, qkvn, col3, row)


MLP_CHUNK = 512


def _anchor(v):
    folded = jnp.sum(v.reshape(v.shape[0] // SUBLANES, SUBLANES, LANES), axis=0)
    bits = lax.shift_right_logical(pltpu.bitcast(folded, jnp.uint32), jnp.uint32(32))
    return pltpu.bitcast(bits, F32)


def _out_mlp_tail(y_ref, x_ref, wo_ref, g_ref, w1_ref, w2_ref, gf_ref, o_ref, x1_ref, acc_ref, final,
                  anchors=()):
    x1_ref[...] = x_ref[...] + jnp.dot(y_ref[...], wo_ref[...], preferred_element_type=F32)
    h = _rms(x1_ref[...], g_ref[...]).astype(BF16)
    starts = list(range(0, w1_ref.shape[1], MLP_CHUNK))
    for n, c in enumerate(starts):
        a = jnp.maximum(jnp.dot(h, w1_ref[:, c:c + MLP_CHUNK], preferred_element_type=F32), 0.0)
        part = jnp.dot((a * a).astype(BF16), w2_ref[c:c + MLP_CHUNK, :], preferred_element_type=F32)
        if c == 0:
            acc_ref[...] = part
        else:
            acc_ref[...] += part
        for k in range(n, len(anchors), len(starts)):
            acc_ref[0:SUBLANES, 0:LANES] += anchors[k]
    out = x1_ref[...] + acc_ref[...]
    if final:
        out = _rms(out, gf_ref[...])
    o_ref[...] = out


def _mixer_out_mlp(prologue, y_ref, tail_refs, final, lookahead):
    if not lookahead:
        prologue(y_ref.at[0])
        _out_mlp_tail(y_ref.at[0], *tail_refs, final)
        return
    i = pl.program_id(0)

    @pl.when(i == 0)
    def _():
        y_ref[...] = jnp.zeros_like(y_ref)

    for parity in range(2):
        @pl.when(i % 2 == parity)
        def _():
            pieces = prologue(y_ref.at[parity])
            _out_mlp_tail(y_ref.at[1 - parity], *tail_refs, final, anchors=[_anchor(v) for v in pieces])


def _out_mlp_call(kernel_fn, mixer_args, mixer_specs, mixer_scratch, x2, w_out, g, w1, w2,
                  g_final, tm, name, lookahead):
    T, D = x2.shape
    n = T // tm
    if lookahead:
        cur = lambda i: jnp.minimum(i, n - 1)
        prev = lambda i: jnp.maximum(i - 1, 0)
    else:
        cur = prev = lambda i: i
    once = lambda shape: pl.BlockSpec(shape, lambda i: (0, 0), pipeline_mode=pl.Buffered(1))
    return pl.pallas_call(
        kernel_fn,
        grid=(n + 1 if lookahead else n,),
        in_specs=mixer_specs(cur) + [pl.BlockSpec((tm, D), lambda i: (prev(i), 0)), once(w_out.shape),
                                     once((1, D)), once(w1.shape), once(w2.shape), once((1, D))],
        out_specs=pl.BlockSpec((tm, D), lambda i: (prev(i), 0)),
        out_shape=jax.ShapeDtypeStruct((T, D), F32),
        scratch_shapes=mixer_scratch + [pltpu.VMEM((2 if lookahead else 1, tm, w_out.shape[0]), BF16),
                                        pltpu.VMEM((tm, D), F32), pltpu.VMEM((tm, D), F32)],
        compiler_params=_params("arbitrary" if lookahead else "parallel"),
        name=name,
    )(*mixer_args, x2, w_out, g.reshape(1, D), w1, w2, g_final.reshape(1, D))


def _gdn_out_mlp_kernel(of_ref, ob_ref, z_ref, nw_ref, x_ref, wo_ref, g_ref, w1_ref, w2_ref, gf_ref,
                        o_ref, y_ref, x1_ref, acc_ref, *, nh, final):
    def prologue(y):
        pieces = []
        for h in range(nh):
            hs = slice(h * LANES, (h + 1) * LANES)
            o = of_ref[:, hs].astype(F32) + ob_ref[:, hs].astype(F32)
            o = o * lax.rsqrt(jnp.mean(o * o, axis=-1, keepdims=True) + RMS_EPS)
            z = z_ref[:, hs].astype(F32)
            pieces.append(o * nw_ref[...] * (z * _sigmoid(z)))
            y[:, hs] = pieces[-1].astype(BF16)
        return pieces
    _mixer_out_mlp(prologue, y_ref, (x_ref, wo_ref, g_ref, w1_ref, w2_ref, gf_ref, o_ref, x1_ref, acc_ref),
                   final, lookahead=True)


def _gdn_out_mlp(o_f, o_b, z, x2, norm_w, w_out, nh, g, w1, w2, g_final, final, tm=512):
    W = o_f.shape[1]
    specs = lambda cur: [pl.BlockSpec((tm, W), lambda i: (cur(i), 0))] * 3 + \
                        [pl.BlockSpec((1, LANES), lambda i: (0, 0))]
    return _out_mlp_call(
        functools.partial(_gdn_out_mlp_kernel, nh=nh, final=final),
        (o_f, o_b, z, norm_w.reshape(1, LANES)), specs,
        [], x2, w_out, g, w1, w2, g_final, tm, "gdn_out_mlp", lookahead=True)


def _dswa_kernel(q_ref, k_ref, kp_ref, kn_ref, v_ref, vp_ref, vn_ref, bias_ref,
                 o_ref, lse_ref, *, sub_len):
    t = pl.program_id(2)
    rb, lt = q_ref.shape[1], q_ref.shape[2]
    hb = DSWA_HALF
    first_head = lax.broadcasted_iota(jnp.int32, (1, LANES), 1) < DSWA_HEAD_DIM
    n_pairs = q_ref.shape[3] // LANES
    blocks = range(lt // hb)
    units = [(rc, qb, pr) for rc in range(rb) for qb in blocks for pr in range(n_pairs)]
    rows = [slice(qb * hb, (qb + 1) * hb) for qb in blocks]
    win = [slice(qb * hb, (qb + 3) * hb) for qb in blocks]
    ls = [slice(pr * LANES, (pr + 1) * LANES) for pr in range(n_pairs)]
    kext = [jnp.concatenate([kp_ref[0, rc], k_ref[0, rc], kn_ref[0, rc]], axis=0) for rc in range(rb)]
    vext = [jnp.concatenate([vp_ref[0, rc], v_ref[0, rc], vn_ref[0, rc]], axis=0) for rc in range(rb)]
    kpos = lambda qb: t * lt + (qb - 1) * hb + lax.broadcasted_iota(jnp.int32, (1, 3 * hb), 1)
    qs = {(rc, qb, pr): jnp.concatenate([jnp.where(first_head, q_ref[0, rc, rows[qb], ls[pr]], 0.0),
                                         jnp.where(first_head, 0.0, q_ref[0, rc, rows[qb], ls[pr]])], axis=0)
          for rc, qb, pr in units}
    s = {(rc, qb, pr): _dot_nt(qs[rc, qb, pr], kext[rc][win[qb], ls[pr]]) + bias_ref[pr] for rc, qb, pr in units}
    for rc, qb, pr in units:
        if qb == blocks[0]:
            s[rc, qb, pr] = jnp.where(kpos(qb) >= 0, s[rc, qb, pr], NEG_INF)
        if qb == blocks[-1]:
            s[rc, qb, pr] = jnp.where(kpos(qb) < sub_len, s[rc, qb, pr], NEG_INF)
    m = {u: jnp.max(s[u], axis=-1, keepdims=True) for u in units}
    p = {u: jnp.exp2(s[u] - m[u]) for u in units}
    l = {u: jnp.sum(p[u], axis=-1, keepdims=True) for u in units}
    pv = {(rc, qb, pr): _dot(p[rc, qb, pr], vext[rc][win[qb], ls[pr]]) * (1.0 / l[rc, qb, pr])
          for rc, qb, pr in units}
    lse = {u: (m[u] + jnp.log2(l[u])) * math.log(2.0) for u in units}
    for rc, qb, pr in units:
        o_ref[0, rc, rows[qb], ls[pr]] = jnp.where(first_head, pv[rc, qb, pr][:hb], pv[rc, qb, pr][hb:])
        lse_ref[0, rc, rows[qb], ls[pr]] = jnp.where(first_head, lse[rc, qb, pr][:hb], lse[rc, qb, pr][hb:])


def _rel_bucket(rel):
    nb = REL_BUCKETS // 2
    max_exact = nb // 2
    ret = jnp.where(rel > 0, nb, 0)
    n = jnp.abs(rel)
    nf = jnp.maximum(n, 1).astype(F32)
    large = max_exact + (jnp.log(nf * (1.0 / max_exact)) / math.log(REL_MAX_DIST / max_exact)
                         * (nb - max_exact)).astype(jnp.int32)
    large = jnp.minimum(large, nb - 1)
    return ret + jnp.where(n < max_exact, n, large)


def _band_bias_kernel(tab_ref, o_ref, *, dils, hg):
    hb = DSWA_HALF
    off = (lax.broadcasted_iota(jnp.int32, (hb, 3 * hb), 1) - hb
           - lax.broadcasted_iota(jnp.int32, (hb, 3 * hb), 0))
    inband = jnp.abs(off) <= hb
    for g, d in enumerate(dils):
        bucket = _rel_bucket(off * d)
        for hh in range(hg):
            acc = jnp.zeros((hb, 3 * hb), F32)
            for b in range(REL_BUCKETS):
                acc = jnp.where(bucket == b, tab_ref[b, g * hg + hh], acc)
            o_ref[g, hh // 2, (hh % 2) * hb:(hh % 2 + 1) * hb, :] = jnp.where(inband, acc * LOG2E, NEG_INF)


def _band_bias(rel_table, dils):
    hb = DSWA_HALF
    hg = rel_table.shape[1] // len(dils)
    return pl.pallas_call(
        functools.partial(_band_bias_kernel, dils=dils, hg=hg),
        in_specs=[pl.BlockSpec(memory_space=pltpu.SMEM)],
        out_shape=jax.ShapeDtypeStruct((len(dils), hg // 2, 2 * hb, 3 * hb), F32),
        name="band_bias",
    )(rel_table)


def _dswa_group(qkv, bias, lt=512):
    B, dil, L, W3 = qkv.shape
    hb = DSWA_HALF
    gw = W3 // 3
    rb = max(1, lt // L)
    lt = min(lt, L)
    r = lt // hb
    main = lambda which: pl.BlockSpec((1, rb, lt, gw), lambda b, d, t: (b, d, t, which))
    prev = lambda which: pl.BlockSpec(
        (1, rb, hb, gw), lambda b, d, t: (b, d, jnp.maximum(t * r - 1, 0), which))
    nxt = lambda which: pl.BlockSpec(
        (1, rb, hb, gw), lambda b, d, t: (b, d, jnp.minimum((t + 1) * r, L // hb - 1), which))
    out = pl.BlockSpec((1, rb, lt, gw), lambda b, d, t: (b, d, t, 0))
    return pl.pallas_call(
        functools.partial(_dswa_kernel, sub_len=L),
        grid=(B, dil // rb, L // lt),
        in_specs=[main(0), main(1), prev(1), nxt(1), main(2), prev(2), nxt(2),
                  pl.BlockSpec(bias.shape, lambda b, d, t: (0, 0, 0))],
        out_specs=[out, out],
        out_shape=[jax.ShapeDtypeStruct((B, dil, L, gw), F32)] * 2,
        compiler_params=_params("parallel", "parallel", "parallel"),
        name=f"dswa_d{dil}",
    )(qkv, qkv, qkv, qkv, qkv, qkv, qkv, bias)


def _dswa_out_mlp_kernel(*refs, dils, final):
    ng = len(dils)
    o_refs, l_refs = refs[:ng], refs[ng:2 * ng]
    x_ref, wo_ref, g_ref, w1_ref, w2_ref, gf_ref, out_ref, o_s, l_s, y_s, x1_ref, acc_ref = refs[2 * ng:]
    tm = x_ref.shape[0]
    tiles = o_refs[0].shape[3] // LANES
    def prologue(y):
        for g, d in enumerate(dils):
            for r in range(d):
                for j in range(tiles):
                    ls = slice(j * LANES, (j + 1) * LANES)
                    o_s[g * tiles + j, pl.ds(r, tm // d, stride=d), :] = o_refs[g][0, r, :, ls]
                    l_s[g * tiles + j, pl.ds(r, tm // d, stride=d), :] = l_refs[g][0, r, :, ls]
        pieces = []
        for j in range(tiles):
            lses = [l_s[g * tiles + j] for g in range(ng)]
            m = functools.reduce(jnp.maximum, lses)
            es = [jnp.exp(l - m) for l in lses]
            inv = 1.0 / functools.reduce(lambda a, b: a + b, es)
            for g in range(ng):
                c = g * tiles + j
                pieces.append(o_s[c] * (es[g] * inv))
                y[:, c * LANES:(c + 1) * LANES] = pieces[-1].astype(BF16)
        return pieces
    _mixer_out_mlp(prologue, y_s, (x_ref, wo_ref, g_ref, w1_ref, w2_ref, gf_ref, out_ref, x1_ref, acc_ref),
                   final, lookahead=False)


def _dswa_out_mlp(os_, lses, x2, w_out, S, g, w1, w2, g_final, final, tm=512):
    dils = tuple(o.shape[1] for o in os_)
    gw = os_[0].shape[3]
    nt = S // tm
    specs = lambda cur: [pl.BlockSpec((1, d, tm // d, gw), lambda i: (cur(i) // nt, 0, cur(i) % nt, 0))
                         for d in dils] * 2
    staging = pltpu.VMEM((len(dils) * gw // LANES, tm, LANES), F32)
    return _out_mlp_call(
        functools.partial(_dswa_out_mlp_kernel, dils=dils, final=final),
        (*os_, *lses), specs, [staging, staging],
        x2, w_out, g, w1, w2, g_final, tm, "dswa_out_mlp", lookahead=False)


def _gdn_layer(x2, g, w_in, conv_w, a_log, dt_bias, norm_w, w_out, B, S, mlp):
    nh = a_log.shape[1]
    z_w = nh * LANES
    qkvn, z, col, row = _gdn_proj(x2, g, w_in, conv_w, a_log, dt_bias, B, S)
    qkvn = qkvn.reshape(B, S, 3 * z_w)
    o_f = _gdn_scan(qkvn, col, row, nh, rev=False)
    o_b = _gdn_scan(qkvn, col, row, nh, rev=True)
    return _gdn_out_mlp(o_f.reshape(B * S, z_w), o_b.reshape(B * S, z_w), z, x2, norm_w,
                        w_out.astype(BF16), nh, *mlp)


def _dswa_proj_kernel(x_ref, g_ref, w_ref, *refs, dils, chunk):
    o_refs, acc_ref = refs[:-1], refs[-1]
    tm = x_ref.shape[0]
    n = w_ref.shape[1]
    ng = len(dils)
    gw = n // (3 * ng)
    tiles = gw // LANES
    h = _rms(x_ref[...], g_ref[...]).astype(BF16)
    def project(which):
        res = jnp.dot(h, w_ref[:, which * chunk:(which + 1) * chunk], preferred_element_type=F32)
        if which == 0:
            res = res * (DSWA_HEAD_DIM ** -0.5 * LOG2E)
        for j in range(chunk // LANES):
            acc_ref[which * ng * tiles + j] = res[:, j * LANES:(j + 1) * LANES]

    def regroup(which):
        for g, d in enumerate(dils):
            for r in range(d):
                for j in range(tiles):
                    dst = slice((which * tiles + j) * LANES, (which * tiles + j + 1) * LANES)
                    o_refs[g][0, r, :, dst] = \
                        acc_ref[(which * ng + g) * tiles + j, pl.ds(r, tm // d, stride=d), :].astype(BF16)

    project(0)
    for which in range(1, 3):
        project(which)
        regroup(which - 1)
    regroup(2)


def _dswa_proj(x2, g, w_in, dils, B, S, tm=512):
    T, D = x2.shape
    n = w_in.shape[1]
    gw3 = n // len(dils)
    nt = S // tm
    return pl.pallas_call(
        functools.partial(_dswa_proj_kernel, dils=dils, chunk=gw3),
        grid=(T // tm,),
        in_specs=[pl.BlockSpec((tm, D), lambda i: (i, 0)),
                  pl.BlockSpec((1, D), lambda i: (0, 0)),
                  pl.BlockSpec((D, n), lambda i: (0, 0))],
        out_specs=[pl.BlockSpec((1, d, tm // d, gw3), lambda i: (i // nt, 0, i % nt, 0)) for d in dils],
        out_shape=[jax.ShapeDtypeStruct((B, d, S // d, gw3), BF16) for d in dils],
        scratch_shapes=[pltpu.VMEM((n // LANES, tm, LANES), F32)],
        compiler_params=_params("parallel"),
        name="dswa_proj",
    )(x2, g.reshape(1, D), w_in)


def _dswa_layer(x2, g, w_in, w_out, bias, B, S, mlp):
    dils = tuple(d for _, d in DSWA_CONFIGS)
    qkvs = _dswa_proj(x2, g, w_in.astype(BF16), dils, B, S)
    os_, lses = [], []
    for gi in range(len(dils)):
        o, lse = _dswa_group(qkvs[gi], bias[gi])
        os_.append(o)
        lses.append(lse)
    return _dswa_out_mlp(os_, lses, x2, w_out.astype(BF16), S, *mlp)


def kernel(x, norm_mix, norm_mlp, norm_final, rel_bias, gdn_w_in, gdn_conv_w, gdn_a_log,
           gdn_dt_bias, gdn_norm_w, gdn_w_out, dswa_w_in, dswa_w_out, mlp_w1, mlp_w2):
    B, S, D = x.shape
    depth = norm_mix.shape[0]
    x2 = x.reshape(B * S, D)
    assert all(window // (2 * dil) == DSWA_HALF for window, dil in DSWA_CONFIGS)
    bias = _band_bias(rel_bias, tuple(d for _, d in DSWA_CONFIGS))
    for i in range(depth):
        j = i // 2
        mlp = (norm_mlp[i], mlp_w1[i].astype(BF16), mlp_w2[i].astype(BF16), norm_final, i == depth - 1)
        if i % 2 == 0:
            x2 = _gdn_layer(x2, norm_mix[i], gdn_w_in[j], gdn_conv_w[j], gdn_a_log[j],
                            gdn_dt_bias[j], gdn_norm_w[j], gdn_w_out[j], B, S, mlp)
        else:
            x2 = _dswa_layer(x2, norm_mix[i], dswa_w_in[j], dswa_w_out[j], bias, B, S, mlp)
    return x2.reshape(B, S, D)
```

```python
import functools
import math

import jax
import jax.numpy as jnp
from jax import lax
from jax.experimental import pallas as pl
from jax.experimental.pallas import tpu as pltpu

F32 = jnp.float32
BF16 = jnp.bfloat16

RMS_EPS = 1e-6
L2_EPS = 1e-6
NEG_INF = -1e30
LOG2E = math.log2(math.e)

LANES = 128
SUBLANES = 8
VMEM_LIMIT = 56 * 1024 * 1024

GDN_DK = 128
GDN_CONV = 5
GDN_CHUNK = LANES
DSWA_HEAD_DIM = 64
DSWA_CONFIGS = ((128, 1), (512, 4), (2048, 16))
DSWA_HALF = 64
REL_BUCKETS = 32
REL_MAX_DIST = 1024


def _params(*sem):
    return pltpu.CompilerParams(dimension_semantics=sem, vmem_limit_bytes=VMEM_LIMIT)


def _dot(a, b):
    return jnp.dot(a.astype(BF16), b.astype(BF16), preferred_element_type=F32)


def _dot_nt(a, b):
    return lax.dot_general(a.astype(BF16), b.astype(BF16), (((1,), (1,)), ((), ())),
                           preferred_element_type=F32)


def _split3(x):
    hi = x.astype(BF16)
    r = x - hi.astype(F32)
    mid = r.astype(BF16)
    return hi, mid, (r - mid.astype(F32)).astype(BF16)


def _rms(x, g):
    return x * lax.rsqrt(jnp.mean(x * x, axis=-1, keepdims=True) + RMS_EPS) * g


def _sigmoid(x):
    return 1.0 / (1.0 + jnp.exp(-x))


def _softplus(x):
    return jnp.maximum(x, 0.0) + jnp.log1p(jnp.exp(-jnp.abs(x)))


CONV_HALO = 16
CONV_BLOCK = 64


def _gdn_proj_kernel(x_ref, xp_ref, xn_ref, g_ref, wqkv_ref, wz_ref, w_ref, wt_ref, cw_ref,
                     prow_ref, pcol_ref, qkv_ref, z_ref, col_ref, row_ref, *, nh, nt, chunk):
    C = GDN_CHUNK
    tm = x_ref.shape[0]
    t = pl.program_id(0) % nt
    pad = GDN_CONV // 2
    g = g_ref[...]
    h = _rms(x_ref[...], g).astype(BF16)
    h_halo = jnp.concatenate([_rms(jnp.where(t > 0, xp_ref[...], 0.0), g).astype(BF16),
                              _rms(jnp.where(t < nt - 1, xn_ref[...], 0.0), g).astype(BF16)], axis=0)
    CB = CONV_BLOCK
    taps = [i for i in range(GDN_CONV) if i != pad]
    sr = lax.broadcasted_iota(jnp.int32, (len(taps) * CB, CB + 2 * CONV_HALO), 0)
    sc = lax.broadcasted_iota(jnp.int32, (len(taps) * CB, CB + 2 * CONV_HALO), 1)
    src = (sr % CB) + (CONV_HALO - pad)
    for k, i in enumerate(taps):
        src = src + jnp.where(sr // CB == k, i, 0)
    shift_mat = (sc == src).astype(BF16)

    def project(c0):
        w = wqkv_ref[:, c0:c0 + chunk]
        main = jnp.dot(h, w, preferred_element_type=F32)
        halo = jnp.dot(h_halo, w, preferred_element_type=F32)
        ext = jnp.concatenate([halo[:CONV_HALO], main, halo[CONV_HALO:]], axis=0).astype(BF16)
        return main, ext

    starts = list(range(0, wqkv_ref.shape[1], chunk))
    nxt = project(starts[0])
    for n, c0 in enumerate(starts):
        cs = slice(c0, c0 + chunk)
        main, ext = nxt
        if n + 1 < len(starts):
            nxt = project(starts[n + 1])
        for b in range(tm // CB):
            rows = slice(b * CB, (b + 1) * CB)
            shifted = jnp.dot(shift_mat, ext[b * CB:(b + 1) * CB + 2 * CONV_HALO],
                              preferred_element_type=F32)
            acc = cw_ref[pad:pad + 1, cs] * main[rows]
            for k, i in enumerate(taps):
                acc = acc + cw_ref[i:i + 1, cs] * shifted[k * CB:(k + 1) * CB]
            y = acc * _sigmoid(acc)
            for j in range(chunk // LANES):
                yj = y[:, j * LANES:(j + 1) * LANES]
                head = c0 // LANES + j
                if head < 2 * nh:
                    inv = lax.rsqrt(jnp.sum(yj * yj, axis=-1, keepdims=True) + L2_EPS)
                    yj = yj * (inv * (GDN_DK ** -0.5) if head < nh else inv)
                qkv_ref[rows, c0 + j * LANES:c0 + (j + 1) * LANES] = yj.astype(BF16)
    for c0 in range(0, wz_ref.shape[1], chunk):
        cs = slice(c0, c0 + chunk)
        z_ref[:, cs] = jnp.dot(h, wz_ref[:, cs], preferred_element_type=F32).astype(BF16)
    a_c = jnp.dot(h, w_ref[...], preferred_element_type=F32)
    a_r = lax.dot_general(wt_ref[...], h, (((1,), (1,)), ((), ())),
                          preferred_element_type=F32)
    ri = lax.broadcasted_iota(jnp.int32, (C, C), 0)
    ci = lax.broadcasted_iota(jnp.int32, (C, C), 1)
    lower = (ri >= ci).astype(BF16)
    upper = (ri <= ci).astype(BF16)
    g_c = -jnp.exp(prow_ref[0:1, :]) * _softplus(a_c + prow_ref[1:2, :])
    g_r = -jnp.exp(pcol_ref[:, 0:1]) * _softplus(a_r + pcol_ref[:, 1:2])
    beta_c = _sigmoid(a_c)
    beta_r = _sigmoid(a_r)
    for c in range(tm // C):
        sl = slice(c * C, (c + 1) * C)
        gc = g_c[sl, :]
        pre = sum(jnp.dot(lower, part, preferred_element_type=F32) for part in _split3(gc))
        cum = jnp.where(ci < nh, pre, pre[C - 1:C, :] - pre + gc)
        col_ref[sl, :] = jnp.where(ci < 2 * nh, cum, beta_c[sl, :])
        gr = g_r[:, sl]
        pre_r = sum(jnp.dot(part, upper, preferred_element_type=F32) for part in _split3(gr))
        cum_r = jnp.where(ri < nh, pre_r, pre_r[:, C - 1:C] - pre_r + gr)
        full = jnp.where(ri < 2 * nh, cum_r, beta_r[:, sl])
        row_ref[0, :, sl] = full[:4 * nh, :]


def _gdn_proj(x2, g, w_in, conv_w, a_log, dt_bias, B, S, tm=1024, chunk=256):
    T, D = x2.shape
    nh = a_log.shape[1]
    qkv_w = 3 * nh * LANES
    z_w = nh * LANES
    w_pad = jnp.zeros((D, LANES), F32).at[:, :4 * nh].set(w_in[:, qkv_w + z_w:])
    alog = jnp.zeros((LANES,), F32).at[:2 * nh].set(a_log.reshape(-1))
    dtb = jnp.zeros((LANES,), F32).at[:2 * nh].set(dt_bias.reshape(-1))
    prow = jnp.zeros((SUBLANES, LANES), F32).at[0].set(alog).at[1].set(dtb)
    pcol = jnp.zeros((LANES, LANES), F32).at[:, 0].set(alog).at[:, 1].set(dtb)
    nt = S // tm
    r = tm // CONV_HALO
    const = lambda shape: pl.BlockSpec(shape, lambda i: (0, 0))
    return pl.pallas_call(
        functools.partial(_gdn_proj_kernel, nh=nh, nt=nt, chunk=chunk),
        grid=(T // tm,),
        in_specs=[pl.BlockSpec((tm, D), lambda i: (i, 0)),
                  pl.BlockSpec((CONV_HALO, D), lambda i: (jnp.maximum(i * r - 1, 0), 0)),
                  pl.BlockSpec((CONV_HALO, D), lambda i: (jnp.minimum((i + 1) * r, T // CONV_HALO - 1), 0)),
                  const((1, D)), const((D, qkv_w)), const((D, z_w)), const((D, LANES)), const((LANES, D)),
                  const((GDN_CONV, qkv_w)), const((SUBLANES, LANES)), const((LANES, LANES))],
        out_specs=[pl.BlockSpec((tm, qkv_w), lambda i: (i, 0)),
                   pl.BlockSpec((tm, z_w), lambda i: (i, 0)),
                   pl.BlockSpec((tm, LANES), lambda i: (i, 0)),
                   pl.BlockSpec((1, 4 * nh, tm), lambda i: (i // nt, 0, i % nt))],
        out_shape=[jax.ShapeDtypeStruct((T, qkv_w), BF16),
                   jax.ShapeDtypeStruct((T, z_w), BF16),
                   jax.ShapeDtypeStruct((T, LANES), F32),
                   jax.ShapeDtypeStruct((B, 4 * nh, S), F32)],
        compiler_params=_params("parallel"),
        name="gdn_proj",
    )(x2, x2, x2, g.reshape(1, D), w_in[:, :qkv_w].astype(BF16), w_in[:, qkv_w:qkv_w + z_w].astype(BF16),
      w_pad.astype(BF16), w_pad.T.astype(BF16), conv_w.reshape(GDN_CONV, qkv_w), prow, pcol)


def _gdn_scan_kernel(q_ref, k_ref, v_ref, col_ref, row_ref, o_ref, state_ref, *, rev, nh):
    C = GDN_CHUNK
    t = pl.program_id(1)

    @pl.when(t == 0)
    def _():
        state_ref[...] = jnp.zeros_like(state_ref)

    ri = lax.broadcasted_iota(jnp.int32, (C, C), 0)
    ci = lax.broadcasted_iota(jnp.int32, (C, C), 1)
    incl = (ri <= ci) if rev else (ri >= ci)
    strict = (ri < ci) if rev else (ri > ci)
    eye = (ri == ci).astype(F32)
    levels = [((ri >> (lg + 1)) == (ci >> (lg + 1))) & ((ri >> lg) != (ci >> lg))
              for lg in range(int(math.log2(C)))]
    last = 0 if rev else C - 1
    nc = q_ref.shape[1] // C
    chunks = list(reversed(range(nc))) if rev else list(range(nc))
    heads = range(nh)
    hs = [slice(h * LANES, (h + 1) * LANES) for h in heads]
    gi = [(nh if rev else 0) + h for h in heads]
    units = [(c, h) for c in chunks for h in heads]
    rows = {c: slice(c * C, (c + 1) * C) for c in chunks}
    col = {c: col_ref[0, rows[c], :] for c in chunks}
    row = {c: row_ref[0, :, rows[c]] for c in chunks}
    kf = {u: k_ref[0, rows[u[0]], hs[u[1]]] for u in units}
    k = {u: kf[u].astype(F32) for u in units}
    g_col = {(c, h): col[c][:, gi[h]:gi[h] + 1] for c, h in units}
    b_col = {(c, h): col[c][:, 2 * nh + gi[h]:2 * nh + gi[h] + 1] for c, h in units}
    g_last = {u: g_col[u][last:last + 1, :] for u in units}
    decay = {(c, h): jnp.where(incl, jnp.exp(jnp.where(incl, g_col[c, h] - row[c][gi[h]:gi[h] + 1, :], 0.0)), 0.0)
             for c, h in units}
    kb = {u: k[u] * b_col[u] for u in units}
    res = {(c, h): _dot_nt(jnp.concatenate([kb[c, h].astype(BF16), q_ref[0, rows[c], hs[h]]], axis=0), kf[c, h])
           for c, h in units}
    a = {u: jnp.where(strict, res[u][:C] * decay[u], 0.0) for u in units}
    intra = {u: (res[u][C:] * decay[u]).astype(BF16) for u in units}
    tinv = {u: eye - jnp.where(levels[0], a[u], 0.0) for u in units}
    for lg, m in enumerate(levels[1:], start=1):
        size = 1 << lg
        moving = [r0 for r0 in range(0, C, size) if ((r0 >> lg) & 1) == (0 if rev else 1)]
        tb = {u: tinv[u].astype(BF16) for u in units}
        if size < SUBLANES:
            pm = {u: _dot(tb[u], jnp.where(m, a[u], 0.0)) for u in units}
            tinv = {u: tinv[u] - _dot(pm[u], tb[u]) for u in units}
            continue
        lhs = {u: jnp.concatenate([tinv[u][r0:r0 + size] for r0 in moving], axis=0) for u in units}
        pm = {u: _dot(lhs[u], jnp.where(m, a[u], 0.0)) for u in units}
        new = {u: lhs[u] - _dot(pm[u], tb[u]) for u in units}
        tinv = {u: jnp.concatenate(
            [new[u][moving.index(r0) * size:(moving.index(r0) + 1) * size] if r0 in moving
             else tinv[u][r0:r0 + size] for r0 in range(0, C, size)], axis=0) for u in units}
    e_col = {u: jnp.exp(g_col[u]) for u in units}
    sol = {(c, h): _dot(tinv[c, h], jnp.concatenate(
        [v_ref[0, rows[c], hs[h]].astype(F32) * b_col[c, h], kb[c, h] * e_col[c, h]], axis=1)) for c, h in units}
    pre = {u: _dot(jnp.concatenate([intra[u], (k[u] * jnp.exp(g_last[u] - g_col[u])).T.astype(BF16)], axis=0),
                   sol[u]) for u in units}
    lhs = {(c, h): jnp.concatenate(
        [pre[c, h][C:, LANES:].astype(BF16),
         (q_ref[0, rows[c], hs[h]].astype(F32) * e_col[c, h] - pre[c, h][:C, LANES:]).astype(BF16)], axis=0)
        for c, h in units}
    state = [state_ref[h] for h in heads]
    for c in chunks:
        r = [_dot(lhs[c, h], state[h]) for h in heads]
        for h in heads:
            o_ref[0, rows[c], hs[h]] = (r[h][C:] + pre[c, h][:C, :LANES]).astype(o_ref.dtype)
        state = [state[h] * jnp.exp(g_last[c, h]) - r[h][:C] + pre[c, h][C:, :LANES] for h in heads]
    for h in heads:
        state_ref[h] = state[h]


def _gdn_scan(qkvn, col, row, nh, rev, nc=4):
    B, S, _ = qkvn.shape
    ts = nc * GDN_CHUNK
    nt = S // ts
    width = nh * LANES
    tmap = (lambda t: nt - 1 - t) if rev else (lambda t: t)
    col3 = col.reshape(B, S, LANES)
    return pl.pallas_call(
        functools.partial(_gdn_scan_kernel, rev=rev, nh=nh),
        grid=(B, nt),
        in_specs=[pl.BlockSpec((1, ts, width), lambda b, t: (b, tmap(t), 0)),
                  pl.BlockSpec((1, ts, width), lambda b, t: (b, tmap(t), 1)),
                  pl.BlockSpec((1, ts, width), lambda b, t: (b, tmap(t), 2)),
                  pl.BlockSpec((1, ts, LANES), lambda b, t: (b, tmap(t), 0)),
                  pl.BlockSpec((1, 4 * nh, ts), lambda b, t: (b, 0, tmap(t)))],
        out_specs=pl.BlockSpec((1, ts, width), lambda b, t: (b, tmap(t), 0)),
        out_shape=jax.ShapeDtypeStruct((B, S, width), BF16),
        scratch_shapes=[pltpu.VMEM((nh, GDN_DK, LANES), F32)],
        compiler_params=_params("parallel", "arbitrary"),
        name="gdn_scan_bwd" if rev else "gdn_scan_fwd",
    )(qkvn, qkvn, qkvn, col3, row)


MLP_CHUNK = 512


def _anchor(v):
    folded = jnp.sum(v.reshape(v.shape[0] // SUBLANES, SUBLANES, LANES), axis=0)
    bits = lax.shift_right_logical(pltpu.bitcast(folded, jnp.uint32), jnp.uint32(32))
    return pltpu.bitcast(bits, F32)


def _out_mlp_tail(y_ref, x_ref, wo_ref, g_ref, w1_ref, w2_ref, gf_ref, o_ref, x1_ref, acc_ref, final,
                  anchors=()):
    x1_ref[...] = x_ref[...] + jnp.dot(y_ref[...], wo_ref[...], preferred_element_type=F32)
    h = _rms(x1_ref[...], g_ref[...]).astype(BF16)
    starts = list(range(0, w1_ref.shape[1], MLP_CHUNK))
    for n, c in enumerate(starts):
        a = jnp.maximum(jnp.dot(h, w1_ref[:, c:c + MLP_CHUNK], preferred_element_type=F32), 0.0)
        part = jnp.dot((a * a).astype(BF16), w2_ref[c:c + MLP_CHUNK, :], preferred_element_type=F32)
        if c == 0:
            acc_ref[...] = part
        else:
            acc_ref[...] += part
        for k in range(n, len(anchors), len(starts)):
            acc_ref[0:SUBLANES, 0:LANES] += anchors[k]
    out = x1_ref[...] + acc_ref[...]
    if final:
        out = _rms(out, gf_ref[...])
    o_ref[...] = out


def _mixer_out_mlp(prologue, y_ref, tail_refs, final, lookahead):
    if not lookahead:
        prologue(y_ref.at[0])
        _out_mlp_tail(y_ref.at[0], *tail_refs, final)
        return
    i = pl.program_id(0)

    @pl.when(i == 0)
    def _():
        y_ref[...] = jnp.zeros_like(y_ref)

    for parity in range(2):
        @pl.when(i % 2 == parity)
        def _():
            pieces = prologue(y_ref.at[parity])
            _out_mlp_tail(y_ref.at[1 - parity], *tail_refs, final, anchors=[_anchor(v) for v in pieces])


def _out_mlp_call(kernel_fn, mixer_args, mixer_specs, mixer_scratch, x2, w_out, g, w1, w2,
                  g_final, tm, name, lookahead):
    T, D = x2.shape
    n = T // tm
    if lookahead:
        cur = lambda i: jnp.minimum(i, n - 1)
        prev = lambda i: jnp.maximum(i - 1, 0)
    else:
        cur = prev = lambda i: i
    once = lambda shape: pl.BlockSpec(shape, lambda i: (0, 0), pipeline_mode=pl.Buffered(1))
    return pl.pallas_call(
        kernel_fn,
        grid=(n + 1 if lookahead else n,),
        in_specs=mixer_specs(cur) + [pl.BlockSpec((tm, D), lambda i: (prev(i), 0)), once(w_out.shape),
                                     once((1, D)), once(w1.shape), once(w2.shape), once((1, D))],
        out_specs=pl.BlockSpec((tm, D), lambda i: (prev(i), 0)),
        out_shape=jax.ShapeDtypeStruct((T, D), F32),
        scratch_shapes=mixer_scratch + [pltpu.VMEM((2 if lookahead else 1, tm, w_out.shape[0]), BF16),
                                        pltpu.VMEM((tm, D), F32), pltpu.VMEM((tm, D), F32)],
        compiler_params=_params("arbitrary" if lookahead else "parallel"),
        name=name,
    )(*mixer_args, x2, w_out, g.reshape(1, D), w1, w2, g_final.reshape(1, D))


def _gdn_out_mlp_kernel(of_ref, ob_ref, z_ref, nw_ref, x_ref, wo_ref, g_ref, w1_ref, w2_ref, gf_ref,
                        o_ref, y_ref, x1_ref, acc_ref, *, nh, final):
    def prologue(y):
        pieces = []
        for h in range(nh):
            hs = slice(h * LANES, (h + 1) * LANES)
            o = of_ref[:, hs].astype(F32) + ob_ref[:, hs].astype(F32)
            o = o * lax.rsqrt(jnp.mean(o * o, axis=-1, keepdims=True) + RMS_EPS)
            z = z_ref[:, hs].astype(F32)
            pieces.append(o * nw_ref[...] * (z * _sigmoid(z)))
            y[:, hs] = pieces[-1].astype(BF16)
        return pieces
    _mixer_out_mlp(prologue, y_ref, (x_ref, wo_ref, g_ref, w1_ref, w2_ref, gf_ref, o_ref, x1_ref, acc_ref),
                   final, lookahead=True)


def _gdn_out_mlp(o_f, o_b, z, x2, norm_w, w_out, nh, g, w1, w2, g_final, final, tm=512):
    W = o_f.shape[1]
    specs = lambda cur: [pl.BlockSpec((tm, W), lambda i: (cur(i), 0))] * 3 + \
                        [pl.BlockSpec((1, LANES), lambda i: (0, 0))]
    return _out_mlp_call(
        functools.partial(_gdn_out_mlp_kernel, nh=nh, final=final),
        (o_f, o_b, z, norm_w.reshape(1, LANES)), specs,
        [], x2, w_out, g, w1, w2, g_final, tm, "gdn_out_mlp", lookahead=True)


def _dswa_kernel(q_ref, k_ref, kp_ref, kn_ref, v_ref, vp_ref, vn_ref, bias_ref,
                 o_ref, lse_ref, *, sub_len):
    t = pl.program_id(2)
    rb, lt = q_ref.shape[1], q_ref.shape[2]
    hb = DSWA_HALF
    first_head = lax.broadcasted_iota(jnp.int32, (1, LANES), 1) < DSWA_HEAD_DIM
    n_pairs = q_ref.shape[3] // LANES
    blocks = range(lt // hb)
    units = [(rc, qb, pr) for rc in range(rb) for qb in blocks for pr in range(n_pairs)]
    rows = [slice(qb * hb, (qb + 1) * hb) for qb in blocks]
    win = [slice(qb * hb, (qb + 3) * hb) for qb in blocks]
    ls = [slice(pr * LANES, (pr + 1) * LANES) for pr in range(n_pairs)]
    kext = [jnp.concatenate([kp_ref[0, rc], k_ref[0, rc], kn_ref[0, rc]], axis=0) for rc in range(rb)]
    vext = [jnp.concatenate([vp_ref[0, rc], v_ref[0, rc], vn_ref[0, rc]], axis=0) for rc in range(rb)]
    kpos = lambda qb: t * lt + (qb - 1) * hb + lax.broadcasted_iota(jnp.int32, (1, 3 * hb), 1)
    qs = {(rc, qb, pr): jnp.concatenate([jnp.where(first_head, q_ref[0, rc, rows[qb], ls[pr]], 0.0),
                                         jnp.where(first_head, 0.0, q_ref[0, rc, rows[qb], ls[pr]])], axis=0)
          for rc, qb, pr in units}
    s = {(rc, qb, pr): _dot_nt(qs[rc, qb, pr], kext[rc][win[qb], ls[pr]]) + bias_ref[pr] for rc, qb, pr in units}
    for rc, qb, pr in units:
        if qb == blocks[0]:
            s[rc, qb, pr] = jnp.where(kpos(qb) >= 0, s[rc, qb, pr], NEG_INF)
        if qb == blocks[-1]:
            s[rc, qb, pr] = jnp.where(kpos(qb) < sub_len, s[rc, qb, pr], NEG_INF)
    m = {u: jnp.max(s[u], axis=-1, keepdims=True) for u in units}
    p = {u: jnp.exp2(s[u] - m[u]) for u in units}
    l = {u: jnp.sum(p[u], axis=-1, keepdims=True) for u in units}
    pv = {(rc, qb, pr): _dot(p[rc, qb, pr], vext[rc][win[qb], ls[pr]]) * (1.0 / l[rc, qb, pr])
          for rc, qb, pr in units}
    lse = {u: (m[u] + jnp.log2(l[u])) * math.log(2.0) for u in units}
    for rc, qb, pr in units:
        o_ref[0, rc, rows[qb], ls[pr]] = jnp.where(first_head, pv[rc, qb, pr][:hb], pv[rc, qb, pr][hb:])
        lse_ref[0, rc, rows[qb], ls[pr]] = jnp.where(first_head, lse[rc, qb, pr][:hb], lse[rc, qb, pr][hb:])


def _rel_bucket(rel):
    nb = REL_BUCKETS // 2
    max_exact = nb // 2
    ret = jnp.where(rel > 0, nb, 0)
    n = jnp.abs(rel)
    nf = jnp.maximum(n, 1).astype(F32)
    large = max_exact + (jnp.log(nf * (1.0 / max_exact)) / math.log(REL_MAX_DIST / max_exact)
                         * (nb - max_exact)).astype(jnp.int32)
    large = jnp.minimum(large, nb - 1)
    return ret + jnp.where(n < max_exact, n, large)


def _band_bias_kernel(tab_ref, o_ref, *, dils, hg):
    hb = DSWA_HALF
    off = (lax.broadcasted_iota(jnp.int32, (hb, 3 * hb), 1) - hb
           - lax.broadcasted_iota(jnp.int32, (hb, 3 * hb), 0))
    inband = jnp.abs(off) <= hb
    for g, d in enumerate(dils):
        bucket = _rel_bucket(off * d)
        for hh in range(hg):
            acc = jnp.zeros((hb, 3 * hb), F32)
            for b in range(REL_BUCKETS):
                acc = jnp.where(bucket == b, tab_ref[b, g * hg + hh], acc)
            o_ref[g, hh // 2, (hh % 2) * hb:(hh % 2 + 1) * hb, :] = jnp.where(inband, acc * LOG2E, NEG_INF)


def _band_bias(rel_table, dils):
    hb = DSWA_HALF
    hg = rel_table.shape[1] // len(dils)
    return pl.pallas_call(
        functools.partial(_band_bias_kernel, dils=dils, hg=hg),
        in_specs=[pl.BlockSpec(memory_space=pltpu.SMEM)],
        out_shape=jax.ShapeDtypeStruct((len(dils), hg // 2, 2 * hb, 3 * hb), F32),
        name="band_bias",
    )(rel_table)


def _dswa_group(qkv, bias, lt=512):
    B, dil, L, W3 = qkv.shape
    hb = DSWA_HALF
    gw = W3 // 3
    rb = max(1, lt // L)
    lt = min(lt, L)
    r = lt // hb
    main = lambda which: pl.BlockSpec((1, rb, lt, gw), lambda b, d, t: (b, d, t, which))
    prev = lambda which: pl.BlockSpec(
        (1, rb, hb, gw), lambda b, d, t: (b, d, jnp.maximum(t * r - 1, 0), which))
    nxt = lambda which: pl.BlockSpec(
        (1, rb, hb, gw), lambda b, d, t: (b, d, jnp.minimum((t + 1) * r, L // hb - 1), which))
    out = pl.BlockSpec((1, rb, lt, gw), lambda b, d, t: (b, d, t, 0))
    return pl.pallas_call(
        functools.partial(_dswa_kernel, sub_len=L),
        grid=(B, dil // rb, L // lt),
        in_specs=[main(0), main(1), prev(1), nxt(1), main(2), prev(2), nxt(2),
                  pl.BlockSpec(bias.shape, lambda b, d, t: (0, 0, 0))],
        out_specs=[out, out],
        out_shape=[jax.ShapeDtypeStruct((B, dil, L, gw), F32)] * 2,
        compiler_params=_params("parallel", "parallel", "parallel"),
        name=f"dswa_d{dil}",
    )(qkv, qkv, qkv, qkv, qkv, qkv, qkv, bias)


def _dswa_out_mlp_kernel(*refs, dils, final):
    ng = len(dils)
    o_refs, l_refs = refs[:ng], refs[ng:2 * ng]
    x_ref, wo_ref, g_ref, w1_ref, w2_ref, gf_ref, out_ref, o_s, l_s, y_s, x1_ref, acc_ref = refs[2 * ng:]
    tm = x_ref.shape[0]
    tiles = o_refs[0].shape[3] // LANES
    def prologue(y):
        for g, d in enumerate(dils):
            for r in range(d):
                for j in range(tiles):
                    ls = slice(j * LANES, (j + 1) * LANES)
                    o_s[g * tiles + j, pl.ds(r, tm // d, stride=d), :] = o_refs[g][0, r, :, ls]
                    l_s[g * tiles + j, pl.ds(r, tm // d, stride=d), :] = l_refs[g][0, r, :, ls]
        pieces = []
        for j in range(tiles):
            lses = [l_s[g * tiles + j] for g in range(ng)]
            m = functools.reduce(jnp.maximum, lses)
            es = [jnp.exp(l - m) for l in lses]
            inv = 1.0 / functools.reduce(lambda a, b: a + b, es)
            for g in range(ng):
                c = g * tiles + j
                pieces.append(o_s[c] * (es[g] * inv))
                y[:, c * LANES:(c + 1) * LANES] = pieces[-1].astype(BF16)
        return pieces
    _mixer_out_mlp(prologue, y_s, (x_ref, wo_ref, g_ref, w1_ref, w2_ref, gf_ref, out_ref, x1_ref, acc_ref),
                   final, lookahead=False)


def _dswa_out_mlp(os_, lses, x2, w_out, S, g, w1, w2, g_final, final, tm=512):
    dils = tuple(o.shape[1] for o in os_)
    gw = os_[0].shape[3]
    nt = S // tm
    specs = lambda cur: [pl.BlockSpec((1, d, tm // d, gw), lambda i: (cur(i) // nt, 0, cur(i) % nt, 0))
                         for d in dils] * 2
    staging = pltpu.VMEM((len(dils) * gw // LANES, tm, LANES), F32)
    return _out_mlp_call(
        functools.partial(_dswa_out_mlp_kernel, dils=dils, final=final),
        (*os_, *lses), specs, [staging, staging],
        x2, w_out, g, w1, w2, g_final, tm, "dswa_out_mlp", lookahead=False)


def _gdn_layer(x2, g, w_in, conv_w, a_log, dt_bias, norm_w, w_out, B, S, mlp):
    nh = a_log.shape[1]
    z_w = nh * LANES
    qkvn, z, col, row = _gdn_proj(x2, g, w_in, conv_w, a_log, dt_bias, B, S)
    qkvn = qkvn.reshape(B, S, 3 * z_w)
    o_f = _gdn_scan(qkvn, col, row, nh, rev=False)
    o_b = _gdn_scan(qkvn, col, row, nh, rev=True)
    return _gdn_out_mlp(o_f.reshape(B * S, z_w), o_b.reshape(B * S, z_w), z, x2, norm_w,
                        w_out.astype(BF16), nh, *mlp)


def _dswa_proj_kernel(x_ref, g_ref, w_ref, *refs, dils, chunk):
    o_refs, acc_ref = refs[:-1], refs[-1]
    tm = x_ref.shape[0]
    n = w_ref.shape[1]
    ng = len(dils)
    gw = n // (3 * ng)
    tiles = gw // LANES
    h = _rms(x_ref[...], g_ref[...]).astype(BF16)
    def project(which):
        res = jnp.dot(h, w_ref[:, which * chunk:(which + 1) * chunk], preferred_element_type=F32)
        if which == 0:
            res = res * (DSWA_HEAD_DIM ** -0.5 * LOG2E)
        for j in range(chunk // LANES):
            acc_ref[which * ng * tiles + j] = res[:, j * LANES:(j + 1) * LANES]

    def regroup(which):
        for g, d in enumerate(dils):
            for r in range(d):
                for j in range(tiles):
                    dst = slice((which * tiles + j) * LANES, (which * tiles + j + 1) * LANES)
                    o_refs[g][0, r, :, dst] = \
                        acc_ref[(which * ng + g) * tiles + j, pl.ds(r, tm // d, stride=d), :].astype(BF16)

    project(0)
    for which in range(1, 3):
        project(which)
        regroup(which - 1)
    regroup(2)


def _dswa_proj(x2, g, w_in, dils, B, S, tm=512):
    T, D = x2.shape
    n = w_in.shape[1]
    gw3 = n // len(dils)
    nt = S // tm
    return pl.pallas_call(
        functools.partial(_dswa_proj_kernel, dils=dils, chunk=gw3),
        grid=(T // tm,),
        in_specs=[pl.BlockSpec((tm, D), lambda i: (i, 0)),
                  pl.BlockSpec((1, D), lambda i: (0, 0)),
                  pl.BlockSpec((D, n), lambda i: (0, 0))],
        out_specs=[pl.BlockSpec((1, d, tm // d, gw3), lambda i: (i // nt, 0, i % nt, 0)) for d in dils],
        out_shape=[jax.ShapeDtypeStruct((B, d, S // d, gw3), BF16) for d in dils],
        scratch_shapes=[pltpu.VMEM((n // LANES, tm, LANES), F32)],
        compiler_params=_params("parallel"),
        name="dswa_proj",
    )(x2, g.reshape(1, D), w_in)


def _dswa_layer(x2, g, w_in, w_out, bias, B, S, mlp):
    dils = tuple(d for _, d in DSWA_CONFIGS)
    qkvs = _dswa_proj(x2, g, w_in.astype(BF16), dils, B, S)
    os_, lses = [], []
    for gi in range(len(dils)):
        o, lse = _dswa_group(qkvs[gi], bias[gi])
        os_.append(o)
        lses.append(lse)
    return _dswa_out_mlp(os_, lses, x2, w_out.astype(BF16), S, *mlp)


def kernel(x, norm_mix, norm_mlp, norm_final, rel_bias, gdn_w_in, gdn_conv_w, gdn_a_log,
           gdn_dt_bias, gdn_norm_w, gdn_w_out, dswa_w_in, dswa_w_out, mlp_w1, mlp_w2):
    B, S, D = x.shape
    depth = norm_mix.shape[0]
    x2 = x.reshape(B * S, D)
    assert all(window // (2 * dil) == DSWA_HALF for window, dil in DSWA_CONFIGS)
    bias = _band_bias(rel_bias, tuple(d for _, d in DSWA_CONFIGS))
    for i in range(depth):
        j = i // 2
        mlp = (norm_mlp[i], mlp_w1[i].astype(BF16), mlp_w2[i].astype(BF16), norm_final, i == depth - 1)
        if i % 2 == 0:
            x2 = _gdn_layer(x2, norm_mix[i], gdn_w_in[j], gdn_conv_w[j], gdn_a_log[j],
                            gdn_dt_bias[j], gdn_norm_w[j], gdn_w_out[j], B, S, mlp)
        else:
            x2 = _dswa_layer(x2, norm_mix[i], dswa_w_in[j], dswa_w_out[j], bias, B, S, mlp)
    return x2.reshape(B, S, D)
```

```python
import functools
import math

import jax
import jax.numpy as jnp
from jax import lax
from jax.experimental import pallas as pl
from jax.experimental.pallas import tpu as pltpu

F32 = jnp.float32
BF16 = jnp.bfloat16

RMS_EPS = 1e-6
L2_EPS = 1e-6
NEG_INF = -1e30
LOG2E = math.log2(math.e)

LANES = 128
SUBLANES = 8
VMEM_LIMIT = 56 * 1024 * 1024

GDN_DK = 128
GDN_CONV = 5
GDN_CHUNK = LANES
DSWA_HEAD_DIM = 64
DSWA_CONFIGS = ((128, 1), (512, 4), (2048, 16))
DSWA_HALF = 64
DSWA_QBLOCK = 128
REL_BUCKETS = 32
REL_MAX_DIST = 1024


def _params(*sem):
    return pltpu.CompilerParams(dimension_semantics=sem, vmem_limit_bytes=VMEM_LIMIT)


def _dot(a, b):
    return jnp.dot(a.astype(BF16), b.astype(BF16), preferred_element_type=F32)


def _dot_nt(a, b):
    return lax.dot_general(a.astype(BF16), b.astype(BF16), (((1,), (1,)), ((), ())),
                           preferred_element_type=F32)


def _split3(x):
    hi = x.astype(BF16)
    r = x - hi.astype(F32)
    mid = r.astype(BF16)
    return hi, mid, (r - mid.astype(F32)).astype(BF16)


def _rms(x, g):
    return x * lax.rsqrt(jnp.mean(x * x, axis=-1, keepdims=True) + RMS_EPS) * g


def _sigmoid(x):
    return 1.0 / (1.0 + jnp.exp(-x))


def _softplus(x):
    return jnp.maximum(x, 0.0) + jnp.log1p(jnp.exp(-jnp.abs(x)))


CONV_HALO = 16
CONV_BLOCK = 64


def _gdn_proj_kernel(x_ref, xp_ref, xn_ref, g_ref, wqkv_ref, wz_ref, w_ref, wt_ref, cw_ref,
                     prow_ref, pcol_ref, qkv_ref, z_ref, col_ref, row_ref, *, nh, nt, chunk):
    C = GDN_CHUNK
    tm = x_ref.shape[0]
    t = pl.program_id(0) % nt
    pad = GDN_CONV // 2
    g = g_ref[...]
    h = _rms(x_ref[...], g).astype(BF16)
    h_halo = jnp.concatenate([_rms(jnp.where(t > 0, xp_ref[...], 0.0), g).astype(BF16),
                              _rms(jnp.where(t < nt - 1, xn_ref[...], 0.0), g).astype(BF16)], axis=0)
    CB = CONV_BLOCK
    taps = [i for i in range(GDN_CONV) if i != pad]
    sr = lax.broadcasted_iota(jnp.int32, (len(taps) * CB, CB + 2 * CONV_HALO), 0)
    sc = lax.broadcasted_iota(jnp.int32, (len(taps) * CB, CB + 2 * CONV_HALO), 1)
    src = (sr % CB) + (CONV_HALO - pad)
    for k, i in enumerate(taps):
        src = src + jnp.where(sr // CB == k, i, 0)
    shift_mat = (sc == src).astype(BF16)

    def project(c0):
        w = wqkv_ref[:, c0:c0 + chunk]
        main = jnp.dot(h, w, preferred_element_type=F32)
        halo = jnp.dot(h_halo, w, preferred_element_type=F32)
        ext = jnp.concatenate([halo[:CONV_HALO], main, halo[CONV_HALO:]], axis=0).astype(BF16)
        return main, ext

    starts = list(range(0, wqkv_ref.shape[1], chunk))
    nxt = project(starts[0])
    for n, c0 in enumerate(starts):
        cs = slice(c0, c0 + chunk)
        main, ext = nxt
        if n + 1 < len(starts):
            nxt = project(starts[n + 1])
        for b in range(tm // CB):
            rows = slice(b * CB, (b + 1) * CB)
            shifted = jnp.dot(shift_mat, ext[b * CB:(b + 1) * CB + 2 * CONV_HALO],
                              preferred_element_type=F32)
            acc = cw_ref[pad:pad + 1, cs] * main[rows]
            for k, i in enumerate(taps):
                acc = acc + cw_ref[i:i + 1, cs] * shifted[k * CB:(k + 1) * CB]
            y = acc * _sigmoid(acc)
            for j in range(chunk // LANES):
                yj = y[:, j * LANES:(j + 1) * LANES]
                head = c0 // LANES + j
                if head < 2 * nh:
                    inv = lax.rsqrt(jnp.sum(yj * yj, axis=-1, keepdims=True) + L2_EPS)
                    yj = yj * (inv * (GDN_DK ** -0.5) if head < nh else inv)
                qkv_ref[rows, c0 + j * LANES:c0 + (j + 1) * LANES] = yj.astype(BF16)
    for c0 in range(0, wz_ref.shape[1], chunk):
        cs = slice(c0, c0 + chunk)
        z_ref[:, cs] = jnp.dot(h, wz_ref[:, cs], preferred_element_type=F32).astype(BF16)
    a_c = jnp.dot(h, w_ref[...], preferred_element_type=F32)
    a_r = lax.dot_general(wt_ref[...], h, (((1,), (1,)), ((), ())),
                          preferred_element_type=F32)
    ri = lax.broadcasted_iota(jnp.int32, (C, C), 0)
    ci = lax.broadcasted_iota(jnp.int32, (C, C), 1)
    lower = (ri >= ci).astype(BF16)
    upper = (ri <= ci).astype(BF16)
    g_c = -jnp.exp(prow_ref[0:1, :]) * _softplus(a_c + prow_ref[1:2, :])
    g_r = -jnp.exp(pcol_ref[:, 0:1]) * _softplus(a_r + pcol_ref[:, 1:2])
    beta_c = _sigmoid(a_c)
    beta_r = _sigmoid(a_r)
    for c in range(tm // C):
        sl = slice(c * C, (c + 1) * C)
        gc = g_c[sl, :]
        pre = sum(jnp.dot(lower, part, preferred_element_type=F32) for part in _split3(gc))
        cum = jnp.where(ci < nh, pre, pre[C - 1:C, :] - pre + gc)
        col_ref[sl, :] = jnp.where(ci < 2 * nh, cum, beta_c[sl, :])
        gr = g_r[:, sl]
        pre_r = sum(jnp.dot(part, upper, preferred_element_type=F32) for part in _split3(gr))
        cum_r = jnp.where(ri < nh, pre_r, pre_r[:, C - 1:C] - pre_r + gr)
        full = jnp.where(ri < 2 * nh, cum_r, beta_r[:, sl])
        row_ref[0, :, sl] = full[:4 * nh, :]


def _gdn_proj(x2, g, w_in, conv_w, a_log, dt_bias, B, S, tm=1024, chunk=256):
    T, D = x2.shape
    nh = a_log.shape[1]
    qkv_w = 3 * nh * LANES
    z_w = nh * LANES
    w_pad = jnp.zeros((D, LANES), F32).at[:, :4 * nh].set(w_in[:, qkv_w + z_w:])
    alog = jnp.zeros((LANES,), F32).at[:2 * nh].set(a_log.reshape(-1))
    dtb = jnp.zeros((LANES,), F32).at[:2 * nh].set(dt_bias.reshape(-1))
    prow = jnp.zeros((SUBLANES, LANES), F32).at[0].set(alog).at[1].set(dtb)
    pcol = jnp.zeros((LANES, LANES), F32).at[:, 0].set(alog).at[:, 1].set(dtb)
    nt = S // tm
    r = tm // CONV_HALO
    const = lambda shape: pl.BlockSpec(shape, lambda i: (0, 0))
    return pl.pallas_call(
        functools.partial(_gdn_proj_kernel, nh=nh, nt=nt, chunk=chunk),
        grid=(T // tm,),
        in_specs=[pl.BlockSpec((tm, D), lambda i: (i, 0)),
                  pl.BlockSpec((CONV_HALO, D), lambda i: (jnp.maximum(i * r - 1, 0), 0)),
                  pl.BlockSpec((CONV_HALO, D), lambda i: (jnp.minimum((i + 1) * r, T // CONV_HALO - 1), 0)),
                  const((1, D)), const((D, qkv_w)), const((D, z_w)), const((D, LANES)), const((LANES, D)),
                  const((GDN_CONV, qkv_w)), const((SUBLANES, LANES)), const((LANES, LANES))],
        out_specs=[pl.BlockSpec((tm, qkv_w), lambda i: (i, 0)),
                   pl.BlockSpec((tm, z_w), lambda i: (i, 0)),
                   pl.BlockSpec((tm, LANES), lambda i: (i, 0)),
                   pl.BlockSpec((1, 4 * nh, tm), lambda i: (i // nt, 0, i % nt))],
        out_shape=[jax.ShapeDtypeStruct((T, qkv_w), BF16),
                   jax.ShapeDtypeStruct((T, z_w), BF16),
                   jax.ShapeDtypeStruct((T, LANES), F32),
                   jax.ShapeDtypeStruct((B, 4 * nh, S), F32)],
        compiler_params=_params("parallel"),
        name="gdn_proj",
    )(x2, x2, x2, g.reshape(1, D), w_in[:, :qkv_w].astype(BF16), w_in[:, qkv_w:qkv_w + z_w].astype(BF16),
      w_pad.astype(BF16), w_pad.T.astype(BF16), conv_w.reshape(GDN_CONV, qkv_w), prow, pcol)


def _gdn_scan_kernel(q_ref, k_ref, v_ref, col_ref, row_ref, o_ref, state_ref, *, rev, nh):
    C = GDN_CHUNK
    t = pl.program_id(1)

    @pl.when(t == 0)
    def _():
        state_ref[...] = jnp.zeros_like(state_ref)

    ri = lax.broadcasted_iota(jnp.int32, (C, C), 0)
    ci = lax.broadcasted_iota(jnp.int32, (C, C), 1)
    incl = (ri <= ci) if rev else (ri >= ci)
    strict = (ri < ci) if rev else (ri > ci)
    eye = (ri == ci).astype(F32)
    levels = [((ri >> (lg + 1)) == (ci >> (lg + 1))) & ((ri >> lg) != (ci >> lg))
              for lg in range(int(math.log2(C)))]
    last = 0 if rev else C - 1
    nc = q_ref.shape[1] // C
    chunks = list(reversed(range(nc))) if rev else list(range(nc))
    heads = range(nh)
    hs = [slice(h * LANES, (h + 1) * LANES) for h in heads]
    gi = [(nh if rev else 0) + h for h in heads]
    units = [(c, h) for c in chunks for h in heads]
    rows = {c: slice(c * C, (c + 1) * C) for c in chunks}
    col = {c: col_ref[0, rows[c], :] for c in chunks}
    row = {c: row_ref[0, :, rows[c]] for c in chunks}
    kf = {u: k_ref[0, rows[u[0]], hs[u[1]]] for u in units}
    k = {u: kf[u].astype(F32) for u in units}
    g_col = {(c, h): col[c][:, gi[h]:gi[h] + 1] for c, h in units}
    b_col = {(c, h): col[c][:, 2 * nh + gi[h]:2 * nh + gi[h] + 1] for c, h in units}
    g_last = {u: g_col[u][last:last + 1, :] for u in units}
    decay = {(c, h): jnp.where(incl, jnp.exp(jnp.where(incl, g_col[c, h] - row[c][gi[h]:gi[h] + 1, :], 0.0)), 0.0)
             for c, h in units}
    kb = {u: k[u] * b_col[u] for u in units}
    res = {(c, h): _dot_nt(jnp.concatenate([kb[c, h].astype(BF16), q_ref[0, rows[c], hs[h]]], axis=0), kf[c, h])
           for c, h in units}
    a = {u: jnp.where(strict, res[u][:C] * decay[u], 0.0) for u in units}
    intra = {u: (res[u][C:] * decay[u]).astype(BF16) for u in units}
    tinv = {u: eye - jnp.where(levels[0], a[u], 0.0) for u in units}
    for lg, m in enumerate(levels[1:], start=1):
        size = 1 << lg
        moving = [r0 for r0 in range(0, C, size) if ((r0 >> lg) & 1) == (0 if rev else 1)]
        tb = {u: tinv[u].astype(BF16) for u in units}
        if size < SUBLANES:
            pm = {u: _dot(tb[u], jnp.where(m, a[u], 0.0)) for u in units}
            tinv = {u: tinv[u] - _dot(pm[u], tb[u]) for u in units}
            continue
        lhs = {u: jnp.concatenate([tinv[u][r0:r0 + size] for r0 in moving], axis=0) for u in units}
        pm = {u: _dot(lhs[u], jnp.where(m, a[u], 0.0)) for u in units}
        new = {u: lhs[u] - _dot(pm[u], tb[u]) for u in units}
        tinv = {u: jnp.concatenate(
            [new[u][moving.index(r0) * size:(moving.index(r0) + 1) * size] if r0 in moving
             else tinv[u][r0:r0 + size] for r0 in range(0, C, size)], axis=0) for u in units}
    e_col = {u: jnp.exp(g_col[u]) for u in units}
    sol = {(c, h): _dot(tinv[c, h], jnp.concatenate(
        [v_ref[0, rows[c], hs[h]].astype(F32) * b_col[c, h], kb[c, h] * e_col[c, h]], axis=1)) for c, h in units}
    qd = {(c, h): (q_ref[0, rows[c], hs[h]].astype(F32) * e_col[c, h]).astype(BF16) for c, h in units}
    post = {u: jnp.concatenate([intra[u], (k[u] * jnp.exp(g_last[u] - g_col[u])).T.astype(BF16)], axis=0)
            for u in units}
    state = [state_ref[h] for h in heads]
    for c in chunks:
        sb = [state[h].astype(BF16) for h in heads]
        r2 = [_dot(jnp.concatenate([sol[c, h][:, LANES:].astype(BF16), qd[c, h]], axis=0), sb[h]) for h in heads]
        v_new = [(sol[c, h][:, :LANES] - r2[h][:C]).astype(BF16) for h in heads]
        r3 = [_dot(post[c, h], v_new[h]) for h in heads]
        for h in heads:
            o_ref[0, rows[c], hs[h]] = (r2[h][C:] + r3[h][:C]).astype(o_ref.dtype)
        state = [state[h] * jnp.exp(g_last[c, h]) + r3[h][C:] for h in heads]
    for h in heads:
        state_ref[h] = state[h]


def _gdn_scan(qkvn, col, row, nh, rev, nc=4):
    B, S, _ = qkvn.shape
    ts = nc * GDN_CHUNK
    nt = S // ts
    width = nh * LANES
    tmap = (lambda t: nt - 1 - t) if rev else (lambda t: t)
    col3 = col.reshape(B, S, LANES)
    return pl.pallas_call(
        functools.partial(_gdn_scan_kernel, rev=rev, nh=nh),
        grid=(B, nt),
        in_specs=[pl.BlockSpec((1, ts, width), lambda b, t: (b, tmap(t), 0)),
                  pl.BlockSpec((1, ts, width), lambda b, t: (b, tmap(t), 1)),
                  pl.BlockSpec((1, ts, width), lambda b, t: (b, tmap(t), 2)),
                  pl.BlockSpec((1, ts, LANES), lambda b, t: (b, tmap(t), 0)),
                  pl.BlockSpec((1, 4 * nh, ts), lambda b, t: (b, 0, tmap(t)))],
        out_specs=pl.BlockSpec((1, ts, width), lambda b, t: (b, tmap(t), 0)),
        out_shape=jax.ShapeDtypeStruct((B, S, width), BF16),
        scratch_shapes=[pltpu.VMEM((nh, GDN_DK, LANES), F32)],
        compiler_params=_params("parallel", "arbitrary"),
        name="gdn_scan_bwd" if rev else "gdn_scan_fwd",
    )(qkvn, qkvn, qkvn, col3, row)


MLP_CHUNK = 512


def _anchor(v):
    folded = jnp.sum(v.reshape(v.shape[0] // SUBLANES, SUBLANES, LANES), axis=0)
    bits = lax.shift_right_logical(pltpu.bitcast(folded, jnp.uint32), jnp.uint32(32))
    return pltpu.bitcast(bits, F32)


def _out_mlp_tail(y_ref, x_ref, wo_ref, g_ref, w1_ref, w2_ref, gf_ref, o_ref, x1_ref, acc_ref, final,
                  anchors=()):
    x1_ref[...] = x_ref[...] + jnp.dot(y_ref[...], wo_ref[...], preferred_element_type=F32)
    h = _rms(x1_ref[...], g_ref[...]).astype(BF16)
    starts = list(range(0, w1_ref.shape[1], MLP_CHUNK))
    for n, c in enumerate(starts):
        a = jnp.maximum(jnp.dot(h, w1_ref[:, c:c + MLP_CHUNK], preferred_element_type=F32), 0.0)
        part = jnp.dot((a * a).astype(BF16), w2_ref[c:c + MLP_CHUNK, :], preferred_element_type=F32)
        if c == 0:
            acc_ref[...] = part
        else:
            acc_ref[...] += part
        for k in range(n, len(anchors), len(starts)):
            acc_ref[0:SUBLANES, 0:LANES] += anchors[k]
    out = x1_ref[...] + acc_ref[...]
    if final:
        out = _rms(out, gf_ref[...])
    o_ref[...] = out


def _mixer_out_mlp(prologue, y_ref, tail_refs, final, lookahead):
    if not lookahead:
        prologue(y_ref.at[0])
        _out_mlp_tail(y_ref.at[0], *tail_refs, final)
        return
    i = pl.program_id(0)

    @pl.when(i == 0)
    def _():
        y_ref[...] = jnp.zeros_like(y_ref)

    for parity in range(2):
        @pl.when(i % 2 == parity)
        def _():
            pieces = prologue(y_ref.at[parity])
            _out_mlp_tail(y_ref.at[1 - parity], *tail_refs, final, anchors=[_anchor(v) for v in pieces])


def _out_mlp_call(kernel_fn, mixer_args, mixer_specs, mixer_scratch, x2, w_out, g, w1, w2,
                  g_final, tm, name, lookahead):
    T, D = x2.shape
    n = T // tm
    if lookahead:
        cur = lambda i: jnp.minimum(i, n - 1)
        prev = lambda i: jnp.maximum(i - 1, 0)
    else:
        cur = prev = lambda i: i
    once = lambda shape: pl.BlockSpec(shape, lambda i: (0, 0), pipeline_mode=pl.Buffered(1))
    return pl.pallas_call(
        kernel_fn,
        grid=(n + 1 if lookahead else n,),
        in_specs=mixer_specs(cur) + [pl.BlockSpec((tm, D), lambda i: (prev(i), 0)), once(w_out.shape),
                                     once((1, D)), once(w1.shape), once(w2.shape), once((1, D))],
        out_specs=pl.BlockSpec((tm, D), lambda i: (prev(i), 0)),
        out_shape=jax.ShapeDtypeStruct((T, D), F32),
        scratch_shapes=mixer_scratch + [pltpu.VMEM((2 if lookahead else 1, tm, w_out.shape[0]), BF16),
                                        pltpu.VMEM((tm, D), F32), pltpu.VMEM((tm, D), F32)],
        compiler_params=_params("arbitrary" if lookahead else "parallel"),
        name=name,
    )(*mixer_args, x2, w_out, g.reshape(1, D), w1, w2, g_final.reshape(1, D))


def _gdn_out_mlp_kernel(of_ref, ob_ref, z_ref, nw_ref, x_ref, wo_ref, g_ref, w1_ref, w2_ref, gf_ref,
                        o_ref, y_ref, x1_ref, acc_ref, *, nh, final):
    def prologue(y):
        pieces = []
        for h in range(nh):
            hs = slice(h * LANES, (h + 1) * LANES)
            o = of_ref[:, hs].astype(F32) + ob_ref[:, hs].astype(F32)
            o = o * lax.rsqrt(jnp.mean(o * o, axis=-1, keepdims=True) + RMS_EPS)
            z = z_ref[:, hs].astype(F32)
            pieces.append(o * nw_ref[...] * (z * _sigmoid(z)))
            y[:, hs] = pieces[-1].astype(BF16)
        return pieces
    _mixer_out_mlp(prologue, y_ref, (x_ref, wo_ref, g_ref, w1_ref, w2_ref, gf_ref, o_ref, x1_ref, acc_ref),
                   final, lookahead=True)


def _gdn_out_mlp(o_f, o_b, z, x2, norm_w, w_out, nh, g, w1, w2, g_final, final, tm=512):
    W = o_f.shape[1]
    specs = lambda cur: [pl.BlockSpec((tm, W), lambda i: (cur(i), 0))] * 3 + \
                        [pl.BlockSpec((1, LANES), lambda i: (0, 0))]
    return _out_mlp_call(
        functools.partial(_gdn_out_mlp_kernel, nh=nh, final=final),
        (o_f, o_b, z, norm_w.reshape(1, LANES)), specs,
        [], x2, w_out, g, w1, w2, g_final, tm, "gdn_out_mlp", lookahead=True)


def _dswa_kernel(q_ref, k_ref, kp_ref, kn_ref, v_ref, vp_ref, vn_ref, bias_ref,
                 o_ref, lse_ref, *, sub_len):
    t = pl.program_id(2)
    rb, lt = q_ref.shape[1], q_ref.shape[2]
    hb, qn = DSWA_HALF, DSWA_QBLOCK
    kw = qn + 2 * hb
    first_head = lax.broadcasted_iota(jnp.int32, (1, LANES), 1) < DSWA_HEAD_DIM
    n_pairs = q_ref.shape[3] // LANES
    blocks = range(lt // qn)
    units = [(rc, qb, pr) for rc in range(rb) for qb in blocks for pr in range(n_pairs)]
    rows = [slice(qb * qn, (qb + 1) * qn) for qb in blocks]
    win = [slice(qb * qn, qb * qn + kw) for qb in blocks]
    ls = [slice(pr * LANES, (pr + 1) * LANES) for pr in range(n_pairs)]
    kext = [jnp.concatenate([kp_ref[0, rc], k_ref[0, rc], kn_ref[0, rc]], axis=0) for rc in range(rb)]
    vext = [jnp.concatenate([vp_ref[0, rc], v_ref[0, rc], vn_ref[0, rc]], axis=0) for rc in range(rb)]
    kpos = lambda qb: t * lt + qb * qn - hb + lax.broadcasted_iota(jnp.int32, (1, kw), 1)
    qs = {(rc, qb, pr): jnp.concatenate([jnp.where(first_head, q_ref[0, rc, rows[qb], ls[pr]], 0.0),
                                         jnp.where(first_head, 0.0, q_ref[0, rc, rows[qb], ls[pr]])], axis=0)
          for rc, qb, pr in units}
    s = {(rc, qb, pr): _dot_nt(qs[rc, qb, pr], kext[rc][win[qb], ls[pr]]) + bias_ref[pr] for rc, qb, pr in units}
    for rc, qb, pr in units:
        if qb == blocks[0]:
            s[rc, qb, pr] = jnp.where(kpos(qb) >= 0, s[rc, qb, pr], NEG_INF)
        if qb == blocks[-1]:
            s[rc, qb, pr] = jnp.where(kpos(qb) < sub_len, s[rc, qb, pr], NEG_INF)
    m = {u: jnp.max(s[u], axis=-1, keepdims=True) for u in units}
    p = {u: jnp.exp2(s[u] - m[u]) for u in units}
    l = {u: jnp.sum(p[u], axis=-1, keepdims=True) for u in units}
    pv = {(rc, qb, pr): _dot(p[rc, qb, pr], vext[rc][win[qb], ls[pr]]) * (1.0 / l[rc, qb, pr])
          for rc, qb, pr in units}
    lse = {u: (m[u] + jnp.log2(l[u])) * math.log(2.0) for u in units}
    for rc, qb, pr in units:
        o_ref[0, rc, rows[qb], ls[pr]] = jnp.where(first_head, pv[rc, qb, pr][:qn], pv[rc, qb, pr][qn:])
        lse_ref[0, rc, rows[qb], ls[pr]] = jnp.where(first_head, lse[rc, qb, pr][:qn], lse[rc, qb, pr][qn:])


def _rel_bucket(rel):
    nb = REL_BUCKETS // 2
    max_exact = nb // 2
    ret = jnp.where(rel > 0, nb, 0)
    n = jnp.abs(rel)
    nf = jnp.maximum(n, 1).astype(F32)
    large = max_exact + (jnp.log(nf * (1.0 / max_exact)) / math.log(REL_MAX_DIST / max_exact)
                         * (nb - max_exact)).astype(jnp.int32)
    large = jnp.minimum(large, nb - 1)
    return ret + jnp.where(n < max_exact, n, large)


def _band_bias_kernel(tab_ref, o_ref, *, dils, hg):
    hb, qn = DSWA_HALF, DSWA_QBLOCK
    kw = qn + 2 * hb
    off = (lax.broadcasted_iota(jnp.int32, (qn, kw), 1) - hb
           - lax.broadcasted_iota(jnp.int32, (qn, kw), 0))
    inband = jnp.abs(off) <= hb
    for g, d in enumerate(dils):
        bucket = _rel_bucket(off * d)
        for hh in range(hg):
            acc = jnp.zeros((qn, kw), F32)
            for b in range(REL_BUCKETS):
                acc = jnp.where(bucket == b, tab_ref[b, g * hg + hh], acc)
            o_ref[g, hh // 2, (hh % 2) * qn:(hh % 2 + 1) * qn, :] = jnp.where(inband, acc * LOG2E, NEG_INF)


def _band_bias(rel_table, dils):
    qn = DSWA_QBLOCK
    hg = rel_table.shape[1] // len(dils)
    return pl.pallas_call(
        functools.partial(_band_bias_kernel, dils=dils, hg=hg),
        in_specs=[pl.BlockSpec(memory_space=pltpu.SMEM)],
        out_shape=jax.ShapeDtypeStruct((len(dils), hg // 2, 2 * qn, qn + 2 * DSWA_HALF), F32),
        name="band_bias",
    )(rel_table)


def _dswa_group(qkv, bias, lt=512):
    B, dil, L, W3 = qkv.shape
    hb = DSWA_HALF
    gw = W3 // 3
    rb = max(1, lt // L)
    lt = min(lt, L)
    r = lt // hb
    main = lambda which: pl.BlockSpec((1, rb, lt, gw), lambda b, d, t: (b, d, t, which))
    prev = lambda which: pl.BlockSpec(
        (1, rb, hb, gw), lambda b, d, t: (b, d, jnp.maximum(t * r - 1, 0), which))
    nxt = lambda which: pl.BlockSpec(
        (1, rb, hb, gw), lambda b, d, t: (b, d, jnp.minimum((t + 1) * r, L // hb - 1), which))
    out = pl.BlockSpec((1, rb, lt, gw), lambda b, d, t: (b, d, t, 0))
    return pl.pallas_call(
        functools.partial(_dswa_kernel, sub_len=L),
        grid=(B, dil // rb, L // lt),
        in_specs=[main(0), main(1), prev(1), nxt(1), main(2), prev(2), nxt(2),
                  pl.BlockSpec(bias.shape, lambda b, d, t: (0, 0, 0))],
        out_specs=[out, out],
        out_shape=[jax.ShapeDtypeStruct((B, dil, L, gw), F32)] * 2,
        compiler_params=_params("parallel", "parallel", "parallel"),
        name=f"dswa_d{dil}",
    )(qkv, qkv, qkv, qkv, qkv, qkv, qkv, bias)


def _dswa_out_mlp_kernel(*refs, dils, final):
    ng = len(dils)
    o_refs, l_refs = refs[:ng], refs[ng:2 * ng]
    x_ref, wo_ref, g_ref, w1_ref, w2_ref, gf_ref, out_ref, o_s, l_s, y_s, x1_ref, acc_ref = refs[2 * ng:]
    tm = x_ref.shape[0]
    tiles = o_refs[0].shape[3] // LANES
    def prologue(y):
        for g, d in enumerate(dils):
            for r in range(d):
                for j in range(tiles):
                    ls = slice(j * LANES, (j + 1) * LANES)
                    o_s[g * tiles + j, pl.ds(r, tm // d, stride=d), :] = o_refs[g][0, r, :, ls]
                    l_s[g * tiles + j, pl.ds(r, tm // d, stride=d), :] = l_refs[g][0, r, :, ls]
        pieces = []
        for j in range(tiles):
            lses = [l_s[g * tiles + j] for g in range(ng)]
            m = functools.reduce(jnp.maximum, lses)
            es = [jnp.exp(l - m) for l in lses]
            inv = 1.0 / functools.reduce(lambda a, b: a + b, es)
            for g in range(ng):
                c = g * tiles + j
                pieces.append(o_s[c] * (es[g] * inv))
                y[:, c * LANES:(c + 1) * LANES] = pieces[-1].astype(BF16)
        return pieces
    _mixer_out_mlp(prologue, y_s, (x_ref, wo_ref, g_ref, w1_ref, w2_ref, gf_ref, out_ref, x1_ref, acc_ref),
                   final, lookahead=False)


def _dswa_out_mlp(os_, lses, x2, w_out, S, g, w1, w2, g_final, final, tm=512):
    dils = tuple(o.shape[1] for o in os_)
    gw = os_[0].shape[3]
    nt = S // tm
    specs = lambda cur: [pl.BlockSpec((1, d, tm // d, gw), lambda i: (cur(i) // nt, 0, cur(i) % nt, 0))
                         for d in dils] * 2
    staging = pltpu.VMEM((len(dils) * gw // LANES, tm, LANES), F32)
    return _out_mlp_call(
        functools.partial(_dswa_out_mlp_kernel, dils=dils, final=final),
        (*os_, *lses), specs, [staging, staging],
        x2, w_out, g, w1, w2, g_final, tm, "dswa_out_mlp", lookahead=False)


def _gdn_layer(x2, g, w_in, conv_w, a_log, dt_bias, norm_w, w_out, B, S, mlp):
    nh = a_log.shape[1]
    z_w = nh * LANES
    qkvn, z, col, row = _gdn_proj(x2, g, w_in, conv_w, a_log, dt_bias, B, S)
    qkvn = qkvn.reshape(B, S, 3 * z_w)
    o_f = _gdn_scan(qkvn, col, row, nh, rev=False)
    o_b = _gdn_scan(qkvn, col, row, nh, rev=True)
    return _gdn_out_mlp(o_f.reshape(B * S, z_w), o_b.reshape(B * S, z_w), z, x2, norm_w,
                        w_out.astype(BF16), nh, *mlp)


def _dswa_proj_kernel(x_ref, g_ref, w_ref, *refs, dils, chunk):
    o_refs, acc_ref = refs[:-1], refs[-1]
    tm = x_ref.shape[0]
    n = w_ref.shape[1]
    ng = len(dils)
    gw = n // (3 * ng)
    tiles = gw // LANES
    h = _rms(x_ref[...], g_ref[...]).astype(BF16)
    def project(which):
        res = jnp.dot(h, w_ref[:, which * chunk:(which + 1) * chunk], preferred_element_type=F32)
        if which == 0:
            res = res * (DSWA_HEAD_DIM ** -0.5 * LOG2E)
        for j in range(chunk // LANES):
            acc_ref[which * ng * tiles + j] = res[:, j * LANES:(j + 1) * LANES]

    def regroup(which):
        for g, d in enumerate(dils):
            for r in range(d):
                for j in range(tiles):
                    dst = slice((which * tiles + j) * LANES, (which * tiles + j + 1) * LANES)
                    o_refs[g][0, r, :, dst] = \
                        acc_ref[(which * ng + g) * tiles + j, pl.ds(r, tm // d, stride=d), :].astype(BF16)

    project(0)
    for which in range(1, 3):
        project(which)
        regroup(which - 1)
    regroup(2)


def _dswa_proj(x2, g, w_in, dils, B, S, tm=512):
    T, D = x2.shape
    n = w_in.shape[1]
    gw3 = n // len(dils)
    nt = S // tm
    return pl.pallas_call(
        functools.partial(_dswa_proj_kernel, dils=dils, chunk=gw3),
        grid=(T // tm,),
        in_specs=[pl.BlockSpec((tm, D), lambda i: (i, 0)),
                  pl.BlockSpec((1, D), lambda i: (0, 0)),
                  pl.BlockSpec((D, n), lambda i: (0, 0))],
        out_specs=[pl.BlockSpec((1, d, tm // d, gw3), lambda i: (i // nt, 0, i % nt, 0)) for d in dils],
        out_shape=[jax.ShapeDtypeStruct((B, d, S // d, gw3), BF16) for d in dils],
        scratch_shapes=[pltpu.VMEM((n // LANES, tm, LANES), F32)],
        compiler_params=_params("parallel"),
        name="dswa_proj",
    )(x2, g.reshape(1, D), w_in)


def _dswa_layer(x2, g, w_in, w_out, bias, B, S, mlp):
    dils = tuple(d for _, d in DSWA_CONFIGS)
    qkvs = _dswa_proj(x2, g, w_in.astype(BF16), dils, B, S)
    os_, lses = [], []
    for gi in range(len(dils)):
        o, lse = _dswa_group(qkvs[gi], bias[gi])
        os_.append(o)
        lses.append(lse)
    return _dswa_out_mlp(os_, lses, x2, w_out.astype(BF16), S, *mlp)


def kernel(x, norm_mix, norm_mlp, norm_final, rel_bias, gdn_w_in, gdn_conv_w, gdn_a_log,
           gdn_dt_bias, gdn_norm_w, gdn_w_out, dswa_w_in, dswa_w_out, mlp_w1, mlp_w2):
    B, S, D = x.shape
    depth = norm_mix.shape[0]
    x2 = x.reshape(B * S, D)
    assert all(window // (2 * dil) == DSWA_HALF for window, dil in DSWA_CONFIGS)
    bias = _band_bias(rel_bias, tuple(d for _, d in DSWA_CONFIGS))
    for i in range(depth):
        j = i // 2
        mlp = (norm_mlp[i], mlp_w1[i].astype(BF16), mlp_w2[i].astype(BF16), norm_final, i == depth - 1)
        if i % 2 == 0:
            x2 = _gdn_layer(x2, norm_mix[i], gdn_w_in[j], gdn_conv_w[j], gdn_a_log[j],
                            gdn_dt_bias[j], gdn_norm_w[j], gdn_w_out[j], B, S, mlp)
        else:
            x2 = _dswa_layer(x2, norm_mix[i], dswa_w_in[j], dswa_w_out[j], bias, B, S, mlp)
    return x2.reshape(B, S, D)
```

```python
import functools
import math

import jax
import jax.numpy as jnp
from jax import lax
from jax.experimental import pallas as pl
from jax.experimental.pallas import tpu as pltpu

F32 = jnp.float32
BF16 = jnp.bfloat16

RMS_EPS = 1e-6
L2_EPS = 1e-6
NEG_INF = -1e30
LOG2E = math.log2(math.e)

LANES = 128
SUBLANES = 8
VMEM_LIMIT = 56 * 1024 * 1024

GDN_DK = 128
GDN_CONV = 5
GDN_CHUNK = LANES
DSWA_HEAD_DIM = 64
DSWA_CONFIGS = ((128, 1), (512, 4), (2048, 16))
DSWA_HALF = 64
DSWA_QBLOCK = 128
REL_BUCKETS = 32
REL_MAX_DIST = 1024


def _params(*sem):
    return pltpu.CompilerParams(dimension_semantics=sem, vmem_limit_bytes=VMEM_LIMIT)


def _dot(a, b):
    return jnp.dot(a.astype(BF16), b.astype(BF16), preferred_element_type=F32)


def _dot_nt(a, b):
    return lax.dot_general(a.astype(BF16), b.astype(BF16), (((1,), (1,)), ((), ())),
                           preferred_element_type=F32)


def _split3(x):
    hi = x.astype(BF16)
    r = x - hi.astype(F32)
    mid = r.astype(BF16)
    return hi, mid, (r - mid.astype(F32)).astype(BF16)


def _rms(x, g):
    return x * lax.rsqrt(jnp.mean(x * x, axis=-1, keepdims=True) + RMS_EPS) * g


def _sigmoid(x):
    return 1.0 / (1.0 + jnp.exp(-x))


def _softplus(x):
    return jnp.maximum(x, 0.0) + jnp.log1p(jnp.exp(-jnp.abs(x)))


CONV_HALO = 16
CONV_BLOCK = 128


def _gdn_proj_kernel(x_ref, xp_ref, xn_ref, g_ref, wqkv_ref, wz_ref, w_ref, wt_ref, cw_ref,
                     prow_ref, pcol_ref, qkv_ref, z_ref, col_ref, row_ref, *, nh, nt, chunk):
    C = GDN_CHUNK
    tm = x_ref.shape[0]
    t = pl.program_id(0) % nt
    pad = GDN_CONV // 2
    g = g_ref[...]
    h = _rms(x_ref[...], g).astype(BF16)
    h_halo = jnp.concatenate([_rms(jnp.where(t > 0, xp_ref[...], 0.0), g).astype(BF16),
                              _rms(jnp.where(t < nt - 1, xn_ref[...], 0.0), g).astype(BF16)], axis=0)
    CB = CONV_BLOCK
    taps = [i for i in range(GDN_CONV) if i != pad]
    sr = lax.broadcasted_iota(jnp.int32, (len(taps) * CB, CB + 2 * CONV_HALO), 0)
    sc = lax.broadcasted_iota(jnp.int32, (len(taps) * CB, CB + 2 * CONV_HALO), 1)
    src = (sr % CB) + (CONV_HALO - pad)
    for k, i in enumerate(taps):
        src = src + jnp.where(sr // CB == k, i, 0)
    shift_mat = (sc == src).astype(BF16)

    def project(c0):
        w = wqkv_ref[:, c0:c0 + chunk]
        main = jnp.dot(h, w, preferred_element_type=F32)
        halo = jnp.dot(h_halo, w, preferred_element_type=F32)
        ext = jnp.concatenate([halo[:CONV_HALO], main, halo[CONV_HALO:]], axis=0).astype(BF16)
        return main, ext

    starts = list(range(0, wqkv_ref.shape[1], chunk))
    nxt = project(starts[0])
    for n, c0 in enumerate(starts):
        cs = slice(c0, c0 + chunk)
        main, ext = nxt
        if n + 1 < len(starts):
            nxt = project(starts[n + 1])
        for b in range(tm // CB):
            rows = slice(b * CB, (b + 1) * CB)
            shifted = jnp.dot(shift_mat, ext[b * CB:(b + 1) * CB + 2 * CONV_HALO],
                              preferred_element_type=F32)
            acc = cw_ref[pad:pad + 1, cs] * main[rows]
            for k, i in enumerate(taps):
                acc = acc + cw_ref[i:i + 1, cs] * shifted[k * CB:(k + 1) * CB]
            y = acc * _sigmoid(acc)
            for j in range(chunk // LANES):
                yj = y[:, j * LANES:(j + 1) * LANES]
                head = c0 // LANES + j
                if head < 2 * nh:
                    inv = lax.rsqrt(jnp.sum(yj * yj, axis=-1, keepdims=True) + L2_EPS)
                    yj = yj * (inv * (GDN_DK ** -0.5) if head < nh else inv)
                qkv_ref[rows, c0 + j * LANES:c0 + (j + 1) * LANES] = yj.astype(BF16)
    for c0 in range(0, wz_ref.shape[1], chunk):
        cs = slice(c0, c0 + chunk)
        z_ref[:, cs] = jnp.dot(h, wz_ref[:, cs], preferred_element_type=F32).astype(BF16)
    a_c = jnp.dot(h, w_ref[...], preferred_element_type=F32)
    a_r = lax.dot_general(wt_ref[...], h, (((1,), (1,)), ((), ())),
                          preferred_element_type=F32)
    ri = lax.broadcasted_iota(jnp.int32, (C, C), 0)
    ci = lax.broadcasted_iota(jnp.int32, (C, C), 1)
    lower = (ri >= ci).astype(BF16)
    upper = (ri <= ci).astype(BF16)
    g_c = -jnp.exp(prow_ref[0:1, :]) * _softplus(a_c + prow_ref[1:2, :])
    g_r = -jnp.exp(pcol_ref[:, 0:1]) * _softplus(a_r + pcol_ref[:, 1:2])
    beta_c = _sigmoid(a_c)
    beta_r = _sigmoid(a_r)
    for c in range(tm // C):
        sl = slice(c * C, (c + 1) * C)
        gc = g_c[sl, :]
        pre = sum(jnp.dot(lower, part, preferred_element_type=F32) for part in _split3(gc))
        cum = jnp.where(ci < nh, pre, pre[C - 1:C, :] - pre + gc)
        col_ref[sl, :] = jnp.where(ci < 2 * nh, cum, beta_c[sl, :])
        gr = g_r[:, sl]
        pre_r = sum(jnp.dot(part, upper, preferred_element_type=F32) for part in _split3(gr))
        cum_r = jnp.where(ri < nh, pre_r, pre_r[:, C - 1:C] - pre_r + gr)
        full = jnp.where(ri < 2 * nh, cum_r, beta_r[:, sl])
        row_ref[0, :, sl] = full[:4 * nh, :]


def _gdn_proj(x2, g, w_in, conv_w, a_log, dt_bias, B, S, tm=1024, chunk=256):
    T, D = x2.shape
    nh = a_log.shape[1]
    qkv_w = 3 * nh * LANES
    z_w = nh * LANES
    w_pad = jnp.zeros((D, LANES), F32).at[:, :4 * nh].set(w_in[:, qkv_w + z_w:])
    alog = jnp.zeros((LANES,), F32).at[:2 * nh].set(a_log.reshape(-1))
    dtb = jnp.zeros((LANES,), F32).at[:2 * nh].set(dt_bias.reshape(-1))
    prow = jnp.zeros((SUBLANES, LANES), F32).at[0].set(alog).at[1].set(dtb)
    pcol = jnp.zeros((LANES, LANES), F32).at[:, 0].set(alog).at[:, 1].set(dtb)
    nt = S // tm
    r = tm // CONV_HALO
    const = lambda shape: pl.BlockSpec(shape, lambda i: (0, 0))
    return pl.pallas_call(
        functools.partial(_gdn_proj_kernel, nh=nh, nt=nt, chunk=chunk),
        grid=(T // tm,),
        in_specs=[pl.BlockSpec((tm, D), lambda i: (i, 0)),
                  pl.BlockSpec((CONV_HALO, D), lambda i: (jnp.maximum(i * r - 1, 0), 0)),
                  pl.BlockSpec((CONV_HALO, D), lambda i: (jnp.minimum((i + 1) * r, T // CONV_HALO - 1), 0)),
                  const((1, D)), const((D, qkv_w)), const((D, z_w)), const((D, LANES)), const((LANES, D)),
                  const((GDN_CONV, qkv_w)), const((SUBLANES, LANES)), const((LANES, LANES))],
        out_specs=[pl.BlockSpec((tm, qkv_w), lambda i: (i, 0)),
                   pl.BlockSpec((tm, z_w), lambda i: (i, 0)),
                   pl.BlockSpec((tm, LANES), lambda i: (i, 0)),
                   pl.BlockSpec((1, 4 * nh, tm), lambda i: (i // nt, 0, i % nt))],
        out_shape=[jax.ShapeDtypeStruct((T, qkv_w), BF16),
                   jax.ShapeDtypeStruct((T, z_w), BF16),
                   jax.ShapeDtypeStruct((T, LANES), F32),
                   jax.ShapeDtypeStruct((B, 4 * nh, S), F32)],
        compiler_params=_params("parallel"),
        name="gdn_proj",
    )(x2, x2, x2, g.reshape(1, D), w_in[:, :qkv_w].astype(BF16), w_in[:, qkv_w:qkv_w + z_w].astype(BF16),
      w_pad.astype(BF16), w_pad.T.astype(BF16), conv_w.reshape(GDN_CONV, qkv_w), prow, pcol)


def _gdn_scan_kernel(q_ref, k_ref, v_ref, col_ref, row_ref, o_ref, state_ref, *, rev, nh):
    C = GDN_CHUNK
    t = pl.program_id(1)

    @pl.when(t == 0)
    def _():
        state_ref[...] = jnp.zeros_like(state_ref)

    ri = lax.broadcasted_iota(jnp.int32, (C, C), 0)
    ci = lax.broadcasted_iota(jnp.int32, (C, C), 1)
    incl = (ri <= ci) if rev else (ri >= ci)
    strict = (ri < ci) if rev else (ri > ci)
    eye = (ri == ci).astype(F32)
    levels = [((ri >> (lg + 1)) == (ci >> (lg + 1))) & ((ri >> lg) != (ci >> lg))
              for lg in range(int(math.log2(C)))]
    last = 0 if rev else C - 1
    nc = q_ref.shape[1] // C
    chunks = list(reversed(range(nc))) if rev else list(range(nc))
    heads = range(nh)
    hs = [slice(h * LANES, (h + 1) * LANES) for h in heads]
    gi = [(nh if rev else 0) + h for h in heads]
    units = [(c, h) for c in chunks for h in heads]
    rows = {c: slice(c * C, (c + 1) * C) for c in chunks}
    col = {c: col_ref[0, rows[c], :] for c in chunks}
    row = {c: row_ref[0, :, rows[c]] for c in chunks}
    kf = {u: k_ref[0, rows[u[0]], hs[u[1]]] for u in units}
    k = {u: kf[u].astype(F32) for u in units}
    g_col = {(c, h): col[c][:, gi[h]:gi[h] + 1] for c, h in units}
    b_col = {(c, h): col[c][:, 2 * nh + gi[h]:2 * nh + gi[h] + 1] for c, h in units}
    g_last = {u: g_col[u][last:last + 1, :] for u in units}
    decay = {(c, h): jnp.where(incl, jnp.exp(jnp.where(incl, g_col[c, h] - row[c][gi[h]:gi[h] + 1, :], 0.0)), 0.0)
             for c, h in units}
    kb = {u: k[u] * b_col[u] for u in units}
    res = {(c, h): _dot_nt(jnp.concatenate([kb[c, h].astype(BF16), q_ref[0, rows[c], hs[h]]], axis=0), kf[c, h])
           for c, h in units}
    a = {u: jnp.where(strict, res[u][:C] * decay[u], 0.0) for u in units}
    intra = {u: (res[u][C:] * decay[u]).astype(BF16) for u in units}
    tinv = {u: eye - jnp.where(levels[0], a[u], 0.0) for u in units}
    for lg, m in enumerate(levels[1:], start=1):
        size = 1 << lg
        moving = [r0 for r0 in range(0, C, size) if ((r0 >> lg) & 1) == (0 if rev else 1)]
        tb = {u: tinv[u].astype(BF16) for u in units}
        if size < SUBLANES:
            pm = {u: _dot(tb[u], jnp.where(m, a[u], 0.0)) for u in units}
            tinv = {u: tinv[u] - _dot(pm[u], tb[u]) for u in units}
            continue
        lhs = {u: jnp.concatenate([tinv[u][r0:r0 + size] for r0 in moving], axis=0) for u in units}
        pm = {u: _dot(lhs[u], jnp.where(m, a[u], 0.0)) for u in units}
        new = {u: lhs[u] - _dot(pm[u], tb[u]) for u in units}
        tinv = {u: jnp.concatenate(
            [new[u][moving.index(r0) * size:(moving.index(r0) + 1) * size] if r0 in moving
             else tinv[u][r0:r0 + size] for r0 in range(0, C, size)], axis=0) for u in units}
    e_col = {u: jnp.exp(g_col[u]) for u in units}
    sol = {(c, h): _dot(tinv[c, h], jnp.concatenate(
        [v_ref[0, rows[c], hs[h]].astype(F32) * b_col[c, h], kb[c, h] * e_col[c, h]], axis=1)) for c, h in units}
    qd = {(c, h): (q_ref[0, rows[c], hs[h]].astype(F32) * e_col[c, h]).astype(BF16) for c, h in units}
    post = {u: jnp.concatenate([intra[u], (k[u] * jnp.exp(g_last[u] - g_col[u])).T.astype(BF16)], axis=0)
            for u in units}
    state = [state_ref[h] for h in heads]
    for c in chunks:
        sb = [state[h].astype(BF16) for h in heads]
        r2 = [_dot(jnp.concatenate([sol[c, h][:, LANES:].astype(BF16), qd[c, h]], axis=0), sb[h]) for h in heads]
        v_new = [(sol[c, h][:, :LANES] - r2[h][:C]).astype(BF16) for h in heads]
        r3 = [_dot(post[c, h], v_new[h]) for h in heads]
        for h in heads:
            o_ref[0, rows[c], hs[h]] = (r2[h][C:] + r3[h][:C]).astype(o_ref.dtype)
        state = [state[h] * jnp.exp(g_last[c, h]) + r3[h][C:] for h in heads]
    for h in heads:
        state_ref[h] = state[h]


def _gdn_scan(qkvn, col, row, nh, rev, nc=4):
    B, S, _ = qkvn.shape
    ts = nc * GDN_CHUNK
    nt = S // ts
    width = nh * LANES
    tmap = (lambda t: nt - 1 - t) if rev else (lambda t: t)
    col3 = col.reshape(B, S, LANES)
    return pl.pallas_call(
        functools.partial(_gdn_scan_kernel, rev=rev, nh=nh),
        grid=(B, nt),
        in_specs=[pl.BlockSpec((1, ts, width), lambda b, t: (b, tmap(t), 0)),
                  pl.BlockSpec((1, ts, width), lambda b, t: (b, tmap(t), 1)),
                  pl.BlockSpec((1, ts, width), lambda b, t: (b, tmap(t), 2)),
                  pl.BlockSpec((1, ts, LANES), lambda b, t: (b, tmap(t), 0)),
                  pl.BlockSpec((1, 4 * nh, ts), lambda b, t: (b, 0, tmap(t)))],
        out_specs=pl.BlockSpec((1, ts, width), lambda b, t: (b, tmap(t), 0)),
        out_shape=jax.ShapeDtypeStruct((B, S, width), BF16),
        scratch_shapes=[pltpu.VMEM((nh, GDN_DK, LANES), F32)],
        compiler_params=_params("parallel", "arbitrary"),
        name="gdn_scan_bwd" if rev else "gdn_scan_fwd",
    )(qkvn, qkvn, qkvn, col3, row)


MLP_CHUNK = 512


def _anchor(v):
    folded = jnp.sum(v.reshape(v.shape[0] // SUBLANES, SUBLANES, LANES), axis=0)
    bits = lax.shift_right_logical(pltpu.bitcast(folded, jnp.uint32), jnp.uint32(32))
    return pltpu.bitcast(bits, F32)


def _out_mlp_tail(y_ref, x_ref, wo_ref, g_ref, w1_ref, w2_ref, gf_ref, o_ref, x1_ref, acc_ref, final,
                  anchors=()):
    x1_ref[...] = x_ref[...] + jnp.dot(y_ref[...], wo_ref[...], preferred_element_type=F32)
    h = _rms(x1_ref[...], g_ref[...]).astype(BF16)
    starts = list(range(0, w1_ref.shape[1], MLP_CHUNK))
    for n, c in enumerate(starts):
        a = jnp.maximum(jnp.dot(h, w1_ref[:, c:c + MLP_CHUNK], preferred_element_type=F32), 0.0)
        part = jnp.dot((a * a).astype(BF16), w2_ref[c:c + MLP_CHUNK, :], preferred_element_type=F32)
        if c == 0:
            acc_ref[...] = part
        else:
            acc_ref[...] += part
        for k in range(n, len(anchors), len(starts)):
            acc_ref[0:SUBLANES, 0:LANES] += anchors[k]
    out = x1_ref[...] + acc_ref[...]
    if final:
        out = _rms(out, gf_ref[...])
    o_ref[...] = out


def _mixer_out_mlp(prologue, y_ref, tail_refs, final, lookahead):
    if not lookahead:
        prologue(y_ref.at[0])
        _out_mlp_tail(y_ref.at[0], *tail_refs, final)
        return
    i = pl.program_id(0)

    @pl.when(i == 0)
    def _():
        y_ref[...] = jnp.zeros_like(y_ref)

    for parity in range(2):
        @pl.when(i % 2 == parity)
        def _():
            pieces = prologue(y_ref.at[parity])
            _out_mlp_tail(y_ref.at[1 - parity], *tail_refs, final, anchors=[_anchor(v) for v in pieces])


def _out_mlp_call(kernel_fn, mixer_args, mixer_specs, mixer_scratch, x2, w_out, g, w1, w2,
                  g_final, tm, name, lookahead):
    T, D = x2.shape
    n = T // tm
    if lookahead:
        cur = lambda i: jnp.minimum(i, n - 1)
        prev = lambda i: jnp.maximum(i - 1, 0)
    else:
        cur = prev = lambda i: i
    once = lambda shape: pl.BlockSpec(shape, lambda i: (0, 0), pipeline_mode=pl.Buffered(1))
    return pl.pallas_call(
        kernel_fn,
        grid=(n + 1 if lookahead else n,),
        in_specs=mixer_specs(cur) + [pl.BlockSpec((tm, D), lambda i: (prev(i), 0)), once(w_out.shape),
                                     once((1, D)), once(w1.shape), once(w2.shape), once((1, D))],
        out_specs=pl.BlockSpec((tm, D), lambda i: (prev(i), 0)),
        out_shape=jax.ShapeDtypeStruct((T, D), F32),
        scratch_shapes=mixer_scratch + [pltpu.VMEM((2 if lookahead else 1, tm, w_out.shape[0]), BF16),
                                        pltpu.VMEM((tm, D), F32), pltpu.VMEM((tm, D), F32)],
        compiler_params=_params("arbitrary" if lookahead else "parallel"),
        name=name,
    )(*mixer_args, x2, w_out, g.reshape(1, D), w1, w2, g_final.reshape(1, D))


def _gdn_out_mlp_kernel(of_ref, ob_ref, z_ref, nw_ref, x_ref, wo_ref, g_ref, w1_ref, w2_ref, gf_ref,
                        o_ref, y_ref, x1_ref, acc_ref, *, nh, final):
    def prologue(y):
        pieces = []
        for h in range(nh):
            hs = slice(h * LANES, (h + 1) * LANES)
            o = of_ref[:, hs].astype(F32) + ob_ref[:, hs].astype(F32)
            o = o * lax.rsqrt(jnp.mean(o * o, axis=-1, keepdims=True) + RMS_EPS)
            z = z_ref[:, hs].astype(F32)
            pieces.append(o * nw_ref[...] * (z * _sigmoid(z)))
            y[:, hs] = pieces[-1].astype(BF16)
        return pieces
    _mixer_out_mlp(prologue, y_ref, (x_ref, wo_ref, g_ref, w1_ref, w2_ref, gf_ref, o_ref, x1_ref, acc_ref),
                   final, lookahead=True)


def _gdn_out_mlp(o_f, o_b, z, x2, norm_w, w_out, nh, g, w1, w2, g_final, final, tm=512):
    W = o_f.shape[1]
    specs = lambda cur: [pl.BlockSpec((tm, W), lambda i: (cur(i), 0))] * 3 + \
                        [pl.BlockSpec((1, LANES), lambda i: (0, 0))]
    return _out_mlp_call(
        functools.partial(_gdn_out_mlp_kernel, nh=nh, final=final),
        (o_f, o_b, z, norm_w.reshape(1, LANES)), specs,
        [], x2, w_out, g, w1, w2, g_final, tm, "gdn_out_mlp", lookahead=True)


def _dswa_kernel(q_ref, k_ref, kp_ref, kn_ref, v_ref, vp_ref, vn_ref, bias_ref,
                 o_ref, lse_ref, *, sub_len):
    t = pl.program_id(2)
    rb, lt = q_ref.shape[1], q_ref.shape[2]
    hb, qn = DSWA_HALF, DSWA_QBLOCK
    kw = qn + 2 * hb
    first_head = lax.broadcasted_iota(jnp.int32, (1, LANES), 1) < DSWA_HEAD_DIM
    n_pairs = q_ref.shape[3] // LANES
    blocks = range(lt // qn)
    units = [(rc, qb, pr) for rc in range(rb) for qb in blocks for pr in range(n_pairs)]
    rows = [slice(qb * qn, (qb + 1) * qn) for qb in blocks]
    win = [slice(qb * qn, qb * qn + kw) for qb in blocks]
    ls = [slice(pr * LANES, (pr + 1) * LANES) for pr in range(n_pairs)]
    kext = [jnp.concatenate([kp_ref[0, rc], k_ref[0, rc], kn_ref[0, rc]], axis=0) for rc in range(rb)]
    vext = [jnp.concatenate([vp_ref[0, rc], v_ref[0, rc], vn_ref[0, rc]], axis=0) for rc in range(rb)]
    kpos = lambda qb: t * lt + qb * qn - hb + lax.broadcasted_iota(jnp.int32, (1, kw), 1)
    qs = {(rc, qb, pr): jnp.concatenate([jnp.where(first_head, q_ref[0, rc, rows[qb], ls[pr]], 0.0),
                                         jnp.where(first_head, 0.0, q_ref[0, rc, rows[qb], ls[pr]])], axis=0)
          for rc, qb, pr in units}
    s = {(rc, qb, pr): _dot_nt(qs[rc, qb, pr], kext[rc][win[qb], ls[pr]]) + bias_ref[pr] for rc, qb, pr in units}
    for rc, qb, pr in units:
        if qb == blocks[0]:
            s[rc, qb, pr] = jnp.where(kpos(qb) >= 0, s[rc, qb, pr], NEG_INF)
        if qb == blocks[-1]:
            s[rc, qb, pr] = jnp.where(kpos(qb) < sub_len, s[rc, qb, pr], NEG_INF)
    m = {u: jnp.max(s[u], axis=-1, keepdims=True) for u in units}
    p = {u: jnp.exp2(s[u] - m[u]) for u in units}
    l = {u: jnp.sum(p[u], axis=-1, keepdims=True) for u in units}
    pv = {(rc, qb, pr): _dot(p[rc, qb, pr], vext[rc][win[qb], ls[pr]]) * (1.0 / l[rc, qb, pr])
          for rc, qb, pr in units}
    lse = {u: (m[u] + jnp.log2(l[u])) * math.log(2.0) for u in units}
    for rc, qb, pr in units:
        o_ref[0, rc, rows[qb], ls[pr]] = jnp.where(first_head, pv[rc, qb, pr][:qn], pv[rc, qb, pr][qn:])
        lse_ref[0, rc, rows[qb], ls[pr]] = jnp.where(first_head, lse[rc, qb, pr][:qn], lse[rc, qb, pr][qn:])


def _rel_bucket(rel):
    nb = REL_BUCKETS // 2
    max_exact = nb // 2
    ret = jnp.where(rel > 0, nb, 0)
    n = jnp.abs(rel)
    nf = jnp.maximum(n, 1).astype(F32)
    large = max_exact + (jnp.log(nf * (1.0 / max_exact)) / math.log(REL_MAX_DIST / max_exact)
                         * (nb - max_exact)).astype(jnp.int32)
    large = jnp.minimum(large, nb - 1)
    return ret + jnp.where(n < max_exact, n, large)


def _band_bias_kernel(tab_ref, o_ref, *, dils, hg):
    hb, qn = DSWA_HALF, DSWA_QBLOCK
    kw = qn + 2 * hb
    off = (lax.broadcasted_iota(jnp.int32, (qn, kw), 1) - hb
           - lax.broadcasted_iota(jnp.int32, (qn, kw), 0))
    inband = jnp.abs(off) <= hb
    for g, d in enumerate(dils):
        bucket = _rel_bucket(off * d)
        for hh in range(hg):
            acc = jnp.zeros((qn, kw), F32)
            for b in range(REL_BUCKETS):
                acc = jnp.where(bucket == b, tab_ref[b, g * hg + hh], acc)
            o_ref[g, hh // 2, (hh % 2) * qn:(hh % 2 + 1) * qn, :] = jnp.where(inband, acc * LOG2E, NEG_INF)


def _band_bias(rel_table, dils):
    qn = DSWA_QBLOCK
    hg = rel_table.shape[1] // len(dils)
    return pl.pallas_call(
        functools.partial(_band_bias_kernel, dils=dils, hg=hg),
        in_specs=[pl.BlockSpec(memory_space=pltpu.SMEM)],
        out_shape=jax.ShapeDtypeStruct((len(dils), hg // 2, 2 * qn, qn + 2 * DSWA_HALF), F32),
        name="band_bias",
    )(rel_table)


def _dswa_group(qkv, bias, lt=512):
    B, dil, L, W3 = qkv.shape
    hb = DSWA_HALF
    gw = W3 // 3
    rb = max(1, lt // L)
    lt = min(lt, L)
    r = lt // hb
    main = lambda which: pl.BlockSpec((1, rb, lt, gw), lambda b, d, t: (b, d, t, which))
    prev = lambda which: pl.BlockSpec(
        (1, rb, hb, gw), lambda b, d, t: (b, d, jnp.maximum(t * r - 1, 0), which))
    nxt = lambda which: pl.BlockSpec(
        (1, rb, hb, gw), lambda b, d, t: (b, d, jnp.minimum((t + 1) * r, L // hb - 1), which))
    out = pl.BlockSpec((1, rb, lt, gw), lambda b, d, t: (b, d, t, 0))
    return pl.pallas_call(
        functools.partial(_dswa_kernel, sub_len=L),
        grid=(B, dil // rb, L // lt),
        in_specs=[main(0), main(1), prev(1), nxt(1), main(2), prev(2), nxt(2),
                  pl.BlockSpec(bias.shape, lambda b, d, t: (0, 0, 0))],
        out_specs=[out, out],
        out_shape=[jax.ShapeDtypeStruct((B, dil, L, gw), F32)] * 2,
        compiler_params=_params("parallel", "parallel", "parallel"),
        name=f"dswa_d{dil}",
    )(qkv, qkv, qkv, qkv, qkv, qkv, qkv, bias)


def _dswa_out_mlp_kernel(*refs, dils, final):
    ng = len(dils)
    o_refs, l_refs = refs[:ng], refs[ng:2 * ng]
    x_ref, wo_ref, g_ref, w1_ref, w2_ref, gf_ref, out_ref, o_s, l_s, y_s, x1_ref, acc_ref = refs[2 * ng:]
    tm = x_ref.shape[0]
    tiles = o_refs[0].shape[3] // LANES
    def prologue(y):
        for g, d in enumerate(dils):
            for r in range(d):
                for j in range(tiles):
                    ls = slice(j * LANES, (j + 1) * LANES)
                    o_s[g * tiles + j, pl.ds(r, tm // d, stride=d), :] = o_refs[g][0, r, :, ls]
                    l_s[g * tiles + j, pl.ds(r, tm // d, stride=d), :] = l_refs[g][0, r, :, ls]
        pieces = []
        for j in range(tiles):
            lses = [l_s[g * tiles + j] for g in range(ng)]
            m = functools.reduce(jnp.maximum, lses)
            es = [jnp.exp(l - m) for l in lses]
            inv = 1.0 / functools.reduce(lambda a, b: a + b, es)
            for g in range(ng):
                c = g * tiles + j
                pieces.append(o_s[c] * (es[g] * inv))
                y[:, c * LANES:(c + 1) * LANES] = pieces[-1].astype(BF16)
        return pieces
    _mixer_out_mlp(prologue, y_s, (x_ref, wo_ref, g_ref, w1_ref, w2_ref, gf_ref, out_ref, x1_ref, acc_ref),
                   final, lookahead=False)


def _dswa_out_mlp(os_, lses, x2, w_out, S, g, w1, w2, g_final, final, tm=512):
    dils = tuple(o.shape[1] for o in os_)
    gw = os_[0].shape[3]
    nt = S // tm
    specs = lambda cur: [pl.BlockSpec((1, d, tm // d, gw), lambda i: (cur(i) // nt, 0, cur(i) % nt, 0))
                         for d in dils] * 2
    staging = pltpu.VMEM((len(dils) * gw // LANES, tm, LANES), F32)
    return _out_mlp_call(
        functools.partial(_dswa_out_mlp_kernel, dils=dils, final=final),
        (*os_, *lses), specs, [staging, staging],
        x2, w_out, g, w1, w2, g_final, tm, "dswa_out_mlp", lookahead=False)


def _gdn_layer(x2, g, w_in, conv_w, a_log, dt_bias, norm_w, w_out, B, S, mlp):
    nh = a_log.shape[1]
    z_w = nh * LANES
    qkvn, z, col, row = _gdn_proj(x2, g, w_in, conv_w, a_log, dt_bias, B, S)
    qkvn = qkvn.reshape(B, S, 3 * z_w)
    o_f = _gdn_scan(qkvn, col, row, nh, rev=False)
    o_b = _gdn_scan(qkvn, col, row, nh, rev=True)
    return _gdn_out_mlp(o_f.reshape(B * S, z_w), o_b.reshape(B * S, z_w), z, x2, norm_w,
                        w_out.astype(BF16), nh, *mlp)


def _dswa_proj_kernel(x_ref, g_ref, w_ref, *refs, dils, chunk):
    o_refs, acc_ref = refs[:-1], refs[-1]
    tm = x_ref.shape[0]
    n = w_ref.shape[1]
    ng = len(dils)
    gw = n // (3 * ng)
    tiles = gw // LANES
    h = _rms(x_ref[...], g_ref[...]).astype(BF16)
    def project(which):
        res = jnp.dot(h, w_ref[:, which * chunk:(which + 1) * chunk], preferred_element_type=F32)
        if which == 0:
            res = res * (DSWA_HEAD_DIM ** -0.5 * LOG2E)
        for j in range(chunk // LANES):
            acc_ref[which * ng * tiles + j] = res[:, j * LANES:(j + 1) * LANES]

    def regroup(which):
        for g, d in enumerate(dils):
            for r in range(d):
                for j in range(tiles):
                    dst = slice((which * tiles + j) * LANES, (which * tiles + j + 1) * LANES)
                    o_refs[g][0, r, :, dst] = \
                        acc_ref[(which * ng + g) * tiles + j, pl.ds(r, tm // d, stride=d), :].astype(BF16)

    project(0)
    for which in range(1, 3):
        project(which)
        regroup(which - 1)
    regroup(2)


def _dswa_proj(x2, g, w_in, dils, B, S, tm=512):
    T, D = x2.shape
    n = w_in.shape[1]
    gw3 = n // len(dils)
    nt = S // tm
    return pl.pallas_call(
        functools.partial(_dswa_proj_kernel, dils=dils, chunk=gw3),
        grid=(T // tm,),
        in_specs=[pl.BlockSpec((tm, D), lambda i: (i, 0)),
                  pl.BlockSpec((1, D), lambda i: (0, 0)),
                  pl.BlockSpec((D, n), lambda i: (0, 0))],
        out_specs=[pl.BlockSpec((1, d, tm // d, gw3), lambda i: (i // nt, 0, i % nt, 0)) for d in dils],
        out_shape=[jax.ShapeDtypeStruct((B, d, S // d, gw3), BF16) for d in dils],
        scratch_shapes=[pltpu.VMEM((n // LANES, tm, LANES), F32)],
        compiler_params=_params("parallel"),
        name="dswa_proj",
    )(x2, g.reshape(1, D), w_in)


def _dswa_layer(x2, g, w_in, w_out, bias, B, S, mlp):
    dils = tuple(d for _, d in DSWA_CONFIGS)
    qkvs = _dswa_proj(x2, g, w_in.astype(BF16), dils, B, S)
    os_, lses = [], []
    for gi in range(len(dils)):
        o, lse = _dswa_group(qkvs[gi], bias[gi])
        os_.append(o)
        lses.append(lse)
    return _dswa_out_mlp(os_, lses, x2, w_out.astype(BF16), S, *mlp)


def kernel(x, norm_mix, norm_mlp, norm_final, rel_bias, gdn_w_in, gdn_conv_w, gdn_a_log,
           gdn_dt_bias, gdn_norm_w, gdn_w_out, dswa_w_in, dswa_w_out, mlp_w1, mlp_w2):
    B, S, D = x.shape
    depth = norm_mix.shape[0]
    x2 = x.reshape(B * S, D)
    assert all(window // (2 * dil) == DSWA_HALF for window, dil in DSWA_CONFIGS)
    bias = _band_bias(rel_bias, tuple(d for _, d in DSWA_CONFIGS))
    for i in range(depth):
        j = i // 2
        mlp = (norm_mlp[i], mlp_w1[i].astype(BF16), mlp_w2[i].astype(BF16), norm_final, i == depth - 1)
        if i % 2 == 0:
            x2 = _gdn_layer(x2, norm_mix[i], gdn_w_in[j], gdn_conv_w[j], gdn_a_log[j],
                            gdn_dt_bias[j], gdn_norm_w[j], gdn_w_out[j], B, S, mlp)
        else:
            x2 = _dswa_layer(x2, norm_mix[i], dswa_w_in[j], dswa_w_out[j], bias, B, S, mlp)
    return x2.reshape(B, S, D)
```

```python
import functools
import math

import jax
import jax.numpy as jnp
from jax import lax
from jax.experimental import pallas as pl
from jax.experimental.pallas import tpu as pltpu

F32 = jnp.float32
BF16 = jnp.bfloat16

RMS_EPS = 1e-6
L2_EPS = 1e-6
NEG_INF = -1e30
LOG2E = math.log2(math.e)

LANES = 128
SUBLANES = 8
VMEM_LIMIT = 56 * 1024 * 1024

GDN_DK = 128
GDN_CONV = 5
GDN_CHUNK = LANES
DSWA_HEAD_DIM = 64
DSWA_CONFIGS = ((128, 1), (512, 4), (2048, 16))
DSWA_HALF = 64
DSWA_QBLOCK = 128
REL_BUCKETS = 32
REL_MAX_DIST = 1024


def _params(*sem):
    return pltpu.CompilerParams(dimension_semantics=sem, vmem_limit_bytes=VMEM_LIMIT)


def _dot(a, b):
    return jnp.dot(a.astype(BF16), b.astype(BF16), preferred_element_type=F32)


def _dot_nt(a, b):
    return lax.dot_general(a.astype(BF16), b.astype(BF16), (((1,), (1,)), ((), ())),
                           preferred_element_type=F32)


def _split3(x):
    hi = x.astype(BF16)
    r = x - hi.astype(F32)
    mid = r.astype(BF16)
    return hi, mid, (r - mid.astype(F32)).astype(BF16)


def _rms(x, g):
    return x * lax.rsqrt(jnp.mean(x * x, axis=-1, keepdims=True) + RMS_EPS) * g


def _sigmoid(x):
    return 1.0 / (1.0 + jnp.exp(-x))


def _softplus(x):
    return jnp.maximum(x, 0.0) + jnp.log1p(jnp.exp(-jnp.abs(x)))


CONV_HALO = 16
CONV_BLOCK = 64


def _gdn_proj_kernel(x_ref, xp_ref, xn_ref, g_ref, wqkv_ref, wz_ref, w_ref, wt_ref, cw_ref,
                     prow_ref, pcol_ref, qkv_ref, z_ref, col_ref, row_ref, *, nh, nt, chunk):
    C = GDN_CHUNK
    tm = x_ref.shape[0]
    t = pl.program_id(0) % nt
    pad = GDN_CONV // 2
    g = g_ref[...]
    h = _rms(x_ref[...], g).astype(BF16)
    h_halo = jnp.concatenate([_rms(jnp.where(t > 0, xp_ref[...], 0.0), g).astype(BF16),
                              _rms(jnp.where(t < nt - 1, xn_ref[...], 0.0), g).astype(BF16)], axis=0)
    CB = CONV_BLOCK
    taps = [i for i in range(GDN_CONV) if i != pad]
    sr = lax.broadcasted_iota(jnp.int32, (len(taps) * CB, CB + 2 * CONV_HALO), 0)
    sc = lax.broadcasted_iota(jnp.int32, (len(taps) * CB, CB + 2 * CONV_HALO), 1)
    src = (sr % CB) + (CONV_HALO - pad)
    for k, i in enumerate(taps):
        src = src + jnp.where(sr // CB == k, i, 0)
    shift_mat = (sc == src).astype(BF16)

    def project(c0):
        w = wqkv_ref[:, c0:c0 + chunk]
        main = jnp.dot(h, w, preferred_element_type=F32)
        halo = jnp.dot(h_halo, w, preferred_element_type=F32)
        ext = jnp.concatenate([halo[:CONV_HALO], main, halo[CONV_HALO:]], axis=0).astype(BF16)
        return main, ext

    starts = list(range(0, wqkv_ref.shape[1], chunk))
    nxt = project(starts[0])
    for n, c0 in enumerate(starts):
        cs = slice(c0, c0 + chunk)
        main, ext = nxt
        if n + 1 < len(starts):
            nxt = project(starts[n + 1])
        for b in range(tm // CB):
            rows = slice(b * CB, (b + 1) * CB)
            shifted = jnp.dot(shift_mat, ext[b * CB:(b + 1) * CB + 2 * CONV_HALO],
                              preferred_element_type=F32)
            acc = cw_ref[pad:pad + 1, cs] * main[rows]
            for k, i in enumerate(taps):
                acc = acc + cw_ref[i:i + 1, cs] * shifted[k * CB:(k + 1) * CB]
            y = acc * _sigmoid(acc)
            for j in range(chunk // LANES):
                yj = y[:, j * LANES:(j + 1) * LANES]
                head = c0 // LANES + j
                if head < 2 * nh:
                    inv = lax.rsqrt(jnp.sum(yj * yj, axis=-1, keepdims=True) + L2_EPS)
                    yj = yj * (inv * (GDN_DK ** -0.5) if head < nh else inv)
                qkv_ref[rows, c0 + j * LANES:c0 + (j + 1) * LANES] = yj.astype(BF16)
    for c0 in range(0, wz_ref.shape[1], chunk):
        cs = slice(c0, c0 + chunk)
        z_ref[:, cs] = jnp.dot(h, wz_ref[:, cs], preferred_element_type=F32).astype(BF16)
    a_c = jnp.dot(h, w_ref[...], preferred_element_type=F32)
    a_r = lax.dot_general(wt_ref[...], h, (((1,), (1,)), ((), ())),
                          preferred_element_type=F32)
    ri = lax.broadcasted_iota(jnp.int32, (C, C), 0)
    ci = lax.broadcasted_iota(jnp.int32, (C, C), 1)
    lower = (ri >= ci).astype(BF16)
    upper = (ri <= ci).astype(BF16)
    g_c = -jnp.exp(prow_ref[0:1, :]) * _softplus(a_c + prow_ref[1:2, :])
    g_r = -jnp.exp(pcol_ref[:, 0:1]) * _softplus(a_r + pcol_ref[:, 1:2])
    beta_c = _sigmoid(a_c)
    beta_r = _sigmoid(a_r)
    for c in range(tm // C):
        sl = slice(c * C, (c + 1) * C)
        gc = g_c[sl, :]
        pre = sum(jnp.dot(lower, part, preferred_element_type=F32) for part in _split3(gc))
        cum = jnp.where(ci < nh, pre, pre[C - 1:C, :] - pre + gc)
        col_ref[sl, :] = jnp.where(ci < 2 * nh, cum, beta_c[sl, :])
        gr = g_r[:, sl]
        pre_r = sum(jnp.dot(part, upper, preferred_element_type=F32) for part in _split3(gr))
        cum_r = jnp.where(ri < nh, pre_r, pre_r[:, C - 1:C] - pre_r + gr)
        full = jnp.where(ri < 2 * nh, cum_r, beta_r[:, sl])
        row_ref[0, :, sl] = full[:4 * nh, :]


def _gdn_proj(x2, g, w_in, conv_w, a_log, dt_bias, B, S, tm=1024, chunk=256):
    T, D = x2.shape
    nh = a_log.shape[1]
    qkv_w = 3 * nh * LANES
    z_w = nh * LANES
    w_pad = jnp.zeros((D, LANES), F32).at[:, :4 * nh].set(w_in[:, qkv_w + z_w:])
    alog = jnp.zeros((LANES,), F32).at[:2 * nh].set(a_log.reshape(-1))
    dtb = jnp.zeros((LANES,), F32).at[:2 * nh].set(dt_bias.reshape(-1))
    prow = jnp.zeros((SUBLANES, LANES), F32).at[0].set(alog).at[1].set(dtb)
    pcol = jnp.zeros((LANES, LANES), F32).at[:, 0].set(alog).at[:, 1].set(dtb)
    nt = S // tm
    r = tm // CONV_HALO
    const = lambda shape: pl.BlockSpec(shape, lambda i: (0, 0))
    return pl.pallas_call(
        functools.partial(_gdn_proj_kernel, nh=nh, nt=nt, chunk=chunk),
        grid=(T // tm,),
        in_specs=[pl.BlockSpec((tm, D), lambda i: (i, 0)),
                  pl.BlockSpec((CONV_HALO, D), lambda i: (jnp.maximum(i * r - 1, 0), 0)),
                  pl.BlockSpec((CONV_HALO, D), lambda i: (jnp.minimum((i + 1) * r, T // CONV_HALO - 1), 0)),
                  const((1, D)), const((D, qkv_w)), const((D, z_w)), const((D, LANES)), const((LANES, D)),
                  const((GDN_CONV, qkv_w)), const((SUBLANES, LANES)), const((LANES, LANES))],
        out_specs=[pl.BlockSpec((tm, qkv_w), lambda i: (i, 0)),
                   pl.BlockSpec((tm, z_w), lambda i: (i, 0)),
                   pl.BlockSpec((tm, LANES), lambda i: (i, 0)),
                   pl.BlockSpec((1, 4 * nh, tm), lambda i: (i // nt, 0, i % nt))],
        out_shape=[jax.ShapeDtypeStruct((T, qkv_w), BF16),
                   jax.ShapeDtypeStruct((T, z_w), BF16),
                   jax.ShapeDtypeStruct((T, LANES), F32),
                   jax.ShapeDtypeStruct((B, 4 * nh, S), F32)],
        compiler_params=_params("parallel"),
        name="gdn_proj",
    )(x2, x2, x2, g.reshape(1, D), w_in[:, :qkv_w].astype(BF16), w_in[:, qkv_w:qkv_w + z_w].astype(BF16),
      w_pad.astype(BF16), w_pad.T.astype(BF16), conv_w.reshape(GDN_CONV, qkv_w), prow, pcol)


def _gdn_scan_kernel(q_ref, k_ref, v_ref, col_ref, row_ref, o_ref, state_ref, *, rev, nh):
    C = GDN_CHUNK
    t = pl.program_id(1)

    @pl.when(t == 0)
    def _():
        state_ref[...] = jnp.zeros_like(state_ref)

    ri = lax.broadcasted_iota(jnp.int32, (C, C), 0)
    ci = lax.broadcasted_iota(jnp.int32, (C, C), 1)
    incl = (ri <= ci) if rev else (ri >= ci)
    strict = (ri < ci) if rev else (ri > ci)
    eye = (ri == ci).astype(F32)
    levels = [((ri >> (lg + 1)) == (ci >> (lg + 1))) & ((ri >> lg) != (ci >> lg))
              for lg in range(int(math.log2(C)))]
    last = 0 if rev else C - 1
    nc = q_ref.shape[1] // C
    chunks = list(reversed(range(nc))) if rev else list(range(nc))
    heads = range(nh)
    hs = [slice(h * LANES, (h + 1) * LANES) for h in heads]
    gi = [(nh if rev else 0) + h for h in heads]
    units = [(c, h) for c in chunks for h in heads]
    rows = {c: slice(c * C, (c + 1) * C) for c in chunks}
    col = {c: col_ref[0, rows[c], :] for c in chunks}
    row = {c: row_ref[0, :, rows[c]] for c in chunks}
    kf = {u: k_ref[0, rows[u[0]], hs[u[1]]] for u in units}
    k = {u: kf[u].astype(F32) for u in units}
    g_col = {(c, h): col[c][:, gi[h]:gi[h] + 1] for c, h in units}
    b_col = {(c, h): col[c][:, 2 * nh + gi[h]:2 * nh + gi[h] + 1] for c, h in units}
    g_last = {u: g_col[u][last:last + 1, :] for u in units}
    decay = {(c, h): jnp.where(incl, jnp.exp(jnp.where(incl, g_col[c, h] - row[c][gi[h]:gi[h] + 1, :], 0.0)), 0.0)
             for c, h in units}
    kb = {u: k[u] * b_col[u] for u in units}
    res = {(c, h): _dot_nt(jnp.concatenate([kb[c, h].astype(BF16), q_ref[0, rows[c], hs[h]]], axis=0), kf[c, h])
           for c, h in units}
    a = {u: jnp.where(strict, res[u][:C] * decay[u], 0.0) for u in units}
    intra = {u: (res[u][C:] * decay[u]).astype(BF16) for u in units}
    tinv = {u: eye - jnp.where(levels[0], a[u], 0.0) for u in units}
    for lg, m in enumerate(levels[1:], start=1):
        size = 1 << lg
        moving = [r0 for r0 in range(0, C, size) if ((r0 >> lg) & 1) == (0 if rev else 1)]
        tb = {u: tinv[u].astype(BF16) for u in units}
        if size < SUBLANES:
            pm = {u: _dot(tb[u], jnp.where(m, a[u], 0.0)) for u in units}
            tinv = {u: tinv[u] - _dot(pm[u], tb[u]) for u in units}
            continue
        lhs = {u: jnp.concatenate([tinv[u][r0:r0 + size] for r0 in moving], axis=0) for u in units}
        pm = {u: _dot(lhs[u], jnp.where(m, a[u], 0.0)) for u in units}
        new = {u: lhs[u] - _dot(pm[u], tb[u]) for u in units}
        tinv = {u: jnp.concatenate(
            [new[u][moving.index(r0) * size:(moving.index(r0) + 1) * size] if r0 in moving
             else tinv[u][r0:r0 + size] for r0 in range(0, C, size)], axis=0) for u in units}
    e_col = {u: jnp.exp(g_col[u]) for u in units}
    sol = {(c, h): _dot(tinv[c, h], jnp.concatenate(
        [v_ref[0, rows[c], hs[h]].astype(F32) * b_col[c, h], kb[c, h] * e_col[c, h]], axis=1)) for c, h in units}
    qd = {(c, h): (q_ref[0, rows[c], hs[h]].astype(F32) * e_col[c, h]).astype(BF16) for c, h in units}
    post = {u: jnp.concatenate([intra[u], (k[u] * jnp.exp(g_last[u] - g_col[u])).T.astype(BF16)], axis=0)
            for u in units}
    state = [state_ref[h] for h in heads]
    for c in chunks:
        sb = [state[h].astype(BF16) for h in heads]
        r2 = [_dot(jnp.concatenate([sol[c, h][:, LANES:].astype(BF16), qd[c, h]], axis=0), sb[h]) for h in heads]
        v_new = [(sol[c, h][:, :LANES] - r2[h][:C]).astype(BF16) for h in heads]
        r3 = [_dot(post[c, h], v_new[h]) for h in heads]
        for h in heads:
            o_ref[0, rows[c], hs[h]] = (r2[h][C:] + r3[h][:C]).astype(o_ref.dtype)
        state = [state[h] * jnp.exp(g_last[c, h]) + r3[h][C:] for h in heads]
    for h in heads:
        state_ref[h] = state[h]


def _gdn_scan(qkvn, col, row, nh, rev, nc=4):
    B, S, _ = qkvn.shape
    ts = nc * GDN_CHUNK
    nt = S // ts
    width = nh * LANES
    tmap = (lambda t: nt - 1 - t) if rev else (lambda t: t)
    col3 = col.reshape(B, S, LANES)
    return pl.pallas_call(
        functools.partial(_gdn_scan_kernel, rev=rev, nh=nh),
        grid=(B, nt),
        in_specs=[pl.BlockSpec((1, ts, width), lambda b, t: (b, tmap(t), 0)),
                  pl.BlockSpec((1, ts, width), lambda b, t: (b, tmap(t), 1)),
                  pl.BlockSpec((1, ts, width), lambda b, t: (b, tmap(t), 2)),
                  pl.BlockSpec((1, ts, LANES), lambda b, t: (b, tmap(t), 0)),
                  pl.BlockSpec((1, 4 * nh, ts), lambda b, t: (b, 0, tmap(t)))],
        out_specs=pl.BlockSpec((1, ts, width), lambda b, t: (b, tmap(t), 0)),
        out_shape=jax.ShapeDtypeStruct((B, S, width), BF16),
        scratch_shapes=[pltpu.VMEM((nh, GDN_DK, LANES), F32)],
        compiler_params=_params("parallel", "arbitrary"),
        name="gdn_scan_bwd" if rev else "gdn_scan_fwd",
    )(qkvn, qkvn, qkvn, col3, row)


MLP_CHUNK = 512


def _anchor(v):
    folded = jnp.sum(v.reshape(v.shape[0] // SUBLANES, SUBLANES, LANES), axis=0)
    bits = lax.shift_right_logical(pltpu.bitcast(folded, jnp.uint32), jnp.uint32(32))
    return pltpu.bitcast(bits, F32)


def _out_mlp_tail(y_ref, x_ref, wo_ref, g_ref, w1_ref, w2_ref, gf_ref, o_ref, x1_ref, acc_ref, final,
                  anchors=()):
    x1_ref[...] = x_ref[...] + jnp.dot(y_ref[...], wo_ref[...], preferred_element_type=F32)
    h = _rms(x1_ref[...], g_ref[...]).astype(BF16)
    starts = list(range(0, w1_ref.shape[1], MLP_CHUNK))
    for n, c in enumerate(starts):
        a = jnp.maximum(jnp.dot(h, w1_ref[:, c:c + MLP_CHUNK], preferred_element_type=F32), 0.0)
        part = jnp.dot((a * a).astype(BF16), w2_ref[c:c + MLP_CHUNK, :], preferred_element_type=F32)
        if c == 0:
            acc_ref[...] = part
        else:
            acc_ref[...] += part
        for k in range(n, len(anchors), len(starts)):
            acc_ref[0:SUBLANES, 0:LANES] += anchors[k]
    out = x1_ref[...] + acc_ref[...]
    if final:
        out = _rms(out, gf_ref[...])
    o_ref[...] = out


def _mixer_out_mlp(prologue, y_ref, tail_refs, final, lookahead):
    if not lookahead:
        prologue(y_ref.at[0])
        _out_mlp_tail(y_ref.at[0], *tail_refs, final)
        return
    i = pl.program_id(0)

    @pl.when(i == 0)
    def _():
        y_ref[...] = jnp.zeros_like(y_ref)

    for parity in range(2):
        @pl.when(i % 2 == parity)
        def _():
            pieces = prologue(y_ref.at[parity])
            _out_mlp_tail(y_ref.at[1 - parity], *tail_refs, final, anchors=[_anchor(v) for v in pieces])


def _out_mlp_call(kernel_fn, mixer_args, mixer_specs, mixer_scratch, x2, w_out, g, w1, w2,
                  g_final, tm, name, lookahead):
    T, D = x2.shape
    n = T // tm
    if lookahead:
        cur = lambda i: jnp.minimum(i, n - 1)
        prev = lambda i: jnp.maximum(i - 1, 0)
    else:
        cur = prev = lambda i: i
    once = lambda shape: pl.BlockSpec(shape, lambda i: (0, 0), pipeline_mode=pl.Buffered(1))
    return pl.pallas_call(
        kernel_fn,
        grid=(n + 1 if lookahead else n,),
        in_specs=mixer_specs(cur) + [pl.BlockSpec((tm, D), lambda i: (prev(i), 0)), once(w_out.shape),
                                     once((1, D)), once(w1.shape), once(w2.shape), once((1, D))],
        out_specs=pl.BlockSpec((tm, D), lambda i: (prev(i), 0)),
        out_shape=jax.ShapeDtypeStruct((T, D), F32),
        scratch_shapes=mixer_scratch + [pltpu.VMEM((2 if lookahead else 1, tm, w_out.shape[0]), BF16),
                                        pltpu.VMEM((tm, D), F32), pltpu.VMEM((tm, D), F32)],
        compiler_params=_params("arbitrary" if lookahead else "parallel"),
        name=name,
    )(*mixer_args, x2, w_out, g.reshape(1, D), w1, w2, g_final.reshape(1, D))


def _gdn_out_mlp_kernel(of_ref, ob_ref, z_ref, nw_ref, x_ref, wo_ref, g_ref, w1_ref, w2_ref, gf_ref,
                        o_ref, y_ref, x1_ref, acc_ref, *, nh, final):
    def prologue(y):
        pieces = []
        for h in range(nh):
            hs = slice(h * LANES, (h + 1) * LANES)
            o = of_ref[:, hs].astype(F32) + ob_ref[:, hs].astype(F32)
            o = o * lax.rsqrt(jnp.mean(o * o, axis=-1, keepdims=True) + RMS_EPS)
            z = z_ref[:, hs].astype(F32)
            pieces.append(o * nw_ref[...] * (z * _sigmoid(z)))
            y[:, hs] = pieces[-1].astype(BF16)
        return pieces
    _mixer_out_mlp(prologue, y_ref, (x_ref, wo_ref, g_ref, w1_ref, w2_ref, gf_ref, o_ref, x1_ref, acc_ref),
                   final, lookahead=True)


def _gdn_out_mlp(o_f, o_b, z, x2, norm_w, w_out, nh, g, w1, w2, g_final, final, tm=512):
    W = o_f.shape[1]
    specs = lambda cur: [pl.BlockSpec((tm, W), lambda i: (cur(i), 0))] * 3 + \
                        [pl.BlockSpec((1, LANES), lambda i: (0, 0))]
    return _out_mlp_call(
        functools.partial(_gdn_out_mlp_kernel, nh=nh, final=final),
        (o_f, o_b, z, norm_w.reshape(1, LANES)), specs,
        [], x2, w_out, g, w1, w2, g_final, tm, "gdn_out_mlp", lookahead=True)


def _dswa_kernel(q_ref, k_ref, kp_ref, kn_ref, v_ref, vp_ref, vn_ref, bias_ref,
                 o_ref, lse_ref, *, sub_len):
    t = pl.program_id(2)
    rb, lt = q_ref.shape[1], q_ref.shape[2]
    hb, qn = DSWA_HALF, DSWA_QBLOCK
    kw = qn + 2 * hb
    first_head = lax.broadcasted_iota(jnp.int32, (1, LANES), 1) < DSWA_HEAD_DIM
    n_pairs = q_ref.shape[3] // LANES
    blocks = range(lt // qn)
    units = [(rc, qb, pr) for rc in range(rb) for qb in blocks for pr in range(n_pairs)]
    rows = [slice(qb * qn, (qb + 1) * qn) for qb in blocks]
    win = [slice(qb * qn, qb * qn + kw) for qb in blocks]
    ls = [slice(pr * LANES, (pr + 1) * LANES) for pr in range(n_pairs)]
    kext = [jnp.concatenate([kp_ref[0, rc], k_ref[0, rc], kn_ref[0, rc]], axis=0) for rc in range(rb)]
    vext = [jnp.concatenate([vp_ref[0, rc], v_ref[0, rc], vn_ref[0, rc]], axis=0) for rc in range(rb)]
    kpos = lambda qb: t * lt + qb * qn - hb + lax.broadcasted_iota(jnp.int32, (1, kw), 1)
    qs = {(rc, qb, pr): jnp.concatenate([jnp.where(first_head, q_ref[0, rc, rows[qb], ls[pr]], 0.0),
                                         jnp.where(first_head, 0.0, q_ref[0, rc, rows[qb], ls[pr]])], axis=0)
          for rc, qb, pr in units}
    s = {(rc, qb, pr): _dot_nt(qs[rc, qb, pr], kext[rc][win[qb], ls[pr]]) + bias_ref[pr] for rc, qb, pr in units}
    for rc, qb, pr in units:
        if qb == blocks[0]:
            s[rc, qb, pr] = jnp.where(kpos(qb) >= 0, s[rc, qb, pr], NEG_INF)
        if qb == blocks[-1]:
            s[rc, qb, pr] = jnp.where(kpos(qb) < sub_len, s[rc, qb, pr], NEG_INF)
    m = {u: jnp.max(s[u], axis=-1, keepdims=True) for u in units}
    p = {u: jnp.exp2(s[u] - m[u]) for u in units}
    l = {u: jnp.sum(p[u], axis=-1, keepdims=True) for u in units}
    pv = {(rc, qb, pr): _dot(p[rc, qb, pr], vext[rc][win[qb], ls[pr]]) * (1.0 / l[rc, qb, pr])
          for rc, qb, pr in units}
    lse = {u: (m[u] + jnp.log2(l[u])) * math.log(2.0) for u in units}
    for rc, qb, pr in units:
        o_ref[0, rc, rows[qb], ls[pr]] = jnp.where(first_head, pv[rc, qb, pr][:qn], pv[rc, qb, pr][qn:])
        lse_ref[0, rc, rows[qb], ls[pr]] = jnp.where(first_head, lse[rc, qb, pr][:qn], lse[rc, qb, pr][qn:])


def _rel_bucket(rel):
    nb = REL_BUCKETS // 2
    max_exact = nb // 2
    ret = jnp.where(rel > 0, nb, 0)
    n = jnp.abs(rel)
    nf = jnp.maximum(n, 1).astype(F32)
    large = max_exact + (jnp.log(nf * (1.0 / max_exact)) / math.log(REL_MAX_DIST / max_exact)
                         * (nb - max_exact)).astype(jnp.int32)
    large = jnp.minimum(large, nb - 1)
    return ret + jnp.where(n < max_exact, n, large)


def _band_bias_kernel(tab_ref, o_ref, *, dils, hg):
    hb, qn = DSWA_HALF, DSWA_QBLOCK
    kw = qn + 2 * hb
    off = (lax.broadcasted_iota(jnp.int32, (qn, kw), 1) - hb
           - lax.broadcasted_iota(jnp.int32, (qn, kw), 0))
    inband = jnp.abs(off) <= hb
    for g, d in enumerate(dils):
        bucket = _rel_bucket(off * d)
        for hh in range(hg):
            acc = jnp.zeros((qn, kw), F32)
            for b in range(REL_BUCKETS):
                acc = jnp.where(bucket == b, tab_ref[b, g * hg + hh], acc)
            o_ref[g, hh // 2, (hh % 2) * qn:(hh % 2 + 1) * qn, :] = jnp.where(inband, acc * LOG2E, NEG_INF)


def _band_bias(rel_table, dils):
    qn = DSWA_QBLOCK
    hg = rel_table.shape[1] // len(dils)
    return pl.pallas_call(
        functools.partial(_band_bias_kernel, dils=dils, hg=hg),
        in_specs=[pl.BlockSpec(memory_space=pltpu.SMEM)],
        out_shape=jax.ShapeDtypeStruct((len(dils), hg // 2, 2 * qn, qn + 2 * DSWA_HALF), F32),
        name="band_bias",
    )(rel_table)


def _dswa_group(qkv, bias, lt=512):
    B, dil, L, W3 = qkv.shape
    hb = DSWA_HALF
    gw = W3 // 3
    rb = max(1, lt // L)
    lt = min(lt, L)
    r = lt // hb
    main = lambda which: pl.BlockSpec((1, rb, lt, gw), lambda b, d, t: (b, d, t, which))
    prev = lambda which: pl.BlockSpec(
        (1, rb, hb, gw), lambda b, d, t: (b, d, jnp.maximum(t * r - 1, 0), which))
    nxt = lambda which: pl.BlockSpec(
        (1, rb, hb, gw), lambda b, d, t: (b, d, jnp.minimum((t + 1) * r, L // hb - 1), which))
    out = pl.BlockSpec((1, rb, lt, gw), lambda b, d, t: (b, d, t, 0))
    return pl.pallas_call(
        functools.partial(_dswa_kernel, sub_len=L),
        grid=(B, dil // rb, L // lt),
        in_specs=[main(0), main(1), prev(1), nxt(1), main(2), prev(2), nxt(2),
                  pl.BlockSpec(bias.shape, lambda b, d, t: (0, 0, 0))],
        out_specs=[out, out],
        out_shape=[jax.ShapeDtypeStruct((B, dil, L, gw), F32)] * 2,
        compiler_params=_params("parallel", "parallel", "parallel"),
        name=f"dswa_d{dil}",
    )(qkv, qkv, qkv, qkv, qkv, qkv, qkv, bias)


def _dswa_out_mlp_kernel(*refs, dils, final):
    ng = len(dils)
    o_refs, l_refs = refs[:ng], refs[ng:2 * ng]
    x_ref, wo_ref, g_ref, w1_ref, w2_ref, gf_ref, out_ref, o_s, l_s, y_s, x1_ref, acc_ref = refs[2 * ng:]
    tm = x_ref.shape[0]
    tiles = o_refs[0].shape[3] // LANES
    def prologue(y):
        for g, d in enumerate(dils):
            for r in range(d):
                for j in range(tiles):
                    ls = slice(j * LANES, (j + 1) * LANES)
                    o_s[g * tiles + j, pl.ds(r, tm // d, stride=d), :] = o_refs[g][0, r, :, ls]
                    l_s[g * tiles + j, pl.ds(r, tm // d, stride=d), :] = l_refs[g][0, r, :, ls]
        pieces = []
        for j in range(tiles):
            lses = [l_s[g * tiles + j] for g in range(ng)]
            m = functools.reduce(jnp.maximum, lses)
            es = [jnp.exp(l - m) for l in lses]
            inv = 1.0 / functools.reduce(lambda a, b: a + b, es)
            for g in range(ng):
                c = g * tiles + j
                pieces.append(o_s[c] * (es[g] * inv))
                y[:, c * LANES:(c + 1) * LANES] = pieces[-1].astype(BF16)
        return pieces
    _mixer_out_mlp(prologue, y_s, (x_ref, wo_ref, g_ref, w1_ref, w2_ref, gf_ref, out_ref, x1_ref, acc_ref),
                   final, lookahead=False)


def _dswa_out_mlp(os_, lses, x2, w_out, S, g, w1, w2, g_final, final, tm=512):
    dils = tuple(o.shape[1] for o in os_)
    gw = os_[0].shape[3]
    nt = S // tm
    specs = lambda cur: [pl.BlockSpec((1, d, tm // d, gw), lambda i: (cur(i) // nt, 0, cur(i) % nt, 0))
                         for d in dils] * 2
    staging = pltpu.VMEM((len(dils) * gw // LANES, tm, LANES), F32)
    return _out_mlp_call(
        functools.partial(_dswa_out_mlp_kernel, dils=dils, final=final),
        (*os_, *lses), specs, [staging, staging],
        x2, w_out, g, w1, w2, g_final, tm, "dswa_out_mlp", lookahead=False)


def _gdn_layer(x2, g, w_in, conv_w, a_log, dt_bias, norm_w, w_out, B, S, mlp):
    nh = a_log.shape[1]
    z_w = nh * LANES
    qkvn, z, col, row = _gdn_proj(x2, g, w_in, conv_w, a_log, dt_bias, B, S)
    qkvn = qkvn.reshape(B, S, 3 * z_w)
    o_f = _gdn_scan(qkvn, col, row, nh, rev=False)
    o_b = _gdn_scan(qkvn, col, row, nh, rev=True)
    return _gdn_out_mlp(o_f.reshape(B * S, z_w), o_b.reshape(B * S, z_w), z, x2, norm_w,
                        w_out.astype(BF16), nh, *mlp)


def _dswa_proj_kernel(x_ref, g_ref, w_ref, *refs, dils, chunk):
    o_refs = refs
    n = w_ref.shape[1]
    ng = len(dils)
    gw = n // (3 * ng)
    tiles = gw // LANES
    h = _rms(x_ref[...], g_ref[...]).astype(BF16)

    def project(which):
        res = jnp.dot(h, w_ref[:, which * chunk:(which + 1) * chunk], preferred_element_type=F32)
        if which == 0:
            res = res * (DSWA_HEAD_DIM ** -0.5 * LOG2E)
        return res

    def regroup(which, res):
        for g, d in enumerate(dils):
            for j in range(tiles):
                src = slice((g * tiles + j) * LANES, (g * tiles + j + 1) * LANES)
                dst = slice((which * tiles + j) * LANES, (which * tiles + j + 1) * LANES)
                o_refs[g][0, :, :, dst] = pltpu.einshape("(rd)l->drl", res[:, src], d=d).astype(BF16)

    res = project(0)
    for which in range(1, 3):
        nxt = project(which)
        regroup(which - 1, res)
        res = nxt
    regroup(2, res)


def _dswa_proj(x2, g, w_in, dils, B, S, tm=512):
    T, D = x2.shape
    n = w_in.shape[1]
    gw3 = n // len(dils)
    nt = S // tm
    return pl.pallas_call(
        functools.partial(_dswa_proj_kernel, dils=dils, chunk=gw3),
        grid=(T // tm,),
        in_specs=[pl.BlockSpec((tm, D), lambda i: (i, 0)),
                  pl.BlockSpec((1, D), lambda i: (0, 0)),
                  pl.BlockSpec((D, n), lambda i: (0, 0))],
        out_specs=[pl.BlockSpec((1, d, tm // d, gw3), lambda i: (i // nt, 0, i % nt, 0)) for d in dils],
        out_shape=[jax.ShapeDtypeStruct((B, d, S // d, gw3), BF16) for d in dils],
        compiler_params=_params("parallel"),
        name="dswa_proj",
    )(x2, g.reshape(1, D), w_in)


def _dswa_layer(x2, g, w_in, w_out, bias, B, S, mlp):
    dils = tuple(d for _, d in DSWA_CONFIGS)
    qkvs = _dswa_proj(x2, g, w_in.astype(BF16), dils, B, S)
    os_, lses = [], []
    for gi in range(len(dils)):
        o, lse = _dswa_group(qkvs[gi], bias[gi])
        os_.append(o)
        lses.append(lse)
    return _dswa_out_mlp(os_, lses, x2, w_out.astype(BF16), S, *mlp)


def kernel(x, norm_mix, norm_mlp, norm_final, rel_bias, gdn_w_in, gdn_conv_w, gdn_a_log,
           gdn_dt_bias, gdn_norm_w, gdn_w_out, dswa_w_in, dswa_w_out, mlp_w1, mlp_w2):
    B, S, D = x.shape
    depth = norm_mix.shape[0]
    x2 = x.reshape(B * S, D)
    assert all(window // (2 * dil) == DSWA_HALF for window, dil in DSWA_CONFIGS)
    bias = _band_bias(rel_bias, tuple(d for _, d in DSWA_CONFIGS))
    for i in range(depth):
        j = i // 2
        mlp = (norm_mlp[i], mlp_w1[i].astype(BF16), mlp_w2[i].astype(BF16), norm_final, i == depth - 1)
        if i % 2 == 0:
            x2 = _gdn_layer(x2, norm_mix[i], gdn_w_in[j], gdn_conv_w[j], gdn_a_log[j],
                            gdn_dt_bias[j], gdn_norm_w[j], gdn_w_out[j], B, S, mlp)
        else:
            x2 = _dswa_layer(x2, norm_mix[i], dswa_w_in[j], dswa_w_out[j], bias, B, S, mlp)
    return x2.reshape(B, S, D)
```

```python
import functools
import math

import jax
import jax.numpy as jnp
from jax import lax
from jax.experimental import pallas as pl
from jax.experimental.pallas import tpu as pltpu

F32 = jnp.float32
BF16 = jnp.bfloat16

RMS_EPS = 1e-6
L2_EPS = 1e-6
NEG_INF = -1e30
LOG2E = math.log2(math.e)

LANES = 128
SUBLANES = 8
VMEM_LIMIT = 56 * 1024 * 1024

GDN_DK = 128
GDN_CONV = 5
GDN_CHUNK = LANES
DSWA_HEAD_DIM = 64
DSWA_CONFIGS = ((128, 1), (512, 4), (2048, 16))
DSWA_HALF = 64
DSWA_QBLOCK = 128
REL_BUCKETS = 32
REL_MAX_DIST = 1024


def _params(*sem):
    return pltpu.CompilerParams(dimension_semantics=sem, vmem_limit_bytes=VMEM_LIMIT)


def _dot(a, b):
    return jnp.dot(a.astype(BF16), b.astype(BF16), preferred_element_type=F32)


def _dot_nt(a, b):
    return lax.dot_general(a.astype(BF16), b.astype(BF16), (((1,), (1,)), ((), ())),
                           preferred_element_type=F32)


def _split3(x):
    hi = x.astype(BF16)
    r = x - hi.astype(F32)
    mid = r.astype(BF16)
    return hi, mid, (r - mid.astype(F32)).astype(BF16)


def _rms(x, g):
    return x * lax.rsqrt(jnp.mean(x * x, axis=-1, keepdims=True) + RMS_EPS) * g


def _sigmoid(x):
    return 1.0 / (1.0 + jnp.exp(-x))


def _softplus(x):
    return jnp.maximum(x, 0.0) + jnp.log1p(jnp.exp(-jnp.abs(x)))


CONV_HALO = 16
CONV_BLOCK = 64


def _gdn_proj_kernel(x_ref, xp_ref, xn_ref, g_ref, wqkv_ref, wz_ref, w_ref, wt_ref, cw_ref,
                     prow_ref, pcol_ref, qkv_ref, z_ref, col_ref, row_ref, *, nh, nt, chunk):
    C = GDN_CHUNK
    tm = x_ref.shape[0]
    t = pl.program_id(0) % nt
    pad = GDN_CONV // 2
    g = g_ref[...]
    h = _rms(x_ref[...], g).astype(BF16)
    h_halo = jnp.concatenate([_rms(jnp.where(t > 0, xp_ref[...], 0.0), g).astype(BF16),
                              _rms(jnp.where(t < nt - 1, xn_ref[...], 0.0), g).astype(BF16)], axis=0)
    CB = CONV_BLOCK
    taps = [i for i in range(GDN_CONV) if i != pad]
    sr = lax.broadcasted_iota(jnp.int32, (len(taps) * CB, CB + 2 * CONV_HALO), 0)
    sc = lax.broadcasted_iota(jnp.int32, (len(taps) * CB, CB + 2 * CONV_HALO), 1)
    src = (sr % CB) + (CONV_HALO - pad)
    for k, i in enumerate(taps):
        src = src + jnp.where(sr // CB == k, i, 0)
    shift_mat = (sc == src).astype(BF16)

    def project(c0):
        w = wqkv_ref[:, c0:c0 + chunk]
        main = jnp.dot(h, w, preferred_element_type=F32)
        halo = jnp.dot(h_halo, w, preferred_element_type=F32)
        ext = jnp.concatenate([halo[:CONV_HALO], main, halo[CONV_HALO:]], axis=0).astype(BF16)
        return main, ext

    starts = list(range(0, wqkv_ref.shape[1], chunk))
    nxt = project(starts[0])
    for n, c0 in enumerate(starts):
        cs = slice(c0, c0 + chunk)
        main, ext = nxt
        if n + 1 < len(starts):
            nxt = project(starts[n + 1])
        for b in range(tm // CB):
            rows = slice(b * CB, (b + 1) * CB)
            shifted = jnp.dot(shift_mat, ext[b * CB:(b + 1) * CB + 2 * CONV_HALO],
                              preferred_element_type=F32)
            acc = cw_ref[pad:pad + 1, cs] * main[rows]
            for k, i in enumerate(taps):
                acc = acc + cw_ref[i:i + 1, cs] * shifted[k * CB:(k + 1) * CB]
            y = acc * _sigmoid(acc)
            for j in range(chunk // LANES):
                yj = y[:, j * LANES:(j + 1) * LANES]
                head = c0 // LANES + j
                if head < 2 * nh:
                    inv = lax.rsqrt(jnp.sum(yj * yj, axis=-1, keepdims=True) + L2_EPS)
                    yj = yj * (inv * (GDN_DK ** -0.5) if head < nh else inv)
                qkv_ref[rows, c0 + j * LANES:c0 + (j + 1) * LANES] = yj.astype(BF16)
    for c0 in range(0, wz_ref.shape[1], chunk):
        cs = slice(c0, c0 + chunk)
        z_ref[:, cs] = jnp.dot(h, wz_ref[:, cs], preferred_element_type=F32).astype(BF16)
    a_c = jnp.dot(h, w_ref[...], preferred_element_type=F32)
    a_r = lax.dot_general(wt_ref[...], h, (((1,), (1,)), ((), ())),
                          preferred_element_type=F32)
    ri = lax.broadcasted_iota(jnp.int32, (C, C), 0)
    ci = lax.broadcasted_iota(jnp.int32, (C, C), 1)
    lower = (ri >= ci).astype(BF16)
    upper = (ri <= ci).astype(BF16)
    g_c = -jnp.exp(prow_ref[0:1, :]) * _softplus(a_c + prow_ref[1:2, :])
    g_r = -jnp.exp(pcol_ref[:, 0:1]) * _softplus(a_r + pcol_ref[:, 1:2])
    beta_c = _sigmoid(a_c)
    beta_r = _sigmoid(a_r)
    for c in range(tm // C):
        sl = slice(c * C, (c + 1) * C)
        gc = g_c[sl, :]
        pre = sum(jnp.dot(lower, part, preferred_element_type=F32) for part in _split3(gc))
        cum = jnp.where(ci < nh, pre, pre[C - 1:C, :] - pre + gc)
        col_ref[sl, :] = jnp.where(ci < 2 * nh, cum, beta_c[sl, :])
        gr = g_r[:, sl]
        pre_r = sum(jnp.dot(part, upper, preferred_element_type=F32) for part in _split3(gr))
        cum_r = jnp.where(ri < nh, pre_r, pre_r[:, C - 1:C] - pre_r + gr)
        full = jnp.where(ri < 2 * nh, cum_r, beta_r[:, sl])
        row_ref[0, :, sl] = full[:4 * nh, :]


def _gdn_proj(x2, g, w_in, conv_w, a_log, dt_bias, B, S, tm=1024, chunk=256):
    T, D = x2.shape
    nh = a_log.shape[1]
    qkv_w = 3 * nh * LANES
    z_w = nh * LANES
    w_pad = jnp.zeros((D, LANES), F32).at[:, :4 * nh].set(w_in[:, qkv_w + z_w:])
    alog = jnp.zeros((LANES,), F32).at[:2 * nh].set(a_log.reshape(-1))
    dtb = jnp.zeros((LANES,), F32).at[:2 * nh].set(dt_bias.reshape(-1))
    prow = jnp.zeros((SUBLANES, LANES), F32).at[0].set(alog).at[1].set(dtb)
    pcol = jnp.zeros((LANES, LANES), F32).at[:, 0].set(alog).at[:, 1].set(dtb)
    nt = S // tm
    r = tm // CONV_HALO
    const = lambda shape: pl.BlockSpec(shape, lambda i: (0, 0))
    return pl.pallas_call(
        functools.partial(_gdn_proj_kernel, nh=nh, nt=nt, chunk=chunk),
        grid=(T // tm,),
        in_specs=[pl.BlockSpec((tm, D), lambda i: (i, 0)),
                  pl.BlockSpec((CONV_HALO, D), lambda i: (jnp.maximum(i * r - 1, 0), 0)),
                  pl.BlockSpec((CONV_HALO, D), lambda i: (jnp.minimum((i + 1) * r, T // CONV_HALO - 1), 0)),
                  const((1, D)), const((D, qkv_w)), const((D, z_w)), const((D, LANES)), const((LANES, D)),
                  const((GDN_CONV, qkv_w)), const((SUBLANES, LANES)), const((LANES, LANES))],
        out_specs=[pl.BlockSpec((tm, qkv_w), lambda i: (i, 0)),
                   pl.BlockSpec((tm, z_w), lambda i: (i, 0)),
                   pl.BlockSpec((tm, LANES), lambda i: (i, 0)),
                   pl.BlockSpec((1, 4 * nh, tm), lambda i: (i // nt, 0, i % nt))],
        out_shape=[jax.ShapeDtypeStruct((T, qkv_w), BF16),
                   jax.ShapeDtypeStruct((T, z_w), BF16),
                   jax.ShapeDtypeStruct((T, LANES), F32),
                   jax.ShapeDtypeStruct((B, 4 * nh, S), F32)],
        compiler_params=_params("parallel"),
        name="gdn_proj",
    )(x2, x2, x2, g.reshape(1, D), w_in[:, :qkv_w].astype(BF16), w_in[:, qkv_w:qkv_w + z_w].astype(BF16),
      w_pad.astype(BF16), w_pad.T.astype(BF16), conv_w.reshape(GDN_CONV, qkv_w), prow, pcol)


def _gdn_scan_kernel(q_ref, k_ref, v_ref, col_ref, row_ref, o_ref, state_ref, *, rev, nh):
    C = GDN_CHUNK
    t = pl.program_id(1)

    @pl.when(t == 0)
    def _():
        state_ref[...] = jnp.zeros_like(state_ref)

    ri = lax.broadcasted_iota(jnp.int32, (C, C), 0)
    ci = lax.broadcasted_iota(jnp.int32, (C, C), 1)
    incl = (ri <= ci) if rev else (ri >= ci)
    strict = (ri < ci) if rev else (ri > ci)
    eye = (ri == ci).astype(F32)
    levels = [((ri >> (lg + 1)) == (ci >> (lg + 1))) & ((ri >> lg) != (ci >> lg))
              for lg in range(int(math.log2(C)))]
    last = 0 if rev else C - 1
    nc = q_ref.shape[1] // C
    chunks = list(reversed(range(nc))) if rev else list(range(nc))
    heads = range(nh)
    hs = [slice(h * LANES, (h + 1) * LANES) for h in heads]
    gi = [(nh if rev else 0) + h for h in heads]
    units = [(c, h) for c in chunks for h in heads]
    rows = {c: slice(c * C, (c + 1) * C) for c in chunks}
    col = {c: col_ref[0, rows[c], :] for c in chunks}
    row = {c: row_ref[0, :, rows[c]] for c in chunks}
    kf = {u: k_ref[0, rows[u[0]], hs[u[1]]] for u in units}
    k = {u: kf[u].astype(F32) for u in units}
    g_col = {(c, h): col[c][:, gi[h]:gi[h] + 1] for c, h in units}
    b_col = {(c, h): col[c][:, 2 * nh + gi[h]:2 * nh + gi[h] + 1] for c, h in units}
    g_last = {u: g_col[u][last:last + 1, :] for u in units}
    decay = {(c, h): jnp.where(incl, jnp.exp(jnp.where(incl, g_col[c, h] - row[c][gi[h]:gi[h] + 1, :], 0.0)), 0.0)
             for c, h in units}
    kb = {u: k[u] * b_col[u] for u in units}
    res = {(c, h): _dot_nt(jnp.concatenate([kb[c, h].astype(BF16), q_ref[0, rows[c], hs[h]]], axis=0), kf[c, h])
           for c, h in units}
    a = {u: jnp.where(strict, res[u][:C] * decay[u], 0.0) for u in units}
    intra = {u: (res[u][C:] * decay[u]).astype(BF16) for u in units}
    tinv = {u: eye - jnp.where(levels[0], a[u], 0.0) for u in units}
    for lg, m in enumerate(levels[1:], start=1):
        size = 1 << lg
        moving = [r0 for r0 in range(0, C, size) if ((r0 >> lg) & 1) == (0 if rev else 1)]
        tb = {u: tinv[u].astype(BF16) for u in units}
        if size < SUBLANES:
            pm = {u: _dot(tb[u], jnp.where(m, a[u], 0.0)) for u in units}
            tinv = {u: tinv[u] - _dot(pm[u], tb[u]) for u in units}
            continue
        lhs = {u: jnp.concatenate([tinv[u][r0:r0 + size] for r0 in moving], axis=0) for u in units}
        pm = {u: _dot(lhs[u], jnp.where(m, a[u], 0.0)) for u in units}
        new = {u: lhs[u] - _dot(pm[u], tb[u]) for u in units}
        tinv = {u: jnp.concatenate(
            [new[u][moving.index(r0) * size:(moving.index(r0) + 1) * size] if r0 in moving
             else tinv[u][r0:r0 + size] for r0 in range(0, C, size)], axis=0) for u in units}
    e_col = {u: jnp.exp(g_col[u]) for u in units}
    sol = {(c, h): _dot(tinv[c, h], jnp.concatenate(
        [v_ref[0, rows[c], hs[h]].astype(F32) * b_col[c, h], kb[c, h] * e_col[c, h]], axis=1)) for c, h in units}
    qd = {(c, h): (q_ref[0, rows[c], hs[h]].astype(F32) * e_col[c, h]).astype(BF16) for c, h in units}
    post = {u: jnp.concatenate([intra[u], (k[u] * jnp.exp(g_last[u] - g_col[u])).T.astype(BF16)], axis=0)
            for u in units}
    state = [state_ref[h] for h in heads]
    for c in chunks:
        sb = [state[h].astype(BF16) for h in heads]
        r2 = [_dot(jnp.concatenate([sol[c, h][:, LANES:].astype(BF16), qd[c, h]], axis=0), sb[h]) for h in heads]
        v_new = [(sol[c, h][:, :LANES] - r2[h][:C]).astype(BF16) for h in heads]
        r3 = [_dot(post[c, h], v_new[h]) for h in heads]
        for h in heads:
            o_ref[0, rows[c], hs[h]] = (r2[h][C:] + r3[h][:C]).astype(o_ref.dtype)
        state = [state[h] * jnp.exp(g_last[c, h]) + r3[h][C:] for h in heads]
    for h in heads:
        state_ref[h] = state[h]


def _gdn_scan(qkvn, col, row, nh, rev, nc=4):
    B, S, _ = qkvn.shape
    ts = nc * GDN_CHUNK
    nt = S // ts
    width = nh * LANES
    tmap = (lambda t: nt - 1 - t) if rev else (lambda t: t)
    col3 = col.reshape(B, S, LANES)
    return pl.pallas_call(
        functools.partial(_gdn_scan_kernel, rev=rev, nh=nh),
        grid=(B, nt),
        in_specs=[pl.BlockSpec((1, ts, width), lambda b, t: (b, tmap(t), 0)),
                  pl.BlockSpec((1, ts, width), lambda b, t: (b, tmap(t), 1)),
                  pl.BlockSpec((1, ts, width), lambda b, t: (b, tmap(t), 2)),
                  pl.BlockSpec((1, ts, LANES), lambda b, t: (b, tmap(t), 0)),
                  pl.BlockSpec((1, 4 * nh, ts), lambda b, t: (b, 0, tmap(t)))],
        out_specs=pl.BlockSpec((1, ts, width), lambda b, t: (b, tmap(t), 0)),
        out_shape=jax.ShapeDtypeStruct((B, S, width), BF16),
        scratch_shapes=[pltpu.VMEM((nh, GDN_DK, LANES), F32)],
        compiler_params=_params("parallel", "arbitrary"),
        name="gdn_scan_bwd" if rev else "gdn_scan_fwd",
    )(qkvn, qkvn, qkvn, col3, row)


MLP_CHUNK = 512


def _anchor(v):
    folded = jnp.sum(v.reshape(v.shape[0] // SUBLANES, SUBLANES, LANES), axis=0)
    bits = lax.shift_right_logical(pltpu.bitcast(folded, jnp.uint32), jnp.uint32(32))
    return pltpu.bitcast(bits, F32)


def _out_mlp_tail(y_ref, x_ref, wo_ref, g_ref, w1_ref, w2_ref, gf_ref, o_ref, x1_ref, acc_ref, final,
                  anchors=()):
    x1_ref[...] = x_ref[...] + jnp.dot(y_ref[...], wo_ref[...], preferred_element_type=F32)
    h = _rms(x1_ref[...], g_ref[...]).astype(BF16)
    starts = list(range(0, w1_ref.shape[1], MLP_CHUNK))
    for n, c in enumerate(starts):
        a = jnp.maximum(jnp.dot(h, w1_ref[:, c:c + MLP_CHUNK], preferred_element_type=F32), 0.0)
        part = jnp.dot((a * a).astype(BF16), w2_ref[c:c + MLP_CHUNK, :], preferred_element_type=F32)
        if c == 0:
            acc_ref[...] = part
        else:
            acc_ref[...] += part
        for k in range(n, len(anchors), len(starts)):
            acc_ref[0:SUBLANES, 0:LANES] += anchors[k]
    out = x1_ref[...] + acc_ref[...]
    if final:
        out = _rms(out, gf_ref[...])
    o_ref[...] = out


def _mixer_out_mlp(prologue, y_ref, tail_refs, final, lookahead):
    if not lookahead:
        prologue(y_ref.at[0])
        _out_mlp_tail(y_ref.at[0], *tail_refs, final)
        return
    i = pl.program_id(0)

    @pl.when(i == 0)
    def _():
        y_ref[...] = jnp.zeros_like(y_ref)

    for parity in range(2):
        @pl.when(i % 2 == parity)
        def _():
            pieces = prologue(y_ref.at[parity])
            _out_mlp_tail(y_ref.at[1 - parity], *tail_refs, final, anchors=[_anchor(v) for v in pieces])


def _out_mlp_call(kernel_fn, mixer_args, mixer_specs, mixer_scratch, x2, w_out, g, w1, w2,
                  g_final, tm, name, lookahead):
    T, D = x2.shape
    n = T // tm
    if lookahead:
        cur = lambda i: jnp.minimum(i, n - 1)
        prev = lambda i: jnp.maximum(i - 1, 0)
    else:
        cur = prev = lambda i: i
    once = lambda shape: pl.BlockSpec(shape, lambda i: (0, 0), pipeline_mode=pl.Buffered(1))
    return pl.pallas_call(
        kernel_fn,
        grid=(n + 1 if lookahead else n,),
        in_specs=mixer_specs(cur) + [pl.BlockSpec((tm, D), lambda i: (prev(i), 0)), once(w_out.shape),
                                     once((1, D)), once(w1.shape), once(w2.shape), once((1, D))],
        out_specs=pl.BlockSpec((tm, D), lambda i: (prev(i), 0)),
        out_shape=jax.ShapeDtypeStruct((T, D), F32),
        scratch_shapes=mixer_scratch + [pltpu.VMEM((2 if lookahead else 1, tm, w_out.shape[0]), BF16),
                                        pltpu.VMEM((tm, D), F32), pltpu.VMEM((tm, D), F32)],
        compiler_params=_params("arbitrary" if lookahead else "parallel"),
        name=name,
    )(*mixer_args, x2, w_out, g.reshape(1, D), w1, w2, g_final.reshape(1, D))


def _gdn_out_mlp_kernel(of_ref, ob_ref, z_ref, nw_ref, x_ref, wo_ref, g_ref, w1_ref, w2_ref, gf_ref,
                        o_ref, y_ref, x1_ref, acc_ref, *, nh, final):
    def prologue(y):
        pieces = []
        for h in range(nh):
            hs = slice(h * LANES, (h + 1) * LANES)
            o = of_ref[:, hs].astype(F32) + ob_ref[:, hs].astype(F32)
            o = o * lax.rsqrt(jnp.mean(o * o, axis=-1, keepdims=True) + RMS_EPS)
            z = z_ref[:, hs].astype(F32)
            pieces.append(o * nw_ref[...] * (z * _sigmoid(z)))
            y[:, hs] = pieces[-1].astype(BF16)
        return pieces
    _mixer_out_mlp(prologue, y_ref, (x_ref, wo_ref, g_ref, w1_ref, w2_ref, gf_ref, o_ref, x1_ref, acc_ref),
                   final, lookahead=True)


def _gdn_out_mlp(o_f, o_b, z, x2, norm_w, w_out, nh, g, w1, w2, g_final, final, tm=512):
    W = o_f.shape[1]
    specs = lambda cur: [pl.BlockSpec((tm, W), lambda i: (cur(i), 0))] * 3 + \
                        [pl.BlockSpec((1, LANES), lambda i: (0, 0))]
    return _out_mlp_call(
        functools.partial(_gdn_out_mlp_kernel, nh=nh, final=final),
        (o_f, o_b, z, norm_w.reshape(1, LANES)), specs,
        [], x2, w_out, g, w1, w2, g_final, tm, "gdn_out_mlp", lookahead=True)


def _dswa_kernel(q_ref, k_ref, kp_ref, kn_ref, v_ref, vp_ref, vn_ref, bias_ref,
                 o_ref, lse_ref, *, sub_len):
    t = pl.program_id(2)
    rb, lt = q_ref.shape[1], q_ref.shape[2]
    hb, qn = DSWA_HALF, DSWA_QBLOCK
    kw = qn + 2 * hb
    first_head = lax.broadcasted_iota(jnp.int32, (1, LANES), 1) < DSWA_HEAD_DIM
    n_pairs = q_ref.shape[3] // LANES
    blocks = range(lt // qn)
    units = [(rc, qb, pr) for rc in range(rb) for qb in blocks for pr in range(n_pairs)]
    rows = [slice(qb * qn, (qb + 1) * qn) for qb in blocks]
    win = [slice(qb * qn, qb * qn + kw) for qb in blocks]
    ls = [slice(pr * LANES, (pr + 1) * LANES) for pr in range(n_pairs)]
    kext = [jnp.concatenate([kp_ref[0, rc], k_ref[0, rc], kn_ref[0, rc]], axis=0) for rc in range(rb)]
    vext = [jnp.concatenate([vp_ref[0, rc], v_ref[0, rc], vn_ref[0, rc]], axis=0) for rc in range(rb)]
    kpos = lambda qb: t * lt + qb * qn - hb + lax.broadcasted_iota(jnp.int32, (1, kw), 1)
    qs = {(rc, qb, pr): jnp.concatenate([jnp.where(first_head, q_ref[0, rc, rows[qb], ls[pr]], 0.0),
                                         jnp.where(first_head, 0.0, q_ref[0, rc, rows[qb], ls[pr]])], axis=0)
          for rc, qb, pr in units}
    s = {(rc, qb, pr): _dot_nt(qs[rc, qb, pr], kext[rc][win[qb], ls[pr]]) + bias_ref[pr] for rc, qb, pr in units}
    for rc, qb, pr in units:
        if qb == blocks[0]:
            s[rc, qb, pr] = jnp.where(kpos(qb) >= 0, s[rc, qb, pr], NEG_INF)
        if qb == blocks[-1]:
            s[rc, qb, pr] = jnp.where(kpos(qb) < sub_len, s[rc, qb, pr], NEG_INF)
    m = {u: jnp.max(s[u], axis=-1, keepdims=True) for u in units}
    p = {u: jnp.exp2(s[u] - m[u]) for u in units}
    l = {u: jnp.sum(p[u], axis=-1, keepdims=True) for u in units}
    pv = {(rc, qb, pr): _dot(p[rc, qb, pr], vext[rc][win[qb], ls[pr]]) * (1.0 / l[rc, qb, pr])
          for rc, qb, pr in units}
    lse = {u: (m[u] + jnp.log2(l[u])) * math.log(2.0) for u in units}
    for rc, qb, pr in units:
        o_ref[0, rc, rows[qb], ls[pr]] = jnp.where(first_head, pv[rc, qb, pr][:qn], pv[rc, qb, pr][qn:])
        lse_ref[0, rc, rows[qb], ls[pr]] = jnp.where(first_head, lse[rc, qb, pr][:qn], lse[rc, qb, pr][qn:])


def _rel_bucket(rel):
    nb = REL_BUCKETS // 2
    max_exact = nb // 2
    ret = jnp.where(rel > 0, nb, 0)
    n = jnp.abs(rel)
    nf = jnp.maximum(n, 1).astype(F32)
    large = max_exact + (jnp.log(nf * (1.0 / max_exact)) / math.log(REL_MAX_DIST / max_exact)
                         * (nb - max_exact)).astype(jnp.int32)
    large = jnp.minimum(large, nb - 1)
    return ret + jnp.where(n < max_exact, n, large)


def _band_bias_kernel(tab_ref, o_ref, *, dils, hg):
    hb, qn = DSWA_HALF, DSWA_QBLOCK
    kw = qn + 2 * hb
    off = (lax.broadcasted_iota(jnp.int32, (qn, kw), 1) - hb
           - lax.broadcasted_iota(jnp.int32, (qn, kw), 0))
    inband = jnp.abs(off) <= hb
    for g, d in enumerate(dils):
        bucket = _rel_bucket(off * d)
        for hh in range(hg):
            acc = jnp.zeros((qn, kw), F32)
            for b in range(REL_BUCKETS):
                acc = jnp.where(bucket == b, tab_ref[b, g * hg + hh], acc)
            o_ref[g, hh // 2, (hh % 2) * qn:(hh % 2 + 1) * qn, :] = jnp.where(inband, acc * LOG2E, NEG_INF)


def _band_bias(rel_table, dils):
    qn = DSWA_QBLOCK
    hg = rel_table.shape[1] // len(dils)
    return pl.pallas_call(
        functools.partial(_band_bias_kernel, dils=dils, hg=hg),
        in_specs=[pl.BlockSpec(memory_space=pltpu.SMEM)],
        out_shape=jax.ShapeDtypeStruct((len(dils), hg // 2, 2 * qn, qn + 2 * DSWA_HALF), F32),
        name="band_bias",
    )(rel_table)


def _dswa_group(qkv, bias, lt=512):
    B, dil, L, W3 = qkv.shape
    hb = DSWA_HALF
    gw = W3 // 3
    rb = max(1, lt // L)
    lt = min(lt, L)
    r = lt // hb
    main = lambda which: pl.BlockSpec((1, rb, lt, gw), lambda b, d, t: (b, d, t, which))
    prev = lambda which: pl.BlockSpec(
        (1, rb, hb, gw), lambda b, d, t: (b, d, jnp.maximum(t * r - 1, 0), which))
    nxt = lambda which: pl.BlockSpec(
        (1, rb, hb, gw), lambda b, d, t: (b, d, jnp.minimum((t + 1) * r, L // hb - 1), which))
    out = pl.BlockSpec((1, rb, lt, gw), lambda b, d, t: (b, d, t, 0))
    return pl.pallas_call(
        functools.partial(_dswa_kernel, sub_len=L),
        grid=(B, dil // rb, L // lt),
        in_specs=[main(0), main(1), prev(1), nxt(1), main(2), prev(2), nxt(2),
                  pl.BlockSpec(bias.shape, lambda b, d, t: (0, 0, 0))],
        out_specs=[out, out],
        out_shape=[jax.ShapeDtypeStruct((B, dil, L, gw), F32)] * 2,
        compiler_params=_params("parallel", "parallel", "parallel"),
        name=f"dswa_d{dil}",
    )(qkv, qkv, qkv, qkv, qkv, qkv, qkv, bias)


def _dswa_out_mlp_kernel(*refs, dils, final):
    ng = len(dils)
    o_refs, l_refs = refs[:ng], refs[ng:2 * ng]
    x_ref, wo_ref, g_ref, w1_ref, w2_ref, gf_ref, out_ref, o_s, l_s, y_s, x1_ref, acc_ref = refs[2 * ng:]
    tm = x_ref.shape[0]
    tiles = o_refs[0].shape[3] // LANES
    def prologue(y):
        for g in range(ng):
            for j in range(tiles):
                ls = slice(j * LANES, (j + 1) * LANES)
                o_s[g * tiles + j] = pltpu.einshape("drl->(rd)l", o_refs[g][0, :, :, ls])
                l_s[g * tiles + j] = pltpu.einshape("drl->(rd)l", l_refs[g][0, :, :, ls])
        pieces = []
        for j in range(tiles):
            lses = [l_s[g * tiles + j] for g in range(ng)]
            m = functools.reduce(jnp.maximum, lses)
            es = [jnp.exp(l - m) for l in lses]
            inv = 1.0 / functools.reduce(lambda a, b: a + b, es)
            for g in range(ng):
                c = g * tiles + j
                pieces.append(o_s[c] * (es[g] * inv))
                y[:, c * LANES:(c + 1) * LANES] = pieces[-1].astype(BF16)
        return pieces
    _mixer_out_mlp(prologue, y_s, (x_ref, wo_ref, g_ref, w1_ref, w2_ref, gf_ref, out_ref, x1_ref, acc_ref),
                   final, lookahead=False)


def _dswa_out_mlp(os_, lses, x2, w_out, S, g, w1, w2, g_final, final, tm=512):
    dils = tuple(o.shape[1] for o in os_)
    gw = os_[0].shape[3]
    nt = S // tm
    specs = lambda cur: [pl.BlockSpec((1, d, tm // d, gw), lambda i: (cur(i) // nt, 0, cur(i) % nt, 0))
                         for d in dils] * 2
    staging = pltpu.VMEM((len(dils) * gw // LANES, tm, LANES), F32)
    return _out_mlp_call(
        functools.partial(_dswa_out_mlp_kernel, dils=dils, final=final),
        (*os_, *lses), specs, [staging, staging],
        x2, w_out, g, w1, w2, g_final, tm, "dswa_out_mlp", lookahead=False)


def _gdn_layer(x2, g, w_in, conv_w, a_log, dt_bias, norm_w, w_out, B, S, mlp):
    nh = a_log.shape[1]
    z_w = nh * LANES
    qkvn, z, col, row = _gdn_proj(x2, g, w_in, conv_w, a_log, dt_bias, B, S)
    qkvn = qkvn.reshape(B, S, 3 * z_w)
    o_f = _gdn_scan(qkvn, col, row, nh, rev=False)
    o_b = _gdn_scan(qkvn, col, row, nh, rev=True)
    return _gdn_out_mlp(o_f.reshape(B * S, z_w), o_b.reshape(B * S, z_w), z, x2, norm_w,
                        w_out.astype(BF16), nh, *mlp)


def _dswa_proj_kernel(x_ref, g_ref, w_ref, *refs, dils, chunk):
    o_refs = refs
    n = w_ref.shape[1]
    ng = len(dils)
    gw = n // (3 * ng)
    tiles = gw // LANES
    h = _rms(x_ref[...], g_ref[...]).astype(BF16)

    def project(which):
        res = jnp.dot(h, w_ref[:, which * chunk:(which + 1) * chunk], preferred_element_type=F32)
        if which == 0:
            res = res * (DSWA_HEAD_DIM ** -0.5 * LOG2E)
        return res

    def regroup(which, res):
        for g, d in enumerate(dils):
            for j in range(tiles):
                src = slice((g * tiles + j) * LANES, (g * tiles + j + 1) * LANES)
                dst = slice((which * tiles + j) * LANES, (which * tiles + j + 1) * LANES)
                o_refs[g][0, :, :, dst] = pltpu.einshape("(rd)l->drl", res[:, src], d=d).astype(BF16)

    res = project(0)
    for which in range(1, 3):
        nxt = project(which)
        regroup(which - 1, res)
        res = nxt
    regroup(2, res)


def _dswa_proj(x2, g, w_in, dils, B, S, tm=512):
    T, D = x2.shape
    n = w_in.shape[1]
    gw3 = n // len(dils)
    nt = S // tm
    return pl.pallas_call(
        functools.partial(_dswa_proj_kernel, dils=dils, chunk=gw3),
        grid=(T // tm,),
        in_specs=[pl.BlockSpec((tm, D), lambda i: (i, 0)),
                  pl.BlockSpec((1, D), lambda i: (0, 0)),
                  pl.BlockSpec((D, n), lambda i: (0, 0))],
        out_specs=[pl.BlockSpec((1, d, tm // d, gw3), lambda i: (i // nt, 0, i % nt, 0)) for d in dils],
        out_shape=[jax.ShapeDtypeStruct((B, d, S // d, gw3), BF16) for d in dils],
        compiler_params=_params("parallel"),
        name="dswa_proj",
    )(x2, g.reshape(1, D), w_in)


def _dswa_layer(x2, g, w_in, w_out, bias, B, S, mlp):
    dils = tuple(d for _, d in DSWA_CONFIGS)
    qkvs = _dswa_proj(x2, g, w_in.astype(BF16), dils, B, S)
    os_, lses = [], []
    for gi in range(len(dils)):
        o, lse = _dswa_group(qkvs[gi], bias[gi])
        os_.append(o)
        lses.append(lse)
    return _dswa_out_mlp(os_, lses, x2, w_out.astype(BF16), S, *mlp)


def kernel(x, norm_mix, norm_mlp, norm_final, rel_bias, gdn_w_in, gdn_conv_w, gdn_a_log,
           gdn_dt_bias, gdn_norm_w, gdn_w_out, dswa_w_in, dswa_w_out, mlp_w1, mlp_w2):
    B, S, D = x.shape
    depth = norm_mix.shape[0]
    x2 = x.reshape(B * S, D)
    assert all(window // (2 * dil) == DSWA_HALF for window, dil in DSWA_CONFIGS)
    bias = _band_bias(rel_bias, tuple(d for _, d in DSWA_CONFIGS))
    for i in range(depth):
        j = i // 2
        mlp = (norm_mlp[i], mlp_w1[i].astype(BF16), mlp_w2[i].astype(BF16), norm_final, i == depth - 1)
        if i % 2 == 0:
            x2 = _gdn_layer(x2, norm_mix[i], gdn_w_in[j], gdn_conv_w[j], gdn_a_log[j],
                            gdn_dt_bias[j], gdn_norm_w[j], gdn_w_out[j], B, S, mlp)
        else:
            x2 = _dswa_layer(x2, norm_mix[i], dswa_w_in[j], dswa_w_out[j], bias, B, S, mlp)
    return x2.reshape(B, S, D)
```

```python
import functools
import math

import jax
import jax.numpy as jnp
from jax import lax
from jax.experimental import pallas as pl
from jax.experimental.pallas import tpu as pltpu

F32 = jnp.float32
BF16 = jnp.bfloat16

RMS_EPS = 1e-6
L2_EPS = 1e-6
NEG_INF = -1e30
LOG2E = math.log2(math.e)

LANES = 128
SUBLANES = 8
VMEM_LIMIT = 56 * 1024 * 1024

GDN_DK = 128
GDN_CONV = 5
GDN_CHUNK = LANES
DSWA_HEAD_DIM = 64
DSWA_CONFIGS = ((128, 1), (512, 4), (2048, 16))
DSWA_HALF = 64
DSWA_QBLOCK = 128
REL_BUCKETS = 32
REL_MAX_DIST = 1024


def _params(*sem):
    return pltpu.CompilerParams(dimension_semantics=sem, vmem_limit_bytes=VMEM_LIMIT)


def _dot(a, b):
    return jnp.dot(a.astype(BF16), b.astype(BF16), preferred_element_type=F32)


def _dot_nt(a, b):
    return lax.dot_general(a.astype(BF16), b.astype(BF16), (((1,), (1,)), ((), ())),
                           preferred_element_type=F32)


def _split3(x):
    hi = x.astype(BF16)
    r = x - hi.astype(F32)
    mid = r.astype(BF16)
    return hi, mid, (r - mid.astype(F32)).astype(BF16)


def _rms(x, g):
    return x * lax.rsqrt(jnp.mean(x * x, axis=-1, keepdims=True) + RMS_EPS) * g


def _sigmoid(x):
    return 1.0 / (1.0 + jnp.exp(-x))


def _softplus(x):
    return jnp.maximum(x, 0.0) + jnp.log1p(jnp.exp(-jnp.abs(x)))


CONV_HALO = 16
CONV_BLOCK = 64


def _gdn_proj_kernel(x_ref, xp_ref, xn_ref, g_ref, wqkv_ref, wz_ref, w_ref, wt_ref, cw_ref,
                     prow_ref, pcol_ref, qkv_ref, z_ref, col_ref, row_ref, *, nh, nt, chunk):
    C = GDN_CHUNK
    tm = x_ref.shape[0]
    t = pl.program_id(0) % nt
    pad = GDN_CONV // 2
    g = g_ref[...]
    h = _rms(x_ref[...], g).astype(BF16)
    h_halo = jnp.concatenate([_rms(jnp.where(t > 0, xp_ref[...], 0.0), g).astype(BF16),
                              _rms(jnp.where(t < nt - 1, xn_ref[...], 0.0), g).astype(BF16)], axis=0)
    CB = CONV_BLOCK
    taps = [i for i in range(GDN_CONV) if i != pad]
    sr = lax.broadcasted_iota(jnp.int32, (len(taps) * CB, CB + 2 * CONV_HALO), 0)
    sc = lax.broadcasted_iota(jnp.int32, (len(taps) * CB, CB + 2 * CONV_HALO), 1)
    src = (sr % CB) + (CONV_HALO - pad)
    for k, i in enumerate(taps):
        src = src + jnp.where(sr // CB == k, i, 0)
    shift_mat = (sc == src).astype(BF16)

    def project(c0):
        w = wqkv_ref[:, c0:c0 + chunk]
        main = jnp.dot(h, w, preferred_element_type=F32)
        halo = jnp.dot(h_halo, w, preferred_element_type=F32)
        ext = jnp.concatenate([halo[:CONV_HALO], main, halo[CONV_HALO:]], axis=0).astype(BF16)
        return main, ext

    starts = list(range(0, wqkv_ref.shape[1], chunk))
    nxt = project(starts[0])
    for n, c0 in enumerate(starts):
        cs = slice(c0, c0 + chunk)
        main, ext = nxt
        if n + 1 < len(starts):
            nxt = project(starts[n + 1])
        for b in range(tm // CB):
            rows = slice(b * CB, (b + 1) * CB)
            shifted = jnp.dot(shift_mat, ext[b * CB:(b + 1) * CB + 2 * CONV_HALO],
                              preferred_element_type=F32)
            acc = cw_ref[pad:pad + 1, cs] * main[rows]
            for k, i in enumerate(taps):
                acc = acc + cw_ref[i:i + 1, cs] * shifted[k * CB:(k + 1) * CB]
            y = acc * _sigmoid(acc)
            for j in range(chunk // LANES):
                yj = y[:, j * LANES:(j + 1) * LANES]
                head = c0 // LANES + j
                if head < 2 * nh:
                    inv = lax.rsqrt(jnp.sum(yj * yj, axis=-1, keepdims=True) + L2_EPS)
                    yj = yj * (inv * (GDN_DK ** -0.5) if head < nh else inv)
                qkv_ref[rows, c0 + j * LANES:c0 + (j + 1) * LANES] = yj.astype(BF16)
    for c0 in range(0, wz_ref.shape[1], chunk):
        cs = slice(c0, c0 + chunk)
        z_ref[:, cs] = jnp.dot(h, wz_ref[:, cs], preferred_element_type=F32).astype(BF16)
    a_c = jnp.dot(h, w_ref[...], preferred_element_type=F32)
    a_r = lax.dot_general(wt_ref[...], h, (((1,), (1,)), ((), ())),
                          preferred_element_type=F32)
    ri = lax.broadcasted_iota(jnp.int32, (C, C), 0)
    ci = lax.broadcasted_iota(jnp.int32, (C, C), 1)
    lower = (ri >= ci).astype(BF16)
    upper = (ri <= ci).astype(BF16)
    g_c = -jnp.exp(prow_ref[0:1, :]) * _softplus(a_c + prow_ref[1:2, :])
    g_r = -jnp.exp(pcol_ref[:, 0:1]) * _softplus(a_r + pcol_ref[:, 1:2])
    beta_c = _sigmoid(a_c)
    beta_r = _sigmoid(a_r)
    for c in range(tm // C):
        sl = slice(c * C, (c + 1) * C)
        gc = g_c[sl, :]
        pre = sum(jnp.dot(lower, part, preferred_element_type=F32) for part in _split3(gc))
        cum = jnp.where(ci < nh, pre, pre[C - 1:C, :] - pre + gc)
        col_ref[sl, :] = jnp.where(ci < 2 * nh, cum, beta_c[sl, :])
        gr = g_r[:, sl]
        pre_r = sum(jnp.dot(part, upper, preferred_element_type=F32) for part in _split3(gr))
        cum_r = jnp.where(ri < nh, pre_r, pre_r[:, C - 1:C] - pre_r + gr)
        full = jnp.where(ri < 2 * nh, cum_r, beta_r[:, sl])
        row_ref[0, :, sl] = full[:4 * nh, :]


def _gdn_proj(x2, g, w_in, conv_w, a_log, dt_bias, B, S, tm=1024, chunk=256):
    T, D = x2.shape
    nh = a_log.shape[1]
    qkv_w = 3 * nh * LANES
    z_w = nh * LANES
    w_pad = jnp.zeros((D, LANES), F32).at[:, :4 * nh].set(w_in[:, qkv_w + z_w:])
    alog = jnp.zeros((LANES,), F32).at[:2 * nh].set(a_log.reshape(-1))
    dtb = jnp.zeros((LANES,), F32).at[:2 * nh].set(dt_bias.reshape(-1))
    prow = jnp.zeros((SUBLANES, LANES), F32).at[0].set(alog).at[1].set(dtb)
    pcol = jnp.zeros((LANES, LANES), F32).at[:, 0].set(alog).at[:, 1].set(dtb)
    nt = S // tm
    r = tm // CONV_HALO
    const = lambda shape: pl.BlockSpec(shape, lambda i: (0, 0))
    return pl.pallas_call(
        functools.partial(_gdn_proj_kernel, nh=nh, nt=nt, chunk=chunk),
        grid=(T // tm,),
        in_specs=[pl.BlockSpec((tm, D), lambda i: (i, 0)),
                  pl.BlockSpec((CONV_HALO, D), lambda i: (jnp.maximum(i * r - 1, 0), 0)),
                  pl.BlockSpec((CONV_HALO, D), lambda i: (jnp.minimum((i + 1) * r, T // CONV_HALO - 1), 0)),
                  const((1, D)), const((D, qkv_w)), const((D, z_w)), const((D, LANES)), const((LANES, D)),
                  const((GDN_CONV, qkv_w)), const((SUBLANES, LANES)), const((LANES, LANES))],
        out_specs=[pl.BlockSpec((tm, qkv_w), lambda i: (i, 0)),
                   pl.BlockSpec((tm, z_w), lambda i: (i, 0)),
                   pl.BlockSpec((tm, LANES), lambda i: (i, 0)),
                   pl.BlockSpec((1, 4 * nh, tm), lambda i: (i // nt, 0, i % nt))],
        out_shape=[jax.ShapeDtypeStruct((T, qkv_w), BF16),
                   jax.ShapeDtypeStruct((T, z_w), BF16),
                   jax.ShapeDtypeStruct((T, LANES), F32),
                   jax.ShapeDtypeStruct((B, 4 * nh, S), F32)],
        compiler_params=_params("parallel"),
        name="gdn_proj",
    )(x2, x2, x2, g.reshape(1, D), w_in[:, :qkv_w].astype(BF16), w_in[:, qkv_w:qkv_w + z_w].astype(BF16),
      w_pad.astype(BF16), w_pad.T.astype(BF16), conv_w.reshape(GDN_CONV, qkv_w), prow, pcol)


def _gdn_scan_kernel(q_ref, k_ref, v_ref, col_ref, row_ref, o_ref, state_ref, *, rev, nh):
    C = GDN_CHUNK
    t = pl.program_id(1)

    @pl.when(t == 0)
    def _():
        state_ref[...] = jnp.zeros_like(state_ref)

    ri = lax.broadcasted_iota(jnp.int32, (C, C), 0)
    ci = lax.broadcasted_iota(jnp.int32, (C, C), 1)
    incl = (ri <= ci) if rev else (ri >= ci)
    strict = (ri < ci) if rev else (ri > ci)
    eye = (ri == ci).astype(F32)
    levels = [((ri >> (lg + 1)) == (ci >> (lg + 1))) & ((ri >> lg) != (ci >> lg))
              for lg in range(int(math.log2(C)))]
    last = 0 if rev else C - 1
    nc = q_ref.shape[1] // C
    chunks = list(reversed(range(nc))) if rev else list(range(nc))
    heads = range(nh)
    hs = [slice(h * LANES, (h + 1) * LANES) for h in heads]
    gi = [(nh if rev else 0) + h for h in heads]
    units = [(c, h) for c in chunks for h in heads]
    rows = {c: slice(c * C, (c + 1) * C) for c in chunks}
    col = {c: col_ref[0, rows[c], :] for c in chunks}
    row = {c: row_ref[0, :, rows[c]] for c in chunks}
    kf = {u: k_ref[0, rows[u[0]], hs[u[1]]] for u in units}
    k = {u: kf[u].astype(F32) for u in units}
    g_col = {(c, h): col[c][:, gi[h]:gi[h] + 1] for c, h in units}
    b_col = {(c, h): col[c][:, 2 * nh + gi[h]:2 * nh + gi[h] + 1] for c, h in units}
    g_last = {u: g_col[u][last:last + 1, :] for u in units}
    decay = {(c, h): jnp.where(incl, jnp.exp(jnp.where(incl, g_col[c, h] - row[c][gi[h]:gi[h] + 1, :], 0.0)), 0.0)
             for c, h in units}
    kb = {u: k[u] * b_col[u] for u in units}
    res = {(c, h): _dot_nt(jnp.concatenate([kb[c, h].astype(BF16), q_ref[0, rows[c], hs[h]]], axis=0), kf[c, h])
           for c, h in units}
    a = {u: jnp.where(strict, res[u][:C] * decay[u], 0.0) for u in units}
    intra = {u: (res[u][C:] * decay[u]).astype(BF16) for u in units}
    tinv = {u: eye - jnp.where(levels[0], a[u], 0.0) for u in units}
    for lg, m in enumerate(levels[1:], start=1):
        size = 1 << lg
        moving = [r0 for r0 in range(0, C, size) if ((r0 >> lg) & 1) == (0 if rev else 1)]
        tb = {u: tinv[u].astype(BF16) for u in units}
        if size < SUBLANES:
            pm = {u: _dot(tb[u], jnp.where(m, a[u], 0.0)) for u in units}
            tinv = {u: tinv[u] - _dot(pm[u], tb[u]) for u in units}
            continue
        lhs = {u: jnp.concatenate([tinv[u][r0:r0 + size] for r0 in moving], axis=0) for u in units}
        pm = {u: _dot(lhs[u], jnp.where(m, a[u], 0.0)) for u in units}
        new = {u: lhs[u] - _dot(pm[u], tb[u]) for u in units}
        tinv = {u: jnp.concatenate(
            [new[u][moving.index(r0) * size:(moving.index(r0) + 1) * size] if r0 in moving
             else tinv[u][r0:r0 + size] for r0 in range(0, C, size)], axis=0) for u in units}
    e_col = {u: jnp.exp(g_col[u]) for u in units}
    sol = {(c, h): _dot(tinv[c, h], jnp.concatenate(
        [v_ref[0, rows[c], hs[h]].astype(F32) * b_col[c, h], kb[c, h] * e_col[c, h]], axis=1)) for c, h in units}
    qd = {(c, h): (q_ref[0, rows[c], hs[h]].astype(F32) * e_col[c, h]).astype(BF16) for c, h in units}
    post = {u: jnp.concatenate([intra[u], (k[u] * jnp.exp(g_last[u] - g_col[u])).T.astype(BF16)], axis=0)
            for u in units}
    state = [state_ref[h] for h in heads]
    for c in chunks:
        sb = [state[h].astype(BF16) for h in heads]
        r2 = [_dot(jnp.concatenate([sol[c, h][:, LANES:].astype(BF16), qd[c, h]], axis=0), sb[h]) for h in heads]
        v_new = [(sol[c, h][:, :LANES] - r2[h][:C]).astype(BF16) for h in heads]
        r3 = [_dot(post[c, h], v_new[h]) for h in heads]
        for h in heads:
            o_ref[0, rows[c], hs[h]] = (r2[h][C:] + r3[h][:C]).astype(o_ref.dtype)
        state = [state[h] * jnp.exp(g_last[c, h]) + r3[h][C:] for h in heads]
    for h in heads:
        state_ref[h] = state[h]


def _gdn_scan(qkvn, col, row, nh, rev, nc=4):
    B, S, _ = qkvn.shape
    ts = nc * GDN_CHUNK
    nt = S // ts
    width = nh * LANES
    tmap = (lambda t: nt - 1 - t) if rev else (lambda t: t)
    col3 = col.reshape(B, S, LANES)
    return pl.pallas_call(
        functools.partial(_gdn_scan_kernel, rev=rev, nh=nh),
        grid=(B, nt),
        in_specs=[pl.BlockSpec((1, ts, width), lambda b, t: (b, tmap(t), 0)),
                  pl.BlockSpec((1, ts, width), lambda b, t: (b, tmap(t), 1)),
                  pl.BlockSpec((1, ts, width), lambda b, t: (b, tmap(t), 2)),
                  pl.BlockSpec((1, ts, LANES), lambda b, t: (b, tmap(t), 0)),
                  pl.BlockSpec((1, 4 * nh, ts), lambda b, t: (b, 0, tmap(t)))],
        out_specs=pl.BlockSpec((1, ts, width), lambda b, t: (b, tmap(t), 0)),
        out_shape=jax.ShapeDtypeStruct((B, S, width), BF16),
        scratch_shapes=[pltpu.VMEM((nh, GDN_DK, LANES), F32)],
        compiler_params=_params("parallel", "arbitrary"),
        name="gdn_scan_bwd" if rev else "gdn_scan_fwd",
    )(qkvn, qkvn, qkvn, col3, row)


MLP_CHUNK = 512


def _anchor(v):
    folded = jnp.sum(v.reshape(v.shape[0] // SUBLANES, SUBLANES, LANES), axis=0)
    bits = lax.shift_right_logical(pltpu.bitcast(folded, jnp.uint32), jnp.uint32(32))
    return pltpu.bitcast(bits, F32)


def _out_mlp_tail(y_ref, x_ref, wo_ref, g_ref, w1_ref, w2_ref, gf_ref, o_ref, x1_ref, acc_ref, final,
                  anchors=()):
    x1_ref[...] = x_ref[...] + jnp.dot(y_ref[...], wo_ref[...], preferred_element_type=F32)
    h = _rms(x1_ref[...], g_ref[...]).astype(BF16)
    starts = list(range(0, w1_ref.shape[1], MLP_CHUNK))
    for n, c in enumerate(starts):
        a = jnp.maximum(jnp.dot(h, w1_ref[:, c:c + MLP_CHUNK], preferred_element_type=F32), 0.0)
        part = jnp.dot((a * a).astype(BF16), w2_ref[c:c + MLP_CHUNK, :], preferred_element_type=F32)
        if c == 0:
            acc_ref[...] = part
        else:
            acc_ref[...] += part
        for k in range(n, len(anchors), len(starts)):
            acc_ref[0:SUBLANES, 0:LANES] += anchors[k]
    out = x1_ref[...] + acc_ref[...]
    if final:
        out = _rms(out, gf_ref[...])
    o_ref[...] = out


def _mixer_out_mlp(prologue, y_ref, tail_refs, final, lookahead):
    if not lookahead:
        prologue(y_ref.at[0])
        _out_mlp_tail(y_ref.at[0], *tail_refs, final)
        return
    i = pl.program_id(0)

    @pl.when(i == 0)
    def _():
        y_ref[...] = jnp.zeros_like(y_ref)

    for parity in range(2):
        @pl.when(i % 2 == parity)
        def _():
            pieces = prologue(y_ref.at[parity])
            _out_mlp_tail(y_ref.at[1 - parity], *tail_refs, final, anchors=[_anchor(v) for v in pieces])


def _out_mlp_call(kernel_fn, mixer_args, mixer_specs, mixer_scratch, x2, w_out, g, w1, w2,
                  g_final, tm, name, lookahead):
    T, D = x2.shape
    n = T // tm
    if lookahead:
        cur = lambda i: jnp.minimum(i, n - 1)
        prev = lambda i: jnp.maximum(i - 1, 0)
    else:
        cur = prev = lambda i: i
    once = lambda shape: pl.BlockSpec(shape, lambda i: (0, 0), pipeline_mode=pl.Buffered(1))
    return pl.pallas_call(
        kernel_fn,
        grid=(n + 1 if lookahead else n,),
        in_specs=mixer_specs(cur) + [pl.BlockSpec((tm, D), lambda i: (prev(i), 0)), once(w_out.shape),
                                     once((1, D)), once(w1.shape), once(w2.shape), once((1, D))],
        out_specs=pl.BlockSpec((tm, D), lambda i: (prev(i), 0)),
        out_shape=jax.ShapeDtypeStruct((T, D), F32),
        scratch_shapes=mixer_scratch + [pltpu.VMEM((2 if lookahead else 1, tm, w_out.shape[0]), BF16),
                                        pltpu.VMEM((tm, D), F32), pltpu.VMEM((tm, D), F32)],
        compiler_params=_params("arbitrary" if lookahead else "parallel"),
        name=name,
    )(*mixer_args, x2, w_out, g.reshape(1, D), w1, w2, g_final.reshape(1, D))


def _gdn_out_mlp_kernel(of_ref, ob_ref, z_ref, nw_ref, x_ref, wo_ref, g_ref, w1_ref, w2_ref, gf_ref,
                        o_ref, y_ref, x1_ref, acc_ref, *, nh, final):
    def prologue(y):
        pieces = []
        for h in range(nh):
            hs = slice(h * LANES, (h + 1) * LANES)
            o = of_ref[:, hs].astype(F32) + ob_ref[:, hs].astype(F32)
            o = o * lax.rsqrt(jnp.mean(o * o, axis=-1, keepdims=True) + RMS_EPS)
            z = z_ref[:, hs].astype(F32)
            pieces.append(o * nw_ref[...] * (z * _sigmoid(z)))
            y[:, hs] = pieces[-1].astype(BF16)
        return pieces
    _mixer_out_mlp(prologue, y_ref, (x_ref, wo_ref, g_ref, w1_ref, w2_ref, gf_ref, o_ref, x1_ref, acc_ref),
                   final, lookahead=True)


def _gdn_out_mlp(o_f, o_b, z, x2, norm_w, w_out, nh, g, w1, w2, g_final, final, tm=512):
    W = o_f.shape[1]
    specs = lambda cur: [pl.BlockSpec((tm, W), lambda i: (cur(i), 0))] * 3 + \
                        [pl.BlockSpec((1, LANES), lambda i: (0, 0))]
    return _out_mlp_call(
        functools.partial(_gdn_out_mlp_kernel, nh=nh, final=final),
        (o_f, o_b, z, norm_w.reshape(1, LANES)), specs,
        [], x2, w_out, g, w1, w2, g_final, tm, "gdn_out_mlp", lookahead=True)


def _dswa_kernel(q_ref, k_ref, kp_ref, kn_ref, v_ref, vp_ref, vn_ref, bias_ref,
                 o_ref, lse_ref, *, sub_len):
    t = pl.program_id(2)
    rb, lt = q_ref.shape[1], q_ref.shape[2]
    hb, qn = DSWA_HALF, DSWA_QBLOCK
    kw = qn + 2 * hb
    first_head = lax.broadcasted_iota(jnp.int32, (1, LANES), 1) < DSWA_HEAD_DIM
    n_pairs = q_ref.shape[3] // LANES
    blocks = range(lt // qn)
    units = [(rc, qb, pr) for rc in range(rb) for qb in blocks for pr in range(n_pairs)]
    rows = [slice(qb * qn, (qb + 1) * qn) for qb in blocks]
    win = [slice(qb * qn, qb * qn + kw) for qb in blocks]
    ls = [slice(pr * LANES, (pr + 1) * LANES) for pr in range(n_pairs)]
    kext = [jnp.concatenate([kp_ref[0, rc], k_ref[0, rc], kn_ref[0, rc]], axis=0) for rc in range(rb)]
    vext = [jnp.concatenate([vp_ref[0, rc], v_ref[0, rc], vn_ref[0, rc]], axis=0) for rc in range(rb)]
    kpos = lambda qb: t * lt + qb * qn - hb + lax.broadcasted_iota(jnp.int32, (1, kw), 1)
    qs = {(rc, qb, pr): jnp.concatenate([jnp.where(first_head, q_ref[0, rc, rows[qb], ls[pr]], 0.0),
                                         jnp.where(first_head, 0.0, q_ref[0, rc, rows[qb], ls[pr]])], axis=0)
          for rc, qb, pr in units}
    s = {(rc, qb, pr): _dot_nt(qs[rc, qb, pr], kext[rc][win[qb], ls[pr]]) + bias_ref[pr] for rc, qb, pr in units}
    for rc, qb, pr in units:
        if qb == blocks[0]:
            s[rc, qb, pr] = jnp.where(kpos(qb) >= 0, s[rc, qb, pr], NEG_INF)
        if qb == blocks[-1]:
            s[rc, qb, pr] = jnp.where(kpos(qb) < sub_len, s[rc, qb, pr], NEG_INF)
    m = {u: jnp.max(s[u], axis=-1, keepdims=True) for u in units}
    p = {u: jnp.exp2(s[u] - m[u]) for u in units}
    l = {u: jnp.sum(p[u], axis=-1, keepdims=True) for u in units}
    pv = {(rc, qb, pr): _dot(p[rc, qb, pr], vext[rc][win[qb], ls[pr]]) * (1.0 / l[rc, qb, pr])
          for rc, qb, pr in units}
    lse = {u: (m[u] + jnp.log2(l[u])) * math.log(2.0) for u in units}
    for rc, qb, pr in units:
        o_ref[0, rc, rows[qb], ls[pr]] = jnp.where(first_head, pv[rc, qb, pr][:qn], pv[rc, qb, pr][qn:])
        lse_ref[0, rc, rows[qb], ls[pr]] = jnp.where(first_head, lse[rc, qb, pr][:qn], lse[rc, qb, pr][qn:])


def _rel_bucket(rel):
    nb = REL_BUCKETS // 2
    max_exact = nb // 2
    ret = jnp.where(rel > 0, nb, 0)
    n = jnp.abs(rel)
    nf = jnp.maximum(n, 1).astype(F32)
    large = max_exact + (jnp.log(nf * (1.0 / max_exact)) / math.log(REL_MAX_DIST / max_exact)
                         * (nb - max_exact)).astype(jnp.int32)
    large = jnp.minimum(large, nb - 1)
    return ret + jnp.where(n < max_exact, n, large)


def _band_bias_kernel(tab_ref, o_ref, *, dils, hg):
    hb, qn = DSWA_HALF, DSWA_QBLOCK
    kw = qn + 2 * hb
    off = (lax.broadcasted_iota(jnp.int32, (qn, kw), 1) - hb
           - lax.broadcasted_iota(jnp.int32, (qn, kw), 0))
    inband = jnp.abs(off) <= hb
    for g, d in enumerate(dils):
        bucket = _rel_bucket(off * d)
        for hh in range(hg):
            acc = jnp.zeros((qn, kw), F32)
            for b in range(REL_BUCKETS):
                acc = jnp.where(bucket == b, tab_ref[b, g * hg + hh], acc)
            o_ref[g, hh // 2, (hh % 2) * qn:(hh % 2 + 1) * qn, :] = jnp.where(inband, acc * LOG2E, NEG_INF)


def _band_bias(rel_table, dils):
    qn = DSWA_QBLOCK
    hg = rel_table.shape[1] // len(dils)
    return pl.pallas_call(
        functools.partial(_band_bias_kernel, dils=dils, hg=hg),
        in_specs=[pl.BlockSpec(memory_space=pltpu.SMEM)],
        out_shape=jax.ShapeDtypeStruct((len(dils), hg // 2, 2 * qn, qn + 2 * DSWA_HALF), F32),
        name="band_bias",
    )(rel_table)


def _dswa_group(qkv, bias, lt=1024):
    B, dil, L, W3 = qkv.shape
    hb = DSWA_HALF
    gw = W3 // 3
    rb = max(1, lt // L)
    lt = min(lt, L)
    r = lt // hb
    main = lambda which: pl.BlockSpec((1, rb, lt, gw), lambda b, d, t: (b, d, t, which))
    prev = lambda which: pl.BlockSpec(
        (1, rb, hb, gw), lambda b, d, t: (b, d, jnp.maximum(t * r - 1, 0), which))
    nxt = lambda which: pl.BlockSpec(
        (1, rb, hb, gw), lambda b, d, t: (b, d, jnp.minimum((t + 1) * r, L // hb - 1), which))
    out = pl.BlockSpec((1, rb, lt, gw), lambda b, d, t: (b, d, t, 0))
    return pl.pallas_call(
        functools.partial(_dswa_kernel, sub_len=L),
        grid=(B, dil // rb, L // lt),
        in_specs=[main(0), main(1), prev(1), nxt(1), main(2), prev(2), nxt(2),
                  pl.BlockSpec(bias.shape, lambda b, d, t: (0, 0, 0))],
        out_specs=[out, out],
        out_shape=[jax.ShapeDtypeStruct((B, dil, L, gw), F32)] * 2,
        compiler_params=_params("parallel", "parallel", "parallel"),
        name=f"dswa_d{dil}",
    )(qkv, qkv, qkv, qkv, qkv, qkv, qkv, bias)


def _dswa_out_mlp_kernel(*refs, dils, final):
    ng = len(dils)
    o_refs, l_refs = refs[:ng], refs[ng:2 * ng]
    x_ref, wo_ref, g_ref, w1_ref, w2_ref, gf_ref, out_ref, o_s, l_s, y_s, x1_ref, acc_ref = refs[2 * ng:]
    tm = x_ref.shape[0]
    tiles = o_refs[0].shape[3] // LANES
    def prologue(y):
        for g in range(ng):
            for j in range(tiles):
                ls = slice(j * LANES, (j + 1) * LANES)
                o_s[g * tiles + j] = pltpu.einshape("drl->(rd)l", o_refs[g][0, :, :, ls])
                l_s[g * tiles + j] = pltpu.einshape("drl->(rd)l", l_refs[g][0, :, :, ls])
        pieces = []
        for j in range(tiles):
            lses = [l_s[g * tiles + j] for g in range(ng)]
            m = functools.reduce(jnp.maximum, lses)
            es = [jnp.exp(l - m) for l in lses]
            inv = 1.0 / functools.reduce(lambda a, b: a + b, es)
            for g in range(ng):
                c = g * tiles + j
                pieces.append(o_s[c] * (es[g] * inv))
                y[:, c * LANES:(c + 1) * LANES] = pieces[-1].astype(BF16)
        return pieces
    _mixer_out_mlp(prologue, y_s, (x_ref, wo_ref, g_ref, w1_ref, w2_ref, gf_ref, out_ref, x1_ref, acc_ref),
                   final, lookahead=False)


def _dswa_out_mlp(os_, lses, x2, w_out, S, g, w1, w2, g_final, final, tm=512):
    dils = tuple(o.shape[1] for o in os_)
    gw = os_[0].shape[3]
    nt = S // tm
    specs = lambda cur: [pl.BlockSpec((1, d, tm // d, gw), lambda i: (cur(i) // nt, 0, cur(i) % nt, 0))
                         for d in dils] * 2
    staging = pltpu.VMEM((len(dils) * gw // LANES, tm, LANES), F32)
    return _out_mlp_call(
        functools.partial(_dswa_out_mlp_kernel, dils=dils, final=final),
        (*os_, *lses), specs, [staging, staging],
        x2, w_out, g, w1, w2, g_final, tm, "dswa_out_mlp", lookahead=False)


def _gdn_layer(x2, g, w_in, conv_w, a_log, dt_bias, norm_w, w_out, B, S, mlp):
    nh = a_log.shape[1]
    z_w = nh * LANES
    qkvn, z, col, row = _gdn_proj(x2, g, w_in, conv_w, a_log, dt_bias, B, S)
    qkvn = qkvn.reshape(B, S, 3 * z_w)
    o_f = _gdn_scan(qkvn, col, row, nh, rev=False)
    o_b = _gdn_scan(qkvn, col, row, nh, rev=True)
    return _gdn_out_mlp(o_f.reshape(B * S, z_w), o_b.reshape(B * S, z_w), z, x2, norm_w,
                        w_out.astype(BF16), nh, *mlp)


def _dswa_proj_kernel(x_ref, g_ref, w_ref, *refs, dils, chunk):
    o_refs = refs
    n = w_ref.shape[1]
    ng = len(dils)
    gw = n // (3 * ng)
    tiles = gw // LANES
    h = _rms(x_ref[...], g_ref[...]).astype(BF16)

    def project(which):
        res = jnp.dot(h, w_ref[:, which * chunk:(which + 1) * chunk], preferred_element_type=F32)
        if which == 0:
            res = res * (DSWA_HEAD_DIM ** -0.5 * LOG2E)
        return res

    def regroup(which, res):
        for g, d in enumerate(dils):
            for j in range(tiles):
                src = slice((g * tiles + j) * LANES, (g * tiles + j + 1) * LANES)
                dst = slice((which * tiles + j) * LANES, (which * tiles + j + 1) * LANES)
                o_refs[g][0, :, :, dst] = pltpu.einshape("(rd)l->drl", res[:, src], d=d).astype(BF16)

    res = project(0)
    for which in range(1, 3):
        nxt = project(which)
        regroup(which - 1, res)
        res = nxt
    regroup(2, res)


def _dswa_proj(x2, g, w_in, dils, B, S, tm=1024):
    T, D = x2.shape
    n = w_in.shape[1]
    gw3 = n // len(dils)
    nt = S // tm
    return pl.pallas_call(
        functools.partial(_dswa_proj_kernel, dils=dils, chunk=gw3),
        grid=(T // tm,),
        in_specs=[pl.BlockSpec((tm, D), lambda i: (i, 0)),
                  pl.BlockSpec((1, D), lambda i: (0, 0)),
                  pl.BlockSpec((D, n), lambda i: (0, 0))],
        out_specs=[pl.BlockSpec((1, d, tm // d, gw3), lambda i: (i // nt, 0, i % nt, 0)) for d in dils],
        out_shape=[jax.ShapeDtypeStruct((B, d, S // d, gw3), BF16) for d in dils],
        compiler_params=_params("parallel"),
        name="dswa_proj",
    )(x2, g.reshape(1, D), w_in)


def _dswa_layer(x2, g, w_in, w_out, bias, B, S, mlp):
    dils = tuple(d for _, d in DSWA_CONFIGS)
    qkvs = _dswa_proj(x2, g, w_in.astype(BF16), dils, B, S)
    os_, lses = [], []
    for gi in range(len(dils)):
        o, lse = _dswa_group(qkvs[gi], bias[gi])
        os_.append(o)
        lses.append(lse)
    return _dswa_out_mlp(os_, lses, x2, w_out.astype(BF16), S, *mlp)


def kernel(x, norm_mix, norm_mlp, norm_final, rel_bias, gdn_w_in, gdn_conv_w, gdn_a_log,
           gdn_dt_bias, gdn_norm_w, gdn_w_out, dswa_w_in, dswa_w_out, mlp_w1, mlp_w2):
    B, S, D = x.shape
    depth = norm_mix.shape[0]
    x2 = x.reshape(B * S, D)
    assert all(window // (2 * dil) == DSWA_HALF for window, dil in DSWA_CONFIGS)
    bias = _band_bias(rel_bias, tuple(d for _, d in DSWA_CONFIGS))
    for i in range(depth):
        j = i // 2
        mlp = (norm_mlp[i], mlp_w1[i].astype(BF16), mlp_w2[i].astype(BF16), norm_final, i == depth - 1)
        if i % 2 == 0:
            x2 = _gdn_layer(x2, norm_mix[i], gdn_w_in[j], gdn_conv_w[j], gdn_a_log[j],
                            gdn_dt_bias[j], gdn_norm_w[j], gdn_w_out[j], B, S, mlp)
        else:
            x2 = _dswa_layer(x2, norm_mix[i], dswa_w_in[j], dswa_w_out[j], bias, B, S, mlp)
    return x2.reshape(B, S, D)
```

```python
import functools
import math

import jax
import jax.numpy as jnp
from jax import lax
from jax.experimental import pallas as pl
from jax.experimental.pallas import tpu as pltpu

F32 = jnp.float32
BF16 = jnp.bfloat16

RMS_EPS = 1e-6
L2_EPS = 1e-6
NEG_INF = -1e30
LOG2E = math.log2(math.e)

LANES = 128
SUBLANES = 8
VMEM_LIMIT = 56 * 1024 * 1024

GDN_DK = 128
GDN_CONV = 5
GDN_CHUNK = LANES
DSWA_HEAD_DIM = 64
DSWA_CONFIGS = ((128, 1), (512, 4), (2048, 16))
DSWA_HALF = 64
DSWA_QBLOCK = 128
REL_BUCKETS = 32
REL_MAX_DIST = 1024


def _params(*sem):
    return pltpu.CompilerParams(dimension_semantics=sem, vmem_limit_bytes=VMEM_LIMIT)


def _dot(a, b):
    return jnp.dot(a.astype(BF16), b.astype(BF16), preferred_element_type=F32)


def _dot_nt(a, b):
    return lax.dot_general(a.astype(BF16), b.astype(BF16), (((1,), (1,)), ((), ())),
                           preferred_element_type=F32)


def _split3(x):
    hi = x.astype(BF16)
    r = x - hi.astype(F32)
    mid = r.astype(BF16)
    return hi, mid, (r - mid.astype(F32)).astype(BF16)


def _rms(x, g):
    return x * lax.rsqrt(jnp.mean(x * x, axis=-1, keepdims=True) + RMS_EPS) * g


def _sigmoid(x):
    return 1.0 / (1.0 + jnp.exp(-x))


def _softplus(x):
    return jnp.maximum(x, 0.0) + jnp.log1p(jnp.exp(-jnp.abs(x)))


CONV_HALO = 16
CONV_BLOCK = 64


def _gdn_proj_kernel(x_ref, xp_ref, xn_ref, g_ref, wqkv_ref, wz_ref, w_ref, cw_ref,
                     prow_ref, qkv_ref, z_ref, col_ref, row_ref, *, nh, nt, chunk):
    C = GDN_CHUNK
    tm = x_ref.shape[0]
    t = pl.program_id(0) % nt
    pad = GDN_CONV // 2
    g = g_ref[...]
    h = _rms(x_ref[...], g).astype(BF16)
    h_halo = jnp.concatenate([_rms(jnp.where(t > 0, xp_ref[...], 0.0), g).astype(BF16),
                              _rms(jnp.where(t < nt - 1, xn_ref[...], 0.0), g).astype(BF16)], axis=0)
    CB = CONV_BLOCK
    taps = [i for i in range(GDN_CONV) if i != pad]
    sr = lax.broadcasted_iota(jnp.int32, (len(taps) * CB, CB + 2 * CONV_HALO), 0)
    sc = lax.broadcasted_iota(jnp.int32, (len(taps) * CB, CB + 2 * CONV_HALO), 1)
    src = (sr % CB) + (CONV_HALO - pad)
    for k, i in enumerate(taps):
        src = src + jnp.where(sr // CB == k, i, 0)
    shift_mat = (sc == src).astype(BF16)

    def project(c0):
        w = wqkv_ref[:, c0:c0 + chunk]
        main = jnp.dot(h, w, preferred_element_type=F32)
        halo = jnp.dot(h_halo, w, preferred_element_type=F32)
        ext = jnp.concatenate([halo[:CONV_HALO], main, halo[CONV_HALO:]], axis=0).astype(BF16)
        return main, ext

    starts = list(range(0, wqkv_ref.shape[1], chunk))
    nxt = project(starts[0])
    for n, c0 in enumerate(starts):
        cs = slice(c0, c0 + chunk)
        main, ext = nxt
        if n + 1 < len(starts):
            nxt = project(starts[n + 1])
        for b in range(tm // CB):
            rows = slice(b * CB, (b + 1) * CB)
            shifted = jnp.dot(shift_mat, ext[b * CB:(b + 1) * CB + 2 * CONV_HALO],
                              preferred_element_type=F32)
            acc = cw_ref[pad:pad + 1, cs] * main[rows]
            for k, i in enumerate(taps):
                acc = acc + cw_ref[i:i + 1, cs] * shifted[k * CB:(k + 1) * CB]
            y = acc * _sigmoid(acc)
            for j in range(chunk // LANES):
                yj = y[:, j * LANES:(j + 1) * LANES]
                head = c0 // LANES + j
                if head < 2 * nh:
                    inv = lax.rsqrt(jnp.sum(yj * yj, axis=-1, keepdims=True) + L2_EPS)
                    yj = yj * (inv * (GDN_DK ** -0.5) if head < nh else inv)
                qkv_ref[rows, c0 + j * LANES:c0 + (j + 1) * LANES] = yj.astype(BF16)
    for c0 in range(0, wz_ref.shape[1], chunk):
        cs = slice(c0, c0 + chunk)
        z_ref[:, cs] = jnp.dot(h, wz_ref[:, cs], preferred_element_type=F32).astype(BF16)
    a_c = jnp.dot(h, w_ref[...], preferred_element_type=F32)
    ri = lax.broadcasted_iota(jnp.int32, (C, C), 0)
    ci = lax.broadcasted_iota(jnp.int32, (C, C), 1)
    lower = (ri >= ci).astype(BF16)
    g_c = -jnp.exp(prow_ref[0:1, :]) * _softplus(a_c + prow_ref[1:2, :])
    beta_c = _sigmoid(a_c)
    for c in range(tm // C):
        sl = slice(c * C, (c + 1) * C)
        gc = g_c[sl, :]
        pre = sum(jnp.dot(lower, part, preferred_element_type=F32) for part in _split3(gc))
        cum = jnp.where(ci < nh, pre, pre[C - 1:C, :] - pre + gc)
        both = jnp.where(ci < 2 * nh, cum, beta_c[sl, :])
        col_ref[sl, :] = both
        row_ref[0, :, sl] = both.T[:4 * nh, :]


def _gdn_proj(x2, g, w_in, conv_w, a_log, dt_bias, B, S, tm=1024, chunk=256):
    T, D = x2.shape
    nh = a_log.shape[1]
    qkv_w = 3 * nh * LANES
    z_w = nh * LANES
    w_pad = jnp.zeros((D, LANES), F32).at[:, :4 * nh].set(w_in[:, qkv_w + z_w:])
    alog = jnp.zeros((LANES,), F32).at[:2 * nh].set(a_log.reshape(-1))
    dtb = jnp.zeros((LANES,), F32).at[:2 * nh].set(dt_bias.reshape(-1))
    prow = jnp.zeros((SUBLANES, LANES), F32).at[0].set(alog).at[1].set(dtb)
    nt = S // tm
    r = tm // CONV_HALO
    const = lambda shape: pl.BlockSpec(shape, lambda i: (0, 0))
    return pl.pallas_call(
        functools.partial(_gdn_proj_kernel, nh=nh, nt=nt, chunk=chunk),
        grid=(T // tm,),
        in_specs=[pl.BlockSpec((tm, D), lambda i: (i, 0)),
                  pl.BlockSpec((CONV_HALO, D), lambda i: (jnp.maximum(i * r - 1, 0), 0)),
                  pl.BlockSpec((CONV_HALO, D), lambda i: (jnp.minimum((i + 1) * r, T // CONV_HALO - 1), 0)),
                  const((1, D)), const((D, qkv_w)), const((D, z_w)), const((D, LANES)),
                  const((GDN_CONV, qkv_w)), const((SUBLANES, LANES))],
        out_specs=[pl.BlockSpec((tm, qkv_w), lambda i: (i, 0)),
                   pl.BlockSpec((tm, z_w), lambda i: (i, 0)),
                   pl.BlockSpec((tm, LANES), lambda i: (i, 0)),
                   pl.BlockSpec((1, 4 * nh, tm), lambda i: (i // nt, 0, i % nt))],
        out_shape=[jax.ShapeDtypeStruct((T, qkv_w), BF16),
                   jax.ShapeDtypeStruct((T, z_w), BF16),
                   jax.ShapeDtypeStruct((T, LANES), F32),
                   jax.ShapeDtypeStruct((B, 4 * nh, S), F32)],
        compiler_params=_params("parallel"),
        name="gdn_proj",
    )(x2, x2, x2, g.reshape(1, D), w_in[:, :qkv_w].astype(BF16), w_in[:, qkv_w:qkv_w + z_w].astype(BF16),
      w_pad.astype(BF16), conv_w.reshape(GDN_CONV, qkv_w), prow)


def _gdn_scan_kernel(q_ref, k_ref, v_ref, col_ref, row_ref, o_ref, state_ref, *, rev, nh):
    C = GDN_CHUNK
    t = pl.program_id(1)

    @pl.when(t == 0)
    def _():
        state_ref[...] = jnp.zeros_like(state_ref)

    ri = lax.broadcasted_iota(jnp.int32, (C, C), 0)
    ci = lax.broadcasted_iota(jnp.int32, (C, C), 1)
    incl = (ri <= ci) if rev else (ri >= ci)
    strict = (ri < ci) if rev else (ri > ci)
    eye = (ri == ci).astype(F32)
    levels = [((ri >> (lg + 1)) == (ci >> (lg + 1))) & ((ri >> lg) != (ci >> lg))
              for lg in range(int(math.log2(C)))]
    last = 0 if rev else C - 1
    nc = q_ref.shape[1] // C
    chunks = list(reversed(range(nc))) if rev else list(range(nc))
    heads = range(nh)
    hs = [slice(h * LANES, (h + 1) * LANES) for h in heads]
    gi = [(nh if rev else 0) + h for h in heads]
    units = [(c, h) for c in chunks for h in heads]
    rows = {c: slice(c * C, (c + 1) * C) for c in chunks}
    col = {c: col_ref[0, rows[c], :] for c in chunks}
    row = {c: row_ref[0, :, rows[c]] for c in chunks}
    kf = {u: k_ref[0, rows[u[0]], hs[u[1]]] for u in units}
    k = {u: kf[u].astype(F32) for u in units}
    g_col = {(c, h): col[c][:, gi[h]:gi[h] + 1] for c, h in units}
    b_col = {(c, h): col[c][:, 2 * nh + gi[h]:2 * nh + gi[h] + 1] for c, h in units}
    g_last = {u: g_col[u][last:last + 1, :] for u in units}
    decay = {(c, h): jnp.where(incl, jnp.exp(jnp.where(incl, g_col[c, h] - row[c][gi[h]:gi[h] + 1, :], 0.0)), 0.0)
             for c, h in units}
    kb = {u: k[u] * b_col[u] for u in units}
    res = {(c, h): _dot_nt(jnp.concatenate([kb[c, h].astype(BF16), q_ref[0, rows[c], hs[h]]], axis=0), kf[c, h])
           for c, h in units}
    a = {u: jnp.where(strict, res[u][:C] * decay[u], 0.0) for u in units}
    intra = {u: (res[u][C:] * decay[u]).astype(BF16) for u in units}
    tinv = {u: eye - jnp.where(levels[0], a[u], 0.0) for u in units}
    for lg, m in enumerate(levels[1:], start=1):
        size = 1 << lg
        moving = [r0 for r0 in range(0, C, size) if ((r0 >> lg) & 1) == (0 if rev else 1)]
        tb = {u: tinv[u].astype(BF16) for u in units}
        if size < SUBLANES:
            pm = {u: _dot(tb[u], jnp.where(m, a[u], 0.0)) for u in units}
            tinv = {u: tinv[u] - _dot(pm[u], tb[u]) for u in units}
            continue
        lhs = {u: jnp.concatenate([tinv[u][r0:r0 + size] for r0 in moving], axis=0) for u in units}
        pm = {u: _dot(lhs[u], jnp.where(m, a[u], 0.0)) for u in units}
        new = {u: lhs[u] - _dot(pm[u], tb[u]) for u in units}
        tinv = {u: jnp.concatenate(
            [new[u][moving.index(r0) * size:(moving.index(r0) + 1) * size] if r0 in moving
             else tinv[u][r0:r0 + size] for r0 in range(0, C, size)], axis=0) for u in units}
    e_col = {u: jnp.exp(g_col[u]) for u in units}
    sol = {(c, h): _dot(tinv[c, h], jnp.concatenate(
        [v_ref[0, rows[c], hs[h]].astype(F32) * b_col[c, h], kb[c, h] * e_col[c, h]], axis=1)) for c, h in units}
    qd = {(c, h): (q_ref[0, rows[c], hs[h]].astype(F32) * e_col[c, h]).astype(BF16) for c, h in units}
    post = {u: jnp.concatenate([intra[u], (k[u] * jnp.exp(g_last[u] - g_col[u])).T.astype(BF16)], axis=0)
            for u in units}
    state = [state_ref[h] for h in heads]
    for c in chunks:
        sb = [state[h].astype(BF16) for h in heads]
        r2 = [_dot(jnp.concatenate([sol[c, h][:, LANES:].astype(BF16), qd[c, h]], axis=0), sb[h]) for h in heads]
        v_new = [(sol[c, h][:, :LANES] - r2[h][:C]).astype(BF16) for h in heads]
        r3 = [_dot(post[c, h], v_new[h]) for h in heads]
        for h in heads:
            o_ref[0, rows[c], hs[h]] = (r2[h][C:] + r3[h][:C]).astype(o_ref.dtype)
        state = [state[h] * jnp.exp(g_last[c, h]) + r3[h][C:] for h in heads]
    for h in heads:
        state_ref[h] = state[h]


def _gdn_scan(qkvn, col, row, nh, rev, nc=4):
    B, S, _ = qkvn.shape
    ts = nc * GDN_CHUNK
    nt = S // ts
    width = nh * LANES
    tmap = (lambda t: nt - 1 - t) if rev else (lambda t: t)
    col3 = col.reshape(B, S, LANES)
    return pl.pallas_call(
        functools.partial(_gdn_scan_kernel, rev=rev, nh=nh),
        grid=(B, nt),
        in_specs=[pl.BlockSpec((1, ts, width), lambda b, t: (b, tmap(t), 0)),
                  pl.BlockSpec((1, ts, width), lambda b, t: (b, tmap(t), 1)),
                  pl.BlockSpec((1, ts, width), lambda b, t: (b, tmap(t), 2)),
                  pl.BlockSpec((1, ts, LANES), lambda b, t: (b, tmap(t), 0)),
                  pl.BlockSpec((1, 4 * nh, ts), lambda b, t: (b, 0, tmap(t)))],
        out_specs=pl.BlockSpec((1, ts, width), lambda b, t: (b, tmap(t), 0)),
        out_shape=jax.ShapeDtypeStruct((B, S, width), BF16),
        scratch_shapes=[pltpu.VMEM((nh, GDN_DK, LANES), F32)],
        compiler_params=_params("parallel", "arbitrary"),
        name="gdn_scan_bwd" if rev else "gdn_scan_fwd",
    )(qkvn, qkvn, qkvn, col3, row)


MLP_CHUNK = 512


def _anchor(v):
    folded = jnp.sum(v.reshape(v.shape[0] // SUBLANES, SUBLANES, LANES), axis=0)
    bits = lax.shift_right_logical(pltpu.bitcast(folded, jnp.uint32), jnp.uint32(32))
    return pltpu.bitcast(bits, F32)


def _out_mlp_tail(y_ref, x_ref, wo_ref, g_ref, w1_ref, w2_ref, gf_ref, o_ref, x1_ref, acc_ref, final,
                  anchors=()):
    x1_ref[...] = x_ref[...] + jnp.dot(y_ref[...], wo_ref[...], preferred_element_type=F32)
    h = _rms(x1_ref[...], g_ref[...]).astype(BF16)
    starts = list(range(0, w1_ref.shape[1], MLP_CHUNK))
    for n, c in enumerate(starts):
        a = jnp.maximum(jnp.dot(h, w1_ref[:, c:c + MLP_CHUNK], preferred_element_type=F32), 0.0)
        part = jnp.dot((a * a).astype(BF16), w2_ref[c:c + MLP_CHUNK, :], preferred_element_type=F32)
        if c == 0:
            acc_ref[...] = part
        else:
            acc_ref[...] += part
        for k in range(n, len(anchors), len(starts)):
            acc_ref[0:SUBLANES, 0:LANES] += anchors[k]
    out = x1_ref[...] + acc_ref[...]
    if final:
        out = _rms(out, gf_ref[...])
    o_ref[...] = out


def _mixer_out_mlp(prologue, y_ref, tail_refs, final, lookahead):
    if not lookahead:
        prologue(y_ref.at[0])
        _out_mlp_tail(y_ref.at[0], *tail_refs, final)
        return
    i = pl.program_id(0)

    @pl.when(i == 0)
    def _():
        y_ref[...] = jnp.zeros_like(y_ref)

    for parity in range(2):
        @pl.when(i % 2 == parity)
        def _():
            pieces = prologue(y_ref.at[parity])
            _out_mlp_tail(y_ref.at[1 - parity], *tail_refs, final, anchors=[_anchor(v) for v in pieces])


def _out_mlp_call(kernel_fn, mixer_args, mixer_specs, mixer_scratch, x2, w_out, g, w1, w2,
                  g_final, tm, name, lookahead):
    T, D = x2.shape
    n = T // tm
    if lookahead:
        cur = lambda i: jnp.minimum(i, n - 1)
        prev = lambda i: jnp.maximum(i - 1, 0)
    else:
        cur = prev = lambda i: i
    once = lambda shape: pl.BlockSpec(shape, lambda i: (0, 0), pipeline_mode=pl.Buffered(1))
    return pl.pallas_call(
        kernel_fn,
        grid=(n + 1 if lookahead else n,),
        in_specs=mixer_specs(cur) + [pl.BlockSpec((tm, D), lambda i: (prev(i), 0)), once(w_out.shape),
                                     once((1, D)), once(w1.shape), once(w2.shape), once((1, D))],
        out_specs=pl.BlockSpec((tm, D), lambda i: (prev(i), 0)),
        out_shape=jax.ShapeDtypeStruct((T, D), F32),
        scratch_shapes=mixer_scratch + [pltpu.VMEM((2 if lookahead else 1, tm, w_out.shape[0]), BF16),
                                        pltpu.VMEM((tm, D), F32), pltpu.VMEM((tm, D), F32)],
        compiler_params=_params("arbitrary" if lookahead else "parallel"),
        name=name,
    )(*mixer_args, x2, w_out, g.reshape(1, D), w1, w2, g_final.reshape(1, D))


def _gdn_out_mlp_kernel(of_ref, ob_ref, z_ref, nw_ref, x_ref, wo_ref, g_ref, w1_ref, w2_ref, gf_ref,
                        o_ref, y_ref, x1_ref, acc_ref, *, nh, final):
    def prologue(y):
        pieces = []
        for h in range(nh):
            hs = slice(h * LANES, (h + 1) * LANES)
            o = of_ref[:, hs].astype(F32) + ob_ref[:, hs].astype(F32)
            o = o * lax.rsqrt(jnp.mean(o * o, axis=-1, keepdims=True) + RMS_EPS)
            z = z_ref[:, hs].astype(F32)
            pieces.append(o * nw_ref[...] * (z * _sigmoid(z)))
            y[:, hs] = pieces[-1].astype(BF16)
        return pieces
    _mixer_out_mlp(prologue, y_ref, (x_ref, wo_ref, g_ref, w1_ref, w2_ref, gf_ref, o_ref, x1_ref, acc_ref),
                   final, lookahead=True)


def _gdn_out_mlp(o_f, o_b, z, x2, norm_w, w_out, nh, g, w1, w2, g_final, final, tm=512):
    W = o_f.shape[1]
    specs = lambda cur: [pl.BlockSpec((tm, W), lambda i: (cur(i), 0))] * 3 + \
                        [pl.BlockSpec((1, LANES), lambda i: (0, 0))]
    return _out_mlp_call(
        functools.partial(_gdn_out_mlp_kernel, nh=nh, final=final),
        (o_f, o_b, z, norm_w.reshape(1, LANES)), specs,
        [], x2, w_out, g, w1, w2, g_final, tm, "gdn_out_mlp", lookahead=True)


def _dswa_kernel(q_ref, k_ref, kp_ref, kn_ref, v_ref, vp_ref, vn_ref, bias_ref,
                 o_ref, lse_ref, *, sub_len):
    t = pl.program_id(2)
    rb, lt = q_ref.shape[1], q_ref.shape[2]
    hb, qn = DSWA_HALF, DSWA_QBLOCK
    kw = qn + 2 * hb
    first_head = lax.broadcasted_iota(jnp.int32, (1, LANES), 1) < DSWA_HEAD_DIM
    n_pairs = q_ref.shape[3] // LANES
    blocks = range(lt // qn)
    units = [(rc, qb, pr) for rc in range(rb) for qb in blocks for pr in range(n_pairs)]
    rows = [slice(qb * qn, (qb + 1) * qn) for qb in blocks]
    win = [slice(qb * qn, qb * qn + kw) for qb in blocks]
    ls = [slice(pr * LANES, (pr + 1) * LANES) for pr in range(n_pairs)]
    kext = [jnp.concatenate([kp_ref[0, rc], k_ref[0, rc], kn_ref[0, rc]], axis=0) for rc in range(rb)]
    vext = [jnp.concatenate([vp_ref[0, rc], v_ref[0, rc], vn_ref[0, rc]], axis=0) for rc in range(rb)]
    kpos = lambda qb: t * lt + qb * qn - hb + lax.broadcasted_iota(jnp.int32, (1, kw), 1)
    qs = {(rc, qb, pr): jnp.concatenate([jnp.where(first_head, q_ref[0, rc, rows[qb], ls[pr]], 0.0),
                                         jnp.where(first_head, 0.0, q_ref[0, rc, rows[qb], ls[pr]])], axis=0)
          for rc, qb, pr in units}
    s = {(rc, qb, pr): _dot_nt(qs[rc, qb, pr], kext[rc][win[qb], ls[pr]]) + bias_ref[pr] for rc, qb, pr in units}
    for rc, qb, pr in units:
        if qb == blocks[0]:
            s[rc, qb, pr] = jnp.where(kpos(qb) >= 0, s[rc, qb, pr], NEG_INF)
        if qb == blocks[-1]:
            s[rc, qb, pr] = jnp.where(kpos(qb) < sub_len, s[rc, qb, pr], NEG_INF)
    m = {u: jnp.max(s[u], axis=-1, keepdims=True) for u in units}
    p = {u: jnp.exp2(s[u] - m[u]) for u in units}
    l = {u: jnp.sum(p[u], axis=-1, keepdims=True) for u in units}
    pv = {(rc, qb, pr): _dot(p[rc, qb, pr], vext[rc][win[qb], ls[pr]]) * (1.0 / l[rc, qb, pr])
          for rc, qb, pr in units}
    lse = {u: (m[u] + jnp.log2(l[u])) * math.log(2.0) for u in units}
    for rc, qb, pr in units:
        o_ref[0, rc, rows[qb], ls[pr]] = jnp.where(first_head, pv[rc, qb, pr][:qn], pv[rc, qb, pr][qn:])
        lse_ref[0, rc, rows[qb], ls[pr]] = jnp.where(first_head, lse[rc, qb, pr][:qn], lse[rc, qb, pr][qn:])


def _rel_bucket(rel):
    nb = REL_BUCKETS // 2
    max_exact = nb // 2
    ret = jnp.where(rel > 0, nb, 0)
    n = jnp.abs(rel)
    nf = jnp.maximum(n, 1).astype(F32)
    large = max_exact + (jnp.log(nf * (1.0 / max_exact)) / math.log(REL_MAX_DIST / max_exact)
                         * (nb - max_exact)).astype(jnp.int32)
    large = jnp.minimum(large, nb - 1)
    return ret + jnp.where(n < max_exact, n, large)


def _band_bias_kernel(tab_ref, o_ref, *, dils, hg):
    hb, qn = DSWA_HALF, DSWA_QBLOCK
    kw = qn + 2 * hb
    off = (lax.broadcasted_iota(jnp.int32, (qn, kw), 1) - hb
           - lax.broadcasted_iota(jnp.int32, (qn, kw), 0))
    inband = jnp.abs(off) <= hb
    for g, d in enumerate(dils):
        bucket = _rel_bucket(off * d)
        for hh in range(hg):
            acc = jnp.zeros((qn, kw), F32)
            for b in range(REL_BUCKETS):
                acc = jnp.where(bucket == b, tab_ref[b, g * hg + hh], acc)
            o_ref[g, hh // 2, (hh % 2) * qn:(hh % 2 + 1) * qn, :] = jnp.where(inband, acc * LOG2E, NEG_INF)


def _band_bias(rel_table, dils):
    qn = DSWA_QBLOCK
    hg = rel_table.shape[1] // len(dils)
    return pl.pallas_call(
        functools.partial(_band_bias_kernel, dils=dils, hg=hg),
        in_specs=[pl.BlockSpec(memory_space=pltpu.SMEM)],
        out_shape=jax.ShapeDtypeStruct((len(dils), hg // 2, 2 * qn, qn + 2 * DSWA_HALF), F32),
        name="band_bias",
    )(rel_table)


def _dswa_group(qkv, bias, lt=1024):
    B, dil, L, W3 = qkv.shape
    hb = DSWA_HALF
    gw = W3 // 3
    rb = max(1, lt // L)
    lt = min(lt, L)
    r = lt // hb
    main = lambda which: pl.BlockSpec((1, rb, lt, gw), lambda b, d, t: (b, d, t, which))
    prev = lambda which: pl.BlockSpec(
        (1, rb, hb, gw), lambda b, d, t: (b, d, jnp.maximum(t * r - 1, 0), which))
    nxt = lambda which: pl.BlockSpec(
        (1, rb, hb, gw), lambda b, d, t: (b, d, jnp.minimum((t + 1) * r, L // hb - 1), which))
    out = pl.BlockSpec((1, rb, lt, gw), lambda b, d, t: (b, d, t, 0))
    return pl.pallas_call(
        functools.partial(_dswa_kernel, sub_len=L),
        grid=(B, dil // rb, L // lt),
        in_specs=[main(0), main(1), prev(1), nxt(1), main(2), prev(2), nxt(2),
                  pl.BlockSpec(bias.shape, lambda b, d, t: (0, 0, 0))],
        out_specs=[out, out],
        out_shape=[jax.ShapeDtypeStruct((B, dil, L, gw), F32)] * 2,
        compiler_params=_params("parallel", "parallel", "parallel"),
        name=f"dswa_d{dil}",
    )(qkv, qkv, qkv, qkv, qkv, qkv, qkv, bias)


def _dswa_out_mlp_kernel(*refs, dils, final):
    ng = len(dils)
    o_refs, l_refs = refs[:ng], refs[ng:2 * ng]
    x_ref, wo_ref, g_ref, w1_ref, w2_ref, gf_ref, out_ref, o_s, l_s, y_s, x1_ref, acc_ref = refs[2 * ng:]
    tm = x_ref.shape[0]
    tiles = o_refs[0].shape[3] // LANES
    def prologue(y):
        for g in range(ng):
            for j in range(tiles):
                ls = slice(j * LANES, (j + 1) * LANES)
                o_s[g * tiles + j] = pltpu.einshape("drl->(rd)l", o_refs[g][0, :, :, ls])
                l_s[g * tiles + j] = pltpu.einshape("drl->(rd)l", l_refs[g][0, :, :, ls])
        pieces = []
        for j in range(tiles):
            lses = [l_s[g * tiles + j] for g in range(ng)]
            m = functools.reduce(jnp.maximum, lses)
            es = [jnp.exp(l - m) for l in lses]
            inv = 1.0 / functools.reduce(lambda a, b: a + b, es)
            for g in range(ng):
                c = g * tiles + j
                pieces.append(o_s[c] * (es[g] * inv))
                y[:, c * LANES:(c + 1) * LANES] = pieces[-1].astype(BF16)
        return pieces
    _mixer_out_mlp(prologue, y_s, (x_ref, wo_ref, g_ref, w1_ref, w2_ref, gf_ref, out_ref, x1_ref, acc_ref),
                   final, lookahead=False)


def _dswa_out_mlp(os_, lses, x2, w_out, S, g, w1, w2, g_final, final, tm=512):
    dils = tuple(o.shape[1] for o in os_)
    gw = os_[0].shape[3]
    nt = S // tm
    specs = lambda cur: [pl.BlockSpec((1, d, tm // d, gw), lambda i: (cur(i) // nt, 0, cur(i) % nt, 0))
                         for d in dils] * 2
    staging = pltpu.VMEM((len(dils) * gw // LANES, tm, LANES), F32)
    return _out_mlp_call(
        functools.partial(_dswa_out_mlp_kernel, dils=dils, final=final),
        (*os_, *lses), specs, [staging, staging],
        x2, w_out, g, w1, w2, g_final, tm, "dswa_out_mlp", lookahead=False)


def _gdn_layer(x2, g, w_in, conv_w, a_log, dt_bias, norm_w, w_out, B, S, mlp):
    nh = a_log.shape[1]
    z_w = nh * LANES
    qkvn, z, col, row = _gdn_proj(x2, g, w_in, conv_w, a_log, dt_bias, B, S)
    qkvn = qkvn.reshape(B, S, 3 * z_w)
    o_f = _gdn_scan(qkvn, col, row, nh, rev=False)
    o_b = _gdn_scan(qkvn, col, row, nh, rev=True)
    return _gdn_out_mlp(o_f.reshape(B * S, z_w), o_b.reshape(B * S, z_w), z, x2, norm_w,
                        w_out.astype(BF16), nh, *mlp)


def _dswa_proj_kernel(x_ref, g_ref, w_ref, *refs, dils, chunk):
    o_refs = refs
    n = w_ref.shape[1]
    ng = len(dils)
    gw = n // (3 * ng)
    tiles = gw // LANES
    h = _rms(x_ref[...], g_ref[...]).astype(BF16)

    def project(which):
        res = jnp.dot(h, w_ref[:, which * chunk:(which + 1) * chunk], preferred_element_type=F32)
        if which == 0:
            res = res * (DSWA_HEAD_DIM ** -0.5 * LOG2E)
        return res

    def regroup(which, res):
        for g, d in enumerate(dils):
            for j in range(tiles):
                src = slice((g * tiles + j) * LANES, (g * tiles + j + 1) * LANES)
                dst = slice((which * tiles + j) * LANES, (which * tiles + j + 1) * LANES)
                o_refs[g][0, :, :, dst] = pltpu.einshape("(rd)l->drl", res[:, src], d=d).astype(BF16)

    res = project(0)
    for which in range(1, 3):
        nxt = project(which)
        regroup(which - 1, res)
        res = nxt
    regroup(2, res)


def _dswa_proj(x2, g, w_in, dils, B, S, tm=1024):
    T, D = x2.shape
    n = w_in.shape[1]
    gw3 = n // len(dils)
    nt = S // tm
    return pl.pallas_call(
        functools.partial(_dswa_proj_kernel, dils=dils, chunk=gw3),
        grid=(T // tm,),
        in_specs=[pl.BlockSpec((tm, D), lambda i: (i, 0)),
                  pl.BlockSpec((1, D), lambda i: (0, 0)),
                  pl.BlockSpec((D, n), lambda i: (0, 0))],
        out_specs=[pl.BlockSpec((1, d, tm // d, gw3), lambda i: (i // nt, 0, i % nt, 0)) for d in dils],
        out_shape=[jax.ShapeDtypeStruct((B, d, S // d, gw3), BF16) for d in dils],
        compiler_params=_params("parallel"),
        name="dswa_proj",
    )(x2, g.reshape(1, D), w_in)


def _dswa_layer(x2, g, w_in, w_out, bias, B, S, mlp):
    dils = tuple(d for _, d in DSWA_CONFIGS)
    qkvs = _dswa_proj(x2, g, w_in.astype(BF16), dils, B, S)
    os_, lses = [], []
    for gi in range(len(dils)):
        o, lse = _dswa_group(qkvs[gi], bias[gi])
        os_.append(o)
        lses.append(lse)
    return _dswa_out_mlp(os_, lses, x2, w_out.astype(BF16), S, *mlp)


def kernel(x, norm_mix, norm_mlp, norm_final, rel_bias, gdn_w_in, gdn_conv_w, gdn_a_log,
           gdn_dt_bias, gdn_norm_w, gdn_w_out, dswa_w_in, dswa_w_out, mlp_w1, mlp_w2):
    B, S, D = x.shape
    depth = norm_mix.shape[0]
    x2 = x.reshape(B * S, D)
    assert all(window // (2 * dil) == DSWA_HALF for window, dil in DSWA_CONFIGS)
    bias = _band_bias(rel_bias, tuple(d for _, d in DSWA_CONFIGS))
    for i in range(depth):
        j = i // 2
        mlp = (norm_mlp[i], mlp_w1[i].astype(BF16), mlp_w2[i].astype(BF16), norm_final, i == depth - 1)
        if i % 2 == 0:
            x2 = _gdn_layer(x2, norm_mix[i], gdn_w_in[j], gdn_conv_w[j], gdn_a_log[j],
                            gdn_dt_bias[j], gdn_norm_w[j], gdn_w_out[j], B, S, mlp)
        else:
            x2 = _dswa_layer(x2, norm_mix[i], dswa_w_in[j], dswa_w_out[j], bias, B, S, mlp)
    return x2.reshape(B, S, D)
```

```python
import functools
import math

import jax
import jax.numpy as jnp
from jax import lax
from jax.experimental import pallas as pl
from jax.experimental.pallas import tpu as pltpu

F32 = jnp.float32
BF16 = jnp.bfloat16

RMS_EPS = 1e-6
L2_EPS = 1e-6
NEG_INF = -1e30
LOG2E = math.log2(math.e)

LANES = 128
SUBLANES = 8
VMEM_LIMIT = 56 * 1024 * 1024

GDN_DK = 128
GDN_CONV = 5
GDN_CHUNK = LANES
DSWA_HEAD_DIM = 64
DSWA_CONFIGS = ((128, 1), (512, 4), (2048, 16))
DSWA_HALF = 64
DSWA_QBLOCK = 128
REL_BUCKETS = 32
REL_MAX_DIST = 1024


def _params(*sem):
    return pltpu.CompilerParams(dimension_semantics=sem, vmem_limit_bytes=VMEM_LIMIT)


def _dot(a, b):
    return jnp.dot(a.astype(BF16), b.astype(BF16), preferred_element_type=F32)


def _dot_nt(a, b):
    return lax.dot_general(a.astype(BF16), b.astype(BF16), (((1,), (1,)), ((), ())),
                           preferred_element_type=F32)


def _split3(x):
    hi = x.astype(BF16)
    r = x - hi.astype(F32)
    mid = r.astype(BF16)
    return hi, mid, (r - mid.astype(F32)).astype(BF16)


def _rms(x, g):
    return x * lax.rsqrt(jnp.mean(x * x, axis=-1, keepdims=True) + RMS_EPS) * g


def _sigmoid(x):
    return 1.0 / (1.0 + jnp.exp(-x))


def _softplus(x):
    return jnp.maximum(x, 0.0) + jnp.log1p(jnp.exp(-jnp.abs(x)))


CONV_HALO = 16
CONV_BLOCK = 64


def _gdn_proj_kernel(x_ref, xp_ref, xn_ref, g_ref, wqkv_ref, wz_ref, w_ref, cw_ref,
                     prow_ref, qkv_ref, z_ref, col_ref, row_ref, *, nh, nt, chunk):
    C = GDN_CHUNK
    tm = x_ref.shape[0]
    t = pl.program_id(0) % nt
    pad = GDN_CONV // 2
    g = g_ref[...]
    h = _rms(x_ref[...], g).astype(BF16)
    h_halo = jnp.concatenate([_rms(jnp.where(t > 0, xp_ref[...], 0.0), g).astype(BF16),
                              _rms(jnp.where(t < nt - 1, xn_ref[...], 0.0), g).astype(BF16)], axis=0)
    CB = CONV_BLOCK
    taps = [i for i in range(GDN_CONV) if i != pad]
    sr = lax.broadcasted_iota(jnp.int32, (len(taps) * CB, CB + 2 * CONV_HALO), 0)
    sc = lax.broadcasted_iota(jnp.int32, (len(taps) * CB, CB + 2 * CONV_HALO), 1)
    src = (sr % CB) + (CONV_HALO - pad)
    for k, i in enumerate(taps):
        src = src + jnp.where(sr // CB == k, i, 0)
    shift_mat = (sc == src).astype(BF16)

    def project(c0):
        w = wqkv_ref[:, c0:c0 + chunk]
        main = jnp.dot(h, w, preferred_element_type=F32)
        halo = jnp.dot(h_halo, w, preferred_element_type=F32)
        ext = jnp.concatenate([halo[:CONV_HALO], main, halo[CONV_HALO:]], axis=0).astype(BF16)
        return main, ext

    starts = list(range(0, wqkv_ref.shape[1], chunk))
    nxt = project(starts[0])
    for n, c0 in enumerate(starts):
        cs = slice(c0, c0 + chunk)
        main, ext = nxt
        if n + 1 < len(starts):
            nxt = project(starts[n + 1])
        for b in range(tm // CB):
            rows = slice(b * CB, (b + 1) * CB)
            shifted = jnp.dot(shift_mat, ext[b * CB:(b + 1) * CB + 2 * CONV_HALO],
                              preferred_element_type=F32)
            acc = cw_ref[pad:pad + 1, cs] * main[rows]
            for k, i in enumerate(taps):
                acc = acc + cw_ref[i:i + 1, cs] * shifted[k * CB:(k + 1) * CB]
            y = acc * _sigmoid(acc)
            for j in range(chunk // LANES):
                yj = y[:, j * LANES:(j + 1) * LANES]
                head = c0 // LANES + j
                if head < 2 * nh:
                    inv = lax.rsqrt(jnp.sum(yj * yj, axis=-1, keepdims=True) + L2_EPS)
                    yj = yj * (inv * (GDN_DK ** -0.5) if head < nh else inv)
                qkv_ref[rows, c0 + j * LANES:c0 + (j + 1) * LANES] = yj.astype(BF16)
    for c0 in range(0, wz_ref.shape[1], chunk):
        cs = slice(c0, c0 + chunk)
        z_ref[:, cs] = jnp.dot(h, wz_ref[:, cs], preferred_element_type=F32).astype(BF16)
    a_c = jnp.dot(h, w_ref[...], preferred_element_type=F32)
    ri = lax.broadcasted_iota(jnp.int32, (C, C), 0)
    ci = lax.broadcasted_iota(jnp.int32, (C, C), 1)
    lower = (ri >= ci).astype(BF16)
    g_c = -jnp.exp(prow_ref[0:1, :]) * _softplus(a_c + prow_ref[1:2, :])
    beta_c = _sigmoid(a_c)
    for c in range(tm // C):
        sl = slice(c * C, (c + 1) * C)
        gc = g_c[sl, :]
        pre = sum(jnp.dot(lower, part, preferred_element_type=F32) for part in _split3(gc))
        cum = jnp.where(ci < nh, pre, pre[C - 1:C, :] - pre + gc)
        both = jnp.where(ci < 2 * nh, cum, beta_c[sl, :])
        col_ref[sl, :] = both
        row_ref[0, :, sl] = both.T[:4 * nh, :]


def _gdn_proj(x2, g, w_in, conv_w, a_log, dt_bias, B, S, tm=1024, chunk=256):
    T, D = x2.shape
    nh = a_log.shape[1]
    qkv_w = 3 * nh * LANES
    z_w = nh * LANES
    w_pad = jnp.zeros((D, LANES), F32).at[:, :4 * nh].set(w_in[:, qkv_w + z_w:])
    alog = jnp.zeros((LANES,), F32).at[:2 * nh].set(a_log.reshape(-1))
    dtb = jnp.zeros((LANES,), F32).at[:2 * nh].set(dt_bias.reshape(-1))
    prow = jnp.zeros((SUBLANES, LANES), F32).at[0].set(alog).at[1].set(dtb)
    nt = S // tm
    r = tm // CONV_HALO
    const = lambda shape: pl.BlockSpec(shape, lambda i: (0, 0))
    return pl.pallas_call(
        functools.partial(_gdn_proj_kernel, nh=nh, nt=nt, chunk=chunk),
        grid=(T // tm,),
        in_specs=[pl.BlockSpec((tm, D), lambda i: (i, 0)),
                  pl.BlockSpec((CONV_HALO, D), lambda i: (jnp.maximum(i * r - 1, 0), 0)),
                  pl.BlockSpec((CONV_HALO, D), lambda i: (jnp.minimum((i + 1) * r, T // CONV_HALO - 1), 0)),
                  const((1, D)), const((D, qkv_w)), const((D, z_w)), const((D, LANES)),
                  const((GDN_CONV, qkv_w)), const((SUBLANES, LANES))],
        out_specs=[pl.BlockSpec((tm, qkv_w), lambda i: (i, 0)),
                   pl.BlockSpec((tm, z_w), lambda i: (i, 0)),
                   pl.BlockSpec((tm, LANES), lambda i: (i, 0)),
                   pl.BlockSpec((1, 4 * nh, tm), lambda i: (i // nt, 0, i % nt))],
        out_shape=[jax.ShapeDtypeStruct((T, qkv_w), BF16),
                   jax.ShapeDtypeStruct((T, z_w), BF16),
                   jax.ShapeDtypeStruct((T, LANES), F32),
                   jax.ShapeDtypeStruct((B, 4 * nh, S), F32)],
        compiler_params=_params("parallel"),
        name="gdn_proj",
    )(x2, x2, x2, g.reshape(1, D), w_in[:, :qkv_w].astype(BF16), w_in[:, qkv_w:qkv_w + z_w].astype(BF16),
      w_pad.astype(BF16), conv_w.reshape(GDN_CONV, qkv_w), prow)


def _gdn_scan_kernel(q_ref, k_ref, v_ref, col_ref, row_ref, o_ref, state_ref, *, rev, nh):
    C = GDN_CHUNK
    t = pl.program_id(1)

    @pl.when(t == 0)
    def _():
        state_ref[...] = jnp.zeros_like(state_ref)

    ri = lax.broadcasted_iota(jnp.int32, (C, C), 0)
    ci = lax.broadcasted_iota(jnp.int32, (C, C), 1)
    incl = (ri <= ci) if rev else (ri >= ci)
    strict = (ri < ci) if rev else (ri > ci)
    eye = (ri == ci).astype(F32)
    levels = [((ri >> (lg + 1)) == (ci >> (lg + 1))) & ((ri >> lg) != (ci >> lg))
              for lg in range(int(math.log2(C)))]
    last = 0 if rev else C - 1
    nc = q_ref.shape[1] // C
    chunks = list(reversed(range(nc))) if rev else list(range(nc))
    heads = range(nh)
    hs = [slice(h * LANES, (h + 1) * LANES) for h in heads]
    gi = [(nh if rev else 0) + h for h in heads]
    units = [(c, h) for c in chunks for h in heads]
    rows = {c: slice(c * C, (c + 1) * C) for c in chunks}
    col = {c: col_ref[0, rows[c], :] for c in chunks}
    row = {c: row_ref[0, :, rows[c]] for c in chunks}
    kf = {u: k_ref[0, rows[u[0]], hs[u[1]]] for u in units}
    k = {u: kf[u].astype(F32) for u in units}
    g_col = {(c, h): col[c][:, gi[h]:gi[h] + 1] for c, h in units}
    b_col = {(c, h): col[c][:, 2 * nh + gi[h]:2 * nh + gi[h] + 1] for c, h in units}
    g_last = {u: g_col[u][last:last + 1, :] for u in units}
    decay = {(c, h): jnp.where(incl, jnp.exp(jnp.where(incl, g_col[c, h] - row[c][gi[h]:gi[h] + 1, :], 0.0)), 0.0)
             for c, h in units}
    kb = {u: k[u] * b_col[u] for u in units}
    res = {(c, h): _dot_nt(jnp.concatenate([kb[c, h].astype(BF16), q_ref[0, rows[c], hs[h]]], axis=0), kf[c, h])
           for c, h in units}
    a = {u: jnp.where(strict, res[u][:C] * decay[u], 0.0) for u in units}
    intra = {u: (res[u][C:] * decay[u]).astype(BF16) for u in units}
    tinv = {u: eye - jnp.where(levels[0], a[u], 0.0) for u in units}
    for lg, m in enumerate(levels[1:], start=1):
        size = 1 << lg
        moving = [r0 for r0 in range(0, C, size) if ((r0 >> lg) & 1) == (0 if rev else 1)]
        tb = {u: tinv[u].astype(BF16) for u in units}
        if size < SUBLANES:
            pm = {u: _dot(tb[u], jnp.where(m, a[u], 0.0)) for u in units}
            tinv = {u: tinv[u] - _dot(pm[u], tb[u]) for u in units}
            continue
        lhs = {u: jnp.concatenate([tinv[u][r0:r0 + size] for r0 in moving], axis=0) for u in units}
        pm = {u: _dot(lhs[u], jnp.where(m, a[u], 0.0)) for u in units}
        new = {u: lhs[u] - _dot(pm[u], tb[u]) for u in units}
        tinv = {u: jnp.concatenate(
            [new[u][moving.index(r0) * size:(moving.index(r0) + 1) * size] if r0 in moving
             else tinv[u][r0:r0 + size] for r0 in range(0, C, size)], axis=0) for u in units}
    e_col = {u: jnp.exp(g_col[u]) for u in units}
    sol = {(c, h): _dot(tinv[c, h], jnp.concatenate(
        [v_ref[0, rows[c], hs[h]].astype(F32) * b_col[c, h], kb[c, h] * e_col[c, h]], axis=1)) for c, h in units}
    qd = {(c, h): (q_ref[0, rows[c], hs[h]].astype(F32) * e_col[c, h]).astype(BF16) for c, h in units}
    post = {u: jnp.concatenate([intra[u], (k[u] * jnp.exp(g_last[u] - g_col[u])).T.astype(BF16)], axis=0)
            for u in units}
    state = [state_ref[h] for h in heads]
    for c in chunks:
        sb = [state[h].astype(BF16) for h in heads]
        r2 = [_dot(jnp.concatenate([sol[c, h][:, LANES:].astype(BF16), qd[c, h]], axis=0), sb[h]) for h in heads]
        v_new = [(sol[c, h][:, :LANES] - r2[h][:C]).astype(BF16) for h in heads]
        r3 = [_dot(post[c, h], v_new[h]) for h in heads]
        for h in heads:
            o_ref[0, rows[c], hs[h]] = (r2[h][C:] + r3[h][:C]).astype(o_ref.dtype)
        state = [state[h] * jnp.exp(g_last[c, h]) + r3[h][C:] for h in heads]
    for h in heads:
        state_ref[h] = state[h]


def _gdn_scan(qkvn, col, row, nh, rev, nc=4):
    B, S, _ = qkvn.shape
    ts = nc * GDN_CHUNK
    nt = S // ts
    width = nh * LANES
    tmap = (lambda t: nt - 1 - t) if rev else (lambda t: t)
    col3 = col.reshape(B, S, LANES)
    return pl.pallas_call(
        functools.partial(_gdn_scan_kernel, rev=rev, nh=nh),
        grid=(B, nt),
        in_specs=[pl.BlockSpec((1, ts, width), lambda b, t: (b, tmap(t), 0)),
                  pl.BlockSpec((1, ts, width), lambda b, t: (b, tmap(t), 1)),
                  pl.BlockSpec((1, ts, width), lambda b, t: (b, tmap(t), 2)),
                  pl.BlockSpec((1, ts, LANES), lambda b, t: (b, tmap(t), 0)),
                  pl.BlockSpec((1, 4 * nh, ts), lambda b, t: (b, 0, tmap(t)))],
        out_specs=pl.BlockSpec((1, ts, width), lambda b, t: (b, tmap(t), 0)),
        out_shape=jax.ShapeDtypeStruct((B, S, width), BF16),
        scratch_shapes=[pltpu.VMEM((nh, GDN_DK, LANES), F32)],
        compiler_params=_params("parallel", "arbitrary"),
        name="gdn_scan_bwd" if rev else "gdn_scan_fwd",
    )(qkvn, qkvn, qkvn, col3, row)


MLP_CHUNK = 512


def _anchor(v):
    folded = jnp.sum(v.reshape(v.shape[0] // SUBLANES, SUBLANES, LANES), axis=0)
    bits = lax.shift_right_logical(pltpu.bitcast(folded, jnp.uint32), jnp.uint32(32))
    return pltpu.bitcast(bits, F32)


def _out_mlp_tail(y_ref, x_ref, wo_ref, g_ref, w1_ref, w2_ref, gf_ref, o_ref, x1_ref, acc_ref, final,
                  anchors=()):
    x1_ref[...] = x_ref[...] + jnp.dot(y_ref[...], wo_ref[...], preferred_element_type=F32)
    h = _rms(x1_ref[...], g_ref[...]).astype(BF16)
    starts = list(range(0, w1_ref.shape[1], MLP_CHUNK))
    for n, c in enumerate(starts):
        a = jnp.maximum(jnp.dot(h, w1_ref[:, c:c + MLP_CHUNK], preferred_element_type=F32), 0.0)
        part = jnp.dot((a * a).astype(BF16), w2_ref[c:c + MLP_CHUNK, :], preferred_element_type=F32)
        if c == 0:
            acc_ref[...] = part
        else:
            acc_ref[...] += part
        for k in range(n, len(anchors), len(starts)):
            acc_ref[0:SUBLANES, 0:LANES] += anchors[k]
    out = x1_ref[...] + acc_ref[...]
    if final:
        out = _rms(out, gf_ref[...])
    o_ref[...] = out


def _mixer_out_mlp(prologue, y_ref, tail_refs, final, lookahead):
    if not lookahead:
        prologue(y_ref.at[0])
        _out_mlp_tail(y_ref.at[0], *tail_refs, final)
        return
    i = pl.program_id(0)

    @pl.when(i == 0)
    def _():
        y_ref[...] = jnp.zeros_like(y_ref)

    for parity in range(2):
        @pl.when(i % 2 == parity)
        def _():
            pieces = prologue(y_ref.at[parity])
            _out_mlp_tail(y_ref.at[1 - parity], *tail_refs, final, anchors=[_anchor(v) for v in pieces])


def _out_mlp_call(kernel_fn, mixer_args, mixer_specs, mixer_scratch, x2, w_out, g, w1, w2,
                  g_final, tm, name, lookahead):
    T, D = x2.shape
    n = T // tm
    if lookahead:
        cur = lambda i: jnp.minimum(i, n - 1)
        prev = lambda i: jnp.maximum(i - 1, 0)
    else:
        cur = prev = lambda i: i
    once = lambda shape: pl.BlockSpec(shape, lambda i: (0, 0), pipeline_mode=pl.Buffered(1))
    return pl.pallas_call(
        kernel_fn,
        grid=(n + 1 if lookahead else n,),
        in_specs=mixer_specs(cur) + [pl.BlockSpec((tm, D), lambda i: (prev(i), 0)), once(w_out.shape),
                                     once((1, D)), once(w1.shape), once(w2.shape), once((1, D))],
        out_specs=pl.BlockSpec((tm, D), lambda i: (prev(i), 0)),
        out_shape=jax.ShapeDtypeStruct((T, D), F32),
        scratch_shapes=mixer_scratch + [pltpu.VMEM((2 if lookahead else 1, tm, w_out.shape[0]), BF16),
                                        pltpu.VMEM((tm, D), F32), pltpu.VMEM((tm, D), F32)],
        compiler_params=_params("arbitrary" if lookahead else "parallel"),
        name=name,
    )(*mixer_args, x2, w_out, g.reshape(1, D), w1, w2, g_final.reshape(1, D))


def _gdn_out_mlp_kernel(of_ref, ob_ref, z_ref, nw_ref, x_ref, wo_ref, g_ref, w1_ref, w2_ref, gf_ref,
                        o_ref, y_ref, x1_ref, acc_ref, *, nh, final):
    def prologue(y):
        pieces = []
        for h in range(nh):
            hs = slice(h * LANES, (h + 1) * LANES)
            o = of_ref[:, hs].astype(F32) + ob_ref[:, hs].astype(F32)
            o = o * lax.rsqrt(jnp.mean(o * o, axis=-1, keepdims=True) + RMS_EPS)
            z = z_ref[:, hs].astype(F32)
            pieces.append(o * nw_ref[...] * (z * _sigmoid(z)))
            y[:, hs] = pieces[-1].astype(BF16)
        return pieces
    _mixer_out_mlp(prologue, y_ref, (x_ref, wo_ref, g_ref, w1_ref, w2_ref, gf_ref, o_ref, x1_ref, acc_ref),
                   final, lookahead=True)


def _gdn_out_mlp(o_f, o_b, z, x2, norm_w, w_out, nh, g, w1, w2, g_final, final, tm=512):
    W = o_f.shape[1]
    specs = lambda cur: [pl.BlockSpec((tm, W), lambda i: (cur(i), 0))] * 3 + \
                        [pl.BlockSpec((1, LANES), lambda i: (0, 0))]
    return _out_mlp_call(
        functools.partial(_gdn_out_mlp_kernel, nh=nh, final=final),
        (o_f, o_b, z, norm_w.reshape(1, LANES)), specs,
        [], x2, w_out, g, w1, w2, g_final, tm, "gdn_out_mlp", lookahead=True)


def _dswa_kernel(q_ref, k_ref, kp_ref, kn_ref, v_ref, vp_ref, vn_ref, bias_ref,
                 o_ref, lse_ref, *, sub_len):
    t = pl.program_id(2)
    rb, lt = q_ref.shape[1], q_ref.shape[2]
    hb, qn = DSWA_HALF, DSWA_QBLOCK
    kw = qn + 2 * hb
    first_head = lax.broadcasted_iota(jnp.int32, (1, LANES), 1) < DSWA_HEAD_DIM
    n_pairs = q_ref.shape[3] // LANES
    blocks = range(lt // qn)
    units = [(rc, qb, pr) for rc in range(rb) for qb in blocks for pr in range(n_pairs)]
    rows = [slice(qb * qn, (qb + 1) * qn) for qb in blocks]
    win = [slice(qb * qn, qb * qn + kw) for qb in blocks]
    ls = [slice(pr * LANES, (pr + 1) * LANES) for pr in range(n_pairs)]
    kext = [jnp.concatenate([kp_ref[0, rc], k_ref[0, rc], kn_ref[0, rc]], axis=0) for rc in range(rb)]
    vext = [jnp.concatenate([vp_ref[0, rc], v_ref[0, rc], vn_ref[0, rc]], axis=0) for rc in range(rb)]
    kpos = lambda qb: t * lt + qb * qn - hb + lax.broadcasted_iota(jnp.int32, (1, kw), 1)
    qs = {(rc, qb, pr): jnp.concatenate([jnp.where(first_head, q_ref[0, rc, rows[qb], ls[pr]], 0.0),
                                         jnp.where(first_head, 0.0, q_ref[0, rc, rows[qb], ls[pr]])], axis=0)
          for rc, qb, pr in units}
    s = {(rc, qb, pr): _dot_nt(qs[rc, qb, pr], kext[rc][win[qb], ls[pr]]) + bias_ref[pr] for rc, qb, pr in units}
    for rc, qb, pr in units:
        if qb == blocks[0]:
            s[rc, qb, pr] = jnp.where(kpos(qb) >= 0, s[rc, qb, pr], NEG_INF)
        if qb == blocks[-1]:
            s[rc, qb, pr] = jnp.where(kpos(qb) < sub_len, s[rc, qb, pr], NEG_INF)
    m = {u: jnp.max(s[u], axis=-1, keepdims=True) for u in units}
    p = {u: jnp.exp2(s[u] - m[u]) for u in units}
    l = {u: jnp.sum(p[u], axis=-1, keepdims=True) for u in units}
    pv = {(rc, qb, pr): _dot(p[rc, qb, pr], vext[rc][win[qb], ls[pr]]) * (1.0 / l[rc, qb, pr])
          for rc, qb, pr in units}
    lse = {u: (m[u] + jnp.log2(l[u])) * math.log(2.0) for u in units}
    for rc, qb, pr in units:
        o_ref[0, rc, rows[qb], ls[pr]] = jnp.where(first_head, pv[rc, qb, pr][:qn], pv[rc, qb, pr][qn:])
        lse_ref[0, rc, rows[qb], ls[pr]] = jnp.where(first_head, lse[rc, qb, pr][:qn], lse[rc, qb, pr][qn:])


def _rel_bucket(rel):
    nb = REL_BUCKETS // 2
    max_exact = nb // 2
    ret = jnp.where(rel > 0, nb, 0)
    n = jnp.abs(rel)
    nf = jnp.maximum(n, 1).astype(F32)
    large = max_exact + (jnp.log(nf * (1.0 / max_exact)) / math.log(REL_MAX_DIST / max_exact)
                         * (nb - max_exact)).astype(jnp.int32)
    large = jnp.minimum(large, nb - 1)
    return ret + jnp.where(n < max_exact, n, large)


def _band_bias_kernel(tab_ref, o_ref, *, dils, hg):
    hb, qn = DSWA_HALF, DSWA_QBLOCK
    kw = qn + 2 * hb
    off = (lax.broadcasted_iota(jnp.int32, (qn, kw), 1) - hb
           - lax.broadcasted_iota(jnp.int32, (qn, kw), 0))
    inband = jnp.abs(off) <= hb
    for g, d in enumerate(dils):
        bucket = _rel_bucket(off * d)
        for hh in range(hg):
            acc = jnp.zeros((qn, kw), F32)
            for b in range(REL_BUCKETS):
                acc = jnp.where(bucket == b, tab_ref[b, g * hg + hh], acc)
            o_ref[g, hh // 2, (hh % 2) * qn:(hh % 2 + 1) * qn, :] = jnp.where(inband, acc * LOG2E, NEG_INF)


def _band_bias(rel_table, dils):
    qn = DSWA_QBLOCK
    hg = rel_table.shape[1] // len(dils)
    return pl.pallas_call(
        functools.partial(_band_bias_kernel, dils=dils, hg=hg),
        in_specs=[pl.BlockSpec(memory_space=pltpu.SMEM)],
        out_shape=jax.ShapeDtypeStruct((len(dils), hg // 2, 2 * qn, qn + 2 * DSWA_HALF), F32),
        name="band_bias",
    )(rel_table)


def _dswa_group(qkv, bias, lt=2048):
    B, dil, L, W3 = qkv.shape
    hb = DSWA_HALF
    gw = W3 // 3
    rb = max(1, lt // L)
    lt = min(lt, L)
    r = lt // hb
    main = lambda which: pl.BlockSpec((1, rb, lt, gw), lambda b, d, t: (b, d, t, which))
    prev = lambda which: pl.BlockSpec(
        (1, rb, hb, gw), lambda b, d, t: (b, d, jnp.maximum(t * r - 1, 0), which))
    nxt = lambda which: pl.BlockSpec(
        (1, rb, hb, gw), lambda b, d, t: (b, d, jnp.minimum((t + 1) * r, L // hb - 1), which))
    out = pl.BlockSpec((1, rb, lt, gw), lambda b, d, t: (b, d, t, 0))
    return pl.pallas_call(
        functools.partial(_dswa_kernel, sub_len=L),
        grid=(B, dil // rb, L // lt),
        in_specs=[main(0), main(1), prev(1), nxt(1), main(2), prev(2), nxt(2),
                  pl.BlockSpec(bias.shape, lambda b, d, t: (0, 0, 0))],
        out_specs=[out, out],
        out_shape=[jax.ShapeDtypeStruct((B, dil, L, gw), F32)] * 2,
        compiler_params=_params("parallel", "parallel", "parallel"),
        name=f"dswa_d{dil}",
    )(qkv, qkv, qkv, qkv, qkv, qkv, qkv, bias)


def _dswa_out_mlp_kernel(*refs, dils, final):
    ng = len(dils)
    o_refs, l_refs = refs[:ng], refs[ng:2 * ng]
    x_ref, wo_ref, g_ref, w1_ref, w2_ref, gf_ref, out_ref, o_s, l_s, y_s, x1_ref, acc_ref = refs[2 * ng:]
    tm = x_ref.shape[0]
    tiles = o_refs[0].shape[3] // LANES
    def prologue(y):
        for g in range(ng):
            for j in range(tiles):
                ls = slice(j * LANES, (j + 1) * LANES)
                o_s[g * tiles + j] = pltpu.einshape("drl->(rd)l", o_refs[g][0, :, :, ls])
                l_s[g * tiles + j] = pltpu.einshape("drl->(rd)l", l_refs[g][0, :, :, ls])
        pieces = []
        for j in range(tiles):
            lses = [l_s[g * tiles + j] for g in range(ng)]
            m = functools.reduce(jnp.maximum, lses)
            es = [jnp.exp(l - m) for l in lses]
            inv = 1.0 / functools.reduce(lambda a, b: a + b, es)
            for g in range(ng):
                c = g * tiles + j
                pieces.append(o_s[c] * (es[g] * inv))
                y[:, c * LANES:(c + 1) * LANES] = pieces[-1].astype(BF16)
        return pieces
    _mixer_out_mlp(prologue, y_s, (x_ref, wo_ref, g_ref, w1_ref, w2_ref, gf_ref, out_ref, x1_ref, acc_ref),
                   final, lookahead=False)


def _dswa_out_mlp(os_, lses, x2, w_out, S, g, w1, w2, g_final, final, tm=512):
    dils = tuple(o.shape[1] for o in os_)
    gw = os_[0].shape[3]
    nt = S // tm
    specs = lambda cur: [pl.BlockSpec((1, d, tm // d, gw), lambda i: (cur(i) // nt, 0, cur(i) % nt, 0))
                         for d in dils] * 2
    staging = pltpu.VMEM((len(dils) * gw // LANES, tm, LANES), F32)
    return _out_mlp_call(
        functools.partial(_dswa_out_mlp_kernel, dils=dils, final=final),
        (*os_, *lses), specs, [staging, staging],
        x2, w_out, g, w1, w2, g_final, tm, "dswa_out_mlp", lookahead=False)


def _gdn_layer(x2, g, w_in, conv_w, a_log, dt_bias, norm_w, w_out, B, S, mlp):
    nh = a_log.shape[1]
    z_w = nh * LANES
    qkvn, z, col, row = _gdn_proj(x2, g, w_in, conv_w, a_log, dt_bias, B, S)
    qkvn = qkvn.reshape(B, S, 3 * z_w)
    o_f = _gdn_scan(qkvn, col, row, nh, rev=False)
    o_b = _gdn_scan(qkvn, col, row, nh, rev=True)
    return _gdn_out_mlp(o_f.reshape(B * S, z_w), o_b.reshape(B * S, z_w), z, x2, norm_w,
                        w_out.astype(BF16), nh, *mlp)


def _dswa_proj_kernel(x_ref, g_ref, w_ref, *refs, dils, chunk):
    o_refs = refs
    n = w_ref.shape[1]
    ng = len(dils)
    gw = n // (3 * ng)
    tiles = gw // LANES
    h = _rms(x_ref[...], g_ref[...]).astype(BF16)

    def project(which):
        res = jnp.dot(h, w_ref[:, which * chunk:(which + 1) * chunk], preferred_element_type=F32)
        if which == 0:
            res = res * (DSWA_HEAD_DIM ** -0.5 * LOG2E)
        return res

    def regroup(which, res):
        for g, d in enumerate(dils):
            for j in range(tiles):
                src = slice((g * tiles + j) * LANES, (g * tiles + j + 1) * LANES)
                dst = slice((which * tiles + j) * LANES, (which * tiles + j + 1) * LANES)
                o_refs[g][0, :, :, dst] = pltpu.einshape("(rd)l->drl", res[:, src], d=d).astype(BF16)

    res = project(0)
    for which in range(1, 3):
        nxt = project(which)
        regroup(which - 1, res)
        res = nxt
    regroup(2, res)


def _dswa_proj(x2, g, w_in, dils, B, S, tm=1024):
    T, D = x2.shape
    n = w_in.shape[1]
    gw3 = n // len(dils)
    nt = S // tm
    return pl.pallas_call(
        functools.partial(_dswa_proj_kernel, dils=dils, chunk=gw3),
        grid=(T // tm,),
        in_specs=[pl.BlockSpec((tm, D), lambda i: (i, 0)),
                  pl.BlockSpec((1, D), lambda i: (0, 0)),
                  pl.BlockSpec((D, n), lambda i: (0, 0))],
        out_specs=[pl.BlockSpec((1, d, tm // d, gw3), lambda i: (i // nt, 0, i % nt, 0)) for d in dils],
        out_shape=[jax.ShapeDtypeStruct((B, d, S // d, gw3), BF16) for d in dils],
        compiler_params=_params("parallel"),
        name="dswa_proj",
    )(x2, g.reshape(1, D), w_in)


def _dswa_layer(x2, g, w_in, w_out, bias, B, S, mlp):
    dils = tuple(d for _, d in DSWA_CONFIGS)
    qkvs = _dswa_proj(x2, g, w_in.astype(BF16), dils, B, S)
    os_, lses = [], []
    for gi in range(len(dils)):
        o, lse = _dswa_group(qkvs[gi], bias[gi])
        os_.append(o)
        lses.append(lse)
    return _dswa_out_mlp(os_, lses, x2, w_out.astype(BF16), S, *mlp)


def kernel(x, norm_mix, norm_mlp, norm_final, rel_bias, gdn_w_in, gdn_conv_w, gdn_a_log,
           gdn_dt_bias, gdn_norm_w, gdn_w_out, dswa_w_in, dswa_w_out, mlp_w1, mlp_w2):
    B, S, D = x.shape
    depth = norm_mix.shape[0]
    x2 = x.reshape(B * S, D)
    assert all(window // (2 * dil) == DSWA_HALF for window, dil in DSWA_CONFIGS)
    bias = _band_bias(rel_bias, tuple(d for _, d in DSWA_CONFIGS))
    for i in range(depth):
        j = i // 2
        mlp = (norm_mlp[i], mlp_w1[i].astype(BF16), mlp_w2[i].astype(BF16), norm_final, i == depth - 1)
        if i % 2 == 0:
            x2 = _gdn_layer(x2, norm_mix[i], gdn_w_in[j], gdn_conv_w[j], gdn_a_log[j],
                            gdn_dt_bias[j], gdn_norm_w[j], gdn_w_out[j], B, S, mlp)
        else:
            x2 = _dswa_layer(x2, norm_mix[i], dswa_w_in[j], dswa_w_out[j], bias, B, S, mlp)
    return x2.reshape(B, S, D)
```
